```python
import math
import jax, jax.numpy as jnp
from jax import lax
import numpy as np

D_MODEL = 2048
BATCH = 2
SEQ = 4096
DEPTH = 2

GRID_W = 64
CTX_LEN = 256

BRANCH_WIDTH = 1024
N_BRANCH = 3

HY_WIDTH = BRANCH_WIDTH
HY_ORDER = 2
HY_CONV = 3
HY_EMB = 33
HY_BANDS = (HY_EMB - 1) // 2
HY_FILTER_HIDDEN = 64
HY_DECAY_TARGET = 1e-2
HY_FAST_DECAY = 0.3
HY_SLOW_DECAY = 1.5

HG_HEADS = 8
HG_DK = 128
HG_DV = BRANCH_WIDTH // HG_HEADS
HG_FWIDTH = HG_HEADS * HG_DK
HG_WIDTH = HG_HEADS * HG_DV
HG_F_MIN = 1e-30

DN_QK_HEADS = 4
DN_V_HEADS = 8
DN_DK = 128
DN_DV = BRANCH_WIDTH // DN_V_HEADS
DN_QK_WIDTH = DN_QK_HEADS * DN_DK
DN_WIDTH = DN_V_HEADS * DN_DV
DN_CONV = 3

CHUNK = 64
D_FF = 4 * D_MODEL
ALPHA = (2 * DEPTH) ** 0.25
BETA_INIT = (8 * DEPTH) ** -0.25
LN_EPS = 1e-5
RMS_EPS = 1e-6

STATE_SIZES = (HG_FWIDTH, HG_FWIDTH, HG_WIDTH, DN_QK_WIDTH, DN_WIDTH, 2 * DN_V_HEADS, 2 * DN_V_HEADS)
READ_SIZES = (HG_FWIDTH, HG_WIDTH, DN_QK_WIDTH, DN_WIDTH)
IN_SIZES = STATE_SIZES + READ_SIZES + ((HY_ORDER + 1) * HY_WIDTH, N_BRANCH * D_MODEL)
STATE_COLS = sum(STATE_SIZES)
IN_COLS = sum(IN_SIZES)

kernel_name = "hybrid_hyena_hgrn2_gdn_dit_block"


def _split(h, sizes):
    idx = [int(i) for i in np.cumsum(sizes)[:-1]]
    return jnp.split(h, idx, axis=-1)


def _flip(t):
    return None if t is None else jnp.flip(t, axis=1)


def _layernorm(x, g, b):
    xf = x.astype(jnp.float32)
    mu = jnp.mean(xf, -1, keepdims=True)
    var = jnp.mean(jnp.square(xf - mu), -1, keepdims=True)
    return ((xf - mu) * lax.rsqrt(var + LN_EPS) * g + b).astype(x.dtype)


def _rmsnorm(o, w):
    of = o.astype(jnp.float32)
    return of * lax.rsqrt(jnp.mean(of * of, -1, keepdims=True) + RMS_EPS) * w


def _l2norm(t):
    t = t.astype(jnp.float32)
    return t * lax.rsqrt(jnp.sum(t * t, -1, keepdims=True) + 1e-6)


def _modulate(x, shift, scale):
    return x * (1.0 + scale) + shift


def _short_conv(x, w, is_grid):
    b, l, ch = x.shape
    xs = x.reshape(b, l // GRID_W, GRID_W, ch) if is_grid else x
    k = w.shape[0]
    n = xs.shape[-2]
    xp = jnp.pad(xs, [(0, 0)] * (xs.ndim - 2) + [(k // 2, k // 2), (0, 0)])
    y = xp[..., 0:n, :] * w[0]
    for j in range(1, k):
        y = y + xp[..., j:j + n, :] * w[j]
    return y.reshape(b, l, ch)


def _hyena_filters(L, w1, b1, w2, b2, w3):
    f32 = jnp.float32
    t = jnp.linspace(0.0, 1.0, L, dtype=f32)[:, None]
    ang = (2.0 * math.pi / L) * jnp.arange(L, dtype=f32)[:, None]
    bands = jnp.linspace(1e-4, HY_BANDS - 1, HY_BANDS, dtype=f32)[None, :]
    feats = jnp.concatenate([t, jnp.cos(bands * ang), -jnp.sin(bands * ang)], axis=-1)
    hid = jnp.sin(feats @ w1.astype(f32) + b1.astype(f32))
    hid = jnp.sin(hid @ w2.astype(f32) + b2.astype(f32))
    h = (hid @ w3.astype(f32)).reshape(L, HY_ORDER, 2, HY_WIDTH)
    deltas = jnp.abs(jnp.linspace(math.log(HY_DECAY_TARGET) / HY_SLOW_DECAY,
                                  math.log(HY_DECAY_TARGET) / HY_FAST_DECAY, HY_WIDTH, dtype=f32))
    window = jnp.exp(-t * deltas)
    return h * window[:, None, None, :]


def _fft_long_conv(u, h_fwd, h_bwd, skip):
    L = u.shape[1]
    kfull = jnp.concatenate([h_fwd, jnp.zeros_like(h_fwd[:1]), h_bwd[:0:-1]], axis=0)
    uf = jnp.fft.rfft(u.astype(jnp.float32), n=2 * L, axis=1)
    kf = jnp.fft.rfft(kfull.astype(jnp.float32), n=2 * L, axis=0)
    y = jnp.fft.irfft(uf * kf[None], n=2 * L, axis=1)[:, :L]
    return (y + u.astype(jnp.float32) * skip).astype(u.dtype)


def _hyena(hh, conv_w, conv_b, filt_params, skip, is_grid):
    L = hh.shape[1]
    filt = _hyena_filters(L, *filt_params)
    z = _short_conv(hh, conv_w, is_grid) + conv_b
    v, *gates = jnp.split(z, HY_ORDER + 1, axis=-1)
    y = v
    for n, gate in enumerate(gates):
        y = gate * _fft_long_conv(y, filt[:, n, 0], filt[:, n, 1], skip[n])
    return y


def _to_chunks(t):
    b, l = t.shape[:2]
    t = t.reshape(b, l // CHUNK, CHUNK, *t.shape[2:])
    return jnp.moveaxis(t, (1, 2), (0, 3))


def _from_chunks(t):
    t = jnp.moveaxis(t, (0, 3), (1, 2))
    return t.reshape(t.shape[0], t.shape[1] * t.shape[2], *t.shape[3:])


def _gla_scan(q, k, v, log_f, s0):
    with_output = q is not None
    tri = jnp.tril(jnp.ones((CHUNK, CHUNK), bool))[:, :, None]
    xs = tuple(_to_chunks(t.astype(jnp.float32)) for t in (k, v, log_f))
    if with_output:
        xs = xs + (_to_chunks(q.astype(jnp.float32)),)

    def step(s, inp):
        kc, vc, gc = inp[:3]
        bc = jnp.cumsum(gc, axis=-2)
        b_last = bc[..., -1:, :]
        s_new = jnp.exp(b_last[..., 0, :])[..., None] * s + jnp.einsum(
            'bhsk,bhsv->bhkv', kc * jnp.exp(b_last - bc), vc)
        if not with_output:
            return s_new, None
        qc = inp[3]
        diff = bc[..., :, None, :] - bc[..., None, :, :]
        w = jnp.where(tri, jnp.exp(jnp.where(tri, diff, 0.0)), 0.0)
        att = jnp.einsum('bhtk,bhsk,bhtsk->bhts', qc, kc, w)
        o = jnp.einsum('bhtk,bhkv->bhtv', qc * jnp.exp(bc), s) + jnp.einsum('bhts,bhsv->bhtv', att, vc)
        return s_new, o

    s_fin, ys = lax.scan(step, s0, xs)
    return (_from_chunks(ys) if with_output else None), s_fin


def _delta_scan(q, k, v, g, beta, s0):
    with_output = q is not None
    tri = jnp.tril(jnp.ones((CHUNK, CHUNK), bool))
    strict = jnp.tril(jnp.ones((CHUNK, CHUNK), bool), -1)
    eye = jnp.eye(CHUNK, dtype=jnp.float32)
    xs = tuple(_to_chunks(t.astype(jnp.float32)) for t in (k, v, g, beta))
    if with_output:
        xs = xs + (_to_chunks(q.astype(jnp.float32)),)

    def step(s, inp):
        kc, vc, gc, bc = inp[:4]
        dv = vc.shape[-1]
        gcum = jnp.cumsum(gc, axis=-1)
        diff = gcum[..., :, None] - gcum[..., None, :]
        decay = jnp.where(tri, jnp.exp(jnp.where(tri, diff, 0.0)), 0.0)
        kb = kc * bc[..., None]
        a = eye + jnp.where(strict, jnp.einsum('bhtk,bhsk->bhts', kb, kc) * decay, 0.0)
        rhs = jnp.concatenate([vc * bc[..., None], kb * jnp.exp(gcum)[..., None]], axis=-1)
        sol = lax.linalg.triangular_solve(a, rhs, left_side=True, lower=True, unit_diagonal=False)
        u, w = sol[..., :dv], sol[..., dv:]
        v_new = u - jnp.einsum('bhtk,bhkv->bhtv', w, s)
        g_last = gcum[..., -1:]
        s_new = jnp.exp(g_last)[..., None] * s + jnp.einsum(
            'bhsk,bhsv->bhkv', kc * jnp.exp(g_last - gcum)[..., None], v_new)
        if not with_output:
            return s_new, None
        qc = inp[4]
        att = jnp.einsum('bhtk,bhsk->bhts', qc, kc) * decay
        o = jnp.einsum('bhtk,bhkv->bhtv', qc * jnp.exp(gcum)[..., None], s) + jnp.einsum(
            'bhts,bhsv->bhtv', att, v_new)
        return s_new, o

    s_fin, ys = lax.scan(step, s0, xs)
    return (_from_chunks(ys) if with_output else None), s_fin


def _hgrn2(q, f_f, f_b, i, g, lb_f, lb_b, norm_w, states):
    bsz, L = i.shape[:2]
    heads = lambda t: t.reshape(bsz, L, HG_HEADS, -1)

    def forget(fz, lb):
        fz = fz.astype(jnp.float32)
        f = lb + (1.0 - lb) * jax.nn.sigmoid(fz)
        return heads(jnp.log(jnp.maximum(f, HG_F_MIN))), heads((1.0 - lb) * jax.nn.sigmoid(-fz))

    lf_f, k_f = forget(f_f, lb_f)
    lf_b, k_b = forget(f_b, lb_b)
    ih = heads(i)
    qh = None if q is None else heads(jax.nn.silu(q))
    o_f, s_f = _gla_scan(qh, k_f, ih, lf_f, states[0])
    o_b, s_b = _gla_scan(_flip(qh), _flip(k_b), _flip(ih), _flip(lf_b), states[1])
    if q is None:
        return None, (s_f, s_b)
    o = _rmsnorm(o_f + _flip(o_b), norm_w) * jax.nn.sigmoid(heads(g).astype(jnp.float32))
    return o.reshape(bsz, L, HG_WIDTH).astype(i.dtype), (s_f, s_b)


def _gated_deltanet(q_raw, k_raw, v_raw, z, a_all, b_all, conv_q, conv_k, conv_v,
                    a_log, dt_bias, norm_w, states, is_grid):
    bsz, L = k_raw.shape[:2]
    rep = DN_V_HEADS // DN_QK_HEADS
    k = _l2norm(jax.nn.silu(_short_conv(k_raw, conv_k, is_grid)).reshape(bsz, L, DN_QK_HEADS, DN_DK))
    k = jnp.repeat(k, rep, axis=2)
    v = jax.nn.silu(_short_conv(v_raw, conv_v, is_grid)).reshape(bsz, L, DN_V_HEADS, DN_DV)
    a_f, a_b = jnp.split(a_all.astype(jnp.float32), 2, axis=-1)
    b_f, b_b = jnp.split(b_all.astype(jnp.float32), 2, axis=-1)
    g_f = -jnp.exp(a_log[0]) * jax.nn.softplus(a_f + dt_bias[0])
    g_b = -jnp.exp(a_log[1]) * jax.nn.softplus(a_b + dt_bias[1])
    q = None
    if q_raw is not None:
        q = _l2norm(jax.nn.silu(_short_conv(q_raw, conv_q, is_grid)).reshape(bsz, L, DN_QK_HEADS, DN_DK))
        q = jnp.repeat(q * DN_DK ** -0.5, rep, axis=2)
    o_f, s_f = _delta_scan(q, k, v, g_f, jax.nn.sigmoid(b_f), states[0])
    o_b, s_b = _delta_scan(_flip(q), _flip(k), _flip(v), _flip(g_b), _flip(jax.nn.sigmoid(b_b)), states[1])
    if q is None:
        return None, (s_f, s_b)
    zh = z.reshape(bsz, L, DN_V_HEADS, DN_DV).astype(jnp.float32)
    o = _rmsnorm(o_f + _flip(o_b), norm_w) * jax.nn.silu(zh)
    return o.reshape(bsz, L, DN_WIDTH).astype(k_raw.dtype), (s_f, s_b)


def _token_mixers(u, lp, init_states, is_grid, with_output):
    bsz, L, _ = u.shape
    if with_output:
        parts = _split(u @ lp["w_in"], IN_SIZES)
    else:
        parts = _split(u @ lp["w_in"][:, :STATE_COLS], STATE_SIZES) + [None] * (len(IN_SIZES) - len(STATE_SIZES))
    hg_ff, hg_fb, hg_i, dn_k, dn_v, dn_a, dn_b, hg_q, hg_g, dn_q, dn_z, hy_in, gate_in = parts
    hg_out, hg_states = _hgrn2(hg_q, hg_ff, hg_fb, hg_i, hg_g, lp["lb_fwd"], lp["lb_bwd"],
                               lp["hg_norm_w"], init_states[0])
    dn_out, dn_states = _gated_deltanet(dn_q, dn_k, dn_v, dn_z, dn_a, dn_b, lp["dn_conv_q"], lp["dn_conv_k"],
                                        lp["dn_conv_v"], lp["dn_a_log"], lp["dn_dt_bias"], lp["dn_norm_w"],
                                        init_states[1], is_grid)
    states = (hg_states, dn_states)
    if not with_output:
        return None, states
    hy_out = _hyena(hy_in, lp["hy_conv_w"], lp["hy_conv_b"], lp["hy_filt"], lp["hy_skip"], is_grid)
    branches = jnp.stack([hy_out, hg_out, dn_out], axis=2)
    proj = jnp.einsum('blgc,gcd->blgd', branches, lp["w_branch"])
    gate = jax.nn.sigmoid(gate_in.reshape(bsz, L, N_BRANCH, D_MODEL))
    y = jnp.sum(gate * proj, axis=2) @ lp["w_out"]
    return y, states


def _sq_relu_mlp(u, w1, w2):
    return jnp.square(jax.nn.relu(u @ w1)) @ w2


def setup_inputs(seed: int = 0) -> dict:
    key = jax.random.key(seed)
    ks = jax.random.split(key, 40)
    f32 = jnp.float32
    nrm = lambda k, shape, s: s * jax.random.normal(k, shape, f32)
    dt = jnp.exp(jax.random.uniform(ks[22], (DEPTH, 2, DN_V_HEADS), f32, math.log(1e-3), math.log(1e-1)))
    return {
        "x": nrm(ks[0], (BATCH, SEQ, D_MODEL), 1.0),
        "c": nrm(ks[1], (BATCH, D_MODEL), 1.0),
        "ctx": nrm(ks[2], (BATCH, CTX_LEN, D_MODEL), 1.0),
        "c_ctx": nrm(ks[3], (D_MODEL,), 1.0),
        "w_ada": nrm(ks[4], (DEPTH, D_MODEL, 6 * D_MODEL), 0.5 * D_MODEL ** -0.5),
        "b_ada": nrm(ks[5], (DEPTH, 6 * D_MODEL), 0.02),
        "w_in": nrm(ks[6], (DEPTH, D_MODEL, IN_COLS), D_MODEL ** -0.5),
        "hy_conv_w": nrm(ks[7], (DEPTH, HY_CONV, (HY_ORDER + 1) * HY_WIDTH), HY_CONV ** -0.5),
        "hy_conv_b": nrm(ks[8], (DEPTH, (HY_ORDER + 1) * HY_WIDTH), 0.02),
        "hy_filt_w1": nrm(ks[9], (DEPTH, HY_EMB, HY_FILTER_HIDDEN), HY_EMB ** -0.5),
        "hy_filt_b1": nrm(ks[10], (DEPTH, HY_FILTER_HIDDEN), 0.02),
        "hy_filt_w2": nrm(ks[11], (DEPTH, HY_FILTER_HIDDEN, HY_FILTER_HIDDEN), HY_FILTER_HIDDEN ** -0.5),
        "hy_filt_b2": nrm(ks[12], (DEPTH, HY_FILTER_HIDDEN), 0.02),
        "hy_filt_w3": nrm(ks[13], (DEPTH, HY_FILTER_HIDDEN, HY_ORDER * 2 * HY_WIDTH), 0.1 * HY_FILTER_HIDDEN ** -0.5),
        "hy_skip": nrm(ks[14], (DEPTH, HY_ORDER, HY_WIDTH), 1.0),
        "hg_lb_logits": nrm(ks[15], (2, DEPTH, HG_FWIDTH), 0.1),
        "hg_norm_w": 1.0 + nrm(ks[16], (DEPTH, HG_DV), 0.02),
        "dn_conv_q": nrm(ks[17], (DEPTH, DN_CONV, DN_QK_WIDTH), DN_CONV ** -0.5),
        "dn_conv_k": nrm(ks[18], (DEPTH, DN_CONV, DN_QK_WIDTH), DN_CONV ** -0.5),
        "dn_conv_v": nrm(ks[19], (DEPTH, DN_CONV, DN_WIDTH), DN_CONV ** -0.5),
        "dn_a_log": jnp.log(jax.random.uniform(ks[20], (DEPTH, 2, DN_V_HEADS), f32, 1.0, 16.0)),
        "dn_dt_bias": dt + jnp.log(-jnp.expm1(-dt)),
        "dn_norm_w": 1.0 + nrm(ks[21], (DEPTH, DN_DV), 0.02),
        "w_branch": nrm(ks[23], (DEPTH, N_BRANCH, BRANCH_WIDTH, D_MODEL), BRANCH_WIDTH ** -0.5),
        "w_out": nrm(ks[24], (DEPTH, D_MODEL, D_MODEL), BETA_INIT * D_MODEL ** -0.5),
        "ln1_g": 1.0 + nrm(ks[25], (DEPTH, D_MODEL), 0.02),
        "ln1_b": nrm(ks[26], (DEPTH, D_MODEL), 0.02),
        "w_ff1": nrm(ks[27], (DEPTH, D_MODEL, D_FF), D_MODEL ** -0.5),
        "w_ff2": nrm(ks[28], (DEPTH, D_FF, D_MODEL), BETA_INIT * D_FF ** -0.5),
        "ln2_g": 1.0 + nrm(ks[29], (DEPTH, D_MODEL), 0.02),
        "ln2_b": nrm(ks[30], (DEPTH, D_MODEL), 0.02),
    }


def reference(x, c, ctx, c_ctx, w_ada, b_ada, w_in, hy_conv_w, hy_conv_b, hy_filt_w1, hy_filt_b1,
              hy_filt_w2, hy_filt_b2, hy_filt_w3, hy_skip, hg_lb_logits, hg_norm_w, dn_conv_q, dn_conv_k,
              dn_conv_v, dn_a_log, dn_dt_bias, dn_norm_w, w_branch, w_out, ln1_g, ln1_b, w_ff1, w_ff2,
              ln2_g, ln2_b):
    f32 = jnp.float32
    bsz = x.shape[0]
    p = jax.nn.softmax(hg_lb_logits.astype(f32), axis=1)
    lower_bounds = jnp.cumsum(p, axis=1) - p[:, :1]
    h_ctx = ctx
    for l in range(DEPTH):
        last = l == DEPTH - 1
        lp = {
            "w_in": w_in[l], "hy_conv_w": hy_conv_w[l], "hy_conv_b": hy_conv_b[l],
            "hy_filt": (hy_filt_w1[l], hy_filt_b1[l], hy_filt_w2[l], hy_filt_b2[l], hy_filt_w3[l]),
            "hy_skip": hy_skip[l], "lb_fwd": lower_bounds[0, l], "lb_bwd": lower_bounds[1, l],
            "hg_norm_w": hg_norm_w[l], "dn_conv_q": dn_conv_q[l], "dn_conv_k": dn_conv_k[l],
            "dn_conv_v": dn_conv_v[l], "dn_a_log": dn_a_log[l], "dn_dt_bias": dn_dt_bias[l],
            "dn_norm_w": dn_norm_w[l], "w_branch": w_branch[l], "w_out": w_out[l],
        }
        zero_hg = jnp.zeros((bsz, HG_HEADS, HG_DK, HG_DV), f32)
        zero_dn = jnp.zeros((bsz, DN_V_HEADS, DN_DK, DN_DV), f32)
        init = ((zero_hg, zero_hg), (zero_dn, zero_dn))
        n_mod = 2 if last else 6
        mod_c = jax.nn.silu(c_ctx) @ w_ada[l][:, :n_mod * D_MODEL] + b_ada[l][:n_mod * D_MODEL]
        mods_c = jnp.split(mod_c, n_mod)
        y_ctx, ctx_states = _token_mixers(_modulate(h_ctx, mods_c[0], mods_c[1]), lp, init, False, not last)
        if not last:
            h_ctx = _layernorm(ALPHA * h_ctx + mods_c[2] * y_ctx, ln1_g[l], ln1_b[l])
            h_ctx = _layernorm(ALPHA * h_ctx + mods_c[5] * _sq_relu_mlp(_modulate(h_ctx, mods_c[3], mods_c[4]),
                                                                          w_ff1[l], w_ff2[l]), ln2_g[l], ln2_b[l])
        mod = jax.nn.silu(c) @ w_ada[l] + b_ada[l]
        sh1, sc1, g1, sh2, sc2, g2 = [m[:, None, :] for m in jnp.split(mod, 6, axis=-1)]
        y, _ = _token_mixers(_modulate(x, sh1, sc1), lp, ctx_states, True, True)
        x = _layernorm(ALPHA * x + g1 * y, ln1_g[l], ln1_b[l])
        x = _layernorm(ALPHA * x + g2 * _sq_relu_mlp(_modulate(x, sh2, sc2), w_ff1[l], w_ff2[l]),
                       ln2_g[l], ln2_b[l])
    return x
```

```python
import functools
import math

import numpy as np
import jax
import jax.numpy as jnp
from jax import lax
from jax.experimental import pallas as pl
from jax.experimental.pallas import tpu as pltpu

F32 = jnp.float32
BF16 = jnp.bfloat16

D_MODEL = 2048
GRID_W = 64
BRANCH_WIDTH = 1024
N_BRANCH = 3
HY_ORDER = 2
HY_EMB = 33
HY_BANDS = (HY_EMB - 1) // 2
HY_HIDDEN = 64
HY_DECAY_TARGET = 1e-2
HY_FAST_DECAY = 0.3
HY_SLOW_DECAY = 1.5
HEADS = 8
HEAD_DIM = 128
DN_QK_HEADS = 4
HG_F_MIN = 1e-30
CHUNK = 64
LN_EPS = 1e-5
RMS_EPS = 1e-6
LANE = 128

COL_GATE = 0
COL_HY = 6144
COL_FF = 9216
COL_FB = 10240
COL_HI = 11264
COL_DK = 12288
COL_DV = 12800
COL_AB = 13824
COL_HQ = 13952
COL_HGATE = 14976
COL_DQ = 16000
COL_DZ = 16512
COL_END = 17536
IN_TN = 896
COL_PAD = 17920
STATE_PAD = 5376

VMEM_LIMIT = 56 * 1024 * 1024


def _cparams(sem):
    return pltpu.CompilerParams(dimension_semantics=sem, vmem_limit_bytes=VMEM_LIMIT)


def _dot(a, b):
    return jnp.dot(a.astype(BF16), b.astype(BF16), preferred_element_type=F32)


def _dot_nt(a, b):
    return lax.dot_general(a.astype(BF16), b.astype(BF16), (((1,), (1,)), ((), ())),
                           preferred_element_type=F32)


def _dot_tn(a, b):
    return lax.dot_general(a.astype(BF16), b.astype(BF16), (((0,), (0,)), ((), ())),
                           preferred_element_type=F32)


def _sigmoid(x):
    return 1.0 / (1.0 + jnp.exp(-x))


def _silu(x):
    return x * _sigmoid(x)


def _ada_kernel(c_ref, w_ref, b_ref, o_ref):
    a = _silu(c_ref[...])
    o_ref[...] = _dot(a, w_ref[...]) + b_ref[...]


def _ada_mod(cs, w, b):
    d, n = w.shape
    tn = 1024
    return pl.pallas_call(
        _ada_kernel,
        grid=(n // tn,),
        in_specs=[pl.BlockSpec((8, d), lambda j: (0, 0)),
                  pl.BlockSpec((d, tn), lambda j: (0, j)),
                  pl.BlockSpec((1, tn), lambda j: (0, j))],
        out_specs=pl.BlockSpec((8, tn), lambda j: (0, j)),
        out_shape=jax.ShapeDtypeStruct((8, n), F32),
        compiler_params=_cparams(("arbitrary",)),
    )(cs, w, b)


def _mm_mod_kernel(x_ref, sh_ref, sc_ref, w_ref, o_ref, a_scr, *, relu2):
    @pl.when(pl.program_id(2) == 0)
    def _():
        a_scr[...] = (x_ref[...] * (1.0 + sc_ref[...]) + sh_ref[...]).astype(BF16)

    acc = jnp.dot(a_scr[...], w_ref[...], preferred_element_type=F32)
    if relu2:
        acc = jnp.square(jnp.maximum(acc, 0.0))
    o_ref[...] = acc.astype(o_ref.dtype)


def _mm_mod(x, sh, sc, w, *, tm, tn, relu2, out_dtype):
    bsz, l, d = x.shape
    n = w.shape[1]
    return pl.pallas_call(
        functools.partial(_mm_mod_kernel, relu2=relu2),
        grid=(bsz, l // tm, n // tn),
        in_specs=[pl.BlockSpec((None, tm, d), lambda b, i, j: (b, i, 0)),
                  pl.BlockSpec((None, 1, d), lambda b, i, j: (b, 0, 0)),
                  pl.BlockSpec((None, 1, d), lambda b, i, j: (b, 0, 0)),
                  pl.BlockSpec((d, tn), lambda b, i, j: (0, j))],
        out_specs=pl.BlockSpec((None, tm, tn), lambda b, i, j: (b, i, j)),
        out_shape=jax.ShapeDtypeStruct((bsz, l, n), out_dtype),
        scratch_shapes=[pltpu.VMEM((tm, d), BF16)],
        compiler_params=_cparams(("parallel", "parallel", "arbitrary")),
    )(x, sh, sc, w)


def _mm_ln_kernel(a_ref, w_ref, x_ref, gate_ref, g_ref, b_ref, o_ref, acc_ref, *, nk, alpha):
    k = pl.program_id(2)

    @pl.when(k == 0)
    def _():
        acc_ref[...] = jnp.zeros_like(acc_ref)

    acc_ref[...] += jnp.dot(a_ref[...], w_ref[...], preferred_element_type=F32)

    @pl.when(k == nk - 1)
    def _():
        y = alpha * x_ref[...] + gate_ref[...] * acc_ref[...]
        mu = jnp.mean(y, axis=-1, keepdims=True)
        yc = y - mu
        var = jnp.mean(yc * yc, axis=-1, keepdims=True)
        o_ref[...] = yc * lax.rsqrt(var + LN_EPS) * g_ref[...] + b_ref[...]


def _mm_ln(a, w, x, gate, ln_g, ln_b, *, tm, tk, alpha):
    bsz, l, kdim = a.shape
    d = w.shape[1]
    nk = kdim // tk
    return pl.pallas_call(
        functools.partial(_mm_ln_kernel, nk=nk, alpha=alpha),
        grid=(bsz, l // tm, nk),
        in_specs=[pl.BlockSpec((None, tm, tk), lambda b, i, k: (b, i, k)),
                  pl.BlockSpec((tk, d), lambda b, i, k: (k, 0)),
                  pl.BlockSpec((None, tm, d), lambda b, i, k: (b, i, 0)),
                  pl.BlockSpec((None, 1, d), lambda b, i, k: (b, 0, 0)),
                  pl.BlockSpec((1, d), lambda b, i, k: (0, 0)),
                  pl.BlockSpec((1, d), lambda b, i, k: (0, 0))],
        out_specs=pl.BlockSpec((None, tm, d), lambda b, i, k: (b, i, 0)),
        out_shape=jax.ShapeDtypeStruct((bsz, l, d), F32),
        scratch_shapes=[pltpu.VMEM((tm, d), F32)],
        compiler_params=_cparams(("parallel", "parallel", "arbitrary")),
    )(a, w, x, gate, ln_g, ln_b)


def _branch_kernel(b0_ref, b1_ref, b2_ref, w_ref, g0_ref, g1_ref, g2_ref, o_ref):
    acc = _sigmoid(g0_ref[...]) * jnp.dot(b0_ref[...], w_ref[0], preferred_element_type=F32)
    acc += _sigmoid(g1_ref[...]) * jnp.dot(b1_ref[...], w_ref[1], preferred_element_type=F32)
    acc += _sigmoid(g2_ref[...]) * jnp.dot(b2_ref[...], w_ref[2], preferred_element_type=F32)
    o_ref[...] = acc.astype(o_ref.dtype)


def _branch_merge(branches, w, h, *, tm, tn):
    bsz, l, c = branches[0].shape
    d = w.shape[2]
    nj = d // tn
    br_spec = pl.BlockSpec((None, tm, c), lambda b, i, j: (b, i, 0))

    def gate_spec(g):
        return pl.BlockSpec((None, tm, tn), lambda b, i, j: (b, i, COL_GATE // tn + g * nj + j))

    return pl.pallas_call(
        _branch_kernel,
        grid=(bsz, l // tm, nj),
        in_specs=[br_spec, br_spec, br_spec,
                  pl.BlockSpec((N_BRANCH, c, tn), lambda b, i, j: (0, 0, j)),
                  gate_spec(0), gate_spec(1), gate_spec(2)],
        out_specs=pl.BlockSpec((None, tm, tn), lambda b, i, j: (b, i, j)),
        out_shape=jax.ShapeDtypeStruct((bsz, l, d), BF16),
        compiler_params=_cparams(("parallel", "parallel", "arbitrary")),
    )(*branches, w, h, h, h)


def _iota2(shape, axis):
    return lax.broadcasted_iota(jnp.int32, shape, axis)


def _split3(g):
    g1 = g.astype(BF16)
    r = g - g1.astype(F32)
    g2 = r.astype(BF16)
    g3 = (r - g2.astype(F32)).astype(BF16)
    return g1, g2, g3


def _cumsum_rows(g, rev):
    row = _iota2((CHUNK, 3 * CHUNK), 0)
    col = jnp.bitwise_and(_iota2((CHUNK, 3 * CHUNK), 1), CHUNK - 1)
    tri = jnp.where((col >= row) if rev else (col <= row), 1.0, 0.0).astype(BF16)
    g3 = jnp.concatenate(_split3(g), axis=0)
    return jnp.dot(tri, g3, preferred_element_type=F32)


def _hg_chunk(fz, v, q, lb, st, rev):
    sg = _sigmoid(fz)
    f = lb + (1.0 - lb) * sg
    g = jnp.log(jnp.maximum(f, HG_F_MIN))
    kk = (1.0 - lb) * _sigmoid(-fz)
    bc = _cumsum_rows(g, rev)
    half = CHUNK // 2
    if rev:
        mid, last = bc[half:half + 1], bc[0:1]
    else:
        mid, last = bc[half - 1:half], bc[CHUNK - 1:CHUNK]
    ks = kk * jnp.exp(last - bc)
    st_new = st * jnp.exp(last) + _dot_tn(v, ks)
    if q is None:
        return None, st_new
    qs = _silu(q)
    qd = qs * jnp.exp(bc - mid)
    kd = kk * jnp.exp(mid - bc)
    att = _dot_nt(qd, kd)
    r = _iota2((CHUNK, CHUNK), 0)
    c = _iota2((CHUNK, CHUNK), 1)
    att = jnp.where((c >= r) if rev else (c <= r), att, 0.0)
    o = _dot(att, v) + _dot_nt(qs * jnp.exp(bc), st)
    return o, st_new


def _hg_kernel(*refs, nchunk, with_output):
    if with_output:
        (ff_ref, vf_ref, qf_ref, fb_ref, vb_ref, qb_ref, lbf_ref, lbb_ref, s0f_ref, s0b_ref,
         of_ref, ob_ref, sf_ref, sb_ref, stf, stb) = refs
    else:
        (ff_ref, vf_ref, fb_ref, vb_ref, lbf_ref, lbb_ref, s0f_ref, s0b_ref,
         sf_ref, sb_ref, stf, stb) = refs
        qf_ref = qb_ref = of_ref = ob_ref = None

    @pl.when(pl.program_id(2) == 0)
    def _():
        stf[...] = s0f_ref[...]
        stb[...] = s0b_ref[...]

    s_f = stf[...]
    s_b = stb[...]
    lbf = lbf_ref[...]
    lbb = lbb_ref[...]
    for ci in range(nchunk):
        sl = pl.ds(ci * CHUNK, CHUNK)
        o, s_f = _hg_chunk(ff_ref[sl, :], vf_ref[sl, :], qf_ref[sl, :] if with_output else None,
                           lbf, s_f, False)
        if with_output:
            of_ref[sl, :] = o
        sl = pl.ds((nchunk - 1 - ci) * CHUNK, CHUNK)
        o, s_b = _hg_chunk(fb_ref[sl, :], vb_ref[sl, :], qb_ref[sl, :] if with_output else None,
                           lbb, s_b, True)
        if with_output:
            ob_ref[sl, :] = o
    stf[...] = s_f
    stb[...] = s_b
    sf_ref[...] = s_f
    sb_ref[...] = s_b


def _hgrn2_scan(h, cols, lb_f, lb_b, s0, *, tblk, with_output):
    bsz, l, _ = h.shape
    nt = l // tblk
    c_ff, c_fb, c_i, c_q = (c // LANE for c in cols)

    def fwd(cb):
        return pl.BlockSpec((None, tblk, LANE), lambda b, hh, t: (b, t, cb + hh))

    def bwd(cb):
        return pl.BlockSpec((None, tblk, LANE), lambda b, hh, t: (b, nt - 1 - t, cb + hh))

    lb_spec = pl.BlockSpec((1, LANE), lambda b, hh, t: (0, hh))
    st_spec = pl.BlockSpec((None, None, HEAD_DIM, HEAD_DIM), lambda b, hh, t: (b, hh, 0, 0))
    if with_output:
        in_specs = [fwd(c_ff), fwd(c_i), fwd(c_q), bwd(c_fb), bwd(c_i), bwd(c_q)]
        args = [h] * 6
    else:
        in_specs = [fwd(c_ff), fwd(c_i), bwd(c_fb), bwd(c_i)]
        args = [h] * 4
    in_specs += [lb_spec, lb_spec, st_spec, st_spec]
    args += [lb_f, lb_b, s0[0], s0[1]]
    st_shape = jax.ShapeDtypeStruct((bsz, HEADS, HEAD_DIM, HEAD_DIM), F32)
    out_specs = [st_spec, st_spec]
    out_shape = [st_shape, st_shape]
    if with_output:
        o_shape = jax.ShapeDtypeStruct((bsz, l, HEADS * HEAD_DIM), F32)
        out_specs = [pl.BlockSpec((None, tblk, LANE), lambda b, hh, t: (b, t, hh)),
                     pl.BlockSpec((None, tblk, LANE), lambda b, hh, t: (b, nt - 1 - t, hh))] + out_specs
        out_shape = [o_shape, o_shape] + out_shape
    outs = pl.pallas_call(
        functools.partial(_hg_kernel, nchunk=tblk // CHUNK, with_output=with_output),
        grid=(bsz, HEADS, nt),
        in_specs=in_specs,
        out_specs=out_specs,
        out_shape=out_shape,
        scratch_shapes=[pltpu.VMEM((HEAD_DIM, HEAD_DIM), F32), pltpu.VMEM((HEAD_DIM, HEAD_DIM), F32)],
        compiler_params=_cparams(("parallel", "parallel", "arbitrary")),
    )(*args)
    if with_output:
        return outs[0], outs[1], (outs[2], outs[3])
    return None, None, (outs[0], outs[1])


def _conv3(x, w, is_grid):
    t = x.shape[0]
    row = _iota2((t, 1), 0)
    if is_grid:
        pos = jnp.bitwise_and(row, GRID_W - 1)
        has_l, has_r = pos != 0, pos != GRID_W - 1
    else:
        has_l, has_r = row != 0, row != t - 1
    xl = jnp.where(has_l, pltpu.roll(x, 1, 0), 0.0)
    xr = jnp.where(has_r, pltpu.roll(x, t - 1, 0), 0.0)
    return xl * w[0:1] + x * w[1:2] + xr * w[2:3]


def _l2norm(t):
    return t * lax.rsqrt(jnp.sum(t * t, axis=-1, keepdims=True) + 1e-6)


def _softplus(x):
    return jnp.maximum(x, 0.0) + jnp.log(1.0 + jnp.exp(-jnp.abs(x)))


def _dn_chunk(k, v, q, g_row, beta_row, s, rev):
    r = _iota2((CHUNK, CHUNK), 0)
    c = _iota2((CHUNK, CHUNK), 1)
    eye = r == c
    incl = (c >= r) if rev else (c <= r)
    strict = (c > r) if rev else (c < r)
    incl_t = (r >= c) if rev else (r <= c)
    g_b = jnp.broadcast_to(g_row, (CHUNK, CHUNK))
    gc_col = jnp.sum(jnp.where(incl, g_b, 0.0), axis=1, keepdims=True)
    g_col = jnp.sum(jnp.where(eye, g_b, 0.0), axis=1, keepdims=True)
    gc_row = jnp.sum(jnp.where(incl_t, jnp.broadcast_to(g_col, (CHUNK, CHUNK)), 0.0), axis=0, keepdims=True)
    beta_col = jnp.sum(jnp.where(eye, jnp.broadcast_to(beta_row, (CHUNK, CHUNK)), 0.0), axis=1, keepdims=True)
    decay = jnp.where(incl, jnp.exp(jnp.where(incl, gc_col - gc_row, 0.0)), 0.0)
    kb = k * beta_col
    n = jnp.where(strict, _dot_nt(kb, k) * decay, 0.0)
    p = jnp.where(eye, 1.0, 0.0) - n
    m = _dot(n, n)
    for i in range(5):
        p = p + _dot(p, m)
        if i < 4:
            m = _dot(m, m)
    egc = jnp.exp(gc_col)
    sol = _dot(p, jnp.concatenate([v * beta_col, kb * egc], axis=1))
    v_new = sol[:, :HEAD_DIM] - _dot(sol[:, HEAD_DIM:], s)
    g_last = gc_col[0:1] if rev else gc_col[CHUNK - 1:CHUNK]
    s_new = s * jnp.exp(g_last) + _dot_tn(k * jnp.exp(g_last - gc_col), v_new)
    if q is None:
        return None, s_new
    o = _dot(q * egc, s) + _dot(_dot_nt(q, k) * decay, v_new)
    return o, s_new


def _dn_kernel(*refs, nchunk, with_output, is_grid):
    if with_output:
        (alog_ref, dtb_ref, kf_ref, vf_ref, qf_ref, abf_ref, kb_ref, vb_ref, qb_ref, abb_ref,
         ck_ref, cv_ref, cq_ref, s0f_ref, s0b_ref, of_ref, ob_ref, sf_ref, sb_ref, stf, stb) = refs
    else:
        (alog_ref, dtb_ref, kf_ref, vf_ref, abf_ref, kb_ref, vb_ref, abb_ref,
         ck_ref, cv_ref, s0f_ref, s0b_ref, sf_ref, sb_ref, stf, stb) = refs
        qf_ref = qb_ref = cq_ref = of_ref = ob_ref = None
    hh = pl.program_id(1)

    @pl.when(pl.program_id(2) == 0)
    def _():
        stf[...] = s0f_ref[...]
        stb[...] = s0b_ref[...]

    def prep(k_ref, v_ref, q_ref, ab_ref, d):
        k = _l2norm(_silu(_conv3(k_ref[...], ck_ref[...], is_grid)))
        v = _silu(_conv3(v_ref[...], cv_ref[...], is_grid))
        q = None
        if with_output:
            q = _l2norm(_silu(_conv3(q_ref[...], cq_ref[...], is_grid))) * (HEAD_DIM ** -0.5)
        a = ab_ref[pl.ds(d * HEADS + hh, 1), :]
        bt = ab_ref[pl.ds(2 * HEADS + d * HEADS + hh, 1), :]
        g = -jnp.exp(alog_ref[d, hh]) * _softplus(a + dtb_ref[d, hh])
        return k, v, q, g, _sigmoid(bt)

    kf, vf, qf, gf, bf = prep(kf_ref, vf_ref, qf_ref, abf_ref, 0)
    kb, vb, qb, gb, bb = prep(kb_ref, vb_ref, qb_ref, abb_ref, 1)
    s_f = stf[...]
    s_b = stb[...]
    for ci in range(nchunk):
        lo = ci * CHUNK
        o, s_f = _dn_chunk(kf[lo:lo + CHUNK], vf[lo:lo + CHUNK], qf[lo:lo + CHUNK] if with_output else None,
                           gf[:, lo:lo + CHUNK], bf[:, lo:lo + CHUNK], s_f, False)
        if with_output:
            of_ref[pl.ds(lo, CHUNK), :] = o
        lo = (nchunk - 1 - ci) * CHUNK
        o, s_b = _dn_chunk(kb[lo:lo + CHUNK], vb[lo:lo + CHUNK], qb[lo:lo + CHUNK] if with_output else None,
                           gb[:, lo:lo + CHUNK], bb[:, lo:lo + CHUNK], s_b, True)
        if with_output:
            ob_ref[pl.ds(lo, CHUNK), :] = o
    stf[...] = s_f
    stb[...] = s_b
    sf_ref[...] = s_f
    sb_ref[...] = s_b


def _deltanet_scan(h, ab_t, cols, conv_k, conv_v, conv_q, a_log, dt_bias, s0, *, tblk, with_output, is_grid):
    bsz, l, _ = h.shape
    nt = l // tblk
    if not is_grid:
        assert nt == 1
    c_k, c_v, c_q = (c // LANE for c in cols)
    rep = HEADS // DN_QK_HEADS

    def tok(cb, shared, rev):
        def imap(b, hh, t):
            return (b, (nt - 1 - t) if rev else t, cb + (hh // rep if shared else hh))
        return pl.BlockSpec((None, tblk, LANE), imap)

    def ab_spec(rev):
        return pl.BlockSpec((None, 4 * HEADS, tblk), lambda b, hh, t: (b, 0, (nt - 1 - t) if rev else t))

    smem = pl.BlockSpec(memory_space=pltpu.SMEM)
    ck_spec = pl.BlockSpec((3, LANE), lambda b, hh, t: (0, hh // rep))
    cv_spec = pl.BlockSpec((3, LANE), lambda b, hh, t: (0, hh))
    st_spec = pl.BlockSpec((None, None, HEAD_DIM, HEAD_DIM), lambda b, hh, t: (b, hh, 0, 0))
    if with_output:
        in_specs = [smem, smem, tok(c_k, True, False), tok(c_v, False, False), tok(c_q, True, False), ab_spec(False),
                    tok(c_k, True, True), tok(c_v, False, True), tok(c_q, True, True), ab_spec(True),
                    ck_spec, cv_spec, ck_spec, st_spec, st_spec]
        args = [a_log, dt_bias, h, h, h, ab_t, h, h, h, ab_t, conv_k, conv_v, conv_q, s0[0], s0[1]]
    else:
        in_specs = [smem, smem, tok(c_k, True, False), tok(c_v, False, False), ab_spec(False),
                    tok(c_k, True, True), tok(c_v, False, True), ab_spec(True),
                    ck_spec, cv_spec, st_spec, st_spec]
        args = [a_log, dt_bias, h, h, ab_t, h, h, ab_t, conv_k, conv_v, s0[0], s0[1]]
    st_shape = jax.ShapeDtypeStruct((bsz, HEADS, HEAD_DIM, HEAD_DIM), F32)
    out_specs = [st_spec, st_spec]
    out_shape = [st_shape, st_shape]
    if with_output:
        o_shape = jax.ShapeDtypeStruct((bsz, l, HEADS * HEAD_DIM), F32)
        out_specs = [pl.BlockSpec((None, tblk, LANE), lambda b, hh, t: (b, t, hh)),
                     pl.BlockSpec((None, tblk, LANE), lambda b, hh, t: (b, nt - 1 - t, hh))] + out_specs
        out_shape = [o_shape, o_shape] + out_shape
    outs = pl.pallas_call(
        functools.partial(_dn_kernel, nchunk=tblk // CHUNK, with_output=with_output, is_grid=is_grid),
        grid=(bsz, HEADS, nt),
        in_specs=in_specs,
        out_specs=out_specs,
        out_shape=out_shape,
        scratch_shapes=[pltpu.VMEM((HEAD_DIM, HEAD_DIM), F32), pltpu.VMEM((HEAD_DIM, HEAD_DIM), F32)],
        compiler_params=_cparams(("parallel", "parallel", "arbitrary")),
    )(*args)
    if with_output:
        return outs[0], outs[1], (outs[2], outs[3])
    return None, None, (outs[0], outs[1])


def _combine_kernel(of_ref, ob_ref, z_ref, w_ref, o_ref, *, use_silu):
    o = of_ref[...] + ob_ref[...]
    o = o * lax.rsqrt(jnp.mean(o * o, axis=-1, keepdims=True) + RMS_EPS) * w_ref[...]
    z = z_ref[...]
    o_ref[...] = (o * (_silu(z) if use_silu else _sigmoid(z))).astype(o_ref.dtype)


def _combine(o_f, o_b, h, col_z, norm_w, *, tl, use_silu):
    bsz, l, c = o_f.shape
    cz = col_z // LANE
    spec = pl.BlockSpec((None, tl, LANE), lambda b, i, hh: (b, i, hh))
    return pl.pallas_call(
        functools.partial(_combine_kernel, use_silu=use_silu),
        grid=(bsz, l // tl, c // LANE),
        in_specs=[spec, spec,
                  pl.BlockSpec((None, tl, LANE), lambda b, i, hh: (b, i, cz + hh)),
                  pl.BlockSpec((1, LANE), lambda b, i, hh: (0, 0))],
        out_specs=spec,
        out_shape=jax.ShapeDtypeStruct((bsz, l, c), BF16),
        compiler_params=_cparams(("parallel", "parallel", "arbitrary")),
    )(o_f, o_b, h, norm_w)


def _dot_f32(a, b):
    return jnp.dot(a, b, precision=lax.Precision.HIGHEST, preferred_element_type=F32)


def _hy_filt_kernel(w1_ref, b1_ref, w2_ref, b2_ref, w3_ref, o_ref, *, l_total, tl):
    i = pl.program_id(0)
    j = pl.program_id(1)
    row = (_iota2((tl, LANE), 0) + i * tl).astype(F32)
    lane = _iota2((tl, LANE), 1)
    t = row * (1.0 / (l_total - 1))
    ang = row * (2.0 * math.pi / l_total)
    bidx = jnp.where(lane <= HY_BANDS, lane - 1, lane - 1 - HY_BANDS).astype(F32)
    band = 1e-4 + bidx * ((HY_BANDS - 1 - 1e-4) / (HY_BANDS - 1))
    arg = band * ang
    feats = jnp.where(lane == 0, t,
                      jnp.where(lane <= HY_BANDS, jnp.cos(arg),
                                jnp.where(lane <= 2 * HY_BANDS, -jnp.sin(arg), 0.0)))
    hid = jnp.sin(_dot_f32(feats, w1_ref[...]) + b1_ref[...])
    hid = jnp.sin(_dot_f32(hid, w2_ref[...]) + b2_ref[...])
    hcol = _dot_f32(hid, w3_ref[...])
    width = hcol.shape[1]
    ch = _iota2((tl, width), 1).astype(F32)
    lo = math.log(HY_DECAY_TARGET) / HY_SLOW_DECAY
    hi = math.log(HY_DECAY_TARGET) / HY_FAST_DECAY
    delta = jnp.abs(lo + ch * ((hi - lo) / (width - 1)))
    trow = (_iota2((tl, width), 0) + i * tl)
    out = hcol * jnp.exp(-(trow.astype(F32) * (1.0 / (l_total - 1))) * delta)
    drop = jnp.logical_and(trow == 0, jnp.bitwise_and(j, 1) == 1)
    o_ref[...] = jnp.where(drop, 0.0, out)


def _hy_filters(l_total, w1p, b1, w2, b2, w3):
    tl = min(l_total, 512)
    n = w3.shape[1]
    full = lambda a: pl.BlockSpec(a.shape, lambda i, j: (0,) * a.ndim)
    return pl.pallas_call(
        functools.partial(_hy_filt_kernel, l_total=l_total, tl=tl),
        grid=(l_total // tl, n // BRANCH_WIDTH),
        in_specs=[full(w1p), full(b1), full(w2), full(b2),
                  pl.BlockSpec((HY_HIDDEN, BRANCH_WIDTH), lambda i, j: (0, j))],
        out_specs=pl.BlockSpec((tl, BRANCH_WIDTH), lambda i, j: (i, j)),
        out_shape=jax.ShapeDtypeStruct((l_total, n), F32),
        compiler_params=_cparams(("parallel", "arbitrary")),
    )(w1p, b1, w2, b2, w3)


@functools.lru_cache(maxsize=None)
def _fft_tables(n1, n2):
    n = n1 * n2
    n2h = n2 // 2
    k2 = np.arange(n2, dtype=np.int64)[:, None]
    m = np.arange(n2h, dtype=np.int64)[None, :]
    ta = np.zeros((n1, 2 * n2, 2 * n2h))
    tak = np.zeros((n1, 4 * n2, 2 * n2h))
    for a in range(n1):
        th = 2.0 * np.pi * ((k2 * (a + n1 * m)) % n) / n
        c, s = np.cos(th), np.sin(th)
        ta[a] = np.block([[c, s], [-s, c]])
        z = np.zeros_like(c)
        tak[a] = np.block([[c, z], [-s, z], [z, c], [z, -s]])
    tai = np.transpose(ta, (0, 2, 1)) / n
    k1 = np.arange(n1, dtype=np.int64)
    ph = 2.0 * np.pi * ((k1[:, None] * k1[None, :]) % n1) / n1
    c, s = np.cos(ph), np.sin(ph)
    fb = np.block([[c, s], [-s, c]])
    return tuple(np.asarray(t, np.float32) for t in (ta, tak, tai, fb, fb.T))


def _fft_split(l_total):
    n = 2 * l_total
    n1 = min(128, l_total // 8)
    return n1, n // n1


def _hy_stage_a_kernel(gr_ref, cr_ref, gi_ref, ci_ref, xr_ref, xi_ref, t_ref, o_ref, *, planes, n2):
    del gr_ref, cr_ref, gi_ref, ci_ref
    for j in range(8):
        x = jnp.concatenate([xr_ref[:, j, :], xi_ref[:, j, :]], axis=0).astype(BF16)
        y = jnp.dot(t_ref[j], x, preferred_element_type=F32)
        for p in range(planes):
            o_ref[p, :, j, :] = y[p * n2:(p + 1) * n2].astype(o_ref.dtype)


def _hy_stage_a(x4, pairs, tab, *, ct, planes, n2):
    _, n2h, n1, _ = x4.shape
    npair = len(pairs)
    nct = BRANCH_WIDTH // ct
    gr = jnp.asarray([p[0][0] for p in pairs], jnp.int32)
    cr = jnp.asarray([p[0][1] for p in pairs], jnp.int32)
    gi = jnp.asarray([p[1][0] for p in pairs], jnp.int32)
    ci = jnp.asarray([p[1][1] for p in pairs], jnp.int32)
    grid_spec = pltpu.PrefetchScalarGridSpec(
        num_scalar_prefetch=4,
        grid=(npair, nct, n1 // 8),
        in_specs=[pl.BlockSpec((None, n2h, 8, ct), lambda p, c, g, gr, cr, gi, ci: (gr[p], 0, g, cr[p] * nct + c)),
                  pl.BlockSpec((None, n2h, 8, ct), lambda p, c, g, gr, cr, gi, ci: (gi[p], 0, g, ci[p] * nct + c)),
                  pl.BlockSpec((8, planes * n2, 2 * n2h), lambda p, c, g, *_: (g, 0, 0))],
        out_specs=pl.BlockSpec((None, planes, n2, 8, ct), lambda p, c, g, *_: (p, 0, 0, g, c)),
    )
    return pl.pallas_call(
        functools.partial(_hy_stage_a_kernel, planes=planes, n2=n2),
        grid_spec=grid_spec,
        out_shape=jax.ShapeDtypeStruct((npair, planes, n2, n1, BRANCH_WIDTH), BF16),
        compiler_params=_cparams(("parallel", "parallel", "arbitrary")),
    )(gr, cr, gi, ci, x4, x4, tab)


def _hy_spec_kernel(c_ref, fb_ref, o_ref, *, kg, n1):
    for k in range(kg):
        zf = jnp.dot(fb_ref[...], jnp.concatenate([c_ref[0, k], c_ref[1, k]], axis=0), preferred_element_type=F32)
        zb = jnp.dot(fb_ref[...], jnp.concatenate([c_ref[2, k], c_ref[3, k]], axis=0), preferred_element_type=F32)
        o_ref[0, k] = zf[:n1] + zb[:n1]
        o_ref[1, k] = zf[n1:] - zb[n1:]


def _hy_spectrum(cs, fb, *, ct, kg):
    no, _, n2, n1, c = cs.shape
    return pl.pallas_call(
        functools.partial(_hy_spec_kernel, kg=kg, n1=n1),
        grid=(no, c // ct, n2 // kg),
        in_specs=[pl.BlockSpec((None, 4, kg, n1, ct), lambda o, j, g: (o, 0, g, 0, j)),
                  pl.BlockSpec(fb.shape, lambda o, j, g: (0, 0))],
        out_specs=pl.BlockSpec((None, 2, kg, n1, ct), lambda o, j, g: (o, 0, g, 0, j)),
        out_shape=jax.ShapeDtypeStruct((no, 2, n2, n1, c), F32),
        compiler_params=_cparams(("parallel", "parallel", "arbitrary")),
    )(cs, fb)


def _hy_stage_b_kernel(c_ref, k_ref, fb_ref, fbi_ref, o_ref, *, kg, n1):
    for k in range(kg):
        z = jnp.dot(fb_ref[...], jnp.concatenate([c_ref[0, k], c_ref[1, k]], axis=0), preferred_element_type=F32)
        zr, zi = z[:n1], z[n1:]
        kr, ki = k_ref[0, k], k_ref[1, k]
        y = jnp.concatenate([zr * kr - zi * ki, zr * ki + zi * kr], axis=0).astype(BF16)
        d = jnp.dot(fbi_ref[...], y, preferred_element_type=F32)
        o_ref[0, k] = d[:n1].astype(o_ref.dtype)
        o_ref[1, k] = d[n1:].astype(o_ref.dtype)


def _hy_stage_b(cs, spec, order, fb, fbi, *, ct, kg):
    _, _, n2, n1, c = cs.shape
    return pl.pallas_call(
        functools.partial(_hy_stage_b_kernel, kg=kg, n1=n1),
        grid=(c // ct, n2 // kg),
        in_specs=[pl.BlockSpec((None, 2, kg, n1, ct), lambda j, g: (0, 0, g, 0, j)),
                  pl.BlockSpec((None, 2, kg, n1, ct), lambda j, g: (order, 0, g, 0, j)),
                  pl.BlockSpec(fb.shape, lambda j, g: (0, 0)),
                  pl.BlockSpec(fbi.shape, lambda j, g: (0, 0))],
        out_specs=pl.BlockSpec((2, kg, n1, ct), lambda j, g: (0, g, 0, j)),
        out_shape=jax.ShapeDtypeStruct((2, n2, n1, c), BF16),
        compiler_params=_cparams(("parallel", "arbitrary")),
    )(cs, spec, fb, fbi)


def _hy_stage_c_kernel(d_ref, t_ref, u_ref, g_ref, skip_ref, o_ref, *, n2h):
    skip = skip_ref[...]
    for j in range(8):
        d = jnp.concatenate([d_ref[0, :, j, :], d_ref[1, :, j, :]], axis=0)
        y = jnp.dot(t_ref[j], d, preferred_element_type=F32)
        for b in range(2):
            yb = y[b * n2h:(b + 1) * n2h]
            o_ref[b, :, j, :] = (g_ref[b, :, j, :] * (yb + skip * u_ref[b, :, j, :])).astype(o_ref.dtype)


def _hy_stage_c(d, tai, u4, ucol, z4, gcol, skip, *, ct, out_dtype):
    _, n2, n1, c = d.shape
    n2h = n2 // 2
    uc, gc = ucol // ct, gcol // ct
    return pl.pallas_call(
        functools.partial(_hy_stage_c_kernel, n2h=n2h),
        grid=(c // ct, n1 // 8),
        in_specs=[pl.BlockSpec((2, n2, 8, ct), lambda j, g: (0, 0, g, j)),
                  pl.BlockSpec((8, 2 * n2h, 2 * n2), lambda j, g: (g, 0, 0)),
                  pl.BlockSpec((2, n2h, 8, ct), lambda j, g: (0, 0, g, uc + j)),
                  pl.BlockSpec((2, n2h, 8, ct), lambda j, g: (0, 0, g, gc + j)),
                  pl.BlockSpec((1, ct), lambda j, g: (0, j))],
        out_specs=pl.BlockSpec((2, n2h, 8, ct), lambda j, g: (0, 0, g, j)),
        out_shape=jax.ShapeDtypeStruct((2, n2h, n1, c), out_dtype),
        compiler_params=_cparams(("parallel", "arbitrary")),
    )(d, tai, u4, z4, skip)


def _hy_pre_kernel(x_ref, w_ref, b_ref, o_ref, *, is_grid):
    o_ref[...] = _conv3(x_ref[...], w_ref[...], is_grid) + b_ref[...]


def _hy_pre(h, conv_w, conv_b, *, tl, tc, is_grid):
    bsz, l, _ = h.shape
    n = conv_w.shape[1]
    c0 = COL_HY // tc
    return pl.pallas_call(
        functools.partial(_hy_pre_kernel, is_grid=is_grid),
        grid=(bsz, l // tl, n // tc),
        in_specs=[pl.BlockSpec((None, tl, tc), lambda b, i, j: (b, i, c0 + j)),
                  pl.BlockSpec((3, tc), lambda b, i, j: (0, j)),
                  pl.BlockSpec((1, tc), lambda b, i, j: (0, j))],
        out_specs=pl.BlockSpec((None, tl, tc), lambda b, i, j: (b, i, j)),
        out_shape=jax.ShapeDtypeStruct((bsz, l, n), F32),
        compiler_params=_cparams(("parallel", "parallel", "arbitrary")),
    )(h, conv_w, conv_b)


def _hyena(h, conv_w, conv_b, filt_params, skip, *, is_grid):
    bsz, l, _ = h.shape
    assert bsz == 2
    n1, n2 = _fft_split(l)
    n2h = n2 // 2
    ta, tak, tai, fb, fbi = (jnp.asarray(t).astype(BF16) for t in _fft_tables(n1, n2))
    ct = 512
    kg = 8

    hfilt = _hy_filters(l, *filt_params)
    cs_k = _hy_stage_a(hfilt.reshape(1, n2h, n1, -1), [((0, 2 * o), (0, 2 * o + 1)) for o in range(HY_ORDER)],
                       tak, ct=ct, planes=4, n2=n2)
    spec = _hy_spectrum(cs_k, fb, ct=ct, kg=kg)

    z = _hy_pre(h, conv_w, conv_b, tl=min(l, 512), tc=512, is_grid=is_grid)
    z4 = z.reshape(bsz, n2h, n1, -1)
    u4, ucol = z4, 0
    for o in range(HY_ORDER):
        last = o == HY_ORDER - 1
        cs = _hy_stage_a(u4, [((0, ucol // BRANCH_WIDTH), (1, ucol // BRANCH_WIDTH))], ta, ct=ct, planes=2, n2=n2)
        d = _hy_stage_b(cs, spec, o, fb, fbi, ct=ct, kg=kg)
        u4 = _hy_stage_c(d, tai, u4, ucol, z4, (o + 1) * BRANCH_WIDTH, skip[o:o + 1], ct=ct,
                         out_dtype=BF16 if last else F32)
        ucol = 0
    return u4.reshape(bsz, l, BRANCH_WIDTH)


def _reorder_w_in(w):
    d = w.shape[0]
    o = np.cumsum([0, 1024, 1024, 1024, 512, 1024, 16, 16, 1024, 1024, 512, 1024, 3072, 6144])
    seg = lambda i: w[:, int(o[i]):int(o[i + 1])]
    ff, fb, hi, dk, dv, da, db, hq, hgate, dq, dz, hy, gate = (seg(i) for i in range(13))
    pad = lambda n: jnp.zeros((d, n), w.dtype)
    out = jnp.concatenate([gate, hy, ff, fb, hi, dk, dv, da, db, pad(LANE - 32), hq, hgate, dq, dz,
                           pad(COL_PAD - COL_END)], axis=1)
    return out.astype(BF16)


def _token_mixers(u_src, sh, sc, w_r, lp, states, *, is_grid, with_output, tm):
    bsz, l, _ = u_src.shape
    tblk = min(l, 256)
    if with_output:
        h = _mm_mod(u_src, sh, sc, w_r, tm=tm, tn=IN_TN, relu2=False, out_dtype=F32)
        base = 0
    else:
        w_s = jnp.pad(w_r[:, COL_FF:COL_HQ], ((0, 0), (0, STATE_PAD - (COL_HQ - COL_FF))))
        h = _mm_mod(u_src, sh, sc, w_s, tm=tm, tn=IN_TN, relu2=False, out_dtype=F32)
        base = COL_FF
    col = lambda c: c - base
    ab_t = jnp.swapaxes(h[:, :, col(COL_AB):col(COL_AB) + 4 * HEADS], 1, 2)
    hg_f, hg_b, hg_states = _hgrn2_scan(h, (col(COL_FF), col(COL_FB), col(COL_HI), col(COL_HQ)),
                                        lp["lb_f"], lp["lb_b"], states[0], tblk=tblk, with_output=with_output)
    dn_f, dn_b, dn_states = _deltanet_scan(h, ab_t, (col(COL_DK), col(COL_DV), col(COL_DQ)),
                                           lp["dn_conv_k"], lp["dn_conv_v"], lp["dn_conv_q"],
                                           lp["dn_a_log"], lp["dn_dt_bias"], states[1],
                                           tblk=tblk, with_output=with_output, is_grid=is_grid)
    new_states = (hg_states, dn_states)
    if not with_output:
        return None, new_states
    tl = min(l, 512)
    hg_out = _combine(hg_f, hg_b, h, COL_HGATE, lp["hg_norm_w"], tl=tl, use_silu=False)
    dn_out = _combine(dn_f, dn_b, h, COL_DZ, lp["dn_norm_w"], tl=tl, use_silu=True)
    hy_out = _hyena(h, lp["hy_conv_w"], lp["hy_conv_b"], lp["hy_filt"], lp["hy_skip"], is_grid=is_grid)
    ysum = _branch_merge([hy_out, hg_out, dn_out], lp["w_branch"], h, tm=min(l, 512), tn=512)
    return ysum, new_states


def kernel(x, c, ctx, c_ctx, w_ada, b_ada, w_in, hy_conv_w, hy_conv_b, hy_filt_w1, hy_filt_b1, hy_filt_w2, hy_filt_b2, hy_filt_w3, hy_skip, hg_lb_logits, hg_norm_w, dn_conv_q, dn_conv_k, dn_conv_v, dn_a_log, dn_dt_bias, dn_norm_w, w_branch, w_out, ln1_g, ln1_b, w_ff1, w_ff2, ln2_g, ln2_b):
    depth = w_in.shape[0]
    bsz, _, d = x.shape
    alpha = (2 * depth) ** 0.25
    p = jax.nn.softmax(hg_lb_logits.astype(F32), axis=1)
    lower = jnp.cumsum(p, axis=1) - p[:, :1]
    cs = jnp.concatenate([c, c_ctx[None], jnp.zeros((8 - bsz - 1, d), F32)], axis=0)
    h_ctx = ctx
    for l in range(depth):
        last = l == depth - 1
        row = lambda v: v.reshape(1, -1)
        lp = {
            "lb_f": row(lower[0, l]), "lb_b": row(lower[1, l]),
            "hg_norm_w": row(hg_norm_w[l]), "dn_norm_w": row(dn_norm_w[l]),
            "dn_conv_q": dn_conv_q[l], "dn_conv_k": dn_conv_k[l], "dn_conv_v": dn_conv_v[l],
            "dn_a_log": dn_a_log[l], "dn_dt_bias": dn_dt_bias[l],
            "hy_conv_w": hy_conv_w[l], "hy_conv_b": row(hy_conv_b[l]), "hy_skip": hy_skip[l],
            "hy_filt": (jnp.pad(hy_filt_w1[l], ((0, LANE - HY_EMB), (0, 0))), row(hy_filt_b1[l]),
                        hy_filt_w2[l], row(hy_filt_b2[l]), hy_filt_w3[l]),
            "w_branch": w_branch[l].astype(BF16),
        }
        w_r = _reorder_w_in(w_in[l])
        w_o = w_out[l].astype(BF16)
        w_1 = w_ff1[l].astype(BF16)
        w_2 = w_ff2[l].astype(BF16)
        g1n, b1n, g2n, b2n = row(ln1_g[l]), row(ln1_b[l]), row(ln2_g[l]), row(ln2_b[l])
        mod = _ada_mod(cs, w_ada[l], row(b_ada[l]))
        m_lat = [mod[:bsz, i * d:(i + 1) * d][:, None, :] for i in range(6)]
        m_ctx = [jnp.broadcast_to(mod[bsz, i * d:(i + 1) * d][None, None, :], (bsz, 1, d)) for i in range(6)]
        zero = jnp.zeros((bsz, HEADS, HEAD_DIM, HEAD_DIM), F32)
        init = ((zero, zero), (zero, zero))

        lc = h_ctx.shape[1]
        y_ctx, ctx_states = _token_mixers(h_ctx, m_ctx[0], m_ctx[1], w_r, lp, init,
                                          is_grid=False, with_output=not last, tm=lc)
        if not last:
            h_ctx = _mm_ln(y_ctx, w_o, h_ctx, m_ctx[2], g1n, b1n, tm=lc, tk=d, alpha=alpha)
            mid = _mm_mod(h_ctx, m_ctx[3], m_ctx[4], w_1, tm=lc, tn=1024, relu2=True, out_dtype=BF16)
            h_ctx = _mm_ln(mid, w_2, h_ctx, m_ctx[5], g2n, b2n, tm=lc, tk=1024, alpha=alpha)

        y, _ = _token_mixers(x, m_lat[0], m_lat[1], w_r, lp, ctx_states, is_grid=True, with_output=True, tm=1024)
        x = _mm_ln(y, w_o, x, m_lat[2], g1n, b1n, tm=512, tk=d, alpha=alpha)
        mid = _mm_mod(x, m_lat[3], m_lat[4], w_1, tm=1024, tn=1024, relu2=True, out_dtype=BF16)
        x = _mm_ln(mid, w_2, x, m_lat[5], g2n, b2n, tm=512, tk=1024, alpha=alpha)
    return x
```

```python
import functools
import math

import numpy as np
import jax
import jax.numpy as jnp
from jax import lax
from jax.experimental import pallas as pl
from jax.experimental.pallas import tpu as pltpu

F32 = jnp.float32
BF16 = jnp.bfloat16

D_MODEL = 2048
GRID_W = 64
BRANCH_WIDTH = 1024
N_BRANCH = 3
HY_ORDER = 2
HY_EMB = 33
HY_BANDS = (HY_EMB - 1) // 2
HY_HIDDEN = 64
HY_DECAY_TARGET = 1e-2
HY_FAST_DECAY = 0.3
HY_SLOW_DECAY = 1.5
HEADS = 8
HEAD_DIM = 128
DN_QK_HEADS = 4
HG_F_MIN = 1e-30
CHUNK = 64
LN_EPS = 1e-5
RMS_EPS = 1e-6
LANE = 128

COL_GATE = 0
COL_HY = 6144
COL_FF = 9216
COL_FB = 10240
COL_HI = 11264
COL_DK = 12288
COL_DV = 12800
COL_AB = 13824
COL_HQ = 13952
COL_HGATE = 14976
COL_DQ = 16000
COL_DZ = 16512
COL_END = 17536
IN_TN = 896
COL_PAD = 17920
STATE_PAD = 5376

VMEM_LIMIT = 56 * 1024 * 1024


def _cparams(sem):
    return pltpu.CompilerParams(dimension_semantics=sem, vmem_limit_bytes=VMEM_LIMIT)


def _dot(a, b):
    return jnp.dot(a.astype(BF16), b.astype(BF16), preferred_element_type=F32)


def _dot_nt(a, b):
    return lax.dot_general(a.astype(BF16), b.astype(BF16), (((1,), (1,)), ((), ())),
                           preferred_element_type=F32)


def _dot_tn(a, b):
    return lax.dot_general(a.astype(BF16), b.astype(BF16), (((0,), (0,)), ((), ())),
                           preferred_element_type=F32)


def _sigmoid(x):
    return 1.0 / (1.0 + jnp.exp(-x))


def _silu(x):
    return x * _sigmoid(x)


def _ada_kernel(c_ref, w_ref, b_ref, o_ref):
    a = _silu(c_ref[...])
    o_ref[...] = _dot(a, w_ref[...]) + b_ref[...]


def _ada_mod(cs, w, b):
    d, n = w.shape
    tn = 1024
    return pl.pallas_call(
        _ada_kernel,
        grid=(n // tn,),
        in_specs=[pl.BlockSpec((8, d), lambda j: (0, 0)),
                  pl.BlockSpec((d, tn), lambda j: (0, j)),
                  pl.BlockSpec((1, tn), lambda j: (0, j))],
        out_specs=pl.BlockSpec((8, tn), lambda j: (0, j)),
        out_shape=jax.ShapeDtypeStruct((8, n), F32),
        compiler_params=_cparams(("arbitrary",)),
        name="ada_mod",
    )(cs, w, b)


def _mm_mod_kernel(x_ref, sh_ref, sc_ref, w_ref, o_ref, a_scr, *, relu2):
    @pl.when(pl.program_id(2) == 0)
    def _():
        a_scr[...] = (x_ref[...] * (1.0 + sc_ref[...]) + sh_ref[...]).astype(BF16)

    acc = jnp.dot(a_scr[...], w_ref[...], preferred_element_type=F32)
    if relu2:
        acc = jnp.square(jnp.maximum(acc, 0.0))
    o_ref[...] = acc.astype(o_ref.dtype)


def _mm_mod(x, sh, sc, w, *, tm, tn, relu2, out_dtype):
    bsz, l, d = x.shape
    n = w.shape[1]
    return pl.pallas_call(
        functools.partial(_mm_mod_kernel, relu2=relu2),
        grid=(bsz, l // tm, n // tn),
        in_specs=[pl.BlockSpec((None, tm, d), lambda b, i, j: (b, i, 0)),
                  pl.BlockSpec((None, 1, d), lambda b, i, j: (b, 0, 0)),
                  pl.BlockSpec((None, 1, d), lambda b, i, j: (b, 0, 0)),
                  pl.BlockSpec((d, tn), lambda b, i, j: (0, j))],
        out_specs=pl.BlockSpec((None, tm, tn), lambda b, i, j: (b, i, j)),
        out_shape=jax.ShapeDtypeStruct((bsz, l, n), out_dtype),
        scratch_shapes=[pltpu.VMEM((tm, d), BF16)],
        compiler_params=_cparams(("parallel", "parallel", "arbitrary")),
        name="mm_relu2" if relu2 else "mm_in_proj",
    )(x, sh, sc, w)


def _mm_ln_kernel(a_ref, w_ref, x_ref, gate_ref, g_ref, b_ref, o_ref, acc_ref, *, nk, alpha):
    k = pl.program_id(2)

    @pl.when(k == 0)
    def _():
        acc_ref[...] = jnp.zeros_like(acc_ref)

    acc_ref[...] += jnp.dot(a_ref[...], w_ref[...], preferred_element_type=F32)

    @pl.when(k == nk - 1)
    def _():
        y = alpha * x_ref[...] + gate_ref[...] * acc_ref[...]
        mu = jnp.mean(y, axis=-1, keepdims=True)
        yc = y - mu
        var = jnp.mean(yc * yc, axis=-1, keepdims=True)
        o_ref[...] = yc * lax.rsqrt(var + LN_EPS) * g_ref[...] + b_ref[...]


def _mm_ln(a, w, x, gate, ln_g, ln_b, *, tm, tk, alpha):
    bsz, l, kdim = a.shape
    d = w.shape[1]
    nk = kdim // tk
    return pl.pallas_call(
        functools.partial(_mm_ln_kernel, nk=nk, alpha=alpha),
        grid=(bsz, l // tm, nk),
        in_specs=[pl.BlockSpec((None, tm, tk), lambda b, i, k: (b, i, k)),
                  pl.BlockSpec((tk, d), lambda b, i, k: (k, 0)),
                  pl.BlockSpec((None, tm, d), lambda b, i, k: (b, i, 0)),
                  pl.BlockSpec((None, 1, d), lambda b, i, k: (b, 0, 0)),
                  pl.BlockSpec((1, d), lambda b, i, k: (0, 0)),
                  pl.BlockSpec((1, d), lambda b, i, k: (0, 0))],
        out_specs=pl.BlockSpec((None, tm, d), lambda b, i, k: (b, i, 0)),
        out_shape=jax.ShapeDtypeStruct((bsz, l, d), F32),
        scratch_shapes=[pltpu.VMEM((tm, d), F32)],
        compiler_params=_cparams(("parallel", "parallel", "arbitrary")),
        name="mm_resid_ln",
    )(a, w, x, gate, ln_g, ln_b)


def _branch_kernel(b0_ref, b1_ref, b2_ref, w_ref, g0_ref, g1_ref, g2_ref, o_ref):
    acc = _sigmoid(g0_ref[...]) * jnp.dot(b0_ref[...], w_ref[0], preferred_element_type=F32)
    acc += _sigmoid(g1_ref[...]) * jnp.dot(b1_ref[...], w_ref[1], preferred_element_type=F32)
    acc += _sigmoid(g2_ref[...]) * jnp.dot(b2_ref[...], w_ref[2], preferred_element_type=F32)
    o_ref[...] = acc.astype(o_ref.dtype)


def _branch_merge(branches, w, h, *, tm, tn):
    bsz, l, c = branches[0].shape
    d = w.shape[2]
    nj = d // tn
    br_spec = pl.BlockSpec((None, tm, c), lambda b, i, j: (b, i, 0))

    def gate_spec(g):
        return pl.BlockSpec((None, tm, tn), lambda b, i, j: (b, i, COL_GATE // tn + g * nj + j))

    return pl.pallas_call(
        _branch_kernel,
        grid=(bsz, l // tm, nj),
        in_specs=[br_spec, br_spec, br_spec,
                  pl.BlockSpec((N_BRANCH, c, tn), lambda b, i, j: (0, 0, j)),
                  gate_spec(0), gate_spec(1), gate_spec(2)],
        out_specs=pl.BlockSpec((None, tm, tn), lambda b, i, j: (b, i, j)),
        out_shape=jax.ShapeDtypeStruct((bsz, l, d), BF16),
        compiler_params=_cparams(("parallel", "parallel", "arbitrary")),
        name="branch_merge",
    )(*branches, w, h, h, h)


def _iota2(shape, axis):
    return lax.broadcasted_iota(jnp.int32, shape, axis)


def _split3(g):
    g1 = g.astype(BF16)
    r = g - g1.astype(F32)
    g2 = r.astype(BF16)
    g3 = (r - g2.astype(F32)).astype(BF16)
    return g1, g2, g3


def _cumsum_rows(g, rev):
    row = _iota2((CHUNK, 3 * CHUNK), 0)
    col = jnp.bitwise_and(_iota2((CHUNK, 3 * CHUNK), 1), CHUNK - 1)
    tri = jnp.where((col >= row) if rev else (col <= row), 1.0, 0.0).astype(BF16)
    g3 = jnp.concatenate(_split3(g), axis=0)
    return jnp.dot(tri, g3, preferred_element_type=F32)


def _hg_prepare(items):
    for it in items:
        lb = it["lb"]
        f = lb + (1.0 - lb) * _sigmoid(it["fz"])
        it["g"] = jnp.log(jnp.maximum(f, HG_F_MIN))
        it["kk"] = (1.0 - lb) * _sigmoid(-it["fz"])
    for it in items:
        it["bc"] = _cumsum_rows(it["g"], it["rev"])
    half = CHUNK // 2
    for it in items:
        bc = it["bc"]
        if it["rev"]:
            mid, last = bc[half:half + 1], bc[0:1]
        else:
            mid, last = bc[half - 1:half], bc[CHUNK - 1:CHUNK]
        it["e_last"] = jnp.exp(last)
        it["ks"] = it["kk"] * jnp.exp(last - bc)
        if it["q"] is not None:
            qs = _silu(it["q"])
            it["qd"] = qs * jnp.exp(bc - mid)
            it["kd"] = it["kk"] * jnp.exp(mid - bc)
            it["qs"] = qs * jnp.exp(bc)
    r = _iota2((CHUNK, CHUNK), 0)
    c = _iota2((CHUNK, CHUNK), 1)
    for it in items:
        it["kv"] = _dot_tn(it["v"], it["ks"])
        if it["q"] is not None:
            att = _dot_nt(it["qd"], it["kd"])
            it["att"] = jnp.where((c >= r) if it["rev"] else (c <= r), att, 0.0)
    for it in items:
        if it["q"] is not None:
            it["o_c"] = _dot(it["att"], it["v"])


def _hg_kernel(*refs, nchunk, with_output):
    if with_output:
        (ff_ref, vf_ref, qf_ref, fb_ref, vb_ref, qb_ref, lbf_ref, lbb_ref, s0f_ref, s0b_ref,
         of_ref, ob_ref, sf_ref, sb_ref, stf, stb) = refs
    else:
        (ff_ref, vf_ref, fb_ref, vb_ref, lbf_ref, lbb_ref, s0f_ref, s0b_ref,
         sf_ref, sb_ref, stf, stb) = refs
        qf_ref = qb_ref = of_ref = ob_ref = None

    @pl.when(pl.program_id(2) == 0)
    def _():
        stf[...] = s0f_ref[...]
        stb[...] = s0b_ref[...]

    def chunk_item(f_ref, v_ref, q_ref, lb_ref, ci, rev):
        sl = pl.ds(ci * CHUNK, CHUNK)
        return {"fz": f_ref[sl, :], "v": v_ref[sl, :], "q": q_ref[sl, :] if with_output else None,
                "lb": lb_ref[...], "rev": rev, "sl": sl}

    items = []
    for ci in range(nchunk):
        items.append(chunk_item(ff_ref, vf_ref, qf_ref, lbf_ref, ci, False))
        items.append(chunk_item(fb_ref, vb_ref, qb_ref, lbb_ref, nchunk - 1 - ci, True))
    _hg_prepare(items)
    state = {False: stf[...], True: stb[...]}
    for it in items:
        st = state[it["rev"]]
        if with_output:
            o_ref = ob_ref if it["rev"] else of_ref
            o_ref[it["sl"], :] = it["o_c"] + _dot_nt(it["qs"], st)
        state[it["rev"]] = st * it["e_last"] + it["kv"]
    stf[...] = state[False]
    stb[...] = state[True]
    sf_ref[...] = state[False]
    sb_ref[...] = state[True]


def _hgrn2_scan(h, cols, lb_f, lb_b, s0, *, tblk, with_output):
    bsz, l, _ = h.shape
    nt = l // tblk
    c_ff, c_fb, c_i, c_q = (c // LANE for c in cols)

    def fwd(cb):
        return pl.BlockSpec((None, tblk, LANE), lambda b, hh, t: (b, t, cb + hh))

    def bwd(cb):
        return pl.BlockSpec((None, tblk, LANE), lambda b, hh, t: (b, nt - 1 - t, cb + hh))

    lb_spec = pl.BlockSpec((1, LANE), lambda b, hh, t: (0, hh))
    st_spec = pl.BlockSpec((None, None, HEAD_DIM, HEAD_DIM), lambda b, hh, t: (b, hh, 0, 0))
    if with_output:
        in_specs = [fwd(c_ff), fwd(c_i), fwd(c_q), bwd(c_fb), bwd(c_i), bwd(c_q)]
        args = [h] * 6
    else:
        in_specs = [fwd(c_ff), fwd(c_i), bwd(c_fb), bwd(c_i)]
        args = [h] * 4
    in_specs += [lb_spec, lb_spec, st_spec, st_spec]
    args += [lb_f, lb_b, s0[0], s0[1]]
    st_shape = jax.ShapeDtypeStruct((bsz, HEADS, HEAD_DIM, HEAD_DIM), F32)
    out_specs = [st_spec, st_spec]
    out_shape = [st_shape, st_shape]
    if with_output:
        o_shape = jax.ShapeDtypeStruct((bsz, l, HEADS * HEAD_DIM), F32)
        out_specs = [pl.BlockSpec((None, tblk, LANE), lambda b, hh, t: (b, t, hh)),
                     pl.BlockSpec((None, tblk, LANE), lambda b, hh, t: (b, nt - 1 - t, hh))] + out_specs
        out_shape = [o_shape, o_shape] + out_shape
    outs = pl.pallas_call(
        functools.partial(_hg_kernel, nchunk=tblk // CHUNK, with_output=with_output),
        name="hgrn2_scan",
        grid=(bsz, HEADS, nt),
        in_specs=in_specs,
        out_specs=out_specs,
        out_shape=out_shape,
        scratch_shapes=[pltpu.VMEM((HEAD_DIM, HEAD_DIM), F32), pltpu.VMEM((HEAD_DIM, HEAD_DIM), F32)],
        compiler_params=_cparams(("parallel", "parallel", "arbitrary")),
    )(*args)
    if with_output:
        return outs[0], outs[1], (outs[2], outs[3])
    return None, None, (outs[0], outs[1])


def _conv3(x, w, is_grid):
    t = x.shape[0]
    row = _iota2((t, 1), 0)
    if is_grid:
        pos = jnp.bitwise_and(row, GRID_W - 1)
        has_l, has_r = pos != 0, pos != GRID_W - 1
    else:
        has_l, has_r = row != 0, row != t - 1
    xl = jnp.where(has_l, pltpu.roll(x, 1, 0), 0.0)
    xr = jnp.where(has_r, pltpu.roll(x, t - 1, 0), 0.0)
    return xl * w[0:1] + x * w[1:2] + xr * w[2:3]


def _l2norm(t):
    return t * lax.rsqrt(jnp.sum(t * t, axis=-1, keepdims=True) + 1e-6)


def _softplus(x):
    return jnp.maximum(x, 0.0) + jnp.log(1.0 + jnp.exp(-jnp.abs(x)))


def _dn_prepare(items):
    r = _iota2((CHUNK, CHUNK), 0)
    c = _iota2((CHUNK, CHUNK), 1)
    eye = r == c
    for it in items:
        rev = it["rev"]
        incl = (c >= r) if rev else (c <= r)
        incl_t = (r >= c) if rev else (r <= c)
        g_b = jnp.broadcast_to(it["g"], (CHUNK, CHUNK))
        gc_col = jnp.sum(jnp.where(incl, g_b, 0.0), axis=1, keepdims=True)
        g_col = jnp.sum(jnp.where(eye, g_b, 0.0), axis=1, keepdims=True)
        gc_row = jnp.sum(jnp.where(incl_t, jnp.broadcast_to(g_col, (CHUNK, CHUNK)), 0.0), axis=0, keepdims=True)
        beta_col = jnp.sum(jnp.where(eye, jnp.broadcast_to(it["beta"], (CHUNK, CHUNK)), 0.0), axis=1,
                           keepdims=True)
        it["decay"] = jnp.where(incl, jnp.exp(jnp.where(incl, gc_col - gc_row, 0.0)), 0.0)
        g_last = gc_col[0:1] if rev else gc_col[CHUNK - 1:CHUNK]
        it["e_last"] = jnp.exp(g_last)
        egc = jnp.exp(gc_col)
        kb = it["k"] * beta_col
        it["kb"] = kb
        it["rhs"] = jnp.concatenate([it["v"] * beta_col, kb * egc], axis=1)
        it["kd"] = it["k"] * jnp.exp(g_last - gc_col)
        if it["q"] is not None:
            it["qe"] = it["q"] * egc
    for it in items:
        it["kk"] = _dot_nt(it["kb"], it["k"])
        if it["q"] is not None:
            it["qk"] = _dot_nt(it["q"], it["k"]) * it["decay"]
    for it in items:
        strict = (c > r) if it["rev"] else (c < r)
        n = jnp.where(strict, it["kk"] * it["decay"], 0.0)
        it["n"] = n
        it["p"] = jnp.where(eye, 1.0, 0.0) - n
    for it in items:
        it["m"] = _dot(it["n"], it["n"])
    for i in range(5):
        for it in items:
            if i < 4:
                pm = _dot(jnp.concatenate([it["p"], it["m"]], axis=0), it["m"])
                it["p"] = it["p"] + pm[:CHUNK]
                it["m"] = pm[CHUNK:]
            else:
                it["p"] = it["p"] + _dot(it["p"], it["m"])
    for it in items:
        it["uw"] = _dot(it["p"], it["rhs"])
    for it in items:
        bm = _dot_tn(it["kd"], it["uw"])
        it["b_c"], it["m_c"] = bm[:, :HEAD_DIM], bm[:, HEAD_DIM:]
        if it["q"] is not None:
            ow = _dot(it["qk"], it["uw"])
            it["o_c"], it["q_eff"] = ow[:, :HEAD_DIM], it["qe"] - ow[:, HEAD_DIM:]


def _dn_kernel(*refs, nchunk, with_output, is_grid):
    if with_output:
        (alog_ref, dtb_ref, kf_ref, vf_ref, qf_ref, abf_ref, kb_ref, vb_ref, qb_ref, abb_ref,
         ck_ref, cv_ref, cq_ref, s0f_ref, s0b_ref, of_ref, ob_ref, sf_ref, sb_ref, stf, stb) = refs
    else:
        (alog_ref, dtb_ref, kf_ref, vf_ref, abf_ref, kb_ref, vb_ref, abb_ref,
         ck_ref, cv_ref, s0f_ref, s0b_ref, sf_ref, sb_ref, stf, stb) = refs
        qf_ref = qb_ref = cq_ref = of_ref = ob_ref = None
    hh = pl.program_id(1)

    @pl.when(pl.program_id(2) == 0)
    def _():
        stf[...] = s0f_ref[...]
        stb[...] = s0b_ref[...]

    def prep(k_ref, v_ref, q_ref, ab_ref, d):
        k = _l2norm(_silu(_conv3(k_ref[...], ck_ref[...], is_grid)))
        v = _silu(_conv3(v_ref[...], cv_ref[...], is_grid))
        q = None
        if with_output:
            q = _l2norm(_silu(_conv3(q_ref[...], cq_ref[...], is_grid))) * (HEAD_DIM ** -0.5)
        a = ab_ref[pl.ds(d * HEADS + hh, 1), :]
        bt = ab_ref[pl.ds(2 * HEADS + d * HEADS + hh, 1), :]
        g = -jnp.exp(alog_ref[d, hh]) * _softplus(a + dtb_ref[d, hh])
        return k, v, q, g, _sigmoid(bt)

    kf, vf, qf, gf, bf = prep(kf_ref, vf_ref, qf_ref, abf_ref, 0)
    kb, vb, qb, gb, bb = prep(kb_ref, vb_ref, qb_ref, abb_ref, 1)
    def chunk_item(k, v, q, g, beta, ci, rev):
        lo = ci * CHUNK
        return {"k": k[lo:lo + CHUNK], "v": v[lo:lo + CHUNK], "q": q[lo:lo + CHUNK] if with_output else None,
                "g": g[:, lo:lo + CHUNK], "beta": beta[:, lo:lo + CHUNK], "rev": rev, "lo": lo}

    items = []
    for ci in range(nchunk):
        items.append(chunk_item(kf, vf, qf, gf, bf, ci, False))
        items.append(chunk_item(kb, vb, qb, gb, bb, nchunk - 1 - ci, True))
    _dn_prepare(items)
    state = {False: stf[...], True: stb[...]}
    for it in items:
        s = state[it["rev"]]
        if with_output:
            o_ref = ob_ref if it["rev"] else of_ref
            o_ref[pl.ds(it["lo"], CHUNK), :] = it["o_c"] + _dot(it["q_eff"], s)
        state[it["rev"]] = s * it["e_last"] + it["b_c"] - _dot(it["m_c"], s)
    stf[...] = state[False]
    stb[...] = state[True]
    sf_ref[...] = state[False]
    sb_ref[...] = state[True]


def _deltanet_scan(h, ab_t, cols, conv_k, conv_v, conv_q, a_log, dt_bias, s0, *, tblk, with_output, is_grid):
    bsz, l, _ = h.shape
    nt = l // tblk
    if not is_grid:
        assert nt == 1
    c_k, c_v, c_q = (c // LANE for c in cols)
    rep = HEADS // DN_QK_HEADS

    def tok(cb, shared, rev):
        def imap(b, hh, t):
            return (b, (nt - 1 - t) if rev else t, cb + (hh // rep if shared else hh))
        return pl.BlockSpec((None, tblk, LANE), imap)

    def ab_spec(rev):
        return pl.BlockSpec((None, 4 * HEADS, tblk), lambda b, hh, t: (b, 0, (nt - 1 - t) if rev else t))

    smem = pl.BlockSpec(memory_space=pltpu.SMEM)
    ck_spec = pl.BlockSpec((3, LANE), lambda b, hh, t: (0, hh // rep))
    cv_spec = pl.BlockSpec((3, LANE), lambda b, hh, t: (0, hh))
    st_spec = pl.BlockSpec((None, None, HEAD_DIM, HEAD_DIM), lambda b, hh, t: (b, hh, 0, 0))
    if with_output:
        in_specs = [smem, smem, tok(c_k, True, False), tok(c_v, False, False), tok(c_q, True, False), ab_spec(False),
                    tok(c_k, True, True), tok(c_v, False, True), tok(c_q, True, True), ab_spec(True),
                    ck_spec, cv_spec, ck_spec, st_spec, st_spec]
        args = [a_log, dt_bias, h, h, h, ab_t, h, h, h, ab_t, conv_k, conv_v, conv_q, s0[0], s0[1]]
    else:
        in_specs = [smem, smem, tok(c_k, True, False), tok(c_v, False, False), ab_spec(False),
                    tok(c_k, True, True), tok(c_v, False, True), ab_spec(True),
                    ck_spec, cv_spec, st_spec, st_spec]
        args = [a_log, dt_bias, h, h, ab_t, h, h, ab_t, conv_k, conv_v, s0[0], s0[1]]
    st_shape = jax.ShapeDtypeStruct((bsz, HEADS, HEAD_DIM, HEAD_DIM), F32)
    out_specs = [st_spec, st_spec]
    out_shape = [st_shape, st_shape]
    if with_output:
        o_shape = jax.ShapeDtypeStruct((bsz, l, HEADS * HEAD_DIM), F32)
        out_specs = [pl.BlockSpec((None, tblk, LANE), lambda b, hh, t: (b, t, hh)),
                     pl.BlockSpec((None, tblk, LANE), lambda b, hh, t: (b, nt - 1 - t, hh))] + out_specs
        out_shape = [o_shape, o_shape] + out_shape
    outs = pl.pallas_call(
        functools.partial(_dn_kernel, nchunk=tblk // CHUNK, with_output=with_output, is_grid=is_grid),
        name="deltanet_scan",
        grid=(bsz, HEADS, nt),
        in_specs=in_specs,
        out_specs=out_specs,
        out_shape=out_shape,
        scratch_shapes=[pltpu.VMEM((HEAD_DIM, HEAD_DIM), F32), pltpu.VMEM((HEAD_DIM, HEAD_DIM), F32)],
        compiler_params=_cparams(("parallel", "parallel", "arbitrary")),
    )(*args)
    if with_output:
        return outs[0], outs[1], (outs[2], outs[3])
    return None, None, (outs[0], outs[1])


def _combine_kernel(of_ref, ob_ref, z_ref, w_ref, o_ref, *, use_silu):
    o = of_ref[...] + ob_ref[...]
    o = o * lax.rsqrt(jnp.mean(o * o, axis=-1, keepdims=True) + RMS_EPS) * w_ref[...]
    z = z_ref[...]
    o_ref[...] = (o * (_silu(z) if use_silu else _sigmoid(z))).astype(o_ref.dtype)


def _combine(o_f, o_b, h, col_z, norm_w, *, tl, use_silu):
    bsz, l, c = o_f.shape
    cz = col_z // LANE
    spec = pl.BlockSpec((None, tl, LANE), lambda b, i, hh: (b, i, hh))
    return pl.pallas_call(
        functools.partial(_combine_kernel, use_silu=use_silu),
        name="mixer_norm_gate",
        grid=(bsz, l // tl, c // LANE),
        in_specs=[spec, spec,
                  pl.BlockSpec((None, tl, LANE), lambda b, i, hh: (b, i, cz + hh)),
                  pl.BlockSpec((1, LANE), lambda b, i, hh: (0, 0))],
        out_specs=spec,
        out_shape=jax.ShapeDtypeStruct((bsz, l, c), BF16),
        compiler_params=_cparams(("parallel", "parallel", "arbitrary")),
    )(o_f, o_b, h, norm_w)


def _dot_f32(a, b):
    return jnp.dot(a, b, precision=lax.Precision.HIGHEST, preferred_element_type=F32)


def _hy_filt_kernel(w1_ref, b1_ref, w2_ref, b2_ref, w3_ref, o_ref, *, l_total, tl):
    i = pl.program_id(0)
    j = pl.program_id(1)
    row = (_iota2((tl, LANE), 0) + i * tl).astype(F32)
    lane = _iota2((tl, LANE), 1)
    t = row * (1.0 / (l_total - 1))
    ang = row * (2.0 * math.pi / l_total)
    bidx = jnp.where(lane <= HY_BANDS, lane - 1, lane - 1 - HY_BANDS).astype(F32)
    band = 1e-4 + bidx * ((HY_BANDS - 1 - 1e-4) / (HY_BANDS - 1))
    arg = band * ang
    feats = jnp.where(lane == 0, t,
                      jnp.where(lane <= HY_BANDS, jnp.cos(arg),
                                jnp.where(lane <= 2 * HY_BANDS, -jnp.sin(arg), 0.0)))
    hid = jnp.sin(_dot_f32(feats, w1_ref[...]) + b1_ref[...])
    hid = jnp.sin(_dot_f32(hid, w2_ref[...]) + b2_ref[...])
    hcol = _dot_f32(hid, w3_ref[...])
    width = hcol.shape[1]
    ch = _iota2((tl, width), 1).astype(F32)
    lo = math.log(HY_DECAY_TARGET) / HY_SLOW_DECAY
    hi = math.log(HY_DECAY_TARGET) / HY_FAST_DECAY
    delta = jnp.abs(lo + ch * ((hi - lo) / (width - 1)))
    trow = (_iota2((tl, width), 0) + i * tl)
    out = hcol * jnp.exp(-(trow.astype(F32) * (1.0 / (l_total - 1))) * delta)
    drop = jnp.logical_and(trow == 0, jnp.bitwise_and(j, 1) == 1)
    o_ref[...] = jnp.where(drop, 0.0, out)


def _hy_filters(l_total, w1p, b1, w2, b2, w3):
    tl = min(l_total, 512)
    n = w3.shape[1]
    full = lambda a: pl.BlockSpec(a.shape, lambda i, j: (0,) * a.ndim)
    return pl.pallas_call(
        functools.partial(_hy_filt_kernel, l_total=l_total, tl=tl),
        name="hy_filters",
        grid=(l_total // tl, n // BRANCH_WIDTH),
        in_specs=[full(w1p), full(b1), full(w2), full(b2),
                  pl.BlockSpec((HY_HIDDEN, BRANCH_WIDTH), lambda i, j: (0, j))],
        out_specs=pl.BlockSpec((tl, BRANCH_WIDTH), lambda i, j: (i, j)),
        out_shape=jax.ShapeDtypeStruct((l_total, n), F32),
        compiler_params=_cparams(("parallel", "arbitrary")),
    )(w1p, b1, w2, b2, w3)


@functools.lru_cache(maxsize=None)
def _fft_tables(n1, n2):
    n = n1 * n2
    n2h = n2 // 2
    k2 = np.arange(n2, dtype=np.int64)[:, None]
    m = np.arange(n2h, dtype=np.int64)[None, :]
    ta = np.zeros((n1, 2 * n2, 2 * n2h))
    tak = np.zeros((n1, 4 * n2, 2 * n2h))
    for a in range(n1):
        th = 2.0 * np.pi * ((k2 * (a + n1 * m)) % n) / n
        c, s = np.cos(th), np.sin(th)
        ta[a] = np.block([[c, s], [-s, c]])
        z = np.zeros_like(c)
        tak[a] = np.block([[c, z], [-s, z], [z, c], [z, -s]])
    tai = np.transpose(ta, (0, 2, 1)) / n
    k1 = np.arange(n1, dtype=np.int64)
    ph = 2.0 * np.pi * ((k1[:, None] * k1[None, :]) % n1) / n1
    c, s = np.cos(ph), np.sin(ph)
    fb = np.block([[c, s], [-s, c]])
    return tuple(np.asarray(t, np.float32) for t in (ta, tak, tai, fb, fb.T))


def _fft_split(l_total):
    n = 2 * l_total
    n1 = min(128, l_total // 8)
    return n1, n // n1


def _hy_stage_a_kernel(gr_ref, cr_ref, gi_ref, ci_ref, xr_ref, xi_ref, t_ref, o_ref, *, planes, n2):
    del gr_ref, cr_ref, gi_ref, ci_ref
    for j in range(8):
        x = jnp.concatenate([xr_ref[:, j, :], xi_ref[:, j, :]], axis=0).astype(BF16)
        y = jnp.dot(t_ref[j], x, preferred_element_type=F32)
        for p in range(planes):
            o_ref[p, :, j, :] = y[p * n2:(p + 1) * n2].astype(o_ref.dtype)


def _hy_stage_a(x4, pairs, tab, *, ct, planes, n2):
    _, n2h, n1, _ = x4.shape
    npair = len(pairs)
    nct = BRANCH_WIDTH // ct
    gr = jnp.asarray([p[0][0] for p in pairs], jnp.int32)
    cr = jnp.asarray([p[0][1] for p in pairs], jnp.int32)
    gi = jnp.asarray([p[1][0] for p in pairs], jnp.int32)
    ci = jnp.asarray([p[1][1] for p in pairs], jnp.int32)
    grid_spec = pltpu.PrefetchScalarGridSpec(
        num_scalar_prefetch=4,
        grid=(npair, nct, n1 // 8),
        in_specs=[pl.BlockSpec((None, n2h, 8, ct), lambda p, c, g, gr, cr, gi, ci: (gr[p], 0, g, cr[p] * nct + c)),
                  pl.BlockSpec((None, n2h, 8, ct), lambda p, c, g, gr, cr, gi, ci: (gi[p], 0, g, ci[p] * nct + c)),
                  pl.BlockSpec((8, planes * n2, 2 * n2h), lambda p, c, g, *_: (g, 0, 0))],
        out_specs=pl.BlockSpec((None, planes, n2, 8, ct), lambda p, c, g, *_: (p, 0, 0, g, c)),
    )
    return pl.pallas_call(
        functools.partial(_hy_stage_a_kernel, planes=planes, n2=n2),
        name="hy_stage_a",
        grid_spec=grid_spec,
        out_shape=jax.ShapeDtypeStruct((npair, planes, n2, n1, BRANCH_WIDTH), BF16),
        compiler_params=_cparams(("parallel", "parallel", "arbitrary")),
    )(gr, cr, gi, ci, x4, x4, tab)


def _hy_spec_kernel(c_ref, fb_ref, o_ref, *, kg, n1):
    for k in range(kg):
        zf = jnp.dot(fb_ref[...], jnp.concatenate([c_ref[0, k], c_ref[1, k]], axis=0), preferred_element_type=F32)
        zb = jnp.dot(fb_ref[...], jnp.concatenate([c_ref[2, k], c_ref[3, k]], axis=0), preferred_element_type=F32)
        o_ref[0, k] = zf[:n1] + zb[:n1]
        o_ref[1, k] = zf[n1:] - zb[n1:]


def _hy_spectrum(cs, fb, *, ct, kg):
    no, _, n2, n1, c = cs.shape
    return pl.pallas_call(
        functools.partial(_hy_spec_kernel, kg=kg, n1=n1),
        name="hy_spectrum",
        grid=(no, c // ct, n2 // kg),
        in_specs=[pl.BlockSpec((None, 4, kg, n1, ct), lambda o, j, g: (o, 0, g, 0, j)),
                  pl.BlockSpec(fb.shape, lambda o, j, g: (0, 0))],
        out_specs=pl.BlockSpec((None, 2, kg, n1, ct), lambda o, j, g: (o, 0, g, 0, j)),
        out_shape=jax.ShapeDtypeStruct((no, 2, n2, n1, c), F32),
        compiler_params=_cparams(("parallel", "parallel", "arbitrary")),
    )(cs, fb)


def _hy_stage_b_kernel(c_ref, k_ref, fb_ref, fbi_ref, o_ref, *, kg, n1):
    for k in range(kg):
        z = jnp.dot(fb_ref[...], jnp.concatenate([c_ref[0, k], c_ref[1, k]], axis=0), preferred_element_type=F32)
        zr, zi = z[:n1], z[n1:]
        kr, ki = k_ref[0, k], k_ref[1, k]
        y = jnp.concatenate([zr * kr - zi * ki, zr * ki + zi * kr], axis=0).astype(BF16)
        d = jnp.dot(fbi_ref[...], y, preferred_element_type=F32)
        o_ref[0, k] = d[:n1].astype(o_ref.dtype)
        o_ref[1, k] = d[n1:].astype(o_ref.dtype)


def _hy_stage_b(cs, spec, order, fb, fbi, *, ct, kg):
    _, _, n2, n1, c = cs.shape
    return pl.pallas_call(
        functools.partial(_hy_stage_b_kernel, kg=kg, n1=n1),
        name="hy_stage_b",
        grid=(c // ct, n2 // kg),
        in_specs=[pl.BlockSpec((None, 2, kg, n1, ct), lambda j, g: (0, 0, g, 0, j)),
                  pl.BlockSpec((None, 2, kg, n1, ct), lambda j, g: (order, 0, g, 0, j)),
                  pl.BlockSpec(fb.shape, lambda j, g: (0, 0)),
                  pl.BlockSpec(fbi.shape, lambda j, g: (0, 0))],
        out_specs=pl.BlockSpec((2, kg, n1, ct), lambda j, g: (0, g, 0, j)),
        out_shape=jax.ShapeDtypeStruct((2, n2, n1, c), BF16),
        compiler_params=_cparams(("parallel", "arbitrary")),
    )(cs, spec, fb, fbi)


def _hy_stage_c_kernel(d_ref, t_ref, u_ref, g_ref, skip_ref, o_ref, *, n2h):
    skip = skip_ref[...]
    for j in range(8):
        d = jnp.concatenate([d_ref[0, :, j, :], d_ref[1, :, j, :]], axis=0)
        y = jnp.dot(t_ref[j], d, preferred_element_type=F32)
        for b in range(2):
            yb = y[b * n2h:(b + 1) * n2h]
            o_ref[b, :, j, :] = (g_ref[b, :, j, :] * (yb + skip * u_ref[b, :, j, :])).astype(o_ref.dtype)


def _hy_stage_c(d, tai, u4, ucol, z4, gcol, skip, *, ct, out_dtype):
    _, n2, n1, c = d.shape
    n2h = n2 // 2
    uc, gc = ucol // ct, gcol // ct
    return pl.pallas_call(
        functools.partial(_hy_stage_c_kernel, n2h=n2h),
        name="hy_stage_c",
        grid=(c // ct, n1 // 8),
        in_specs=[pl.BlockSpec((2, n2, 8, ct), lambda j, g: (0, 0, g, j)),
                  pl.BlockSpec((8, 2 * n2h, 2 * n2), lambda j, g: (g, 0, 0)),
                  pl.BlockSpec((2, n2h, 8, ct), lambda j, g: (0, 0, g, uc + j)),
                  pl.BlockSpec((2, n2h, 8, ct), lambda j, g: (0, 0, g, gc + j)),
                  pl.BlockSpec((1, ct), lambda j, g: (0, j))],
        out_specs=pl.BlockSpec((2, n2h, 8, ct), lambda j, g: (0, 0, g, j)),
        out_shape=jax.ShapeDtypeStruct((2, n2h, n1, c), out_dtype),
        compiler_params=_cparams(("parallel", "arbitrary")),
    )(d, tai, u4, z4, skip)


def _hy_pre_kernel(x_ref, w_ref, b_ref, o_ref, *, is_grid):
    o_ref[...] = _conv3(x_ref[...], w_ref[...], is_grid) + b_ref[...]


def _hy_pre(h, conv_w, conv_b, *, tl, tc, is_grid):
    bsz, l, _ = h.shape
    n = conv_w.shape[1]
    c0 = COL_HY // tc
    return pl.pallas_call(
        functools.partial(_hy_pre_kernel, is_grid=is_grid),
        name="hy_short_conv",
        grid=(bsz, l // tl, n // tc),
        in_specs=[pl.BlockSpec((None, tl, tc), lambda b, i, j: (b, i, c0 + j)),
                  pl.BlockSpec((3, tc), lambda b, i, j: (0, j)),
                  pl.BlockSpec((1, tc), lambda b, i, j: (0, j))],
        out_specs=pl.BlockSpec((None, tl, tc), lambda b, i, j: (b, i, j)),
        out_shape=jax.ShapeDtypeStruct((bsz, l, n), F32),
        compiler_params=_cparams(("parallel", "parallel", "arbitrary")),
    )(h, conv_w, conv_b)


def _hyena(h, conv_w, conv_b, filt_params, skip, *, is_grid):
    bsz, l, _ = h.shape
    assert bsz == 2
    n1, n2 = _fft_split(l)
    n2h = n2 // 2
    ta, tak, tai, fb, fbi = (jnp.asarray(t).astype(BF16) for t in _fft_tables(n1, n2))
    ct = 512
    kg = 8

    hfilt = _hy_filters(l, *filt_params)
    cs_k = _hy_stage_a(hfilt.reshape(1, n2h, n1, -1), [((0, 2 * o), (0, 2 * o + 1)) for o in range(HY_ORDER)],
                       tak, ct=ct, planes=4, n2=n2)
    spec = _hy_spectrum(cs_k, fb, ct=ct, kg=kg)

    z = _hy_pre(h, conv_w, conv_b, tl=min(l, 512), tc=512, is_grid=is_grid)
    z4 = z.reshape(bsz, n2h, n1, -1)
    u4, ucol = z4, 0
    for o in range(HY_ORDER):
        last = o == HY_ORDER - 1
        cs = _hy_stage_a(u4, [((0, ucol // BRANCH_WIDTH), (1, ucol // BRANCH_WIDTH))], ta, ct=ct, planes=2, n2=n2)
        d = _hy_stage_b(cs, spec, o, fb, fbi, ct=ct, kg=kg)
        u4 = _hy_stage_c(d, tai, u4, ucol, z4, (o + 1) * BRANCH_WIDTH, skip[o:o + 1], ct=ct,
                         out_dtype=BF16 if last else F32)
        ucol = 0
    return u4.reshape(bsz, l, BRANCH_WIDTH)


def _reorder_w_in(w):
    d = w.shape[0]
    o = np.cumsum([0, 1024, 1024, 1024, 512, 1024, 16, 16, 1024, 1024, 512, 1024, 3072, 6144])
    seg = lambda i: w[:, int(o[i]):int(o[i + 1])]
    ff, fb, hi, dk, dv, da, db, hq, hgate, dq, dz, hy, gate = (seg(i) for i in range(13))
    pad = lambda n: jnp.zeros((d, n), w.dtype)
    out = jnp.concatenate([gate, hy, ff, fb, hi, dk, dv, da, db, pad(LANE - 32), hq, hgate, dq, dz,
                           pad(COL_PAD - COL_END)], axis=1)
    return out.astype(BF16)


def _token_mixers(u_src, sh, sc, w_r, lp, states, *, is_grid, with_output, tm):
    bsz, l, _ = u_src.shape
    tblk = min(l, 512)
    if with_output:
        h = _mm_mod(u_src, sh, sc, w_r, tm=tm, tn=IN_TN, relu2=False, out_dtype=F32)
        base = 0
    else:
        w_s = jnp.pad(w_r[:, COL_FF:COL_HQ], ((0, 0), (0, STATE_PAD - (COL_HQ - COL_FF))))
        h = _mm_mod(u_src, sh, sc, w_s, tm=tm, tn=IN_TN, relu2=False, out_dtype=F32)
        base = COL_FF
    col = lambda c: c - base
    ab_t = jnp.swapaxes(h[:, :, col(COL_AB):col(COL_AB) + 4 * HEADS], 1, 2)
    hg_f, hg_b, hg_states = _hgrn2_scan(h, (col(COL_FF), col(COL_FB), col(COL_HI), col(COL_HQ)),
                                        lp["lb_f"], lp["lb_b"], states[0], tblk=tblk, with_output=with_output)
    dn_f, dn_b, dn_states = _deltanet_scan(h, ab_t, (col(COL_DK), col(COL_DV), col(COL_DQ)),
                                           lp["dn_conv_k"], lp["dn_conv_v"], lp["dn_conv_q"],
                                           lp["dn_a_log"], lp["dn_dt_bias"], states[1],
                                           tblk=min(l, 512), with_output=with_output, is_grid=is_grid)
    new_states = (hg_states, dn_states)
    if not with_output:
        return None, new_states
    tl = min(l, 512)
    hg_out = _combine(hg_f, hg_b, h, COL_HGATE, lp["hg_norm_w"], tl=tl, use_silu=False)
    dn_out = _combine(dn_f, dn_b, h, COL_DZ, lp["dn_norm_w"], tl=tl, use_silu=True)
    hy_out = _hyena(h, lp["hy_conv_w"], lp["hy_conv_b"], lp["hy_filt"], lp["hy_skip"], is_grid=is_grid)
    ysum = _branch_merge([hy_out, hg_out, dn_out], lp["w_branch"], h, tm=min(l, 512), tn=512)
    return ysum, new_states


def kernel(x, c, ctx, c_ctx, w_ada, b_ada, w_in, hy_conv_w, hy_conv_b, hy_filt_w1, hy_filt_b1, hy_filt_w2, hy_filt_b2, hy_filt_w3, hy_skip, hg_lb_logits, hg_norm_w, dn_conv_q, dn_conv_k, dn_conv_v, dn_a_log, dn_dt_bias, dn_norm_w, w_branch, w_out, ln1_g, ln1_b, w_ff1, w_ff2, ln2_g, ln2_b):
    depth = w_in.shape[0]
    bsz, _, d = x.shape
    alpha = (2 * depth) ** 0.25
    p = jax.nn.softmax(hg_lb_logits.astype(F32), axis=1)
    lower = jnp.cumsum(p, axis=1) - p[:, :1]
    cs = jnp.concatenate([c, c_ctx[None], jnp.zeros((8 - bsz - 1, d), F32)], axis=0)
    h_ctx = ctx
    for l in range(depth):
        last = l == depth - 1
        row = lambda v: v.reshape(1, -1)
        lp = {
            "lb_f": row(lower[0, l]), "lb_b": row(lower[1, l]),
            "hg_norm_w": row(hg_norm_w[l]), "dn_norm_w": row(dn_norm_w[l]),
            "dn_conv_q": dn_conv_q[l], "dn_conv_k": dn_conv_k[l], "dn_conv_v": dn_conv_v[l],
            "dn_a_log": dn_a_log[l], "dn_dt_bias": dn_dt_bias[l],
            "hy_conv_w": hy_conv_w[l], "hy_conv_b": row(hy_conv_b[l]), "hy_skip": hy_skip[l],
            "hy_filt": (jnp.pad(hy_filt_w1[l], ((0, LANE - HY_EMB), (0, 0))), row(hy_filt_b1[l]),
                        hy_filt_w2[l], row(hy_filt_b2[l]), hy_filt_w3[l]),
            "w_branch": w_branch[l].astype(BF16),
        }
        w_r = _reorder_w_in(w_in[l])
        w_o = w_out[l].astype(BF16)
        w_1 = w_ff1[l].astype(BF16)
        w_2 = w_ff2[l].astype(BF16)
        g1n, b1n, g2n, b2n = row(ln1_g[l]), row(ln1_b[l]), row(ln2_g[l]), row(ln2_b[l])
        mod = _ada_mod(cs, w_ada[l], row(b_ada[l]))
        m_lat = [mod[:bsz, i * d:(i + 1) * d][:, None, :] for i in range(6)]
        m_ctx = [jnp.broadcast_to(mod[bsz, i * d:(i + 1) * d][None, None, :], (bsz, 1, d)) for i in range(6)]
        zero = jnp.zeros((bsz, HEADS, HEAD_DIM, HEAD_DIM), F32)
        init = ((zero, zero), (zero, zero))

        lc = h_ctx.shape[1]
        y_ctx, ctx_states = _token_mixers(h_ctx, m_ctx[0], m_ctx[1], w_r, lp, init,
                                          is_grid=False, with_output=not last, tm=lc)
        if not last:
            h_ctx = _mm_ln(y_ctx, w_o, h_ctx, m_ctx[2], g1n, b1n, tm=lc, tk=d, alpha=alpha)
            mid = _mm_mod(h_ctx, m_ctx[3], m_ctx[4], w_1, tm=lc, tn=1024, relu2=True, out_dtype=BF16)
            h_ctx = _mm_ln(mid, w_2, h_ctx, m_ctx[5], g2n, b2n, tm=lc, tk=1024, alpha=alpha)

        y, _ = _token_mixers(x, m_lat[0], m_lat[1], w_r, lp, ctx_states, is_grid=True, with_output=True, tm=1024)
        x = _mm_ln(y, w_o, x, m_lat[2], g1n, b1n, tm=512, tk=d, alpha=alpha)
        mid = _mm_mod(x, m_lat[3], m_lat[4], w_1, tm=1024, tn=1024, relu2=True, out_dtype=BF16)
        x = _mm_ln(mid, w_2, x, m_lat[5], g2n, b2n, tm=512, tk=1024, alpha=alpha)
    return x
```

```python
import functools
import math

import numpy as np
import jax
import jax.numpy as jnp
from jax import lax
from jax.experimental import pallas as pl
from jax.experimental.pallas import tpu as pltpu

F32 = jnp.float32
BF16 = jnp.bfloat16

D_MODEL = 2048
GRID_W = 64
BRANCH_WIDTH = 1024
N_BRANCH = 3
HY_ORDER = 2
HY_EMB = 33
HY_BANDS = (HY_EMB - 1) // 2
HY_HIDDEN = 64
HY_DECAY_TARGET = 1e-2
HY_FAST_DECAY = 0.3
HY_SLOW_DECAY = 1.5
HEADS = 8
HEAD_DIM = 128
DN_QK_HEADS = 4
HG_F_MIN = 1e-30
CHUNK = 64
LN_EPS = 1e-5
RMS_EPS = 1e-6
LANE = 128

COL_GATE = 0
COL_HY = 6144
COL_FF = 9216
COL_FB = 10240
COL_HI = 11264
COL_DK = 12288
COL_DV = 12800
COL_AB = 13824
COL_HQ = 13952
COL_HGATE = 14976
COL_DQ = 16000
COL_DZ = 16512
COL_END = 17536
IN_TN = 896
COL_PAD = 17920
STATE_PAD = 5376

VMEM_LIMIT = 56 * 1024 * 1024


def _cparams(sem):
    return pltpu.CompilerParams(dimension_semantics=sem, vmem_limit_bytes=VMEM_LIMIT)


def _dot(a, b):
    return jnp.dot(a.astype(BF16), b.astype(BF16), preferred_element_type=F32)


def _dot_nt(a, b):
    return lax.dot_general(a.astype(BF16), b.astype(BF16), (((1,), (1,)), ((), ())),
                           preferred_element_type=F32)


def _dot_tn(a, b):
    return lax.dot_general(a.astype(BF16), b.astype(BF16), (((0,), (0,)), ((), ())),
                           preferred_element_type=F32)


def _sigmoid(x):
    return 1.0 / (1.0 + jnp.exp(-x))


def _silu(x):
    return x * _sigmoid(x)


def _ada_kernel(c_ref, w_ref, b_ref, o_ref):
    a = _silu(c_ref[...])
    o_ref[...] = _dot(a, w_ref[...]) + b_ref[...]


def _ada_mod(cs, w, b, layer):
    _, d, n = w.shape
    tn = 1024
    return pl.pallas_call(
        _ada_kernel,
        grid=(n // tn,),
        in_specs=[pl.BlockSpec((8, d), lambda j: (0, 0)),
                  pl.BlockSpec((None, d, tn), lambda j: (layer, 0, j)),
                  pl.BlockSpec((None, 1, tn), lambda j: (layer, 0, j))],
        out_specs=pl.BlockSpec((8, tn), lambda j: (0, j)),
        out_shape=jax.ShapeDtypeStruct((8, n), F32),
        compiler_params=_cparams(("arbitrary",)),
        name="ada_mod",
    )(cs, w, b)


def _conv3(x, w, is_grid):
    t = x.shape[0]
    row = _iota2((t, 1), 0)
    if is_grid:
        pos = jnp.bitwise_and(row, GRID_W - 1)
        has_l, has_r = pos != 0, pos != GRID_W - 1
    else:
        has_l, has_r = row != 0, row != t - 1
    xl = jnp.where(has_l, pltpu.roll(x, 1, 0), 0.0)
    xr = jnp.where(has_r, pltpu.roll(x, t - 1, 0), 0.0)
    return xl * w[0:1] + x * w[1:2] + xr * w[2:3]


def _mm_mod_kernel(*refs, relu2, conv_grid):
    if conv_grid is None:
        x_ref, sh_ref, sc_ref, w_ref, o_ref, a_scr = refs
    else:
        x_ref, sh_ref, sc_ref, w_ref, cw_ref, cb_ref, o_ref, a_scr = refs

    @pl.when(pl.program_id(2) == 0)
    def _():
        a_scr[...] = (x_ref[...] * (1.0 + sc_ref[...]) + sh_ref[...]).astype(BF16)

    acc = jnp.dot(a_scr[...], w_ref[...], preferred_element_type=F32)
    if relu2:
        acc = jnp.square(jnp.maximum(acc, 0.0))
    if conv_grid is not None:
        acc = _conv3(acc, cw_ref[...], conv_grid) + cb_ref[...]
    o_ref[...] = acc.astype(o_ref.dtype)


def _mm_mod(x, sh, sc, w, layer, *, tm, tn, relu2, out_dtype, conv=None):
    bsz, l, d = x.shape
    n = w.shape[2]
    in_specs = [pl.BlockSpec((None, tm, d), lambda b, i, j: (b, i, 0)),
                pl.BlockSpec((None, 1, d), lambda b, i, j: (b, 0, 0)),
                pl.BlockSpec((None, 1, d), lambda b, i, j: (b, 0, 0)),
                pl.BlockSpec((None, d, tn), lambda b, i, j: (layer, 0, j))]
    args = [x, sh, sc, w]
    conv_grid = None
    if conv is not None:
        taps, bias, conv_grid = conv
        assert tm == l or (conv_grid and tm % GRID_W == 0)
        in_specs += [pl.BlockSpec((None, 3, tn), lambda b, i, j: (layer, 0, j)),
                     pl.BlockSpec((None, 1, tn), lambda b, i, j: (layer, 0, j))]
        args += [taps, bias]
    return pl.pallas_call(
        functools.partial(_mm_mod_kernel, relu2=relu2, conv_grid=conv_grid),
        grid=(bsz, l // tm, n // tn),
        in_specs=in_specs,
        out_specs=pl.BlockSpec((None, tm, tn), lambda b, i, j: (b, i, j)),
        out_shape=jax.ShapeDtypeStruct((bsz, l, n), out_dtype),
        scratch_shapes=[pltpu.VMEM((tm, d), BF16)],
        compiler_params=_cparams(("parallel", "parallel", "arbitrary")),
        name="mm_relu2" if relu2 else "mm_in_proj",
    )(*args)


def _mm_ln_kernel(a_ref, w_ref, x_ref, gate_ref, g_ref, b_ref, o_ref, acc_ref, *, nk, alpha):
    k = pl.program_id(2)

    @pl.when(k == 0)
    def _():
        acc_ref[...] = jnp.zeros_like(acc_ref)

    acc_ref[...] += jnp.dot(a_ref[...], w_ref[...], preferred_element_type=F32)

    @pl.when(k == nk - 1)
    def _():
        y = alpha * x_ref[...] + gate_ref[...] * acc_ref[...]
        mu = jnp.mean(y, axis=-1, keepdims=True)
        yc = y - mu
        var = jnp.mean(yc * yc, axis=-1, keepdims=True)
        o_ref[...] = yc * lax.rsqrt(var + LN_EPS) * g_ref[...] + b_ref[...]


def _mm_ln(a, w, layer, x, gate, ln_g, ln_b, *, tm, tk, alpha):
    bsz, l, kdim = a.shape
    d = w.shape[2]
    nk = kdim // tk
    return pl.pallas_call(
        functools.partial(_mm_ln_kernel, nk=nk, alpha=alpha),
        grid=(bsz, l // tm, nk),
        in_specs=[pl.BlockSpec((None, tm, tk), lambda b, i, k: (b, i, k)),
                  pl.BlockSpec((None, tk, d), lambda b, i, k: (layer, k, 0)),
                  pl.BlockSpec((None, tm, d), lambda b, i, k: (b, i, 0)),
                  pl.BlockSpec((None, 1, d), lambda b, i, k: (b, 0, 0)),
                  pl.BlockSpec((1, d), lambda b, i, k: (0, 0)),
                  pl.BlockSpec((1, d), lambda b, i, k: (0, 0))],
        out_specs=pl.BlockSpec((None, tm, d), lambda b, i, k: (b, i, 0)),
        out_shape=jax.ShapeDtypeStruct((bsz, l, d), F32),
        scratch_shapes=[pltpu.VMEM((tm, d), F32)],
        compiler_params=_cparams(("parallel", "parallel", "arbitrary")),
        name="mm_resid_ln",
    )(a, w, x, gate, ln_g, ln_b)


def _branch_kernel(b0_ref, b1_ref, b2_ref, w_ref, g0_ref, g1_ref, g2_ref, o_ref):
    acc = _sigmoid(g0_ref[...]) * jnp.dot(b0_ref[...], w_ref[0], preferred_element_type=F32)
    acc += _sigmoid(g1_ref[...]) * jnp.dot(b1_ref[...], w_ref[1], preferred_element_type=F32)
    acc += _sigmoid(g2_ref[...]) * jnp.dot(b2_ref[...], w_ref[2], preferred_element_type=F32)
    o_ref[...] = acc.astype(o_ref.dtype)


def _branch_merge(branches, w, layer, h, *, tm, tn):
    bsz, l, c = branches[0].shape
    d = w.shape[3]
    nj = d // tn
    br_spec = pl.BlockSpec((None, tm, c), lambda b, i, j: (b, i, 0))

    def gate_spec(g):
        return pl.BlockSpec((None, tm, tn), lambda b, i, j: (b, i, COL_GATE // tn + g * nj + j))

    return pl.pallas_call(
        _branch_kernel,
        grid=(bsz, l // tm, nj),
        in_specs=[br_spec, br_spec, br_spec,
                  pl.BlockSpec((None, N_BRANCH, c, tn), lambda b, i, j: (layer, 0, 0, j)),
                  gate_spec(0), gate_spec(1), gate_spec(2)],
        out_specs=pl.BlockSpec((None, tm, tn), lambda b, i, j: (b, i, j)),
        out_shape=jax.ShapeDtypeStruct((bsz, l, d), BF16),
        compiler_params=_cparams(("parallel", "parallel", "arbitrary")),
        name="branch_merge",
    )(*branches, w, h, h, h)


def _iota2(shape, axis):
    return lax.broadcasted_iota(jnp.int32, shape, axis)


def _split3(g):
    g1 = g.astype(BF16)
    r = g - g1.astype(F32)
    g2 = r.astype(BF16)
    g3 = (r - g2.astype(F32)).astype(BF16)
    return g1, g2, g3


def _cumsum_rows(g, rev):
    row = _iota2((CHUNK, 3 * CHUNK), 0)
    col = jnp.bitwise_and(_iota2((CHUNK, 3 * CHUNK), 1), CHUNK - 1)
    tri = jnp.where((col >= row) if rev else (col <= row), 1.0, 0.0).astype(BF16)
    g3 = jnp.concatenate(_split3(g), axis=0)
    return jnp.dot(tri, g3, preferred_element_type=F32)


def _hg_prepare(items):
    for it in items:
        lb = it["lb"]
        f = lb + (1.0 - lb) * _sigmoid(it["fz"])
        it["g"] = jnp.log(jnp.maximum(f, HG_F_MIN))
        it["kk"] = (1.0 - lb) * _sigmoid(-it["fz"])
    for it in items:
        it["bc"] = _cumsum_rows(it["g"], it["rev"])
    half = CHUNK // 2
    for it in items:
        bc = it["bc"]
        if it["rev"]:
            mid, last = bc[half:half + 1], bc[0:1]
        else:
            mid, last = bc[half - 1:half], bc[CHUNK - 1:CHUNK]
        it["e_last"] = jnp.exp(last)
        it["ks"] = it["kk"] * jnp.exp(last - bc)
        if it["q"] is not None:
            qs = _silu(it["q"])
            it["qd"] = qs * jnp.exp(bc - mid)
            it["kd"] = it["kk"] * jnp.exp(mid - bc)
            it["qs"] = qs * jnp.exp(bc)
    r = _iota2((CHUNK, CHUNK), 0)
    c = _iota2((CHUNK, CHUNK), 1)
    for it in items:
        it["kv"] = _dot_tn(it["v"], it["ks"])
        if it["q"] is not None:
            att = _dot_nt(it["qd"], it["kd"])
            it["att"] = jnp.where((c >= r) if it["rev"] else (c <= r), att, 0.0)
    for it in items:
        if it["q"] is not None:
            it["o_c"] = _dot(it["att"], it["v"])


def _hg_kernel(*refs, nchunk, with_output):
    if with_output:
        (ff_ref, vf_ref, qf_ref, fb_ref, vb_ref, qb_ref, lbf_ref, lbb_ref, s0f_ref, s0b_ref,
         of_ref, ob_ref, sf_ref, sb_ref, stf, stb) = refs
    else:
        (ff_ref, vf_ref, fb_ref, vb_ref, lbf_ref, lbb_ref, s0f_ref, s0b_ref,
         sf_ref, sb_ref, stf, stb) = refs
        qf_ref = qb_ref = of_ref = ob_ref = None

    @pl.when(pl.program_id(2) == 0)
    def _():
        stf[...] = s0f_ref[...]
        stb[...] = s0b_ref[...]

    def chunk_item(f_ref, v_ref, q_ref, lb_ref, ci, rev):
        sl = pl.ds(ci * CHUNK, CHUNK)
        return {"fz": f_ref[sl, :], "v": v_ref[sl, :], "q": q_ref[sl, :] if with_output else None,
                "lb": lb_ref[...], "rev": rev, "sl": sl}

    items = []
    for ci in range(nchunk):
        items.append(chunk_item(ff_ref, vf_ref, qf_ref, lbf_ref, ci, False))
        items.append(chunk_item(fb_ref, vb_ref, qb_ref, lbb_ref, nchunk - 1 - ci, True))
    _hg_prepare(items)
    state = {False: stf[...], True: stb[...]}
    for it in items:
        st = state[it["rev"]]
        if with_output:
            o_ref = ob_ref if it["rev"] else of_ref
            o_ref[it["sl"], :] = it["o_c"] + _dot_nt(it["qs"], st)
        state[it["rev"]] = st * it["e_last"] + it["kv"]
    stf[...] = state[False]
    stb[...] = state[True]
    sf_ref[...] = state[False]
    sb_ref[...] = state[True]


def _hgrn2_scan(h, cols, lb_f, lb_b, s0, *, tblk, with_output):
    bsz, l, _ = h.shape
    nt = l // tblk
    c_ff, c_fb, c_i, c_q = (c // LANE for c in cols)

    def fwd(cb):
        return pl.BlockSpec((None, tblk, LANE), lambda b, hh, t: (b, t, cb + hh))

    def bwd(cb):
        return pl.BlockSpec((None, tblk, LANE), lambda b, hh, t: (b, nt - 1 - t, cb + hh))

    lb_spec = pl.BlockSpec((1, LANE), lambda b, hh, t: (0, hh))
    st_spec = pl.BlockSpec((None, None, HEAD_DIM, HEAD_DIM), lambda b, hh, t: (b, hh, 0, 0))
    if with_output:
        in_specs = [fwd(c_ff), fwd(c_i), fwd(c_q), bwd(c_fb), bwd(c_i), bwd(c_q)]
        args = [h] * 6
    else:
        in_specs = [fwd(c_ff), fwd(c_i), bwd(c_fb), bwd(c_i)]
        args = [h] * 4
    in_specs += [lb_spec, lb_spec, st_spec, st_spec]
    args += [lb_f, lb_b, s0[0], s0[1]]
    st_shape = jax.ShapeDtypeStruct((bsz, HEADS, HEAD_DIM, HEAD_DIM), F32)
    out_specs = [st_spec, st_spec]
    out_shape = [st_shape, st_shape]
    if with_output:
        o_shape = jax.ShapeDtypeStruct((bsz, l, HEADS * HEAD_DIM), F32)
        out_specs = [pl.BlockSpec((None, tblk, LANE), lambda b, hh, t: (b, t, hh)),
                     pl.BlockSpec((None, tblk, LANE), lambda b, hh, t: (b, nt - 1 - t, hh))] + out_specs
        out_shape = [o_shape, o_shape] + out_shape
    outs = pl.pallas_call(
        functools.partial(_hg_kernel, nchunk=tblk // CHUNK, with_output=with_output),
        name="hgrn2_scan",
        grid=(bsz, HEADS, nt),
        in_specs=in_specs,
        out_specs=out_specs,
        out_shape=out_shape,
        scratch_shapes=[pltpu.VMEM((HEAD_DIM, HEAD_DIM), F32), pltpu.VMEM((HEAD_DIM, HEAD_DIM), F32)],
        compiler_params=_cparams(("parallel", "parallel", "arbitrary")),
    )(*args)
    if with_output:
        return outs[0], outs[1], (outs[2], outs[3])
    return None, None, (outs[0], outs[1])


def _l2norm(t):
    return t * lax.rsqrt(jnp.sum(t * t, axis=-1, keepdims=True) + 1e-6)


def _softplus(x):
    return jnp.maximum(x, 0.0) + jnp.log(1.0 + jnp.exp(-jnp.abs(x)))


def _dn_prepare(items):
    r = _iota2((CHUNK, CHUNK), 0)
    c = _iota2((CHUNK, CHUNK), 1)
    eye = r == c
    for it in items:
        rev = it["rev"]
        incl = (c >= r) if rev else (c <= r)
        incl_t = (r >= c) if rev else (r <= c)
        g_b = jnp.broadcast_to(it["g"], (CHUNK, CHUNK))
        gc_col = jnp.sum(jnp.where(incl, g_b, 0.0), axis=1, keepdims=True)
        g_col = jnp.sum(jnp.where(eye, g_b, 0.0), axis=1, keepdims=True)
        gc_row = jnp.sum(jnp.where(incl_t, jnp.broadcast_to(g_col, (CHUNK, CHUNK)), 0.0), axis=0, keepdims=True)
        beta_col = jnp.sum(jnp.where(eye, jnp.broadcast_to(it["beta"], (CHUNK, CHUNK)), 0.0), axis=1,
                           keepdims=True)
        it["decay"] = jnp.where(incl, jnp.exp(jnp.where(incl, gc_col - gc_row, 0.0)), 0.0)
        g_last = gc_col[0:1] if rev else gc_col[CHUNK - 1:CHUNK]
        it["e_last"] = jnp.exp(g_last)
        egc = jnp.exp(gc_col)
        kb = it["k"] * beta_col
        it["kb"] = kb
        it["rhs"] = jnp.concatenate([it["v"] * beta_col, kb * egc], axis=1)
        it["kd"] = it["k"] * jnp.exp(g_last - gc_col)
        if it["q"] is not None:
            it["qe"] = it["q"] * egc
    for it in items:
        it["kk"] = _dot_nt(it["kb"], it["k"])
        if it["q"] is not None:
            it["qk"] = _dot_nt(it["q"], it["k"]) * it["decay"]
    for it in items:
        strict = (c > r) if it["rev"] else (c < r)
        n = jnp.where(strict, it["kk"] * it["decay"], 0.0)
        it["n"] = n
        it["p"] = jnp.where(eye, 1.0, 0.0) - n
    for it in items:
        it["m"] = _dot(it["n"], it["n"])
    for i in range(5):
        for it in items:
            if i < 4:
                pm = _dot(jnp.concatenate([it["p"], it["m"]], axis=0), it["m"])
                it["p"] = it["p"] + pm[:CHUNK]
                it["m"] = pm[CHUNK:]
            else:
                it["p"] = it["p"] + _dot(it["p"], it["m"])
    for it in items:
        it["uw"] = _dot(it["p"], it["rhs"])
    for it in items:
        bm = _dot_tn(it["kd"], it["uw"])
        it["b_c"], it["m_c"] = bm[:, :HEAD_DIM], bm[:, HEAD_DIM:]
        if it["q"] is not None:
            ow = _dot(it["qk"], it["uw"])
            it["o_c"], it["q_eff"] = ow[:, :HEAD_DIM], it["qe"] - ow[:, HEAD_DIM:]


def _dn_kernel(*refs, nchunk, with_output):
    if with_output:
        (alog_ref, dtb_ref, kf_ref, vf_ref, qf_ref, abf_ref, kb_ref, vb_ref, qb_ref, abb_ref,
         s0f_ref, s0b_ref, of_ref, ob_ref, sf_ref, sb_ref, stf, stb) = refs
    else:
        (alog_ref, dtb_ref, kf_ref, vf_ref, abf_ref, kb_ref, vb_ref, abb_ref,
         s0f_ref, s0b_ref, sf_ref, sb_ref, stf, stb) = refs
        qf_ref = qb_ref = of_ref = ob_ref = None
    hh = pl.program_id(1)

    @pl.when(pl.program_id(2) == 0)
    def _():
        stf[...] = s0f_ref[...]
        stb[...] = s0b_ref[...]

    def prep(k_ref, v_ref, q_ref, ab_ref, d):
        k = _l2norm(_silu(k_ref[...]))
        v = _silu(v_ref[...])
        q = None
        if with_output:
            q = _l2norm(_silu(q_ref[...])) * (HEAD_DIM ** -0.5)
        a = ab_ref[pl.ds(d * HEADS + hh, 1), :]
        bt = ab_ref[pl.ds(2 * HEADS + d * HEADS + hh, 1), :]
        g = -jnp.exp(alog_ref[d, hh]) * _softplus(a + dtb_ref[d, hh])
        return k, v, q, g, _sigmoid(bt)

    kf, vf, qf, gf, bf = prep(kf_ref, vf_ref, qf_ref, abf_ref, 0)
    kb, vb, qb, gb, bb = prep(kb_ref, vb_ref, qb_ref, abb_ref, 1)
    def chunk_item(k, v, q, g, beta, ci, rev):
        lo = ci * CHUNK
        return {"k": k[lo:lo + CHUNK], "v": v[lo:lo + CHUNK], "q": q[lo:lo + CHUNK] if with_output else None,
                "g": g[:, lo:lo + CHUNK], "beta": beta[:, lo:lo + CHUNK], "rev": rev, "lo": lo}

    items = []
    for ci in range(nchunk):
        items.append(chunk_item(kf, vf, qf, gf, bf, ci, False))
        items.append(chunk_item(kb, vb, qb, gb, bb, nchunk - 1 - ci, True))
    _dn_prepare(items)
    state = {False: stf[...], True: stb[...]}
    for it in items:
        s = state[it["rev"]]
        if with_output:
            o_ref = ob_ref if it["rev"] else of_ref
            o_ref[pl.ds(it["lo"], CHUNK), :] = it["o_c"] + _dot(it["q_eff"], s)
        state[it["rev"]] = s * it["e_last"] + it["b_c"] - _dot(it["m_c"], s)
    stf[...] = state[False]
    stb[...] = state[True]
    sf_ref[...] = state[False]
    sb_ref[...] = state[True]


def _deltanet_scan(h, ab_t, cols, a_log, dt_bias, s0, *, tblk, with_output):
    bsz, l, _ = h.shape
    nt = l // tblk
    c_k, c_v, c_q = (c // LANE for c in cols)
    rep = HEADS // DN_QK_HEADS

    def tok(cb, shared, rev):
        def imap(b, hh, t):
            return (b, (nt - 1 - t) if rev else t, cb + (hh // rep if shared else hh))
        return pl.BlockSpec((None, tblk, LANE), imap)

    def ab_spec(rev):
        return pl.BlockSpec((None, 4 * HEADS, tblk), lambda b, hh, t: (b, 0, (nt - 1 - t) if rev else t))

    smem = pl.BlockSpec(memory_space=pltpu.SMEM)
    st_spec = pl.BlockSpec((None, None, HEAD_DIM, HEAD_DIM), lambda b, hh, t: (b, hh, 0, 0))
    if with_output:
        in_specs = [smem, smem, tok(c_k, True, False), tok(c_v, False, False), tok(c_q, True, False), ab_spec(False),
                    tok(c_k, True, True), tok(c_v, False, True), tok(c_q, True, True), ab_spec(True),
                    st_spec, st_spec]
        args = [a_log, dt_bias, h, h, h, ab_t, h, h, h, ab_t, s0[0], s0[1]]
    else:
        in_specs = [smem, smem, tok(c_k, True, False), tok(c_v, False, False), ab_spec(False),
                    tok(c_k, True, True), tok(c_v, False, True), ab_spec(True),
                    st_spec, st_spec]
        args = [a_log, dt_bias, h, h, ab_t, h, h, ab_t, s0[0], s0[1]]
    st_shape = jax.ShapeDtypeStruct((bsz, HEADS, HEAD_DIM, HEAD_DIM), F32)
    out_specs = [st_spec, st_spec]
    out_shape = [st_shape, st_shape]
    if with_output:
        o_shape = jax.ShapeDtypeStruct((bsz, l, HEADS * HEAD_DIM), F32)
        out_specs = [pl.BlockSpec((None, tblk, LANE), lambda b, hh, t: (b, t, hh)),
                     pl.BlockSpec((None, tblk, LANE), lambda b, hh, t: (b, nt - 1 - t, hh))] + out_specs
        out_shape = [o_shape, o_shape] + out_shape
    outs = pl.pallas_call(
        functools.partial(_dn_kernel, nchunk=tblk // CHUNK, with_output=with_output),
        name="deltanet_scan",
        grid=(bsz, HEADS, nt),
        in_specs=in_specs,
        out_specs=out_specs,
        out_shape=out_shape,
        scratch_shapes=[pltpu.VMEM((HEAD_DIM, HEAD_DIM), F32), pltpu.VMEM((HEAD_DIM, HEAD_DIM), F32)],
        compiler_params=_cparams(("parallel", "parallel", "arbitrary")),
    )(*args)
    if with_output:
        return outs[0], outs[1], (outs[2], outs[3])
    return None, None, (outs[0], outs[1])


def _combine_kernel(of_ref, ob_ref, z_ref, w_ref, o_ref, *, use_silu):
    o = of_ref[...] + ob_ref[...]
    o = o * lax.rsqrt(jnp.mean(o * o, axis=-1, keepdims=True) + RMS_EPS) * w_ref[...]
    z = z_ref[...]
    o_ref[...] = (o * (_silu(z) if use_silu else _sigmoid(z))).astype(o_ref.dtype)


def _combine(o_f, o_b, h, col_z, norm_w, *, tl, use_silu):
    bsz, l, c = o_f.shape
    cz = col_z // LANE
    spec = pl.BlockSpec((None, tl, LANE), lambda b, i, hh: (b, i, hh))
    return pl.pallas_call(
        functools.partial(_combine_kernel, use_silu=use_silu),
        name="mixer_norm_gate",
        grid=(bsz, l // tl, c // LANE),
        in_specs=[spec, spec,
                  pl.BlockSpec((None, tl, LANE), lambda b, i, hh: (b, i, cz + hh)),
                  pl.BlockSpec((1, LANE), lambda b, i, hh: (0, 0))],
        out_specs=spec,
        out_shape=jax.ShapeDtypeStruct((bsz, l, c), BF16),
        compiler_params=_cparams(("parallel", "parallel", "arbitrary")),
    )(o_f, o_b, h, norm_w)


def _dot_f32(a, b):
    return jnp.dot(a, b, precision=lax.Precision.HIGHEST, preferred_element_type=F32)


def _hy_filt_kernel(w1_ref, b1_ref, w2_ref, b2_ref, w3_ref, o_ref, hid_ref, *, l_total, tl):
    i = pl.program_id(0)
    j = pl.program_id(1)

    @pl.when(j == 0)
    def _():
        row = (_iota2((tl, LANE), 0) + i * tl).astype(F32)
        lane = _iota2((tl, LANE), 1)
        t = row * (1.0 / (l_total - 1))
        ang = row * (2.0 * math.pi / l_total)
        bidx = jnp.where(lane <= HY_BANDS, lane - 1, lane - 1 - HY_BANDS).astype(F32)
        band = 1e-4 + bidx * ((HY_BANDS - 1 - 1e-4) / (HY_BANDS - 1))
        arg = band * ang
        feats = jnp.where(lane == 0, t,
                          jnp.where(lane <= HY_BANDS, jnp.cos(arg),
                                    jnp.where(lane <= 2 * HY_BANDS, -jnp.sin(arg), 0.0)))
        hid = jnp.sin(_dot_f32(feats, w1_ref[...]) + b1_ref[...])
        hid_ref[...] = jnp.sin(_dot_f32(hid, w2_ref[...]) + b2_ref[...])

    hcol = _dot_f32(hid_ref[...], w3_ref[...])
    width = hcol.shape[1]
    ch = _iota2((tl, width), 1).astype(F32)
    lo = math.log(HY_DECAY_TARGET) / HY_SLOW_DECAY
    hi = math.log(HY_DECAY_TARGET) / HY_FAST_DECAY
    delta = jnp.abs(lo + ch * ((hi - lo) / (width - 1)))
    trow = (_iota2((tl, width), 0) + i * tl)
    out = hcol * jnp.exp(-(trow.astype(F32) * (1.0 / (l_total - 1))) * delta)
    drop = jnp.logical_and(trow == 0, jnp.bitwise_and(j, 1) == 1)
    o_ref[...] = jnp.where(drop, 0.0, out)


def _hy_filters(l_total, w1p, b1, w2, b2, w3):
    tl = min(l_total, 512)
    n = w3.shape[1]
    full = lambda a: pl.BlockSpec(a.shape, lambda i, j: (0,) * a.ndim)
    return pl.pallas_call(
        functools.partial(_hy_filt_kernel, l_total=l_total, tl=tl),
        name="hy_filters",
        grid=(l_total // tl, n // BRANCH_WIDTH),
        in_specs=[full(w1p), full(b1), full(w2), full(b2),
                  pl.BlockSpec((HY_HIDDEN, BRANCH_WIDTH), lambda i, j: (0, j))],
        out_specs=pl.BlockSpec((tl, BRANCH_WIDTH), lambda i, j: (i, j)),
        out_shape=jax.ShapeDtypeStruct((l_total, n), F32),
        scratch_shapes=[pltpu.VMEM((tl, HY_HIDDEN), F32)],
        compiler_params=_cparams(("parallel", "arbitrary")),
    )(w1p, b1, w2, b2, w3)


@functools.lru_cache(maxsize=None)
def _fft_tables(n1, n2):
    n = n1 * n2
    n2h = n2 // 2
    k2 = np.arange(n2, dtype=np.int64)[:, None]
    m = np.arange(n2h, dtype=np.int64)[None, :]
    ta = np.zeros((n1, 2 * n2, 2 * n2h))
    tak = np.zeros((n1, 4 * n2, 2 * n2h))
    for a in range(n1):
        th = 2.0 * np.pi * ((k2 * (a + n1 * m)) % n) / n
        c, s = np.cos(th), np.sin(th)
        ta[a] = np.block([[c, s], [-s, c]])
        z = np.zeros_like(c)
        tak[a] = np.block([[c, z], [-s, z], [z, c], [z, -s]])
    tai = np.transpose(ta, (0, 2, 1)) / n
    k1 = np.arange(n1, dtype=np.int64)
    ph = 2.0 * np.pi * ((k1[:, None] * k1[None, :]) % n1) / n1
    c, s = np.cos(ph), np.sin(ph)
    fb = np.block([[c, s], [-s, c]])
    return tuple(np.asarray(t, np.float32) for t in (ta, tak, tai, fb, fb.T))


def _fft_split(l_total):
    n = 2 * l_total
    n1 = min(128, l_total // 8)
    return n1, n // n1


def _hy_stage_a_kernel(gr_ref, cr_ref, gi_ref, ci_ref, xr_ref, xi_ref, t_ref, o_ref, *, planes, n2):
    del gr_ref, cr_ref, gi_ref, ci_ref
    for j in range(8):
        x = jnp.concatenate([xr_ref[:, j, :], xi_ref[:, j, :]], axis=0).astype(BF16)
        y = jnp.dot(t_ref[j], x, preferred_element_type=F32)
        for p in range(planes):
            o_ref[p, :, j, :] = y[p * n2:(p + 1) * n2].astype(o_ref.dtype)


def _hy_stage_a(x4, pairs, tab, *, ct, planes, n2):
    _, n2h, n1, _ = x4.shape
    npair = len(pairs)
    nct = BRANCH_WIDTH // ct
    gr = jnp.asarray([p[0][0] for p in pairs], jnp.int32)
    cr = jnp.asarray([p[0][1] for p in pairs], jnp.int32)
    gi = jnp.asarray([p[1][0] for p in pairs], jnp.int32)
    ci = jnp.asarray([p[1][1] for p in pairs], jnp.int32)
    grid_spec = pltpu.PrefetchScalarGridSpec(
        num_scalar_prefetch=4,
        grid=(npair, nct, n1 // 8),
        in_specs=[pl.BlockSpec((None, n2h, 8, ct), lambda p, c, g, gr, cr, gi, ci: (gr[p], 0, g, cr[p] * nct + c)),
                  pl.BlockSpec((None, n2h, 8, ct), lambda p, c, g, gr, cr, gi, ci: (gi[p], 0, g, ci[p] * nct + c)),
                  pl.BlockSpec((8, planes * n2, 2 * n2h), lambda p, c, g, *_: (g, 0, 0))],
        out_specs=pl.BlockSpec((None, planes, n2, 8, ct), lambda p, c, g, *_: (p, 0, 0, g, c)),
    )
    return pl.pallas_call(
        functools.partial(_hy_stage_a_kernel, planes=planes, n2=n2),
        name="hy_stage_a",
        grid_spec=grid_spec,
        out_shape=jax.ShapeDtypeStruct((npair, planes, n2, n1, BRANCH_WIDTH), BF16),
        compiler_params=_cparams(("parallel", "parallel", "arbitrary")),
    )(gr, cr, gi, ci, x4, x4, tab)


def _hy_spec_kernel(c_ref, fb_ref, o_ref, *, kg, n1):
    for k in range(kg):
        zf = jnp.dot(fb_ref[...], jnp.concatenate([c_ref[0, k], c_ref[1, k]], axis=0), preferred_element_type=F32)
        zb = jnp.dot(fb_ref[...], jnp.concatenate([c_ref[2, k], c_ref[3, k]], axis=0), preferred_element_type=F32)
        o_ref[0, k] = zf[:n1] + zb[:n1]
        o_ref[1, k] = zf[n1:] - zb[n1:]


def _hy_spectrum(cs, fb, *, ct, kg):
    no, _, n2, n1, c = cs.shape
    return pl.pallas_call(
        functools.partial(_hy_spec_kernel, kg=kg, n1=n1),
        name="hy_spectrum",
        grid=(no, c // ct, n2 // kg),
        in_specs=[pl.BlockSpec((None, 4, kg, n1, ct), lambda o, j, g: (o, 0, g, 0, j)),
                  pl.BlockSpec(fb.shape, lambda o, j, g: (0, 0))],
        out_specs=pl.BlockSpec((None, 2, kg, n1, ct), lambda o, j, g: (o, 0, g, 0, j)),
        out_shape=jax.ShapeDtypeStruct((no, 2, n2, n1, c), F32),
        compiler_params=_cparams(("parallel", "parallel", "arbitrary")),
    )(cs, fb)


def _hy_stage_b_kernel(c_ref, k_ref, fb_ref, fbi_ref, o_ref, *, kg, n1):
    for k in range(kg):
        z = jnp.dot(fb_ref[...], jnp.concatenate([c_ref[0, k], c_ref[1, k]], axis=0), preferred_element_type=F32)
        zr, zi = z[:n1], z[n1:]
        kr, ki = k_ref[0, k], k_ref[1, k]
        y = jnp.concatenate([zr * kr - zi * ki, zr * ki + zi * kr], axis=0).astype(BF16)
        d = jnp.dot(fbi_ref[...], y, preferred_element_type=F32)
        o_ref[0, k] = d[:n1].astype(o_ref.dtype)
        o_ref[1, k] = d[n1:].astype(o_ref.dtype)


def _hy_stage_b(cs, spec, order, fb, fbi, *, ct, kg):
    _, _, n2, n1, c = cs.shape
    return pl.pallas_call(
        functools.partial(_hy_stage_b_kernel, kg=kg, n1=n1),
        name="hy_stage_b",
        grid=(c // ct, n2 // kg),
        in_specs=[pl.BlockSpec((None, 2, kg, n1, ct), lambda j, g: (0, 0, g, 0, j)),
                  pl.BlockSpec((None, 2, kg, n1, ct), lambda j, g: (order, 0, g, 0, j)),
                  pl.BlockSpec(fb.shape, lambda j, g: (0, 0)),
                  pl.BlockSpec(fbi.shape, lambda j, g: (0, 0))],
        out_specs=pl.BlockSpec((2, kg, n1, ct), lambda j, g: (0, g, 0, j)),
        out_shape=jax.ShapeDtypeStruct((2, n2, n1, c), BF16),
        compiler_params=_cparams(("parallel", "arbitrary")),
    )(cs, spec, fb, fbi)


def _hy_stage_c_kernel(d_ref, t_ref, u_ref, g_ref, skip_ref, o_ref, *, n2h):
    skip = skip_ref[...]
    for j in range(8):
        d = jnp.concatenate([d_ref[0, :, j, :], d_ref[1, :, j, :]], axis=0)
        y = jnp.dot(t_ref[j], d, preferred_element_type=F32)
        for b in range(2):
            yb = y[b * n2h:(b + 1) * n2h]
            o_ref[b, :, j, :] = (g_ref[b, :, j, :] * (yb + skip * u_ref[b, :, j, :])).astype(o_ref.dtype)


def _hy_stage_c(d, tai, u4, ucol, z4, gcol, skip, *, ct, out_dtype):
    _, n2, n1, c = d.shape
    n2h = n2 // 2
    uc, gc = ucol // ct, gcol // ct
    return pl.pallas_call(
        functools.partial(_hy_stage_c_kernel, n2h=n2h),
        name="hy_stage_c",
        grid=(c // ct, n1 // 8),
        in_specs=[pl.BlockSpec((2, n2, 8, ct), lambda j, g: (0, 0, g, j)),
                  pl.BlockSpec((8, 2 * n2h, 2 * n2), lambda j, g: (g, 0, 0)),
                  pl.BlockSpec((2, n2h, 8, ct), lambda j, g: (0, 0, g, uc + j)),
                  pl.BlockSpec((2, n2h, 8, ct), lambda j, g: (0, 0, g, gc + j)),
                  pl.BlockSpec((1, ct), lambda j, g: (0, j))],
        out_specs=pl.BlockSpec((2, n2h, 8, ct), lambda j, g: (0, 0, g, j)),
        out_shape=jax.ShapeDtypeStruct((2, n2h, n1, c), out_dtype),
        compiler_params=_cparams(("parallel", "arbitrary")),
    )(d, tai, u4, z4, skip)


def _hyena(h, filt_params, skip):
    bsz, l, _ = h.shape
    assert bsz == 2
    n1, n2 = _fft_split(l)
    n2h = n2 // 2
    ta, tak, tai, fb, fbi = (jnp.asarray(t).astype(BF16) for t in _fft_tables(n1, n2))
    ct = 512
    kg = 8

    hfilt = _hy_filters(l, *filt_params)
    cs_k = _hy_stage_a(hfilt.reshape(1, n2h, n1, -1), [((0, 2 * o), (0, 2 * o + 1)) for o in range(HY_ORDER)],
                       tak, ct=ct, planes=4, n2=n2)
    spec = _hy_spectrum(cs_k, fb, ct=ct, kg=kg)

    z4 = h.reshape(bsz, n2h, n1, -1)
    u4, ucol = z4, COL_HY
    for o in range(HY_ORDER):
        last = o == HY_ORDER - 1
        cs = _hy_stage_a(u4, [((0, ucol // BRANCH_WIDTH), (1, ucol // BRANCH_WIDTH))], ta, ct=ct, planes=2, n2=n2)
        d = _hy_stage_b(cs, spec, o, fb, fbi, ct=ct, kg=kg)
        u4 = _hy_stage_c(d, tai, u4, ucol, z4, COL_HY + (o + 1) * BRANCH_WIDTH, skip[o:o + 1], ct=ct,
                         out_dtype=BF16 if last else F32)
        ucol = 0
    return u4.reshape(bsz, l, BRANCH_WIDTH)


def _reorder_w_in(w):
    o = np.cumsum([0, 1024, 1024, 1024, 512, 1024, 16, 16, 1024, 1024, 512, 1024, 3072, 6144])
    seg = lambda i: w[:, :, int(o[i]):int(o[i + 1])]
    ff, fb, hi, dk, dv, da, db, hq, hgate, dq, dz, hy, gate = (seg(i) for i in range(13))
    pad = lambda n: jnp.zeros(w.shape[:2] + (n,), w.dtype)
    out = jnp.concatenate([gate, hy, ff, fb, hi, dk, dv, da, db, pad(LANE - 32), hq, hgate, dq, dz,
                           pad(COL_PAD - COL_END)], axis=2)
    return out.astype(BF16)


def _conv_tables(hy_conv_w, hy_conv_b, dn_conv_q, dn_conv_k, dn_conv_v):
    depth = hy_conv_w.shape[0]
    ident = lambda n: jnp.broadcast_to(jnp.asarray([0.0, 1.0, 0.0], F32)[None, :, None], (depth, 3, n))
    taps = jnp.concatenate([ident(COL_HY), hy_conv_w, ident(COL_DK - COL_FF), dn_conv_k, dn_conv_v,
                            ident(COL_DQ - COL_AB), dn_conv_q, ident(COL_PAD - COL_DZ)], axis=2)
    bias = jnp.concatenate([jnp.zeros((depth, 1, COL_HY), F32), hy_conv_b[:, None, :],
                            jnp.zeros((depth, 1, COL_PAD - COL_FF), F32)], axis=2)
    return taps, bias


def _token_mixers(u_src, sh, sc, w_r, conv_t, layer, lp, states, *, is_grid, with_output, tm):
    bsz, l, _ = u_src.shape
    tblk = min(l, 512)
    taps, bias = conv_t
    if with_output:
        h = _mm_mod(u_src, sh, sc, w_r, layer, tm=tm, tn=IN_TN, relu2=False, out_dtype=F32,
                    conv=(taps, bias, is_grid))
        base = 0
    else:
        cut = lambda a: jnp.pad(a[layer:layer + 1, :, COL_FF:COL_HQ],
                                ((0, 0), (0, 0), (0, STATE_PAD - (COL_HQ - COL_FF))))
        h = _mm_mod(u_src, sh, sc, cut(w_r), 0, tm=tm, tn=IN_TN, relu2=False, out_dtype=F32,
                    conv=(cut(taps), cut(bias), is_grid))
        base = COL_FF
    col = lambda c: c - base
    ab_t = jnp.swapaxes(h[:, :, col(COL_AB):col(COL_AB) + 4 * HEADS], 1, 2)
    hg_f, hg_b, hg_states = _hgrn2_scan(h, (col(COL_FF), col(COL_FB), col(COL_HI), col(COL_HQ)),
                                        lp["lb_f"], lp["lb_b"], states[0], tblk=tblk, with_output=with_output)
    dn_f, dn_b, dn_states = _deltanet_scan(h, ab_t, (col(COL_DK), col(COL_DV), col(COL_DQ)),
                                           lp["dn_a_log"], lp["dn_dt_bias"], states[1],
                                           tblk=tblk, with_output=with_output)
    new_states = (hg_states, dn_states)
    if not with_output:
        return None, new_states
    tl = min(l, 2048)
    hg_out = _combine(hg_f, hg_b, h, COL_HGATE, lp["hg_norm_w"], tl=tl, use_silu=False)
    dn_out = _combine(dn_f, dn_b, h, COL_DZ, lp["dn_norm_w"], tl=tl, use_silu=True)
    hy_out = _hyena(h, lp["hy_filt"], lp["hy_skip"])
    ysum = _branch_merge([hy_out, hg_out, dn_out], lp["w_branch"], layer, h, tm=min(l, 512), tn=512)
    return ysum, new_states


def kernel(x, c, ctx, c_ctx, w_ada, b_ada, w_in, hy_conv_w, hy_conv_b, hy_filt_w1, hy_filt_b1, hy_filt_w2, hy_filt_b2, hy_filt_w3, hy_skip, hg_lb_logits, hg_norm_w, dn_conv_q, dn_conv_k, dn_conv_v, dn_a_log, dn_dt_bias, dn_norm_w, w_branch, w_out, ln1_g, ln1_b, w_ff1, w_ff2, ln2_g, ln2_b):
    depth = w_in.shape[0]
    bsz, _, d = x.shape
    alpha = (2 * depth) ** 0.25
    p = jax.nn.softmax(hg_lb_logits.astype(F32), axis=1)
    lower = jnp.cumsum(p, axis=1) - p[:, :1]
    cs = jnp.concatenate([c, c_ctx[None], jnp.zeros((8 - bsz - 1, d), F32)], axis=0)
    w_r = _reorder_w_in(w_in)
    w_b = w_branch.astype(BF16)
    w_o = w_out.astype(BF16)
    w_1 = w_ff1.astype(BF16)
    w_2 = w_ff2.astype(BF16)
    b_ada3 = b_ada[:, None, :]
    conv_t = _conv_tables(hy_conv_w, hy_conv_b, dn_conv_q, dn_conv_k, dn_conv_v)
    h_ctx = ctx
    for l in range(depth):
        last = l == depth - 1
        row = lambda v: v.reshape(1, -1)
        lp = {
            "lb_f": row(lower[0, l]), "lb_b": row(lower[1, l]),
            "hg_norm_w": row(hg_norm_w[l]), "dn_norm_w": row(dn_norm_w[l]),
            "dn_a_log": dn_a_log[l], "dn_dt_bias": dn_dt_bias[l], "hy_skip": hy_skip[l],
            "hy_filt": (jnp.pad(hy_filt_w1[l], ((0, LANE - HY_EMB), (0, 0))), row(hy_filt_b1[l]),
                        hy_filt_w2[l], row(hy_filt_b2[l]), hy_filt_w3[l]),
            "w_branch": w_b,
        }
        g1n, b1n, g2n, b2n = row(ln1_g[l]), row(ln1_b[l]), row(ln2_g[l]), row(ln2_b[l])
        mod = _ada_mod(cs, w_ada, b_ada3, l)
        m_lat = [mod[:bsz, i * d:(i + 1) * d][:, None, :] for i in range(6)]
        m_ctx = [jnp.broadcast_to(mod[bsz, i * d:(i + 1) * d][None, None, :], (bsz, 1, d)) for i in range(6)]
        zero = jnp.zeros((bsz, HEADS, HEAD_DIM, HEAD_DIM), F32)
        init = ((zero, zero), (zero, zero))

        lc = h_ctx.shape[1]
        y_ctx, ctx_states = _token_mixers(h_ctx, m_ctx[0], m_ctx[1], w_r, conv_t, l, lp, init,
                                          is_grid=False, with_output=not last, tm=lc)
        if not last:
            h_ctx = _mm_ln(y_ctx, w_o, l, h_ctx, m_ctx[2], g1n, b1n, tm=lc, tk=d, alpha=alpha)
            mid = _mm_mod(h_ctx, m_ctx[3], m_ctx[4], w_1, l, tm=lc, tn=1024, relu2=True, out_dtype=BF16)
            h_ctx = _mm_ln(mid, w_2, l, h_ctx, m_ctx[5], g2n, b2n, tm=lc, tk=1024, alpha=alpha)

        lx = x.shape[1]
        y, _ = _token_mixers(x, m_lat[0], m_lat[1], w_r, conv_t, l, lp, ctx_states,
                             is_grid=True, with_output=True, tm=min(lx, 1024))
        x = _mm_ln(y, w_o, l, x, m_lat[2], g1n, b1n, tm=min(lx, 512), tk=d, alpha=alpha)
        mid = _mm_mod(x, m_lat[3], m_lat[4], w_1, l, tm=min(lx, 1024), tn=1024, relu2=True, out_dtype=BF16)
        x = _mm_ln(mid, w_2, l, x, m_lat[5], g2n, b2n, tm=min(lx, 512), tk=1024, alpha=alpha)
    return x
```

```python
import functools
import math

import numpy as np
import jax
import jax.numpy as jnp
from jax import lax
from jax.experimental import pallas as pl
from jax.experimental.pallas import tpu as pltpu

F32 = jnp.float32
BF16 = jnp.bfloat16

D_MODEL = 2048
GRID_W = 64
BRANCH_WIDTH = 1024
N_BRANCH = 3
HY_ORDER = 2
HY_EMB = 33
HY_BANDS = (HY_EMB - 1) // 2
HY_HIDDEN = 64
HY_DECAY_TARGET = 1e-2
HY_FAST_DECAY = 0.3
HY_SLOW_DECAY = 1.5
HEADS = 8
HEAD_DIM = 128
DN_QK_HEADS = 4
HG_F_MIN = 1e-30
CHUNK = 64
LN_EPS = 1e-5
RMS_EPS = 1e-6
LANE = 128

COL_GATE = 0
COL_HY = 6144
COL_FF = 9216
COL_FB = 10240
COL_HI = 11264
COL_DK = 12288
COL_DV = 12800
COL_HQ = 13824
COL_HGATE = 14848
COL_DQ = 15872
COL_DZ = 16384
COL_AB = 17408
COL_END = 17536
IN_TN = 896
COL_PAD = 17920
STATE_PAD = 5376
RELAYOUT_TN = 512
SRC_COLS = ((COL_GATE, 11296, 6144), (COL_HY, 8224, 3072), (COL_FF, 0, 1024), (COL_FB, 1024, 1024),
            (COL_HI, 2048, 1024), (COL_DK, 3072, 512), (COL_DV, 3584, 1024), (COL_HQ, 4640, 1024),
            (COL_HGATE, 5664, 1024), (COL_DQ, 6688, 512), (COL_DZ, 7200, 1024), (COL_AB, 4608, 32))
AB_WIDTH = 32

VMEM_LIMIT = 56 * 1024 * 1024


def _cparams(sem):
    return pltpu.CompilerParams(dimension_semantics=sem, vmem_limit_bytes=VMEM_LIMIT)


def _dot(a, b):
    return jnp.dot(a.astype(BF16), b.astype(BF16), preferred_element_type=F32)


def _dot_nt(a, b):
    return lax.dot_general(a.astype(BF16), b.astype(BF16), (((1,), (1,)), ((), ())),
                           preferred_element_type=F32)


def _dot_tn(a, b):
    return lax.dot_general(a.astype(BF16), b.astype(BF16), (((0,), (0,)), ((), ())),
                           preferred_element_type=F32)


def _sigmoid(x):
    return 1.0 / (1.0 + jnp.exp(-x))


def _silu(x):
    return x * _sigmoid(x)


def _ada_kernel(c_ref, w_ref, b_ref, o_ref):
    a = _silu(c_ref[...])
    o_ref[...] = _dot(a, w_ref[...]) + b_ref[...]


def _ada_mod(cs, w, b, layer):
    _, d, n = w.shape
    tn = 1024
    return pl.pallas_call(
        _ada_kernel,
        grid=(n // tn,),
        in_specs=[pl.BlockSpec((8, d), lambda j: (0, 0)),
                  pl.BlockSpec((None, d, tn), lambda j: (layer, 0, j)),
                  pl.BlockSpec((None, 1, tn), lambda j: (layer, 0, j))],
        out_specs=pl.BlockSpec((8, tn), lambda j: (0, j)),
        out_shape=jax.ShapeDtypeStruct((8, n), F32),
        compiler_params=_cparams(("arbitrary",)),
        name="ada_mod",
    )(cs, w, b)


def _conv3(x, w, is_grid):
    t = x.shape[0]
    row = _iota2((t, 1), 0)
    if is_grid:
        pos = jnp.bitwise_and(row, GRID_W - 1)
        has_l, has_r = pos != 0, pos != GRID_W - 1
    else:
        has_l, has_r = row != 0, row != t - 1
    xl = jnp.where(has_l, pltpu.roll(x, 1, 0), 0.0)
    xr = jnp.where(has_r, pltpu.roll(x, t - 1, 0), 0.0)
    return xl * w[0:1] + x * w[1:2] + xr * w[2:3]


def _mm_mod_kernel(*refs, relu2, conv_grid):
    if conv_grid is None:
        x_ref, sh_ref, sc_ref, w_ref, o_ref, a_scr = refs
    else:
        x_ref, sh_ref, sc_ref, w_ref, cw_ref, cb_ref, o_ref, a_scr = refs

    @pl.when(pl.program_id(2) == 0)
    def _():
        a_scr[...] = (x_ref[...] * (1.0 + sc_ref[...]) + sh_ref[...]).astype(BF16)

    acc = jnp.dot(a_scr[...], w_ref[...], preferred_element_type=F32)
    if relu2:
        acc = jnp.square(jnp.maximum(acc, 0.0))
    if conv_grid is not None:
        acc = _conv3(acc, cw_ref[...], conv_grid) + cb_ref[...]
    o_ref[...] = acc.astype(o_ref.dtype)


def _mm_mod(x, sh, sc, w, layer, *, tm, tn, relu2, out_dtype, conv=None):
    bsz, l, d = x.shape
    n = w.shape[2]
    in_specs = [pl.BlockSpec((None, tm, d), lambda b, i, j: (b, i, 0)),
                pl.BlockSpec((None, 1, d), lambda b, i, j: (b, 0, 0)),
                pl.BlockSpec((None, 1, d), lambda b, i, j: (b, 0, 0)),
                pl.BlockSpec((None, d, tn), lambda b, i, j: (layer, 0, j))]
    args = [x, sh, sc, w]
    conv_grid = None
    if conv is not None:
        taps, bias, conv_grid = conv
        assert tm == l or (conv_grid and tm % GRID_W == 0)
        in_specs += [pl.BlockSpec((None, 3, tn), lambda b, i, j: (layer, 0, j)),
                     pl.BlockSpec((None, 1, tn), lambda b, i, j: (layer, 0, j))]
        args += [taps, bias]
    return pl.pallas_call(
        functools.partial(_mm_mod_kernel, relu2=relu2, conv_grid=conv_grid),
        grid=(bsz, l // tm, n // tn),
        in_specs=in_specs,
        out_specs=pl.BlockSpec((None, tm, tn), lambda b, i, j: (b, i, j)),
        out_shape=jax.ShapeDtypeStruct((bsz, l, n), out_dtype),
        scratch_shapes=[pltpu.VMEM((tm, d), BF16)],
        compiler_params=_cparams(("parallel", "parallel", "arbitrary")),
        name="mm_relu2" if relu2 else "mm_in_proj",
    )(*args)


def _mm_ln_kernel(a_ref, w_ref, x_ref, gate_ref, g_ref, b_ref, o_ref, acc_ref, *, nk, alpha):
    k = pl.program_id(2)

    @pl.when(k == 0)
    def _():
        acc_ref[...] = jnp.zeros_like(acc_ref)

    acc_ref[...] += jnp.dot(a_ref[...], w_ref[...], preferred_element_type=F32)

    @pl.when(k == nk - 1)
    def _():
        y = alpha * x_ref[...] + gate_ref[...] * acc_ref[...]
        mu = jnp.mean(y, axis=-1, keepdims=True)
        yc = y - mu
        var = jnp.mean(yc * yc, axis=-1, keepdims=True)
        o_ref[...] = yc * lax.rsqrt(var + LN_EPS) * g_ref[...] + b_ref[...]


def _mm_ln(a, w, layer, x, gate, ln_g, ln_b, *, tm, tk, alpha):
    bsz, l, kdim = a.shape
    d = w.shape[2]
    nk = kdim // tk
    return pl.pallas_call(
        functools.partial(_mm_ln_kernel, nk=nk, alpha=alpha),
        grid=(bsz, l // tm, nk),
        in_specs=[pl.BlockSpec((None, tm, tk), lambda b, i, k: (b, i, k)),
                  pl.BlockSpec((None, tk, d), lambda b, i, k: (layer, k, 0)),
                  pl.BlockSpec((None, tm, d), lambda b, i, k: (b, i, 0)),
                  pl.BlockSpec((None, 1, d), lambda b, i, k: (b, 0, 0)),
                  pl.BlockSpec((1, d), lambda b, i, k: (0, 0)),
                  pl.BlockSpec((1, d), lambda b, i, k: (0, 0))],
        out_specs=pl.BlockSpec((None, tm, d), lambda b, i, k: (b, i, 0)),
        out_shape=jax.ShapeDtypeStruct((bsz, l, d), F32),
        scratch_shapes=[pltpu.VMEM((tm, d), F32)],
        compiler_params=_cparams(("parallel", "parallel", "arbitrary")),
        name="mm_resid_ln",
    )(a, w, x, gate, ln_g, ln_b)


def _branch_kernel(b0_ref, b1_ref, b2_ref, w_ref, g0_ref, g1_ref, g2_ref, o_ref):
    acc = _sigmoid(g0_ref[...]) * jnp.dot(b0_ref[...], w_ref[0], preferred_element_type=F32)
    acc += _sigmoid(g1_ref[...]) * jnp.dot(b1_ref[...], w_ref[1], preferred_element_type=F32)
    acc += _sigmoid(g2_ref[...]) * jnp.dot(b2_ref[...], w_ref[2], preferred_element_type=F32)
    o_ref[...] = acc.astype(o_ref.dtype)


def _branch_merge(branches, w, layer, h, *, tm, tn):
    bsz, l, c = branches[0].shape
    d = w.shape[3]
    nj = d // tn
    br_spec = pl.BlockSpec((None, tm, c), lambda b, i, j: (b, i, 0))

    def gate_spec(g):
        return pl.BlockSpec((None, tm, tn), lambda b, i, j: (b, i, COL_GATE // tn + g * nj + j))

    return pl.pallas_call(
        _branch_kernel,
        grid=(bsz, l // tm, nj),
        in_specs=[br_spec, br_spec, br_spec,
                  pl.BlockSpec((None, N_BRANCH, c, tn), lambda b, i, j: (layer, 0, 0, j)),
                  gate_spec(0), gate_spec(1), gate_spec(2)],
        out_specs=pl.BlockSpec((None, tm, tn), lambda b, i, j: (b, i, j)),
        out_shape=jax.ShapeDtypeStruct((bsz, l, d), BF16),
        compiler_params=_cparams(("parallel", "parallel", "arbitrary")),
        name="branch_merge",
    )(*branches, w, h, h, h)


def _iota2(shape, axis):
    return lax.broadcasted_iota(jnp.int32, shape, axis)


def _split3(g):
    g1 = g.astype(BF16)
    r = g - g1.astype(F32)
    g2 = r.astype(BF16)
    g3 = (r - g2.astype(F32)).astype(BF16)
    return g1, g2, g3


def _cumsum_rows(g, rev):
    row = _iota2((CHUNK, 3 * CHUNK), 0)
    col = jnp.bitwise_and(_iota2((CHUNK, 3 * CHUNK), 1), CHUNK - 1)
    tri = jnp.where((col >= row) if rev else (col <= row), 1.0, 0.0).astype(BF16)
    g3 = jnp.concatenate(_split3(g), axis=0)
    return jnp.dot(tri, g3, preferred_element_type=F32)


def _hg_prepare(items):
    for it in items:
        lb = it["lb"]
        f = lb + (1.0 - lb) * _sigmoid(it["fz"])
        it["g"] = jnp.log(jnp.maximum(f, HG_F_MIN))
        it["kk"] = (1.0 - lb) * _sigmoid(-it["fz"])
    for it in items:
        it["bc"] = _cumsum_rows(it["g"], it["rev"])
    half = CHUNK // 2
    for it in items:
        bc = it["bc"]
        if it["rev"]:
            mid, last = bc[half:half + 1], bc[0:1]
        else:
            mid, last = bc[half - 1:half], bc[CHUNK - 1:CHUNK]
        it["e_last"] = jnp.exp(last)
        it["ks"] = it["kk"] * jnp.exp(last - bc)
        if it["q"] is not None:
            qs = _silu(it["q"])
            it["qd"] = qs * jnp.exp(bc - mid)
            it["kd"] = it["kk"] * jnp.exp(mid - bc)
            it["qs"] = qs * jnp.exp(bc)
    r = _iota2((CHUNK, CHUNK), 0)
    c = _iota2((CHUNK, CHUNK), 1)
    for it in items:
        it["kv"] = _dot_tn(it["v"], it["ks"])
        if it["q"] is not None:
            att = _dot_nt(it["qd"], it["kd"])
            it["att"] = jnp.where((c >= r) if it["rev"] else (c <= r), att, 0.0)
    for it in items:
        if it["q"] is not None:
            it["o_c"] = _dot(it["att"], it["v"])


def _hg_kernel(*refs, nchunk, with_output):
    if with_output:
        (ff_ref, vf_ref, qf_ref, fb_ref, vb_ref, qb_ref, lbf_ref, lbb_ref, s0f_ref, s0b_ref,
         of_ref, ob_ref, sf_ref, sb_ref, stf, stb) = refs
    else:
        (ff_ref, vf_ref, fb_ref, vb_ref, lbf_ref, lbb_ref, s0f_ref, s0b_ref,
         sf_ref, sb_ref, stf, stb) = refs
        qf_ref = qb_ref = of_ref = ob_ref = None

    @pl.when(pl.program_id(2) == 0)
    def _():
        stf[...] = s0f_ref[...]
        stb[...] = s0b_ref[...]

    def chunk_item(f_ref, v_ref, q_ref, lb_ref, ci, rev):
        sl = pl.ds(ci * CHUNK, CHUNK)
        return {"fz": f_ref[sl, :], "v": v_ref[sl, :], "q": q_ref[sl, :] if with_output else None,
                "lb": lb_ref[...], "rev": rev, "sl": sl}

    items = []
    for ci in range(nchunk):
        items.append(chunk_item(ff_ref, vf_ref, qf_ref, lbf_ref, ci, False))
        items.append(chunk_item(fb_ref, vb_ref, qb_ref, lbb_ref, nchunk - 1 - ci, True))
    _hg_prepare(items)
    state = {False: stf[...], True: stb[...]}
    for it in items:
        st = state[it["rev"]]
        if with_output:
            o_ref = ob_ref if it["rev"] else of_ref
            o_ref[it["sl"], :] = it["o_c"] + _dot_nt(it["qs"], st)
        state[it["rev"]] = st * it["e_last"] + it["kv"]
    stf[...] = state[False]
    stb[...] = state[True]
    sf_ref[...] = state[False]
    sb_ref[...] = state[True]


def _hgrn2_scan(h, cols, lb_f, lb_b, s0, *, tblk, with_output):
    bsz, l, _ = h.shape
    nt = l // tblk
    c_ff, c_fb, c_i, c_q = (c // LANE for c in cols)

    def fwd(cb):
        return pl.BlockSpec((None, tblk, LANE), lambda b, hh, t: (b, t, cb + hh))

    def bwd(cb):
        return pl.BlockSpec((None, tblk, LANE), lambda b, hh, t: (b, nt - 1 - t, cb + hh))

    lb_spec = pl.BlockSpec((1, LANE), lambda b, hh, t: (0, hh))
    st_spec = pl.BlockSpec((None, None, HEAD_DIM, HEAD_DIM), lambda b, hh, t: (b, hh, 0, 0))
    if with_output:
        in_specs = [fwd(c_ff), fwd(c_i), fwd(c_q), bwd(c_fb), bwd(c_i), bwd(c_q)]
        args = [h] * 6
    else:
        in_specs = [fwd(c_ff), fwd(c_i), bwd(c_fb), bwd(c_i)]
        args = [h] * 4
    in_specs += [lb_spec, lb_spec, st_spec, st_spec]
    args += [lb_f, lb_b, s0[0], s0[1]]
    st_shape = jax.ShapeDtypeStruct((bsz, HEADS, HEAD_DIM, HEAD_DIM), F32)
    out_specs = [st_spec, st_spec]
    out_shape = [st_shape, st_shape]
    if with_output:
        o_shape = jax.ShapeDtypeStruct((bsz, l, HEADS * HEAD_DIM), F32)
        out_specs = [pl.BlockSpec((None, tblk, LANE), lambda b, hh, t: (b, t, hh)),
                     pl.BlockSpec((None, tblk, LANE), lambda b, hh, t: (b, nt - 1 - t, hh))] + out_specs
        out_shape = [o_shape, o_shape] + out_shape
    outs = pl.pallas_call(
        functools.partial(_hg_kernel, nchunk=tblk // CHUNK, with_output=with_output),
        name="hgrn2_scan",
        grid=(bsz, HEADS, nt),
        in_specs=in_specs,
        out_specs=out_specs,
        out_shape=out_shape,
        scratch_shapes=[pltpu.VMEM((HEAD_DIM, HEAD_DIM), F32), pltpu.VMEM((HEAD_DIM, HEAD_DIM), F32)],
        compiler_params=_cparams(("parallel", "parallel", "arbitrary")),
    )(*args)
    if with_output:
        return outs[0], outs[1], (outs[2], outs[3])
    return None, None, (outs[0], outs[1])


def _l2norm(t):
    return t * lax.rsqrt(jnp.sum(t * t, axis=-1, keepdims=True) + 1e-6)


def _softplus(x):
    return jnp.maximum(x, 0.0) + jnp.log(1.0 + jnp.exp(-jnp.abs(x)))


def _dn_prepare(items):
    r = _iota2((CHUNK, CHUNK), 0)
    c = _iota2((CHUNK, CHUNK), 1)
    eye = r == c
    for it in items:
        rev = it["rev"]
        incl = (c >= r) if rev else (c <= r)
        incl_t = (r >= c) if rev else (r <= c)
        g_b = jnp.broadcast_to(it["g"], (CHUNK, CHUNK))
        gc_col = jnp.sum(jnp.where(incl, g_b, 0.0), axis=1, keepdims=True)
        g_col = jnp.sum(jnp.where(eye, g_b, 0.0), axis=1, keepdims=True)
        gc_row = jnp.sum(jnp.where(incl_t, jnp.broadcast_to(g_col, (CHUNK, CHUNK)), 0.0), axis=0, keepdims=True)
        beta_col = jnp.sum(jnp.where(eye, jnp.broadcast_to(it["beta"], (CHUNK, CHUNK)), 0.0), axis=1,
                           keepdims=True)
        it["decay"] = jnp.where(incl, jnp.exp(jnp.where(incl, gc_col - gc_row, 0.0)), 0.0)
        g_last = gc_col[0:1] if rev else gc_col[CHUNK - 1:CHUNK]
        it["e_last"] = jnp.exp(g_last)
        egc = jnp.exp(gc_col)
        kb = it["k"] * beta_col
        it["kb"] = kb
        it["rhs"] = jnp.concatenate([it["v"] * beta_col, kb * egc], axis=1)
        it["kd"] = it["k"] * jnp.exp(g_last - gc_col)
        if it["q"] is not None:
            it["qe"] = it["q"] * egc
    for it in items:
        it["kk"] = _dot_nt(it["kb"], it["k"])
        if it["q"] is not None:
            it["qk"] = _dot_nt(it["q"], it["k"]) * it["decay"]
    for it in items:
        strict = (c > r) if it["rev"] else (c < r)
        n = jnp.where(strict, it["kk"] * it["decay"], 0.0)
        it["n"] = n
        it["p"] = jnp.where(eye, 1.0, 0.0) - n
    for it in items:
        it["m"] = _dot(it["n"], it["n"])
    for i in range(5):
        for it in items:
            if i < 4:
                pm = _dot(jnp.concatenate([it["p"], it["m"]], axis=0), it["m"])
                it["p"] = it["p"] + pm[:CHUNK]
                it["m"] = pm[CHUNK:]
            else:
                it["p"] = it["p"] + _dot(it["p"], it["m"])
    for it in items:
        it["uw"] = _dot(it["p"], it["rhs"])
    for it in items:
        bm = _dot_tn(it["kd"], it["uw"])
        it["b_c"], it["m_c"] = bm[:, :HEAD_DIM], bm[:, HEAD_DIM:]
        if it["q"] is not None:
            ow = _dot(it["qk"], it["uw"])
            it["o_c"], it["q_eff"] = ow[:, :HEAD_DIM], it["qe"] - ow[:, HEAD_DIM:]


def _dn_kernel(*refs, nchunk, with_output):
    if with_output:
        (alog_ref, dtb_ref, kf_ref, vf_ref, qf_ref, abf_ref, kb_ref, vb_ref, qb_ref, abb_ref,
         s0f_ref, s0b_ref, of_ref, ob_ref, sf_ref, sb_ref, stf, stb) = refs
    else:
        (alog_ref, dtb_ref, kf_ref, vf_ref, abf_ref, kb_ref, vb_ref, abb_ref,
         s0f_ref, s0b_ref, sf_ref, sb_ref, stf, stb) = refs
        qf_ref = qb_ref = of_ref = ob_ref = None
    hh = pl.program_id(1)

    @pl.when(pl.program_id(2) == 0)
    def _():
        stf[...] = s0f_ref[...]
        stb[...] = s0b_ref[...]

    def prep(k_ref, v_ref, q_ref, ab_ref, d):
        k = _l2norm(_silu(k_ref[...]))
        v = _silu(v_ref[...])
        q = None
        if with_output:
            q = _l2norm(_silu(q_ref[...])) * (HEAD_DIM ** -0.5)
        a = ab_ref[pl.ds(d * HEADS + hh, 1), :]
        bt = ab_ref[pl.ds(2 * HEADS + d * HEADS + hh, 1), :]
        g = -jnp.exp(alog_ref[d, hh]) * _softplus(a + dtb_ref[d, hh])
        return k, v, q, g, _sigmoid(bt)

    kf, vf, qf, gf, bf = prep(kf_ref, vf_ref, qf_ref, abf_ref, 0)
    kb, vb, qb, gb, bb = prep(kb_ref, vb_ref, qb_ref, abb_ref, 1)
    def chunk_item(k, v, q, g, beta, ci, rev):
        lo = ci * CHUNK
        return {"k": k[lo:lo + CHUNK], "v": v[lo:lo + CHUNK], "q": q[lo:lo + CHUNK] if with_output else None,
                "g": g[:, lo:lo + CHUNK], "beta": beta[:, lo:lo + CHUNK], "rev": rev, "lo": lo}

    items = []
    for ci in range(nchunk):
        items.append(chunk_item(kf, vf, qf, gf, bf, ci, False))
        items.append(chunk_item(kb, vb, qb, gb, bb, nchunk - 1 - ci, True))
    _dn_prepare(items)
    state = {False: stf[...], True: stb[...]}
    for it in items:
        s = state[it["rev"]]
        if with_output:
            o_ref = ob_ref if it["rev"] else of_ref
            o_ref[pl.ds(it["lo"], CHUNK), :] = it["o_c"] + _dot(it["q_eff"], s)
        state[it["rev"]] = s * it["e_last"] + it["b_c"] - _dot(it["m_c"], s)
    stf[...] = state[False]
    stb[...] = state[True]
    sf_ref[...] = state[False]
    sb_ref[...] = state[True]


def _deltanet_scan(h, ab_t, cols, a_log, dt_bias, s0, *, tblk, with_output):
    bsz, l, _ = h.shape
    nt = l // tblk
    c_k, c_v, c_q = (c // LANE for c in cols)
    rep = HEADS // DN_QK_HEADS

    def tok(cb, shared, rev):
        def imap(b, hh, t):
            return (b, (nt - 1 - t) if rev else t, cb + (hh // rep if shared else hh))
        return pl.BlockSpec((None, tblk, LANE), imap)

    def ab_spec(rev):
        return pl.BlockSpec((None, 4 * HEADS, tblk), lambda b, hh, t: (b, 0, (nt - 1 - t) if rev else t))

    smem = pl.BlockSpec(memory_space=pltpu.SMEM)
    st_spec = pl.BlockSpec((None, None, HEAD_DIM, HEAD_DIM), lambda b, hh, t: (b, hh, 0, 0))
    if with_output:
        in_specs = [smem, smem, tok(c_k, True, False), tok(c_v, False, False), tok(c_q, True, False), ab_spec(False),
                    tok(c_k, True, True), tok(c_v, False, True), tok(c_q, True, True), ab_spec(True),
                    st_spec, st_spec]
        args = [a_log, dt_bias, h, h, h, ab_t, h, h, h, ab_t, s0[0], s0[1]]
    else:
        in_specs = [smem, smem, tok(c_k, True, False), tok(c_v, False, False), ab_spec(False),
                    tok(c_k, True, True), tok(c_v, False, True), ab_spec(True),
                    st_spec, st_spec]
        args = [a_log, dt_bias, h, h, ab_t, h, h, ab_t, s0[0], s0[1]]
    st_shape = jax.ShapeDtypeStruct((bsz, HEADS, HEAD_DIM, HEAD_DIM), F32)
    out_specs = [st_spec, st_spec]
    out_shape = [st_shape, st_shape]
    if with_output:
        o_shape = jax.ShapeDtypeStruct((bsz, l, HEADS * HEAD_DIM), F32)
        out_specs = [pl.BlockSpec((None, tblk, LANE), lambda b, hh, t: (b, t, hh)),
                     pl.BlockSpec((None, tblk, LANE), lambda b, hh, t: (b, nt - 1 - t, hh))] + out_specs
        out_shape = [o_shape, o_shape] + out_shape
    outs = pl.pallas_call(
        functools.partial(_dn_kernel, nchunk=tblk // CHUNK, with_output=with_output),
        name="deltanet_scan",
        grid=(bsz, HEADS, nt),
        in_specs=in_specs,
        out_specs=out_specs,
        out_shape=out_shape,
        scratch_shapes=[pltpu.VMEM((HEAD_DIM, HEAD_DIM), F32), pltpu.VMEM((HEAD_DIM, HEAD_DIM), F32)],
        compiler_params=_cparams(("parallel", "parallel", "arbitrary")),
    )(*args)
    if with_output:
        return outs[0], outs[1], (outs[2], outs[3])
    return None, None, (outs[0], outs[1])


def _combine_kernel(of_ref, ob_ref, z_ref, w_ref, o_ref, *, use_silu):
    o = of_ref[...] + ob_ref[...]
    o = o * lax.rsqrt(jnp.mean(o * o, axis=-1, keepdims=True) + RMS_EPS) * w_ref[...]
    z = z_ref[...]
    o_ref[...] = (o * (_silu(z) if use_silu else _sigmoid(z))).astype(o_ref.dtype)


def _combine(o_f, o_b, h, col_z, norm_w, *, tl, use_silu):
    bsz, l, c = o_f.shape
    cz = col_z // LANE
    spec = pl.BlockSpec((None, tl, LANE), lambda b, i, hh: (b, i, hh))
    return pl.pallas_call(
        functools.partial(_combine_kernel, use_silu=use_silu),
        name="mixer_norm_gate",
        grid=(bsz, l // tl, c // LANE),
        in_specs=[spec, spec,
                  pl.BlockSpec((None, tl, LANE), lambda b, i, hh: (b, i, cz + hh)),
                  pl.BlockSpec((1, LANE), lambda b, i, hh: (0, 0))],
        out_specs=spec,
        out_shape=jax.ShapeDtypeStruct((bsz, l, c), BF16),
        compiler_params=_cparams(("parallel", "parallel", "arbitrary")),
    )(o_f, o_b, h, norm_w)


def _dot_f32(a, b):
    return jnp.dot(a, b, precision=lax.Precision.HIGHEST, preferred_element_type=F32)


def _hy_filt_kernel(w1_ref, b1_ref, w2_ref, b2_ref, w3_ref, o_ref, hid_ref, *, l_total, tl):
    i = pl.program_id(0)
    j = pl.program_id(1)

    @pl.when(j == 0)
    def _():
        row = (_iota2((tl, LANE), 0) + i * tl).astype(F32)
        lane = _iota2((tl, LANE), 1)
        t = row * (1.0 / (l_total - 1))
        ang = row * (2.0 * math.pi / l_total)
        bidx = jnp.where(lane <= HY_BANDS, lane - 1, lane - 1 - HY_BANDS).astype(F32)
        band = 1e-4 + bidx * ((HY_BANDS - 1 - 1e-4) / (HY_BANDS - 1))
        arg = band * ang
        feats = jnp.where(lane == 0, t,
                          jnp.where(lane <= HY_BANDS, jnp.cos(arg),
                                    jnp.where(lane <= 2 * HY_BANDS, -jnp.sin(arg), 0.0)))
        hid = jnp.sin(_dot_f32(feats, w1_ref[...]) + b1_ref[...])
        hid_ref[...] = jnp.sin(_dot_f32(hid, w2_ref[...]) + b2_ref[...])

    hcol = _dot_f32(hid_ref[...], w3_ref[...])
    width = hcol.shape[1]
    ch = _iota2((tl, width), 1).astype(F32)
    lo = math.log(HY_DECAY_TARGET) / HY_SLOW_DECAY
    hi = math.log(HY_DECAY_TARGET) / HY_FAST_DECAY
    delta = jnp.abs(lo + ch * ((hi - lo) / (width - 1)))
    trow = (_iota2((tl, width), 0) + i * tl)
    out = hcol * jnp.exp(-(trow.astype(F32) * (1.0 / (l_total - 1))) * delta)
    drop = jnp.logical_and(trow == 0, jnp.bitwise_and(j, 1) == 1)
    o_ref[...] = jnp.where(drop, 0.0, out)


def _hy_filters(l_total, w1p, b1, w2, b2, w3):
    tl = min(l_total, 512)
    n = w3.shape[1]
    full = lambda a: pl.BlockSpec(a.shape, lambda i, j: (0,) * a.ndim)
    return pl.pallas_call(
        functools.partial(_hy_filt_kernel, l_total=l_total, tl=tl),
        name="hy_filters",
        grid=(l_total // tl, n // BRANCH_WIDTH),
        in_specs=[full(w1p), full(b1), full(w2), full(b2),
                  pl.BlockSpec((HY_HIDDEN, BRANCH_WIDTH), lambda i, j: (0, j))],
        out_specs=pl.BlockSpec((tl, BRANCH_WIDTH), lambda i, j: (i, j)),
        out_shape=jax.ShapeDtypeStruct((l_total, n), F32),
        scratch_shapes=[pltpu.VMEM((tl, HY_HIDDEN), F32)],
        compiler_params=_cparams(("parallel", "arbitrary")),
    )(w1p, b1, w2, b2, w3)


@functools.lru_cache(maxsize=None)
def _fft_tables(n1, n2):
    n = n1 * n2
    n2h = n2 // 2
    k2 = np.arange(n2, dtype=np.int64)[:, None]
    m = np.arange(n2h, dtype=np.int64)[None, :]
    ta = np.zeros((n1, 2 * n2, 2 * n2h))
    for a in range(n1):
        th = 2.0 * np.pi * ((k2 * (a + n1 * m)) % n) / n
        c, s = np.cos(th), np.sin(th)
        ta[a] = np.block([[c, s], [-s, c]])
    tai = np.transpose(ta, (0, 2, 1)) / n
    k1 = np.arange(n1, dtype=np.int64)
    ph = 2.0 * np.pi * ((k1[:, None] * k1[None, :]) % n1) / n1
    c, s = np.cos(ph), np.sin(ph)
    fb = np.block([[c, s], [-s, c]])
    fs = fb[:n1] + fb[n1:]
    fsr = np.stack([fs[(-k1 - e) % n1] for e in (0, 1)])
    return tuple(np.asarray(t, np.float32) for t in (ta, tai, fb, fb.T, fs, fsr))


def _fft_split(l_total):
    n = 2 * l_total
    n1 = min(128, l_total // 8)
    return n1, n // n1


def _hy_stage_a_kernel(gr_ref, cr_ref, gi_ref, ci_ref, xr_ref, xi_ref, t_ref, o_ref, *, planes, n2):
    del gr_ref, cr_ref, gi_ref, ci_ref
    for j in range(8):
        x = jnp.concatenate([xr_ref[:, j, :], xi_ref[:, j, :]], axis=0).astype(BF16)
        y = jnp.dot(t_ref[j], x, preferred_element_type=F32)
        for p in range(planes):
            o_ref[p, :, j, :] = y[p * n2:(p + 1) * n2].astype(o_ref.dtype)


def _hy_stage_a(x4, pairs, tab, *, ct, planes, n2):
    _, n2h, n1, _ = x4.shape
    npair = len(pairs)
    nct = BRANCH_WIDTH // ct
    gr = jnp.asarray([p[0][0] for p in pairs], jnp.int32)
    cr = jnp.asarray([p[0][1] for p in pairs], jnp.int32)
    gi = jnp.asarray([p[1][0] for p in pairs], jnp.int32)
    ci = jnp.asarray([p[1][1] for p in pairs], jnp.int32)
    grid_spec = pltpu.PrefetchScalarGridSpec(
        num_scalar_prefetch=4,
        grid=(npair, nct, n1 // 8),
        in_specs=[pl.BlockSpec((None, n2h, 8, ct), lambda p, c, g, gr, cr, gi, ci: (gr[p], 0, g, cr[p] * nct + c)),
                  pl.BlockSpec((None, n2h, 8, ct), lambda p, c, g, gr, cr, gi, ci: (gi[p], 0, g, ci[p] * nct + c)),
                  pl.BlockSpec((8, planes * n2, 2 * n2h), lambda p, c, g, *_: (g, 0, 0))],
        out_specs=pl.BlockSpec((None, planes, n2, 8, ct), lambda p, c, g, *_: (p, 0, 0, g, c)),
    )
    return pl.pallas_call(
        functools.partial(_hy_stage_a_kernel, planes=planes, n2=n2),
        name="hy_stage_a",
        grid_spec=grid_spec,
        out_shape=jax.ShapeDtypeStruct((npair, planes, n2, n1, BRANCH_WIDTH), BF16),
        compiler_params=_cparams(("parallel", "parallel", "arbitrary")),
    )(gr, cr, gi, ci, x4, x4, tab)


def _hy_spec_kernel(c_ref, call_ref, fs_ref, fsr_ref, o_ref, *, kg, n2):
    g = pl.program_id(2)
    for k in range(kg):
        s = jnp.dot(fs_ref[...], jnp.concatenate([c_ref[0, k], c_ref[1, k]], axis=0), preferred_element_type=F32)
        k2 = g * kg + k
        kp = jnp.where(k2 == 0, 0, n2 - k2)
        fsr = fsr_ref[1] if k else jnp.where(g == 0, fsr_ref[0], fsr_ref[1])
        t = jnp.dot(fsr, jnp.concatenate([call_ref[0, kp], call_ref[1, kp]], axis=0), preferred_element_type=F32)
        o_ref[0, k] = 0.5 * (s + t)
        o_ref[1, k] = 0.5 * (s - t)


def _hy_spectrum(cs, fs, fsr, *, ct, kg):
    no, _, n2, n1, c = cs.shape
    return pl.pallas_call(
        functools.partial(_hy_spec_kernel, kg=kg, n2=n2),
        name="hy_spectrum",
        grid=(no, c // ct, n2 // kg),
        in_specs=[pl.BlockSpec((None, 2, kg, n1, ct), lambda o, j, g: (o, 0, g, 0, j)),
                  pl.BlockSpec((None, 2, n2, n1, ct), lambda o, j, g: (o, 0, 0, 0, j)),
                  pl.BlockSpec(fs.shape, lambda o, j, g: (0, 0)),
                  pl.BlockSpec(fsr.shape, lambda o, j, g: (0, 0, 0))],
        out_specs=pl.BlockSpec((None, 2, kg, n1, ct), lambda o, j, g: (o, 0, g, 0, j)),
        out_shape=jax.ShapeDtypeStruct((no, 2, n2, n1, c), F32),
        compiler_params=_cparams(("parallel", "parallel", "arbitrary")),
    )(cs, cs, fs, fsr)


def _hy_stage_b_kernel(c_ref, k_ref, fb_ref, fbi_ref, o_ref, *, kg, n1):
    for k in range(kg):
        z = jnp.dot(fb_ref[...], jnp.concatenate([c_ref[0, k], c_ref[1, k]], axis=0), preferred_element_type=F32)
        zr, zi = z[:n1], z[n1:]
        kr, ki = k_ref[0, k], k_ref[1, k]
        y = jnp.concatenate([zr * kr - zi * ki, zr * ki + zi * kr], axis=0).astype(BF16)
        d = jnp.dot(fbi_ref[...], y, preferred_element_type=F32)
        o_ref[0, k] = d[:n1].astype(o_ref.dtype)
        o_ref[1, k] = d[n1:].astype(o_ref.dtype)


def _hy_stage_b(cs, spec, order, fb, fbi, *, ct, kg):
    _, _, n2, n1, c = cs.shape
    return pl.pallas_call(
        functools.partial(_hy_stage_b_kernel, kg=kg, n1=n1),
        name="hy_stage_b",
        grid=(c // ct, n2 // kg),
        in_specs=[pl.BlockSpec((None, 2, kg, n1, ct), lambda j, g: (0, 0, g, 0, j)),
                  pl.BlockSpec((None, 2, kg, n1, ct), lambda j, g: (order, 0, g, 0, j)),
                  pl.BlockSpec(fb.shape, lambda j, g: (0, 0)),
                  pl.BlockSpec(fbi.shape, lambda j, g: (0, 0))],
        out_specs=pl.BlockSpec((2, kg, n1, ct), lambda j, g: (0, g, 0, j)),
        out_shape=jax.ShapeDtypeStruct((2, n2, n1, c), BF16),
        compiler_params=_cparams(("parallel", "arbitrary")),
    )(cs, spec, fb, fbi)


def _hy_stage_c_kernel(d_ref, t_ref, u_ref, g_ref, skip_ref, o_ref, *, n2h):
    skip = skip_ref[...]
    for j in range(8):
        d = jnp.concatenate([d_ref[0, :, j, :], d_ref[1, :, j, :]], axis=0)
        y = jnp.dot(t_ref[j], d, preferred_element_type=F32)
        for b in range(2):
            yb = y[b * n2h:(b + 1) * n2h]
            o_ref[b, :, j, :] = (g_ref[b, :, j, :] * (yb + skip * u_ref[b, :, j, :])).astype(o_ref.dtype)


def _hy_stage_c(d, tai, u4, ucol, z4, gcol, skip, *, ct, out_dtype):
    _, n2, n1, c = d.shape
    n2h = n2 // 2
    uc, gc = ucol // ct, gcol // ct
    return pl.pallas_call(
        functools.partial(_hy_stage_c_kernel, n2h=n2h),
        name="hy_stage_c",
        grid=(c // ct, n1 // 8),
        in_specs=[pl.BlockSpec((2, n2, 8, ct), lambda j, g: (0, 0, g, j)),
                  pl.BlockSpec((8, 2 * n2h, 2 * n2), lambda j, g: (g, 0, 0)),
                  pl.BlockSpec((2, n2h, 8, ct), lambda j, g: (0, 0, g, uc + j)),
                  pl.BlockSpec((2, n2h, 8, ct), lambda j, g: (0, 0, g, gc + j)),
                  pl.BlockSpec((1, ct), lambda j, g: (0, j))],
        out_specs=pl.BlockSpec((2, n2h, 8, ct), lambda j, g: (0, 0, g, j)),
        out_shape=jax.ShapeDtypeStruct((2, n2h, n1, c), out_dtype),
        compiler_params=_cparams(("parallel", "arbitrary")),
    )(d, tai, u4, z4, skip)


def _hyena(h, filt_params, skip):
    bsz, l, _ = h.shape
    assert bsz == 2
    n1, n2 = _fft_split(l)
    n2h = n2 // 2
    ta, tai, fb, fbi, fs, fsr = (jnp.asarray(t).astype(BF16) for t in _fft_tables(n1, n2))
    ct = 512
    kg = 8

    hfilt = _hy_filters(l, *filt_params)
    cs_k = _hy_stage_a(hfilt.reshape(1, n2h, n1, -1), [((0, 2 * o), (0, 2 * o + 1)) for o in range(HY_ORDER)],
                       ta, ct=ct, planes=2, n2=n2)
    spec = _hy_spectrum(cs_k, fs, fsr, ct=256, kg=kg)

    z4 = h.reshape(bsz, n2h, n1, -1)
    u4, ucol = z4, COL_HY
    for o in range(HY_ORDER):
        last = o == HY_ORDER - 1
        cs = _hy_stage_a(u4, [((0, ucol // BRANCH_WIDTH), (1, ucol // BRANCH_WIDTH))], ta, ct=ct, planes=2, n2=n2)
        d = _hy_stage_b(cs, spec, o, fb, fbi, ct=ct, kg=kg)
        u4 = _hy_stage_c(d, tai, u4, ucol, z4, COL_HY + (o + 1) * BRANCH_WIDTH, skip[o:o + 1], ct=ct,
                         out_dtype=BF16 if last else F32)
        ucol = 0
    return u4.reshape(bsz, l, BRANCH_WIDTH)


def _relayout_kernel(src_ref, mode_ref, a_ref, b_ref, o_ref):
    del src_ref
    mode = mode_ref[pl.program_id(1)]
    a = a_ref[...]
    shifted = jnp.concatenate([a[:, AB_WIDTH:], b_ref[:, :AB_WIDTH]], axis=1)
    lane = _iota2((1, RELAYOUT_TN), 1)
    out = jnp.where(mode == 1, shifted, jnp.where(jnp.logical_or(mode == 0, lane < AB_WIDTH), a, 0.0))
    o_ref[...] = out.astype(o_ref.dtype)


def _reorder_w_in(w):
    depth, d, n_src = w.shape
    nblk = COL_PAD // RELAYOUT_TN
    src = np.zeros((nblk,), np.int32)
    mode = np.zeros((nblk,), np.int32)
    for dst_col, src_col, width in SRC_COLS:
        shift = src_col % RELAYOUT_TN
        assert dst_col % RELAYOUT_TN == 0 and shift in (0, AB_WIDTH)
        for k in range(-(-width // RELAYOUT_TN)):
            src[dst_col // RELAYOUT_TN + k] = src_col // RELAYOUT_TN + k
            mode[dst_col // RELAYOUT_TN + k] = 2 if width == AB_WIDTH else (1 if shift else 0)
    per = RELAYOUT_TN // LANE
    last_b = (n_src - 1) // LANE
    grid_spec = pltpu.PrefetchScalarGridSpec(
        num_scalar_prefetch=2,
        grid=(depth, nblk),
        in_specs=[pl.BlockSpec((None, d, RELAYOUT_TN), lambda l, j, src, mode: (l, 0, src[j])),
                  pl.BlockSpec((None, d, LANE), lambda l, j, src, mode: (l, 0, jnp.minimum((src[j] + 1) * per, last_b)))],
        out_specs=pl.BlockSpec((None, d, RELAYOUT_TN), lambda l, j, src, mode: (l, 0, j)),
    )
    return pl.pallas_call(
        _relayout_kernel,
        name="w_in_relayout",
        grid_spec=grid_spec,
        out_shape=jax.ShapeDtypeStruct((depth, d, COL_PAD), BF16),
        compiler_params=_cparams(("parallel", "arbitrary")),
    )(jnp.asarray(src), jnp.asarray(mode), w, w)


def _conv_tables(hy_conv_w, hy_conv_b, dn_conv_q, dn_conv_k, dn_conv_v):
    depth = hy_conv_w.shape[0]
    ident = lambda n: jnp.broadcast_to(jnp.asarray([0.0, 1.0, 0.0], F32)[None, :, None], (depth, 3, n))
    taps = jnp.concatenate([ident(COL_HY), hy_conv_w, ident(COL_DK - COL_FF), dn_conv_k, dn_conv_v,
                            ident(COL_DQ - COL_HQ), dn_conv_q, ident(COL_PAD - COL_DZ)], axis=2)
    bias = jnp.concatenate([jnp.zeros((depth, 1, COL_HY), F32), hy_conv_b[:, None, :],
                            jnp.zeros((depth, 1, COL_PAD - COL_FF), F32)], axis=2)
    return taps, bias


def _token_mixers(u_src, sh, sc, w_r, conv_t, layer, lp, states, *, is_grid, with_output, tm):
    bsz, l, _ = u_src.shape
    tblk = min(l, 512)
    taps, bias = conv_t
    if with_output:
        h = _mm_mod(u_src, sh, sc, w_r, layer, tm=tm, tn=IN_TN, relu2=False, out_dtype=F32,
                    conv=(taps, bias, is_grid))
        base = 0
    else:
        def cut(a):
            a = a[layer:layer + 1]
            pad = jnp.zeros(a.shape[:2] + (STATE_PAD - (COL_HQ - COL_FF) - LANE,), a.dtype)
            return jnp.concatenate([a[:, :, COL_FF:COL_HQ], a[:, :, COL_AB:COL_AB + LANE], pad], axis=2)

        h = _mm_mod(u_src, sh, sc, cut(w_r), 0, tm=tm, tn=IN_TN, relu2=False, out_dtype=F32,
                    conv=(cut(taps), cut(bias), is_grid))
        base = COL_FF
    col = lambda c: c - base
    c_ab = COL_AB if with_output else COL_HQ - COL_FF
    ab_t = jnp.swapaxes(h[:, :, c_ab:c_ab + 4 * HEADS], 1, 2)
    hg_f, hg_b, hg_states = _hgrn2_scan(h, (col(COL_FF), col(COL_FB), col(COL_HI), col(COL_HQ)),
                                        lp["lb_f"], lp["lb_b"], states[0], tblk=tblk, with_output=with_output)
    dn_f, dn_b, dn_states = _deltanet_scan(h, ab_t, (col(COL_DK), col(COL_DV), col(COL_DQ)),
                                           lp["dn_a_log"], lp["dn_dt_bias"], states[1],
                                           tblk=tblk, with_output=with_output)
    new_states = (hg_states, dn_states)
    if not with_output:
        return None, new_states
    tl = min(l, 2048)
    hg_out = _combine(hg_f, hg_b, h, COL_HGATE, lp["hg_norm_w"], tl=tl, use_silu=False)
    dn_out = _combine(dn_f, dn_b, h, COL_DZ, lp["dn_norm_w"], tl=tl, use_silu=True)
    hy_out = _hyena(h, lp["hy_filt"], lp["hy_skip"])
    ysum = _branch_merge([hy_out, hg_out, dn_out], lp["w_branch"], layer, h, tm=min(l, 1024), tn=512)
    return ysum, new_states


def kernel(x, c, ctx, c_ctx, w_ada, b_ada, w_in, hy_conv_w, hy_conv_b, hy_filt_w1, hy_filt_b1, hy_filt_w2, hy_filt_b2, hy_filt_w3, hy_skip, hg_lb_logits, hg_norm_w, dn_conv_q, dn_conv_k, dn_conv_v, dn_a_log, dn_dt_bias, dn_norm_w, w_branch, w_out, ln1_g, ln1_b, w_ff1, w_ff2, ln2_g, ln2_b):
    depth = w_in.shape[0]
    bsz, _, d = x.shape
    alpha = (2 * depth) ** 0.25
    p = jax.nn.softmax(hg_lb_logits.astype(F32), axis=1)
    lower = jnp.cumsum(p, axis=1) - p[:, :1]
    cs = jnp.concatenate([c, c_ctx[None], jnp.zeros((8 - bsz - 1, d), F32)], axis=0)
    w_r = _reorder_w_in(w_in)
    w_b = w_branch.astype(BF16)
    w_o = w_out.astype(BF16)
    w_1 = w_ff1.astype(BF16)
    w_2 = w_ff2.astype(BF16)
    b_ada3 = b_ada[:, None, :]
    conv_t = _conv_tables(hy_conv_w, hy_conv_b, dn_conv_q, dn_conv_k, dn_conv_v)
    h_ctx = ctx
    for l in range(depth):
        last = l == depth - 1
        row = lambda v: v.reshape(1, -1)
        lp = {
            "lb_f": row(lower[0, l]), "lb_b": row(lower[1, l]),
            "hg_norm_w": row(hg_norm_w[l]), "dn_norm_w": row(dn_norm_w[l]),
            "dn_a_log": dn_a_log[l], "dn_dt_bias": dn_dt_bias[l], "hy_skip": hy_skip[l],
            "hy_filt": (jnp.pad(hy_filt_w1[l], ((0, LANE - HY_EMB), (0, 0))), row(hy_filt_b1[l]),
                        hy_filt_w2[l], row(hy_filt_b2[l]), hy_filt_w3[l]),
            "w_branch": w_b,
        }
        g1n, b1n, g2n, b2n = row(ln1_g[l]), row(ln1_b[l]), row(ln2_g[l]), row(ln2_b[l])
        mod = _ada_mod(cs, w_ada, b_ada3, l)
        m_lat = [mod[:bsz, i * d:(i + 1) * d][:, None, :] for i in range(6)]
        m_ctx = [jnp.broadcast_to(mod[bsz, i * d:(i + 1) * d][None, None, :], (bsz, 1, d)) for i in range(6)]
        zero = jnp.zeros((bsz, HEADS, HEAD_DIM, HEAD_DIM), F32)
        init = ((zero, zero), (zero, zero))

        lc = h_ctx.shape[1]
        y_ctx, ctx_states = _token_mixers(h_ctx, m_ctx[0], m_ctx[1], w_r, conv_t, l, lp, init,
                                          is_grid=False, with_output=not last, tm=lc)
        if not last:
            h_ctx = _mm_ln(y_ctx, w_o, l, h_ctx, m_ctx[2], g1n, b1n, tm=lc, tk=d, alpha=alpha)
            mid = _mm_mod(h_ctx, m_ctx[3], m_ctx[4], w_1, l, tm=lc, tn=1024, relu2=True, out_dtype=BF16)
            h_ctx = _mm_ln(mid, w_2, l, h_ctx, m_ctx[5], g2n, b2n, tm=lc, tk=1024, alpha=alpha)

        lx = x.shape[1]
        y, _ = _token_mixers(x, m_lat[0], m_lat[1], w_r, conv_t, l, lp, ctx_states,
                             is_grid=True, with_output=True, tm=min(lx, 1024))
        x = _mm_ln(y, w_o, l, x, m_lat[2], g1n, b1n, tm=min(lx, 512), tk=d, alpha=alpha)
        mid = _mm_mod(x, m_lat[3], m_lat[4], w_1, l, tm=min(lx, 1024), tn=1024, relu2=True, out_dtype=BF16)
        x = _mm_ln(mid, w_2, l, x, m_lat[5], g2n, b2n, tm=min(lx, 1024), tk=512, alpha=alpha)
    return x
```

```python
import functools
import math

import numpy as np
import jax
import jax.numpy as jnp
from jax import lax
from jax.experimental import pallas as pl
from jax.experimental.pallas import tpu as pltpu

F32 = jnp.float32
BF16 = jnp.bfloat16

D_MODEL = 2048
GRID_W = 64
BRANCH_WIDTH = 1024
N_BRANCH = 3
HY_ORDER = 2
HY_EMB = 33
HY_BANDS = (HY_EMB - 1) // 2
HY_HIDDEN = 64
HY_DECAY_TARGET = 1e-2
HY_FAST_DECAY = 0.3
HY_SLOW_DECAY = 1.5
HEADS = 8
HEAD_DIM = 128
DN_QK_HEADS = 4
HG_F_MIN = 1e-30
CHUNK = 64
LN_EPS = 1e-5
RMS_EPS = 1e-6
LANE = 128

COL_GATE = 0
COL_HY = 6144
COL_FF = 9216
COL_FB = 10240
COL_HI = 11264
COL_DK = 12288
COL_DV = 12800
COL_HQ = 13824
COL_HGATE = 14848
COL_DQ = 15872
COL_DZ = 16384
COL_AB = 17408
COL_END = 17536
IN_TN = 512
COL_PAD = 17920
SRC_COLS = ((COL_GATE, 11296, 6144), (COL_HY, 8224, 3072), (COL_FF, 0, 1024), (COL_FB, 1024, 1024),
            (COL_HI, 2048, 1024), (COL_DK, 3072, 512), (COL_DV, 3584, 1024), (COL_HQ, 4640, 1024),
            (COL_HGATE, 5664, 1024), (COL_DQ, 6688, 512), (COL_DZ, 7200, 1024), (COL_AB, 4608, 512))


def _source_rows():
    off = np.zeros((COL_PAD // IN_TN,), np.int32)
    for dst_col, src_col, width in SRC_COLS:
        for k in range(width // IN_TN):
            off[dst_col // IN_TN + k] = src_col + k * IN_TN
    return off


SOURCE_ROWS = _source_rows()
STATE_BLOCKS = tuple(range(COL_FF // IN_TN, COL_HQ // IN_TN)) + (COL_AB // IN_TN,)

VMEM_LIMIT = 56 * 1024 * 1024


def _cparams(sem):
    return pltpu.CompilerParams(dimension_semantics=sem, vmem_limit_bytes=VMEM_LIMIT)


def _dot(a, b):
    return jnp.dot(a.astype(BF16), b.astype(BF16), preferred_element_type=F32)


def _dot_nt(a, b):
    return lax.dot_general(a.astype(BF16), b.astype(BF16), (((1,), (1,)), ((), ())),
                           preferred_element_type=F32)


def _dot_tn(a, b):
    return lax.dot_general(a.astype(BF16), b.astype(BF16), (((0,), (0,)), ((), ())),
                           preferred_element_type=F32)


def _sigmoid(x):
    return 1.0 / (1.0 + jnp.exp(-x))


def _silu(x):
    return x * _sigmoid(x)


def _ada_kernel(c_ref, w_ref, b_ref, o_ref):
    a = _silu(c_ref[...])
    o_ref[...] = _dot(a, w_ref[...]) + b_ref[...]


def _ada_mod(cs, w, b, layer):
    _, d, n = w.shape
    tn = 1024
    return pl.pallas_call(
        _ada_kernel,
        grid=(n // tn,),
        in_specs=[pl.BlockSpec((8, d), lambda j: (0, 0)),
                  pl.BlockSpec((None, d, tn), lambda j: (layer, 0, j)),
                  pl.BlockSpec((None, 1, tn), lambda j: (layer, 0, j))],
        out_specs=pl.BlockSpec((8, tn), lambda j: (0, j)),
        out_shape=jax.ShapeDtypeStruct((8, n), F32),
        compiler_params=_cparams(("arbitrary",)),
        name="ada_mod",
    )(cs, w, b)


def _conv3(x, w, is_grid):
    t = x.shape[0]
    row = _iota2((t, 1), 0)
    if is_grid:
        pos = jnp.bitwise_and(row, GRID_W - 1)
        has_l, has_r = pos != 0, pos != GRID_W - 1
    else:
        has_l, has_r = row != 0, row != t - 1
    xl = jnp.where(has_l, pltpu.roll(x, 1, 0), 0.0)
    xr = jnp.where(has_r, pltpu.roll(x, t - 1, 0), 0.0)
    return xl * w[0:1] + x * w[1:2] + xr * w[2:3]


def _mm_mod_kernel(*refs, relu2, conv_grid):
    if conv_grid is None:
        x_ref, sh_ref, sc_ref, w_ref, o_ref, a_scr = refs
    else:
        x_ref, sh_ref, sc_ref, w_ref, cw_ref, cb_ref, o_ref, a_scr = refs

    @pl.when(pl.program_id(2) == 0)
    def _():
        a_scr[...] = (x_ref[...] * (1.0 + sc_ref[...]) + sh_ref[...]).astype(BF16)

    acc = jnp.dot(a_scr[...], w_ref[...], preferred_element_type=F32)
    if relu2:
        acc = jnp.square(jnp.maximum(acc, 0.0))
    if conv_grid is not None:
        acc = _conv3(acc, cw_ref[...], conv_grid) + cb_ref[...]
    o_ref[...] = acc.astype(o_ref.dtype)


def _mm_mod(x, sh, sc, w, layer, *, tm, tn, relu2, out_dtype, conv=None):
    bsz, l, d = x.shape
    n = w.shape[2]
    in_specs = [pl.BlockSpec((None, tm, d), lambda b, i, j: (b, i, 0)),
                pl.BlockSpec((None, 1, d), lambda b, i, j: (b, 0, 0)),
                pl.BlockSpec((None, 1, d), lambda b, i, j: (b, 0, 0)),
                pl.BlockSpec((None, d, tn), lambda b, i, j: (layer, 0, j))]
    args = [x, sh, sc, w]
    conv_grid = None
    if conv is not None:
        taps, bias, conv_grid = conv
        assert tm == l or (conv_grid and tm % GRID_W == 0)
        in_specs += [pl.BlockSpec((None, 3, tn), lambda b, i, j: (layer, 0, j)),
                     pl.BlockSpec((None, 1, tn), lambda b, i, j: (layer, 0, j))]
        args += [taps, bias]
    return pl.pallas_call(
        functools.partial(_mm_mod_kernel, relu2=relu2, conv_grid=conv_grid),
        grid=(bsz, l // tm, n // tn),
        in_specs=in_specs,
        out_specs=pl.BlockSpec((None, tm, tn), lambda b, i, j: (b, i, j)),
        out_shape=jax.ShapeDtypeStruct((bsz, l, n), out_dtype),
        scratch_shapes=[pltpu.VMEM((tm, d), BF16)],
        compiler_params=_cparams(("parallel", "parallel", "arbitrary")),
        name="mm_relu2" if relu2 else "mm_in_proj",
    )(*args)


def _in_proj_kernel(off_ref, x_ref, sh_ref, sc_ref, w_ref, cw_ref, cb_ref, o_ref, a_scr, *, conv_grid):
    del off_ref

    @pl.when(pl.program_id(2) == 0)
    def _():
        a_scr[...] = (x_ref[...] * (1.0 + sc_ref[...]) + sh_ref[...]).astype(BF16)

    acc = lax.dot_general(a_scr[...], w_ref[0].astype(BF16), (((1,), (1,)), ((), ())),
                          preferred_element_type=F32)
    o_ref[...] = _conv3(acc, cw_ref[...], conv_grid) + cb_ref[...]


def _in_proj(x, sh, sc, w_t, layer, row_off, taps, bias, *, tm, is_grid):
    bsz, l, d = x.shape
    nblk = row_off.shape[0]
    assert tm == l or (is_grid and tm % GRID_W == 0)
    grid_spec = pltpu.PrefetchScalarGridSpec(
        num_scalar_prefetch=1,
        grid=(bsz, l // tm, nblk),
        in_specs=[pl.BlockSpec((None, tm, d), lambda b, i, j, off: (b, i, 0)),
                  pl.BlockSpec((None, 1, d), lambda b, i, j, off: (b, 0, 0)),
                  pl.BlockSpec((None, 1, d), lambda b, i, j, off: (b, 0, 0)),
                  pl.BlockSpec((pl.Element(1), pl.Element(IN_TN), pl.Element(d)),
                               lambda b, i, j, off: (layer, pl.multiple_of(off[j], 4 * HEADS), 0)),
                  pl.BlockSpec((None, 3, IN_TN), lambda b, i, j, off: (0, 0, j)),
                  pl.BlockSpec((None, 1, IN_TN), lambda b, i, j, off: (0, 0, j))],
        out_specs=pl.BlockSpec((None, tm, IN_TN), lambda b, i, j, off: (b, i, j)),
        scratch_shapes=[pltpu.VMEM((tm, d), BF16)],
    )
    return pl.pallas_call(
        functools.partial(_in_proj_kernel, conv_grid=is_grid),
        name="mm_in_proj",
        grid_spec=grid_spec,
        out_shape=jax.ShapeDtypeStruct((bsz, l, nblk * IN_TN), F32),
        compiler_params=_cparams(("parallel", "parallel", "arbitrary")),
    )(row_off, x, sh, sc, w_t, taps, bias)


def _mm_ln_kernel(a_ref, w_ref, x_ref, gate_ref, g_ref, b_ref, o_ref, acc_ref, *, nk, alpha):
    k = pl.program_id(2)

    @pl.when(k == 0)
    def _():
        acc_ref[...] = jnp.zeros_like(acc_ref)

    acc_ref[...] += jnp.dot(a_ref[...], w_ref[...], preferred_element_type=F32)

    @pl.when(k == nk - 1)
    def _():
        y = alpha * x_ref[...] + gate_ref[...] * acc_ref[...]
        mu = jnp.mean(y, axis=-1, keepdims=True)
        yc = y - mu
        var = jnp.mean(yc * yc, axis=-1, keepdims=True)
        o_ref[...] = yc * lax.rsqrt(var + LN_EPS) * g_ref[...] + b_ref[...]


def _mm_ln(a, w, layer, x, gate, ln_g, ln_b, *, tm, tk, alpha):
    bsz, l, kdim = a.shape
    d = w.shape[2]
    nk = kdim // tk
    return pl.pallas_call(
        functools.partial(_mm_ln_kernel, nk=nk, alpha=alpha),
        grid=(bsz, l // tm, nk),
        in_specs=[pl.BlockSpec((None, tm, tk), lambda b, i, k: (b, i, k)),
                  pl.BlockSpec((None, tk, d), lambda b, i, k: (layer, k, 0)),
                  pl.BlockSpec((None, tm, d), lambda b, i, k: (b, i, 0)),
                  pl.BlockSpec((None, 1, d), lambda b, i, k: (b, 0, 0)),
                  pl.BlockSpec((1, d), lambda b, i, k: (0, 0)),
                  pl.BlockSpec((1, d), lambda b, i, k: (0, 0))],
        out_specs=pl.BlockSpec((None, tm, d), lambda b, i, k: (b, i, 0)),
        out_shape=jax.ShapeDtypeStruct((bsz, l, d), F32),
        scratch_shapes=[pltpu.VMEM((tm, d), F32)],
        compiler_params=_cparams(("parallel", "parallel", "arbitrary")),
        name="mm_resid_ln",
    )(a, w, x, gate, ln_g, ln_b)


def _branch_kernel(b0_ref, b1_ref, b2_ref, w_ref, g0_ref, g1_ref, g2_ref, o_ref):
    acc = _sigmoid(g0_ref[...]) * jnp.dot(b0_ref[...], w_ref[0], preferred_element_type=F32)
    acc += _sigmoid(g1_ref[...]) * jnp.dot(b1_ref[...], w_ref[1], preferred_element_type=F32)
    acc += _sigmoid(g2_ref[...]) * jnp.dot(b2_ref[...], w_ref[2], preferred_element_type=F32)
    o_ref[...] = acc.astype(o_ref.dtype)


def _branch_merge(branches, w, layer, h, *, tm, tn):
    bsz, l, c = branches[0].shape
    d = w.shape[3]
    nj = d // tn
    br_spec = pl.BlockSpec((None, tm, c), lambda b, i, j: (b, i, 0))

    def gate_spec(g):
        return pl.BlockSpec((None, tm, tn), lambda b, i, j: (b, i, COL_GATE // tn + g * nj + j))

    return pl.pallas_call(
        _branch_kernel,
        grid=(bsz, l // tm, nj),
        in_specs=[br_spec, br_spec, br_spec,
                  pl.BlockSpec((None, N_BRANCH, c, tn), lambda b, i, j: (layer, 0, 0, j)),
                  gate_spec(0), gate_spec(1), gate_spec(2)],
        out_specs=pl.BlockSpec((None, tm, tn), lambda b, i, j: (b, i, j)),
        out_shape=jax.ShapeDtypeStruct((bsz, l, d), BF16),
        compiler_params=_cparams(("parallel", "parallel", "arbitrary")),
        name="branch_merge",
    )(*branches, w, h, h, h)


def _iota2(shape, axis):
    return lax.broadcasted_iota(jnp.int32, shape, axis)


def _split3(g):
    g1 = g.astype(BF16)
    r = g - g1.astype(F32)
    g2 = r.astype(BF16)
    g3 = (r - g2.astype(F32)).astype(BF16)
    return g1, g2, g3


def _cumsum_rows(g, rev):
    row = _iota2((CHUNK, 3 * CHUNK), 0)
    col = jnp.bitwise_and(_iota2((CHUNK, 3 * CHUNK), 1), CHUNK - 1)
    tri = jnp.where((col >= row) if rev else (col <= row), 1.0, 0.0).astype(BF16)
    g3 = jnp.concatenate(_split3(g), axis=0)
    return jnp.dot(tri, g3, preferred_element_type=F32)


def _hg_prepare(items):
    for it in items:
        lb = it["lb"]
        f = lb + (1.0 - lb) * _sigmoid(it["fz"])
        it["g"] = jnp.log(jnp.maximum(f, HG_F_MIN))
        it["kk"] = (1.0 - lb) * _sigmoid(-it["fz"])
    for it in items:
        it["bc"] = _cumsum_rows(it["g"], it["rev"])
    half = CHUNK // 2
    for it in items:
        bc = it["bc"]
        if it["rev"]:
            mid, last = bc[half:half + 1], bc[0:1]
        else:
            mid, last = bc[half - 1:half], bc[CHUNK - 1:CHUNK]
        it["e_last"] = jnp.exp(last)
        it["ks"] = it["kk"] * jnp.exp(last - bc)
        if it["q"] is not None:
            qs = _silu(it["q"])
            it["qd"] = qs * jnp.exp(bc - mid)
            it["kd"] = it["kk"] * jnp.exp(mid - bc)
            it["qs"] = qs * jnp.exp(bc)
    r = _iota2((CHUNK, CHUNK), 0)
    c = _iota2((CHUNK, CHUNK), 1)
    for it in items:
        it["kv"] = _dot_tn(it["v"], it["ks"])
        if it["q"] is not None:
            att = _dot_nt(it["qd"], it["kd"])
            it["att"] = jnp.where((c >= r) if it["rev"] else (c <= r), att, 0.0)
    for it in items:
        if it["q"] is not None:
            it["o_c"] = _dot(it["att"], it["v"])


def _hg_kernel(*refs, nchunk, with_output):
    if with_output:
        (ff_ref, vf_ref, qf_ref, fb_ref, vb_ref, qb_ref, lbf_ref, lbb_ref, s0f_ref, s0b_ref,
         of_ref, ob_ref, sf_ref, sb_ref, stf, stb) = refs
    else:
        (ff_ref, vf_ref, fb_ref, vb_ref, lbf_ref, lbb_ref, s0f_ref, s0b_ref,
         sf_ref, sb_ref, stf, stb) = refs
        qf_ref = qb_ref = of_ref = ob_ref = None

    @pl.when(pl.program_id(2) == 0)
    def _():
        stf[...] = s0f_ref[...]
        stb[...] = s0b_ref[...]

    def chunk_item(f_ref, v_ref, q_ref, lb_ref, ci, rev):
        sl = pl.ds(ci * CHUNK, CHUNK)
        return {"fz": f_ref[sl, :], "v": v_ref[sl, :], "q": q_ref[sl, :] if with_output else None,
                "lb": lb_ref[...], "rev": rev, "sl": sl}

    items = []
    for ci in range(nchunk):
        items.append(chunk_item(ff_ref, vf_ref, qf_ref, lbf_ref, ci, False))
        items.append(chunk_item(fb_ref, vb_ref, qb_ref, lbb_ref, nchunk - 1 - ci, True))
    _hg_prepare(items)
    state = {False: stf[...], True: stb[...]}
    for it in items:
        st = state[it["rev"]]
        if with_output:
            o_ref = ob_ref if it["rev"] else of_ref
            o_ref[it["sl"], :] = it["o_c"] + _dot_nt(it["qs"], st)
        state[it["rev"]] = st * it["e_last"] + it["kv"]
    stf[...] = state[False]
    stb[...] = state[True]
    sf_ref[...] = state[False]
    sb_ref[...] = state[True]


def _hgrn2_scan(h, cols, lb_f, lb_b, s0, *, tblk, with_output):
    bsz, l, _ = h.shape
    nt = l // tblk
    c_ff, c_fb, c_i, c_q = (c // LANE for c in cols)

    def fwd(cb):
        return pl.BlockSpec((None, tblk, LANE), lambda b, hh, t: (b, t, cb + hh))

    def bwd(cb):
        return pl.BlockSpec((None, tblk, LANE), lambda b, hh, t: (b, nt - 1 - t, cb + hh))

    lb_spec = pl.BlockSpec((1, LANE), lambda b, hh, t: (0, hh))
    st_spec = pl.BlockSpec((None, None, HEAD_DIM, HEAD_DIM), lambda b, hh, t: (b, hh, 0, 0))
    if with_output:
        in_specs = [fwd(c_ff), fwd(c_i), fwd(c_q), bwd(c_fb), bwd(c_i), bwd(c_q)]
        args = [h] * 6
    else:
        in_specs = [fwd(c_ff), fwd(c_i), bwd(c_fb), bwd(c_i)]
        args = [h] * 4
    in_specs += [lb_spec, lb_spec, st_spec, st_spec]
    args += [lb_f, lb_b, s0[0], s0[1]]
    st_shape = jax.ShapeDtypeStruct((bsz, HEADS, HEAD_DIM, HEAD_DIM), F32)
    out_specs = [st_spec, st_spec]
    out_shape = [st_shape, st_shape]
    if with_output:
        o_shape = jax.ShapeDtypeStruct((bsz, l, HEADS * HEAD_DIM), F32)
        out_specs = [pl.BlockSpec((None, tblk, LANE), lambda b, hh, t: (b, t, hh)),
                     pl.BlockSpec((None, tblk, LANE), lambda b, hh, t: (b, nt - 1 - t, hh))] + out_specs
        out_shape = [o_shape, o_shape] + out_shape
    outs = pl.pallas_call(
        functools.partial(_hg_kernel, nchunk=tblk // CHUNK, with_output=with_output),
        name="hgrn2_scan",
        grid=(bsz, HEADS, nt),
        in_specs=in_specs,
        out_specs=out_specs,
        out_shape=out_shape,
        scratch_shapes=[pltpu.VMEM((HEAD_DIM, HEAD_DIM), F32), pltpu.VMEM((HEAD_DIM, HEAD_DIM), F32)],
        compiler_params=_cparams(("parallel", "parallel", "arbitrary")),
    )(*args)
    if with_output:
        return outs[0], outs[1], (outs[2], outs[3])
    return None, None, (outs[0], outs[1])


def _l2norm(t):
    return t * lax.rsqrt(jnp.sum(t * t, axis=-1, keepdims=True) + 1e-6)


def _softplus(x):
    return jnp.maximum(x, 0.0) + jnp.log(1.0 + jnp.exp(-jnp.abs(x)))


def _dn_prepare(items):
    r = _iota2((CHUNK, CHUNK), 0)
    c = _iota2((CHUNK, CHUNK), 1)
    eye = r == c
    for it in items:
        rev = it["rev"]
        incl = (c >= r) if rev else (c <= r)
        incl_t = (r >= c) if rev else (r <= c)
        g_b = jnp.broadcast_to(it["g"], (CHUNK, CHUNK))
        gc_col = jnp.sum(jnp.where(incl, g_b, 0.0), axis=1, keepdims=True)
        g_col = jnp.sum(jnp.where(eye, g_b, 0.0), axis=1, keepdims=True)
        gc_row = jnp.sum(jnp.where(incl_t, jnp.broadcast_to(g_col, (CHUNK, CHUNK)), 0.0), axis=0, keepdims=True)
        beta_col = jnp.sum(jnp.where(eye, jnp.broadcast_to(it["beta"], (CHUNK, CHUNK)), 0.0), axis=1,
                           keepdims=True)
        it["decay"] = jnp.where(incl, jnp.exp(jnp.where(incl, gc_col - gc_row, 0.0)), 0.0)
        g_last = gc_col[0:1] if rev else gc_col[CHUNK - 1:CHUNK]
        it["e_last"] = jnp.exp(g_last)
        egc = jnp.exp(gc_col)
        kb = it["k"] * beta_col
        it["kb"] = kb
        it["rhs"] = jnp.concatenate([it["v"] * beta_col, kb * egc], axis=1)
        it["kd"] = it["k"] * jnp.exp(g_last - gc_col)
        if it["q"] is not None:
            it["qe"] = it["q"] * egc
    for it in items:
        it["kk"] = _dot_nt(it["kb"], it["k"])
        if it["q"] is not None:
            it["qk"] = _dot_nt(it["q"], it["k"]) * it["decay"]
    for it in items:
        strict = (c > r) if it["rev"] else (c < r)
        n = jnp.where(strict, it["kk"] * it["decay"], 0.0)
        it["n"] = n
        it["p"] = jnp.where(eye, 1.0, 0.0) - n
    for it in items:
        it["m"] = _dot(it["n"], it["n"])
    for i in range(5):
        for it in items:
            if i < 4:
                pm = _dot(jnp.concatenate([it["p"], it["m"]], axis=0), it["m"])
                it["p"] = it["p"] + pm[:CHUNK]
                it["m"] = pm[CHUNK:]
            else:
                it["p"] = it["p"] + _dot(it["p"], it["m"])
    for it in items:
        it["uw"] = _dot(it["p"], it["rhs"])
    for it in items:
        bm = _dot_tn(it["kd"], it["uw"])
        it["b_c"], it["m_c"] = bm[:, :HEAD_DIM], bm[:, HEAD_DIM:]
        if it["q"] is not None:
            ow = _dot(it["qk"], it["uw"])
            it["o_c"], it["q_eff"] = ow[:, :HEAD_DIM], it["qe"] - ow[:, HEAD_DIM:]


def _dn_kernel(*refs, nchunk, with_output):
    if with_output:
        (alog_ref, dtb_ref, kf_ref, vf_ref, qf_ref, abf_ref, kb_ref, vb_ref, qb_ref, abb_ref,
         s0f_ref, s0b_ref, of_ref, ob_ref, sf_ref, sb_ref, stf, stb) = refs
    else:
        (alog_ref, dtb_ref, kf_ref, vf_ref, abf_ref, kb_ref, vb_ref, abb_ref,
         s0f_ref, s0b_ref, sf_ref, sb_ref, stf, stb) = refs
        qf_ref = qb_ref = of_ref = ob_ref = None
    hh = pl.program_id(1)

    @pl.when(pl.program_id(2) == 0)
    def _():
        stf[...] = s0f_ref[...]
        stb[...] = s0b_ref[...]

    def prep(k_ref, v_ref, q_ref, ab_ref, d):
        k = _l2norm(_silu(k_ref[...]))
        v = _silu(v_ref[...])
        q = None
        if with_output:
            q = _l2norm(_silu(q_ref[...])) * (HEAD_DIM ** -0.5)
        a = ab_ref[pl.ds(d * HEADS + hh, 1), :]
        bt = ab_ref[pl.ds(2 * HEADS + d * HEADS + hh, 1), :]
        g = -jnp.exp(alog_ref[d, hh]) * _softplus(a + dtb_ref[d, hh])
        return k, v, q, g, _sigmoid(bt)

    kf, vf, qf, gf, bf = prep(kf_ref, vf_ref, qf_ref, abf_ref, 0)
    kb, vb, qb, gb, bb = prep(kb_ref, vb_ref, qb_ref, abb_ref, 1)
    def chunk_item(k, v, q, g, beta, ci, rev):
        lo = ci * CHUNK
        return {"k": k[lo:lo + CHUNK], "v": v[lo:lo + CHUNK], "q": q[lo:lo + CHUNK] if with_output else None,
                "g": g[:, lo:lo + CHUNK], "beta": beta[:, lo:lo + CHUNK], "rev": rev, "lo": lo}

    items = []
    for ci in range(nchunk):
        items.append(chunk_item(kf, vf, qf, gf, bf, ci, False))
        items.append(chunk_item(kb, vb, qb, gb, bb, nchunk - 1 - ci, True))
    _dn_prepare(items)
    state = {False: stf[...], True: stb[...]}
    for it in items:
        s = state[it["rev"]]
        if with_output:
            o_ref = ob_ref if it["rev"] else of_ref
            o_ref[pl.ds(it["lo"], CHUNK), :] = it["o_c"] + _dot(it["q_eff"], s)
        state[it["rev"]] = s * it["e_last"] + it["b_c"] - _dot(it["m_c"], s)
    stf[...] = state[False]
    stb[...] = state[True]
    sf_ref[...] = state[False]
    sb_ref[...] = state[True]


def _deltanet_scan(h, ab_t, cols, a_log, dt_bias, s0, *, tblk, with_output):
    bsz, l, _ = h.shape
    nt = l // tblk
    c_k, c_v, c_q = (c // LANE for c in cols)
    rep = HEADS // DN_QK_HEADS

    def tok(cb, shared, rev):
        def imap(b, hh, t):
            return (b, (nt - 1 - t) if rev else t, cb + (hh // rep if shared else hh))
        return pl.BlockSpec((None, tblk, LANE), imap)

    def ab_spec(rev):
        return pl.BlockSpec((None, 4 * HEADS, tblk), lambda b, hh, t: (b, 0, (nt - 1 - t) if rev else t))

    smem = pl.BlockSpec(memory_space=pltpu.SMEM)
    st_spec = pl.BlockSpec((None, None, HEAD_DIM, HEAD_DIM), lambda b, hh, t: (b, hh, 0, 0))
    if with_output:
        in_specs = [smem, smem, tok(c_k, True, False), tok(c_v, False, False), tok(c_q, True, False), ab_spec(False),
                    tok(c_k, True, True), tok(c_v, False, True), tok(c_q, True, True), ab_spec(True),
                    st_spec, st_spec]
        args = [a_log, dt_bias, h, h, h, ab_t, h, h, h, ab_t, s0[0], s0[1]]
    else:
        in_specs = [smem, smem, tok(c_k, True, False), tok(c_v, False, False), ab_spec(False),
                    tok(c_k, True, True), tok(c_v, False, True), ab_spec(True),
                    st_spec, st_spec]
        args = [a_log, dt_bias, h, h, ab_t, h, h, ab_t, s0[0], s0[1]]
    st_shape = jax.ShapeDtypeStruct((bsz, HEADS, HEAD_DIM, HEAD_DIM), F32)
    out_specs = [st_spec, st_spec]
    out_shape = [st_shape, st_shape]
    if with_output:
        o_shape = jax.ShapeDtypeStruct((bsz, l, HEADS * HEAD_DIM), F32)
        out_specs = [pl.BlockSpec((None, tblk, LANE), lambda b, hh, t: (b, t, hh)),
                     pl.BlockSpec((None, tblk, LANE), lambda b, hh, t: (b, nt - 1 - t, hh))] + out_specs
        out_shape = [o_shape, o_shape] + out_shape
    outs = pl.pallas_call(
        functools.partial(_dn_kernel, nchunk=tblk // CHUNK, with_output=with_output),
        name="deltanet_scan",
        grid=(bsz, HEADS, nt),
        in_specs=in_specs,
        out_specs=out_specs,
        out_shape=out_shape,
        scratch_shapes=[pltpu.VMEM((HEAD_DIM, HEAD_DIM), F32), pltpu.VMEM((HEAD_DIM, HEAD_DIM), F32)],
        compiler_params=_cparams(("parallel", "parallel", "arbitrary")),
    )(*args)
    if with_output:
        return outs[0], outs[1], (outs[2], outs[3])
    return None, None, (outs[0], outs[1])


def _combine_kernel(of_ref, ob_ref, z_ref, w_ref, o_ref, *, use_silu):
    o = of_ref[...] + ob_ref[...]
    o = o * lax.rsqrt(jnp.mean(o * o, axis=-1, keepdims=True) + RMS_EPS) * w_ref[...]
    z = z_ref[...]
    o_ref[...] = (o * (_silu(z) if use_silu else _sigmoid(z))).astype(o_ref.dtype)


def _combine(o_f, o_b, h, col_z, norm_w, *, tl, use_silu):
    bsz, l, c = o_f.shape
    cz = col_z // LANE
    spec = pl.BlockSpec((None, tl, LANE), lambda b, i, hh: (b, i, hh))
    return pl.pallas_call(
        functools.partial(_combine_kernel, use_silu=use_silu),
        name="mixer_norm_gate",
        grid=(bsz, l // tl, c // LANE),
        in_specs=[spec, spec,
                  pl.BlockSpec((None, tl, LANE), lambda b, i, hh: (b, i, cz + hh)),
                  pl.BlockSpec((1, LANE), lambda b, i, hh: (0, 0))],
        out_specs=spec,
        out_shape=jax.ShapeDtypeStruct((bsz, l, c), BF16),
        compiler_params=_cparams(("parallel", "parallel", "arbitrary")),
    )(o_f, o_b, h, norm_w)


def _dot_f32(a, b):
    return jnp.dot(a, b, precision=lax.Precision.HIGHEST, preferred_element_type=F32)


def _hy_filt_kernel(w1_ref, b1_ref, w2_ref, b2_ref, w3_ref, o_ref, hid_ref, *, l_total, tl):
    i = pl.program_id(0)
    j = pl.program_id(1)

    @pl.when(j == 0)
    def _():
        row = (_iota2((tl, LANE), 0) + i * tl).astype(F32)
        lane = _iota2((tl, LANE), 1)
        t = row * (1.0 / (l_total - 1))
        ang = row * (2.0 * math.pi / l_total)
        bidx = jnp.where(lane <= HY_BANDS, lane - 1, lane - 1 - HY_BANDS).astype(F32)
        band = 1e-4 + bidx * ((HY_BANDS - 1 - 1e-4) / (HY_BANDS - 1))
        arg = band * ang
        feats = jnp.where(lane == 0, t,
                          jnp.where(lane <= HY_BANDS, jnp.cos(arg),
                                    jnp.where(lane <= 2 * HY_BANDS, -jnp.sin(arg), 0.0)))
        hid = jnp.sin(_dot_f32(feats, w1_ref[...]) + b1_ref[...])
        hid_ref[...] = jnp.sin(_dot_f32(hid, w2_ref[...]) + b2_ref[...])

    hcol = _dot_f32(hid_ref[...], w3_ref[...])
    width = hcol.shape[1]
    ch = _iota2((tl, width), 1).astype(F32)
    lo = math.log(HY_DECAY_TARGET) / HY_SLOW_DECAY
    hi = math.log(HY_DECAY_TARGET) / HY_FAST_DECAY
    delta = jnp.abs(lo + ch * ((hi - lo) / (width - 1)))
    trow = (_iota2((tl, width), 0) + i * tl)
    out = hcol * jnp.exp(-(trow.astype(F32) * (1.0 / (l_total - 1))) * delta)
    drop = jnp.logical_and(trow == 0, jnp.bitwise_and(j, 1) == 1)
    o_ref[...] = jnp.where(drop, 0.0, out)


def _hy_filters(l_total, w1p, b1, w2, b2, w3):
    tl = min(l_total, 512)
    n = w3.shape[1]
    full = lambda a: pl.BlockSpec(a.shape, lambda i, j: (0,) * a.ndim)
    return pl.pallas_call(
        functools.partial(_hy_filt_kernel, l_total=l_total, tl=tl),
        name="hy_filters",
        grid=(l_total // tl, n // BRANCH_WIDTH),
        in_specs=[full(w1p), full(b1), full(w2), full(b2),
                  pl.BlockSpec((HY_HIDDEN, BRANCH_WIDTH), lambda i, j: (0, j))],
        out_specs=pl.BlockSpec((tl, BRANCH_WIDTH), lambda i, j: (i, j)),
        out_shape=jax.ShapeDtypeStruct((l_total, n), F32),
        scratch_shapes=[pltpu.VMEM((tl, HY_HIDDEN), F32)],
        compiler_params=_cparams(("parallel", "arbitrary")),
    )(w1p, b1, w2, b2, w3)


@functools.lru_cache(maxsize=None)
def _fft_tables(n1, n2):
    n = n1 * n2
    n2h = n2 // 2
    k2 = np.arange(n2, dtype=np.int64)[:, None]
    m = np.arange(n2h, dtype=np.int64)[None, :]
    ta = np.zeros((n1, 2 * n2, 2 * n2h))
    for a in range(n1):
        th = 2.0 * np.pi * ((k2 * (a + n1 * m)) % n) / n
        c, s = np.cos(th), np.sin(th)
        ta[a] = np.block([[c, s], [-s, c]])
    tai = np.transpose(ta, (0, 2, 1)) / n
    k1 = np.arange(n1, dtype=np.int64)
    ph = 2.0 * np.pi * ((k1[:, None] * k1[None, :]) % n1) / n1
    c, s = np.cos(ph), np.sin(ph)
    fb = np.block([[c, s], [-s, c]])
    fs = fb[:n1] + fb[n1:]
    fsr = np.stack([fs[(-k1 - e) % n1] for e in (0, 1)])
    return tuple(np.asarray(t, np.float32) for t in (ta, tai, fb, fb.T, fs, fsr))


def _fft_split(l_total):
    n = 2 * l_total
    n1 = min(128, l_total // 8)
    return n1, n // n1


def _hy_stage_a_kernel(gr_ref, cr_ref, gi_ref, ci_ref, xr_ref, xi_ref, t_ref, o_ref, *, planes, n2):
    del gr_ref, cr_ref, gi_ref, ci_ref
    for j in range(8):
        x = jnp.concatenate([xr_ref[:, j, :], xi_ref[:, j, :]], axis=0).astype(BF16)
        y = jnp.dot(t_ref[j], x, preferred_element_type=F32)
        for p in range(planes):
            o_ref[p, :, j, :] = y[p * n2:(p + 1) * n2].astype(o_ref.dtype)


def _hy_stage_a(x4, pairs, tab, *, ct, planes, n2):
    _, n2h, n1, _ = x4.shape
    npair = len(pairs)
    nct = BRANCH_WIDTH // ct
    gr = jnp.asarray([p[0][0] for p in pairs], jnp.int32)
    cr = jnp.asarray([p[0][1] for p in pairs], jnp.int32)
    gi = jnp.asarray([p[1][0] for p in pairs], jnp.int32)
    ci = jnp.asarray([p[1][1] for p in pairs], jnp.int32)
    grid_spec = pltpu.PrefetchScalarGridSpec(
        num_scalar_prefetch=4,
        grid=(npair, nct, n1 // 8),
        in_specs=[pl.BlockSpec((None, n2h, 8, ct), lambda p, c, g, gr, cr, gi, ci: (gr[p], 0, g, cr[p] * nct + c)),
                  pl.BlockSpec((None, n2h, 8, ct), lambda p, c, g, gr, cr, gi, ci: (gi[p], 0, g, ci[p] * nct + c)),
                  pl.BlockSpec((8, planes * n2, 2 * n2h), lambda p, c, g, *_: (g, 0, 0))],
        out_specs=pl.BlockSpec((None, planes, n2, 8, ct), lambda p, c, g, *_: (p, 0, 0, g, c)),
    )
    return pl.pallas_call(
        functools.partial(_hy_stage_a_kernel, planes=planes, n2=n2),
        name="hy_stage_a",
        grid_spec=grid_spec,
        out_shape=jax.ShapeDtypeStruct((npair, planes, n2, n1, BRANCH_WIDTH), BF16),
        compiler_params=_cparams(("parallel", "parallel", "arbitrary")),
    )(gr, cr, gi, ci, x4, x4, tab)


def _hy_spec_kernel(c_ref, call_ref, fs_ref, fsr_ref, o_ref, *, kg, n2):
    g = pl.program_id(2)
    for k in range(kg):
        s = jnp.dot(fs_ref[...], jnp.concatenate([c_ref[0, k], c_ref[1, k]], axis=0), preferred_element_type=F32)
        k2 = g * kg + k
        kp = jnp.where(k2 == 0, 0, n2 - k2)
        fsr = fsr_ref[1] if k else jnp.where(g == 0, fsr_ref[0], fsr_ref[1])
        t = jnp.dot(fsr, jnp.concatenate([call_ref[0, kp], call_ref[1, kp]], axis=0), preferred_element_type=F32)
        o_ref[0, k] = 0.5 * (s + t)
        o_ref[1, k] = 0.5 * (s - t)


def _hy_spectrum(cs, fs, fsr, *, ct, kg):
    no, _, n2, n1, c = cs.shape
    return pl.pallas_call(
        functools.partial(_hy_spec_kernel, kg=kg, n2=n2),
        name="hy_spectrum",
        grid=(no, c // ct, n2 // kg),
        in_specs=[pl.BlockSpec((None, 2, kg, n1, ct), lambda o, j, g: (o, 0, g, 0, j)),
                  pl.BlockSpec((None, 2, n2, n1, ct), lambda o, j, g: (o, 0, 0, 0, j)),
                  pl.BlockSpec(fs.shape, lambda o, j, g: (0, 0)),
                  pl.BlockSpec(fsr.shape, lambda o, j, g: (0, 0, 0))],
        out_specs=pl.BlockSpec((None, 2, kg, n1, ct), lambda o, j, g: (o, 0, g, 0, j)),
        out_shape=jax.ShapeDtypeStruct((no, 2, n2, n1, c), F32),
        compiler_params=_cparams(("parallel", "parallel", "arbitrary")),
    )(cs, cs, fs, fsr)


def _hy_stage_b_kernel(c_ref, k_ref, fb_ref, fbi_ref, o_ref, *, kg, n1):
    for k in range(kg):
        z = jnp.dot(fb_ref[...], jnp.concatenate([c_ref[0, k], c_ref[1, k]], axis=0), preferred_element_type=F32)
        zr, zi = z[:n1], z[n1:]
        kr, ki = k_ref[0, k], k_ref[1, k]
        y = jnp.concatenate([zr * kr - zi * ki, zr * ki + zi * kr], axis=0).astype(BF16)
        d = jnp.dot(fbi_ref[...], y, preferred_element_type=F32)
        o_ref[0, k] = d[:n1].astype(o_ref.dtype)
        o_ref[1, k] = d[n1:].astype(o_ref.dtype)


def _hy_stage_b(cs, spec, order, fb, fbi, *, ct, kg):
    _, _, n2, n1, c = cs.shape
    return pl.pallas_call(
        functools.partial(_hy_stage_b_kernel, kg=kg, n1=n1),
        name="hy_stage_b",
        grid=(c // ct, n2 // kg),
        in_specs=[pl.BlockSpec((None, 2, kg, n1, ct), lambda j, g: (0, 0, g, 0, j)),
                  pl.BlockSpec((None, 2, kg, n1, ct), lambda j, g: (order, 0, g, 0, j)),
                  pl.BlockSpec(fb.shape, lambda j, g: (0, 0)),
                  pl.BlockSpec(fbi.shape, lambda j, g: (0, 0))],
        out_specs=pl.BlockSpec((2, kg, n1, ct), lambda j, g: (0, g, 0, j)),
        out_shape=jax.ShapeDtypeStruct((2, n2, n1, c), BF16),
        compiler_params=_cparams(("parallel", "arbitrary")),
    )(cs, spec, fb, fbi)


def _hy_stage_c_kernel(d_ref, t_ref, u_ref, g_ref, skip_ref, o_ref, *, n2h):
    skip = skip_ref[...]
    for j in range(8):
        d = jnp.concatenate([d_ref[0, :, j, :], d_ref[1, :, j, :]], axis=0)
        y = jnp.dot(t_ref[j], d, preferred_element_type=F32)
        for b in range(2):
            yb = y[b * n2h:(b + 1) * n2h]
            o_ref[b, :, j, :] = (g_ref[b, :, j, :] * (yb + skip * u_ref[b, :, j, :])).astype(o_ref.dtype)


def _hy_stage_c(d, tai, u4, ucol, z4, gcol, skip, *, ct, out_dtype):
    _, n2, n1, c = d.shape
    n2h = n2 // 2
    uc, gc = ucol // ct, gcol // ct
    return pl.pallas_call(
        functools.partial(_hy_stage_c_kernel, n2h=n2h),
        name="hy_stage_c",
        grid=(c // ct, n1 // 8),
        in_specs=[pl.BlockSpec((2, n2, 8, ct), lambda j, g: (0, 0, g, j)),
                  pl.BlockSpec((8, 2 * n2h, 2 * n2), lambda j, g: (g, 0, 0)),
                  pl.BlockSpec((2, n2h, 8, ct), lambda j, g: (0, 0, g, uc + j)),
                  pl.BlockSpec((2, n2h, 8, ct), lambda j, g: (0, 0, g, gc + j)),
                  pl.BlockSpec((1, ct), lambda j, g: (0, j))],
        out_specs=pl.BlockSpec((2, n2h, 8, ct), lambda j, g: (0, 0, g, j)),
        out_shape=jax.ShapeDtypeStruct((2, n2h, n1, c), out_dtype),
        compiler_params=_cparams(("parallel", "arbitrary")),
    )(d, tai, u4, z4, skip)


def _hyena(h, filt_params, skip):
    bsz, l, _ = h.shape
    assert bsz == 2
    n1, n2 = _fft_split(l)
    n2h = n2 // 2
    ta, tai, fb, fbi, fs, fsr = (jnp.asarray(t).astype(BF16) for t in _fft_tables(n1, n2))
    ct = 512
    kg = 8

    hfilt = _hy_filters(l, *filt_params)
    cs_k = _hy_stage_a(hfilt.reshape(1, n2h, n1, -1), [((0, 2 * o), (0, 2 * o + 1)) for o in range(HY_ORDER)],
                       ta, ct=ct, planes=2, n2=n2)
    spec = _hy_spectrum(cs_k, fs, fsr, ct=256, kg=kg)

    z4 = h.reshape(bsz, n2h, n1, -1)
    u4, ucol = z4, COL_HY
    for o in range(HY_ORDER):
        last = o == HY_ORDER - 1
        cs = _hy_stage_a(u4, [((0, ucol // BRANCH_WIDTH), (1, ucol // BRANCH_WIDTH))], ta, ct=ct, planes=2, n2=n2)
        d = _hy_stage_b(cs, spec, o, fb, fbi, ct=ct, kg=kg)
        u4 = _hy_stage_c(d, tai, u4, ucol, z4, COL_HY + (o + 1) * BRANCH_WIDTH, skip[o:o + 1], ct=ct,
                         out_dtype=BF16 if last else F32)
        ucol = 0
    return u4.reshape(bsz, l, BRANCH_WIDTH)


def _conv_tables(hy_conv_w, hy_conv_b, dn_conv_q, dn_conv_k, dn_conv_v):
    depth = hy_conv_w.shape[0]
    ident = lambda n: jnp.broadcast_to(jnp.asarray([0.0, 1.0, 0.0], F32)[None, :, None], (depth, 3, n))
    taps = jnp.concatenate([ident(COL_HY), hy_conv_w, ident(COL_DK - COL_FF), dn_conv_k, dn_conv_v,
                            ident(COL_DQ - COL_HQ), dn_conv_q, ident(COL_PAD - COL_DZ)], axis=2)
    bias = jnp.concatenate([jnp.zeros((depth, 1, COL_HY), F32), hy_conv_b[:, None, :],
                            jnp.zeros((depth, 1, COL_PAD - COL_FF), F32)], axis=2)
    return taps, bias


def _token_mixers(u_src, sh, sc, w_t, conv_t, layer, lp, states, *, is_grid, with_output, tm):
    bsz, l, _ = u_src.shape
    tblk = min(l, 512)
    taps, bias = (a[layer:layer + 1] for a in conv_t)
    if with_output:
        h = _in_proj(u_src, sh, sc, w_t, layer, jnp.asarray(SOURCE_ROWS), taps, bias, tm=tm, is_grid=is_grid)
        base = 0
    else:
        cut = lambda a: jnp.concatenate([a[:, :, b * IN_TN:(b + 1) * IN_TN] for b in STATE_BLOCKS], axis=2)
        h = _in_proj(u_src, sh, sc, w_t, layer, jnp.asarray(SOURCE_ROWS[list(STATE_BLOCKS)]), cut(taps), cut(bias),
                     tm=tm, is_grid=is_grid)
        base = COL_FF
    col = lambda c: c - base
    c_ab = COL_AB if with_output else COL_HQ - COL_FF
    ab_t = jnp.swapaxes(h[:, :, c_ab:c_ab + 4 * HEADS], 1, 2)
    hg_f, hg_b, hg_states = _hgrn2_scan(h, (col(COL_FF), col(COL_FB), col(COL_HI), col(COL_HQ)),
                                        lp["lb_f"], lp["lb_b"], states[0], tblk=tblk, with_output=with_output)
    dn_f, dn_b, dn_states = _deltanet_scan(h, ab_t, (col(COL_DK), col(COL_DV), col(COL_DQ)),
                                           lp["dn_a_log"], lp["dn_dt_bias"], states[1],
                                           tblk=tblk, with_output=with_output)
    new_states = (hg_states, dn_states)
    if not with_output:
        return None, new_states
    tl = min(l, 2048)
    hg_out = _combine(hg_f, hg_b, h, COL_HGATE, lp["hg_norm_w"], tl=tl, use_silu=False)
    dn_out = _combine(dn_f, dn_b, h, COL_DZ, lp["dn_norm_w"], tl=tl, use_silu=True)
    hy_out = _hyena(h, lp["hy_filt"], lp["hy_skip"])
    ysum = _branch_merge([hy_out, hg_out, dn_out], lp["w_branch"], layer, h, tm=min(l, 1024), tn=512)
    return ysum, new_states


def kernel(x, c, ctx, c_ctx, w_ada, b_ada, w_in, hy_conv_w, hy_conv_b, hy_filt_w1, hy_filt_b1, hy_filt_w2, hy_filt_b2, hy_filt_w3, hy_skip, hg_lb_logits, hg_norm_w, dn_conv_q, dn_conv_k, dn_conv_v, dn_a_log, dn_dt_bias, dn_norm_w, w_branch, w_out, ln1_g, ln1_b, w_ff1, w_ff2, ln2_g, ln2_b):
    depth = w_in.shape[0]
    bsz, _, d = x.shape
    alpha = (2 * depth) ** 0.25
    p = jax.nn.softmax(hg_lb_logits.astype(F32), axis=1)
    lower = jnp.cumsum(p, axis=1) - p[:, :1]
    cs = jnp.concatenate([c, c_ctx[None], jnp.zeros((8 - bsz - 1, d), F32)], axis=0)
    w_r = jnp.swapaxes(w_in, 1, 2)
    w_b = w_branch.astype(BF16)
    w_o = w_out.astype(BF16)
    w_1 = w_ff1.astype(BF16)
    w_2 = w_ff2.astype(BF16)
    b_ada3 = b_ada[:, None, :]
    conv_t = _conv_tables(hy_conv_w, hy_conv_b, dn_conv_q, dn_conv_k, dn_conv_v)
    h_ctx = ctx
    for l in range(depth):
        last = l == depth - 1
        row = lambda v: v.reshape(1, -1)
        lp = {
            "lb_f": row(lower[0, l]), "lb_b": row(lower[1, l]),
            "hg_norm_w": row(hg_norm_w[l]), "dn_norm_w": row(dn_norm_w[l]),
            "dn_a_log": dn_a_log[l], "dn_dt_bias": dn_dt_bias[l], "hy_skip": hy_skip[l],
            "hy_filt": (jnp.pad(hy_filt_w1[l], ((0, LANE - HY_EMB), (0, 0))), row(hy_filt_b1[l]),
                        hy_filt_w2[l], row(hy_filt_b2[l]), hy_filt_w3[l]),
            "w_branch": w_b,
        }
        g1n, b1n, g2n, b2n = row(ln1_g[l]), row(ln1_b[l]), row(ln2_g[l]), row(ln2_b[l])
        mod = _ada_mod(cs, w_ada, b_ada3, l)
        m_lat = [mod[:bsz, i * d:(i + 1) * d][:, None, :] for i in range(6)]
        m_ctx = [jnp.broadcast_to(mod[bsz, i * d:(i + 1) * d][None, None, :], (bsz, 1, d)) for i in range(6)]
        zero = jnp.zeros((bsz, HEADS, HEAD_DIM, HEAD_DIM), F32)
        init = ((zero, zero), (zero, zero))

        lc = h_ctx.shape[1]
        y_ctx, ctx_states = _token_mixers(h_ctx, m_ctx[0], m_ctx[1], w_r, conv_t, l, lp, init,
                                          is_grid=False, with_output=not last, tm=lc)
        if not last:
            h_ctx = _mm_ln(y_ctx, w_o, l, h_ctx, m_ctx[2], g1n, b1n, tm=lc, tk=d, alpha=alpha)
            mid = _mm_mod(h_ctx, m_ctx[3], m_ctx[4], w_1, l, tm=lc, tn=1024, relu2=True, out_dtype=BF16)
            h_ctx = _mm_ln(mid, w_2, l, h_ctx, m_ctx[5], g2n, b2n, tm=lc, tk=1024, alpha=alpha)

        lx = x.shape[1]
        y, _ = _token_mixers(x, m_lat[0], m_lat[1], w_r, conv_t, l, lp, ctx_states,
                             is_grid=True, with_output=True, tm=min(lx, 1024))
        x = _mm_ln(y, w_o, l, x, m_lat[2], g1n, b1n, tm=min(lx, 512), tk=d, alpha=alpha)
        mid = _mm_mod(x, m_lat[3], m_lat[4], w_1, l, tm=min(lx, 1024), tn=1024, relu2=True, out_dtype=BF16)
        x = _mm_ln(mid, w_2, l, x, m_lat[5], g2n, b2n, tm=min(lx, 1024), tk=512, alpha=alpha)
    return x
```

```python
import functools
import math

import numpy as np
import jax
import jax.numpy as jnp
from jax import lax
from jax.experimental import pallas as pl
from jax.experimental.pallas import tpu as pltpu

F32 = jnp.float32
BF16 = jnp.bfloat16

D_MODEL = 2048
GRID_W = 64
BRANCH_WIDTH = 1024
N_BRANCH = 3
HY_ORDER = 2
HY_EMB = 33
HY_BANDS = (HY_EMB - 1) // 2
HY_HIDDEN = 64
HY_DECAY_TARGET = 1e-2
HY_FAST_DECAY = 0.3
HY_SLOW_DECAY = 1.5
HEADS = 8
HEAD_DIM = 128
DN_QK_HEADS = 4
HG_F_MIN = 1e-30
CHUNK = 64
LN_EPS = 1e-5
RMS_EPS = 1e-6
LANE = 128

IN_TN = 1024
COL_GATE = 0
COL_HY = 6144
COL_FF = 9216
COL_FB = 10240
COL_HI = 11264
COL_DK = 12288
COL_DV = 12800
COL_AB = 13824
COL_HQ = 14336
COL_HGATE = 15360
COL_DQ = 16384
COL_DZ = 17408
COL_PAD = 18432
SOURCE_ROWS = np.asarray([11296 + IN_TN * k for k in range(6)] + [8224 + IN_TN * k for k in range(3)]
                         + [0, 1024, 2048, 3072, 4096, 4640, 5664, 6688, 7200], np.int32)
CONV_WINDOWS = np.asarray([COL_HY <= IN_TN * j < COL_FF or COL_DK <= IN_TN * j < COL_HQ or IN_TN * j == COL_DQ
                           for j in range(COL_PAD // IN_TN)], np.int32)
STATE_BLOCKS = tuple(range(COL_FF // IN_TN, COL_HQ // IN_TN))

VMEM_LIMIT = 56 * 1024 * 1024


def _cparams(sem):
    return pltpu.CompilerParams(dimension_semantics=sem, vmem_limit_bytes=VMEM_LIMIT)


def _dot(a, b):
    return jnp.dot(a.astype(BF16), b.astype(BF16), preferred_element_type=F32)


def _dot_nt(a, b):
    return lax.dot_general(a.astype(BF16), b.astype(BF16), (((1,), (1,)), ((), ())),
                           preferred_element_type=F32)


def _dot_tn(a, b):
    return lax.dot_general(a.astype(BF16), b.astype(BF16), (((0,), (0,)), ((), ())),
                           preferred_element_type=F32)


def _sigmoid(x):
    return 1.0 / (1.0 + jnp.exp(-x))


def _silu(x):
    return x * _sigmoid(x)


def _ada_kernel(c_ref, w_ref, b_ref, o_ref):
    a = _silu(c_ref[...])
    o_ref[...] = _dot(a, w_ref[...]) + b_ref[...]


def _ada_mod(cs, w, b, layer):
    _, d, n = w.shape
    tn = 1024
    return pl.pallas_call(
        _ada_kernel,
        grid=(n // tn,),
        in_specs=[pl.BlockSpec((8, d), lambda j: (0, 0)),
                  pl.BlockSpec((None, d, tn), lambda j: (layer, 0, j)),
                  pl.BlockSpec((None, 1, tn), lambda j: (layer, 0, j))],
        out_specs=pl.BlockSpec((8, tn), lambda j: (0, j)),
        out_shape=jax.ShapeDtypeStruct((8, n), F32),
        compiler_params=_cparams(("arbitrary",)),
        name="ada_mod",
    )(cs, w, b)


def _conv3(x, w, is_grid):
    t = x.shape[0]
    row = _iota2((t, 1), 0)
    if is_grid:
        pos = jnp.bitwise_and(row, GRID_W - 1)
        has_l, has_r = pos != 0, pos != GRID_W - 1
    else:
        has_l, has_r = row != 0, row != t - 1
    xl = jnp.where(has_l, pltpu.roll(x, 1, 0), 0.0)
    xr = jnp.where(has_r, pltpu.roll(x, t - 1, 0), 0.0)
    return xl * w[0:1] + x * w[1:2] + xr * w[2:3]


def _mlp_up_kernel(x_ref, sh_ref, sc_ref, w_ref, o_ref, a_scr):
    @pl.when(pl.program_id(2) == 0)
    def _():
        a_scr[...] = (x_ref[...] * (1.0 + sc_ref[...]) + sh_ref[...]).astype(BF16)

    acc = jnp.dot(a_scr[...], w_ref[...].astype(BF16), preferred_element_type=F32)
    o_ref[...] = jnp.square(jnp.maximum(acc, 0.0)).astype(o_ref.dtype)


def _mlp_up(x, sh, sc, w, layer, *, tm, tn):
    bsz, l, d = x.shape
    n = w.shape[2]
    return pl.pallas_call(
        _mlp_up_kernel,
        grid=(bsz, l // tm, n // tn),
        in_specs=[pl.BlockSpec((None, tm, d), lambda b, i, j: (b, i, 0)),
                  pl.BlockSpec((None, 1, d), lambda b, i, j: (b, 0, 0)),
                  pl.BlockSpec((None, 1, d), lambda b, i, j: (b, 0, 0)),
                  pl.BlockSpec((None, d, tn), lambda b, i, j: (layer, 0, j))],
        out_specs=pl.BlockSpec((None, tm, tn), lambda b, i, j: (b, i, j)),
        out_shape=jax.ShapeDtypeStruct((bsz, l, n), BF16),
        scratch_shapes=[pltpu.VMEM((tm, d), BF16)],
        compiler_params=_cparams(("parallel", "parallel", "arbitrary")),
        name="mm_relu2",
    )(x, sh, sc, w)


def _in_proj_kernel(off_ref, conv_ref, x_ref, sh_ref, sc_ref, w_ref, cw_ref, cb_ref, o_ref, a_scr, *, conv_grid):
    del off_ref
    has_conv = conv_ref[pl.program_id(2)] != 0

    @pl.when(pl.program_id(2) == 0)
    def _():
        a_scr[...] = (x_ref[...] * (1.0 + sc_ref[...]) + sh_ref[...]).astype(BF16)

    acc = lax.dot_general(a_scr[...], w_ref[0].astype(BF16), (((1,), (1,)), ((), ())),
                          preferred_element_type=F32)

    @pl.when(has_conv)
    def _():
        o_ref[...] = _conv3(acc, cw_ref[...], conv_grid) + cb_ref[...]

    @pl.when(jnp.logical_not(has_conv))
    def _():
        o_ref[...] = acc


def _in_proj(x, sh, sc, w_t, layer, windows, taps, bias, *, tm, is_grid):
    bsz, l, d = x.shape
    nblk = len(windows)
    assert tm == l or (is_grid and tm % GRID_W == 0)
    row_off = jnp.asarray(SOURCE_ROWS[list(windows)])
    has_conv = jnp.asarray(CONV_WINDOWS[list(windows)])
    grid_spec = pltpu.PrefetchScalarGridSpec(
        num_scalar_prefetch=2,
        grid=(bsz, l // tm, nblk),
        in_specs=[pl.BlockSpec((None, tm, d), lambda b, i, j, off, cv: (b, i, 0)),
                  pl.BlockSpec((None, 1, d), lambda b, i, j, off, cv: (b, 0, 0)),
                  pl.BlockSpec((None, 1, d), lambda b, i, j, off, cv: (b, 0, 0)),
                  pl.BlockSpec((pl.Element(1), pl.Element(IN_TN), pl.Element(d)),
                               lambda b, i, j, off, cv: (layer, pl.multiple_of(off[j], 4 * HEADS), 0)),
                  pl.BlockSpec((None, 3, IN_TN), lambda b, i, j, off, cv: (0, 0, j)),
                  pl.BlockSpec((None, 1, IN_TN), lambda b, i, j, off, cv: (0, 0, j))],
        out_specs=pl.BlockSpec((None, tm, IN_TN), lambda b, i, j, off, cv: (b, i, j)),
        scratch_shapes=[pltpu.VMEM((tm, d), BF16)],
    )
    return pl.pallas_call(
        functools.partial(_in_proj_kernel, conv_grid=is_grid),
        name="mm_in_proj",
        grid_spec=grid_spec,
        out_shape=jax.ShapeDtypeStruct((bsz, l, nblk * IN_TN), F32),
        compiler_params=_cparams(("parallel", "parallel", "arbitrary")),
    )(row_off, has_conv, x, sh, sc, w_t, taps, bias)


def _mm_ln_kernel(a_ref, w_ref, x_ref, gate_ref, g_ref, b_ref, o_ref, acc_ref, *, nk, alpha):
    k = pl.program_id(2)

    @pl.when(k == 0)
    def _():
        acc_ref[...] = jnp.zeros_like(acc_ref)

    acc_ref[...] += jnp.dot(a_ref[...], w_ref[...].astype(BF16), preferred_element_type=F32)

    @pl.when(k == nk - 1)
    def _():
        y = alpha * x_ref[...] + gate_ref[...] * acc_ref[...]
        mu = jnp.mean(y, axis=-1, keepdims=True)
        yc = y - mu
        var = jnp.mean(yc * yc, axis=-1, keepdims=True)
        o_ref[...] = yc * lax.rsqrt(var + LN_EPS) * g_ref[...] + b_ref[...]


def _mm_ln(a, w, layer, x, gate, ln_g, ln_b, *, tm, tk, alpha):
    bsz, l, kdim = a.shape
    d = w.shape[2]
    nk = kdim // tk
    return pl.pallas_call(
        functools.partial(_mm_ln_kernel, nk=nk, alpha=alpha),
        grid=(bsz, l // tm, nk),
        in_specs=[pl.BlockSpec((None, tm, tk), lambda b, i, k: (b, i, k)),
                  pl.BlockSpec((None, tk, d), lambda b, i, k: (layer, k, 0)),
                  pl.BlockSpec((None, tm, d), lambda b, i, k: (b, i, 0)),
                  pl.BlockSpec((None, 1, d), lambda b, i, k: (b, 0, 0)),
                  pl.BlockSpec((1, d), lambda b, i, k: (0, 0)),
                  pl.BlockSpec((1, d), lambda b, i, k: (0, 0))],
        out_specs=pl.BlockSpec((None, tm, d), lambda b, i, k: (b, i, 0)),
        out_shape=jax.ShapeDtypeStruct((bsz, l, d), F32),
        scratch_shapes=[pltpu.VMEM((tm, d), F32)],
        compiler_params=_cparams(("parallel", "parallel", "arbitrary")),
        name="mm_resid_ln",
    )(a, w, x, gate, ln_g, ln_b)


def _branch_kernel(b0_ref, b1_ref, b2_ref, w_ref, g0_ref, g1_ref, g2_ref, o_ref):
    acc = _sigmoid(g0_ref[...]) * jnp.dot(b0_ref[...], w_ref[0].astype(BF16), preferred_element_type=F32)
    acc += _sigmoid(g1_ref[...]) * jnp.dot(b1_ref[...], w_ref[1].astype(BF16), preferred_element_type=F32)
    acc += _sigmoid(g2_ref[...]) * jnp.dot(b2_ref[...], w_ref[2].astype(BF16), preferred_element_type=F32)
    o_ref[...] = acc.astype(o_ref.dtype)


def _branch_merge(branches, w, layer, h, *, tm, tn):
    bsz, l, c = branches[0].shape
    d = w.shape[3]
    nj = d // tn
    br_spec = pl.BlockSpec((None, tm, c), lambda b, i, j: (b, i, 0))

    def gate_spec(g):
        return pl.BlockSpec((None, tm, tn), lambda b, i, j: (b, i, COL_GATE // tn + g * nj + j))

    return pl.pallas_call(
        _branch_kernel,
        grid=(bsz, l // tm, nj),
        in_specs=[br_spec, br_spec, br_spec,
                  pl.BlockSpec((None, N_BRANCH, c, tn), lambda b, i, j: (layer, 0, 0, j)),
                  gate_spec(0), gate_spec(1), gate_spec(2)],
        out_specs=pl.BlockSpec((None, tm, tn), lambda b, i, j: (b, i, j)),
        out_shape=jax.ShapeDtypeStruct((bsz, l, d), BF16),
        compiler_params=_cparams(("parallel", "parallel", "arbitrary")),
        name="branch_merge",
    )(*branches, w, h, h, h)


def _iota2(shape, axis):
    return lax.broadcasted_iota(jnp.int32, shape, axis)


def _split3(g):
    g1 = g.astype(BF16)
    r = g - g1.astype(F32)
    g2 = r.astype(BF16)
    g3 = (r - g2.astype(F32)).astype(BF16)
    return g1, g2, g3


def _cumsum_rows(g, rev):
    row = _iota2((CHUNK, 3 * CHUNK), 0)
    col = jnp.bitwise_and(_iota2((CHUNK, 3 * CHUNK), 1), CHUNK - 1)
    tri = jnp.where((col >= row) if rev else (col <= row), 1.0, 0.0).astype(BF16)
    g3 = jnp.concatenate(_split3(g), axis=0)
    return jnp.dot(tri, g3, preferred_element_type=F32)


def _hg_prepare(items):
    for it in items:
        lb = it["lb"]
        f = lb + (1.0 - lb) * _sigmoid(it["fz"])
        it["g"] = jnp.log(jnp.maximum(f, HG_F_MIN))
        it["kk"] = (1.0 - lb) * _sigmoid(-it["fz"])
    for it in items:
        it["bc"] = _cumsum_rows(it["g"], it["rev"])
    half = CHUNK // 2
    for it in items:
        bc = it["bc"]
        if it["rev"]:
            mid, last = bc[half:half + 1], bc[0:1]
        else:
            mid, last = bc[half - 1:half], bc[CHUNK - 1:CHUNK]
        it["e_last"] = jnp.exp(last)
        it["ks"] = it["kk"] * jnp.exp(last - bc)
        if it["q"] is not None:
            qs = _silu(it["q"])
            it["qd"] = qs * jnp.exp(bc - mid)
            it["kd"] = it["kk"] * jnp.exp(mid - bc)
            it["qs"] = qs * jnp.exp(bc)
    r = _iota2((CHUNK, CHUNK), 0)
    c = _iota2((CHUNK, CHUNK), 1)
    for it in items:
        it["kv"] = _dot_tn(it["v"], it["ks"])
        if it["q"] is not None:
            att = _dot_nt(it["qd"], it["kd"])
            it["att"] = jnp.where((c >= r) if it["rev"] else (c <= r), att, 0.0)
    for it in items:
        if it["q"] is not None:
            it["o_c"] = _dot(it["att"], it["v"])


def _hg_kernel(*refs, nchunk, with_output):
    if with_output:
        (ff_ref, vf_ref, qf_ref, fb_ref, vb_ref, qb_ref, lbf_ref, lbb_ref, s0f_ref, s0b_ref,
         of_ref, ob_ref, sf_ref, sb_ref, stf, stb) = refs
    else:
        (ff_ref, vf_ref, fb_ref, vb_ref, lbf_ref, lbb_ref, s0f_ref, s0b_ref,
         sf_ref, sb_ref, stf, stb) = refs
        qf_ref = qb_ref = of_ref = ob_ref = None

    @pl.when(pl.program_id(2) == 0)
    def _():
        stf[...] = s0f_ref[...]
        stb[...] = s0b_ref[...]

    def chunk_item(f_ref, v_ref, q_ref, lb_ref, ci, rev):
        sl = pl.ds(ci * CHUNK, CHUNK)
        return {"fz": f_ref[sl, :], "v": v_ref[sl, :], "q": q_ref[sl, :] if with_output else None,
                "lb": lb_ref[...], "rev": rev, "sl": sl}

    items = []
    for ci in range(nchunk):
        items.append(chunk_item(ff_ref, vf_ref, qf_ref, lbf_ref, ci, False))
        items.append(chunk_item(fb_ref, vb_ref, qb_ref, lbb_ref, nchunk - 1 - ci, True))
    _hg_prepare(items)
    state = {False: stf[...], True: stb[...]}
    for it in items:
        st = state[it["rev"]]
        if with_output:
            o_ref = ob_ref if it["rev"] else of_ref
            o_ref[it["sl"], :] = it["o_c"] + _dot_nt(it["qs"], st)
        state[it["rev"]] = st * it["e_last"] + it["kv"]
    stf[...] = state[False]
    stb[...] = state[True]
    sf_ref[...] = state[False]
    sb_ref[...] = state[True]


def _hgrn2_scan(h, cols, lb_f, lb_b, s0, *, tblk, with_output):
    bsz, l, _ = h.shape
    nt = l // tblk
    c_ff, c_fb, c_i, c_q = (c // LANE for c in cols)

    def fwd(cb):
        return pl.BlockSpec((None, tblk, LANE), lambda b, hh, t: (b, t, cb + hh))

    def bwd(cb):
        return pl.BlockSpec((None, tblk, LANE), lambda b, hh, t: (b, nt - 1 - t, cb + hh))

    lb_spec = pl.BlockSpec((1, LANE), lambda b, hh, t: (0, hh))
    st_spec = pl.BlockSpec((None, None, HEAD_DIM, HEAD_DIM), lambda b, hh, t: (b, hh, 0, 0))
    if with_output:
        in_specs = [fwd(c_ff), fwd(c_i), fwd(c_q), bwd(c_fb), bwd(c_i), bwd(c_q)]
        args = [h] * 6
    else:
        in_specs = [fwd(c_ff), fwd(c_i), bwd(c_fb), bwd(c_i)]
        args = [h] * 4
    in_specs += [lb_spec, lb_spec, st_spec, st_spec]
    args += [lb_f, lb_b, s0[0], s0[1]]
    st_shape = jax.ShapeDtypeStruct((bsz, HEADS, HEAD_DIM, HEAD_DIM), F32)
    out_specs = [st_spec, st_spec]
    out_shape = [st_shape, st_shape]
    if with_output:
        o_shape = jax.ShapeDtypeStruct((bsz, l, HEADS * HEAD_DIM), F32)
        out_specs = [pl.BlockSpec((None, tblk, LANE), lambda b, hh, t: (b, t, hh)),
                     pl.BlockSpec((None, tblk, LANE), lambda b, hh, t: (b, nt - 1 - t, hh))] + out_specs
        out_shape = [o_shape, o_shape] + out_shape
    outs = pl.pallas_call(
        functools.partial(_hg_kernel, nchunk=tblk // CHUNK, with_output=with_output),
        name="hgrn2_scan",
        grid=(bsz, HEADS, nt),
        in_specs=in_specs,
        out_specs=out_specs,
        out_shape=out_shape,
        scratch_shapes=[pltpu.VMEM((HEAD_DIM, HEAD_DIM), F32), pltpu.VMEM((HEAD_DIM, HEAD_DIM), F32)],
        compiler_params=_cparams(("parallel", "parallel", "arbitrary")),
    )(*args)
    if with_output:
        return outs[0], outs[1], (outs[2], outs[3])
    return None, None, (outs[0], outs[1])


def _l2norm(t):
    return t * lax.rsqrt(jnp.sum(t * t, axis=-1, keepdims=True) + 1e-6)


def _softplus(x):
    return jnp.maximum(x, 0.0) + jnp.log(1.0 + jnp.exp(-jnp.abs(x)))


def _dn_prepare(items):
    r = _iota2((CHUNK, CHUNK), 0)
    c = _iota2((CHUNK, CHUNK), 1)
    eye = r == c
    for it in items:
        rev = it["rev"]
        incl = (c >= r) if rev else (c <= r)
        incl_t = (r >= c) if rev else (r <= c)
        g_b = jnp.broadcast_to(it["g"], (CHUNK, CHUNK))
        gc_col = jnp.sum(jnp.where(incl, g_b, 0.0), axis=1, keepdims=True)
        g_col = jnp.sum(jnp.where(eye, g_b, 0.0), axis=1, keepdims=True)
        gc_row = jnp.sum(jnp.where(incl_t, jnp.broadcast_to(g_col, (CHUNK, CHUNK)), 0.0), axis=0, keepdims=True)
        beta_col = jnp.sum(jnp.where(eye, jnp.broadcast_to(it["beta"], (CHUNK, CHUNK)), 0.0), axis=1,
                           keepdims=True)
        it["decay"] = jnp.where(incl, jnp.exp(jnp.where(incl, gc_col - gc_row, 0.0)), 0.0)
        g_last = gc_col[0:1] if rev else gc_col[CHUNK - 1:CHUNK]
        it["e_last"] = jnp.exp(g_last)
        egc = jnp.exp(gc_col)
        kb = it["k"] * beta_col
        it["kb"] = kb
        it["rhs"] = jnp.concatenate([it["v"] * beta_col, kb * egc], axis=1)
        it["kd"] = it["k"] * jnp.exp(g_last - gc_col)
        if it["q"] is not None:
            it["qe"] = it["q"] * egc
    for it in items:
        it["kk"] = _dot_nt(it["kb"], it["k"])
        if it["q"] is not None:
            it["qk"] = _dot_nt(it["q"], it["k"]) * it["decay"]
    for it in items:
        strict = (c > r) if it["rev"] else (c < r)
        n = jnp.where(strict, it["kk"] * it["decay"], 0.0)
        it["n"] = n
        it["p"] = jnp.where(eye, 1.0, 0.0) - n
    for it in items:
        it["m"] = _dot(it["n"], it["n"])
    for i in range(5):
        for it in items:
            if i < 4:
                pm = _dot(jnp.concatenate([it["p"], it["m"]], axis=0), it["m"])
                it["p"] = it["p"] + pm[:CHUNK]
                it["m"] = pm[CHUNK:]
            else:
                it["p"] = it["p"] + _dot(it["p"], it["m"])
    for it in items:
        it["uw"] = _dot(it["p"], it["rhs"])
    for it in items:
        bm = _dot_tn(it["kd"], it["uw"])
        it["b_c"], it["m_c"] = bm[:, :HEAD_DIM], bm[:, HEAD_DIM:]
        if it["q"] is not None:
            ow = _dot(it["qk"], it["uw"])
            it["o_c"], it["q_eff"] = ow[:, :HEAD_DIM], it["qe"] - ow[:, HEAD_DIM:]


def _dn_kernel(*refs, nchunk, with_output):
    if with_output:
        (alog_ref, dtb_ref, kf_ref, vf_ref, qf_ref, abf_ref, kb_ref, vb_ref, qb_ref, abb_ref,
         s0f_ref, s0b_ref, of_ref, ob_ref, sf_ref, sb_ref, stf, stb) = refs
    else:
        (alog_ref, dtb_ref, kf_ref, vf_ref, abf_ref, kb_ref, vb_ref, abb_ref,
         s0f_ref, s0b_ref, sf_ref, sb_ref, stf, stb) = refs
        qf_ref = qb_ref = of_ref = ob_ref = None
    hh = pl.program_id(1)

    @pl.when(pl.program_id(2) == 0)
    def _():
        stf[...] = s0f_ref[...]
        stb[...] = s0b_ref[...]

    def prep(k_ref, v_ref, q_ref, ab_ref, d):
        k = _l2norm(_silu(k_ref[...]))
        v = _silu(v_ref[...])
        q = None
        if with_output:
            q = _l2norm(_silu(q_ref[...])) * (HEAD_DIM ** -0.5)
        a = ab_ref[pl.ds(d * HEADS + hh, 1), :]
        bt = ab_ref[pl.ds(2 * HEADS + d * HEADS + hh, 1), :]
        g = -jnp.exp(alog_ref[d, hh]) * _softplus(a + dtb_ref[d, hh])
        return k, v, q, g, _sigmoid(bt)

    kf, vf, qf, gf, bf = prep(kf_ref, vf_ref, qf_ref, abf_ref, 0)
    kb, vb, qb, gb, bb = prep(kb_ref, vb_ref, qb_ref, abb_ref, 1)
    def chunk_item(k, v, q, g, beta, ci, rev):
        lo = ci * CHUNK
        return {"k": k[lo:lo + CHUNK], "v": v[lo:lo + CHUNK], "q": q[lo:lo + CHUNK] if with_output else None,
                "g": g[:, lo:lo + CHUNK], "beta": beta[:, lo:lo + CHUNK], "rev": rev, "lo": lo}

    items = []
    for ci in range(nchunk):
        items.append(chunk_item(kf, vf, qf, gf, bf, ci, False))
        items.append(chunk_item(kb, vb, qb, gb, bb, nchunk - 1 - ci, True))
    _dn_prepare(items)
    state = {False: stf[...], True: stb[...]}
    for it in items:
        s = state[it["rev"]]
        if with_output:
            o_ref = ob_ref if it["rev"] else of_ref
            o_ref[pl.ds(it["lo"], CHUNK), :] = it["o_c"] + _dot(it["q_eff"], s)
        state[it["rev"]] = s * it["e_last"] + it["b_c"] - _dot(it["m_c"], s)
    stf[...] = state[False]
    stb[...] = state[True]
    sf_ref[...] = state[False]
    sb_ref[...] = state[True]


def _deltanet_scan(h, ab_t, cols, a_log, dt_bias, s0, *, tblk, with_output):
    bsz, l, _ = h.shape
    nt = l // tblk
    c_k, c_v, c_q = (c // LANE for c in cols)
    rep = HEADS // DN_QK_HEADS

    def tok(cb, shared, rev):
        def imap(b, hh, t):
            return (b, (nt - 1 - t) if rev else t, cb + (hh // rep if shared else hh))
        return pl.BlockSpec((None, tblk, LANE), imap)

    def ab_spec(rev):
        return pl.BlockSpec((None, 4 * HEADS, tblk), lambda b, hh, t: (b, 0, (nt - 1 - t) if rev else t))

    smem = pl.BlockSpec(memory_space=pltpu.SMEM)
    st_spec = pl.BlockSpec((None, None, HEAD_DIM, HEAD_DIM), lambda b, hh, t: (b, hh, 0, 0))
    if with_output:
        in_specs = [smem, smem, tok(c_k, True, False), tok(c_v, False, False), tok(c_q, True, False), ab_spec(False),
                    tok(c_k, True, True), tok(c_v, False, True), tok(c_q, True, True), ab_spec(True),
                    st_spec, st_spec]
        args = [a_log, dt_bias, h, h, h, ab_t, h, h, h, ab_t, s0[0], s0[1]]
    else:
        in_specs = [smem, smem, tok(c_k, True, False), tok(c_v, False, False), ab_spec(False),
                    tok(c_k, True, True), tok(c_v, False, True), ab_spec(True),
                    st_spec, st_spec]
        args = [a_log, dt_bias, h, h, ab_t, h, h, ab_t, s0[0], s0[1]]
    st_shape = jax.ShapeDtypeStruct((bsz, HEADS, HEAD_DIM, HEAD_DIM), F32)
    out_specs = [st_spec, st_spec]
    out_shape = [st_shape, st_shape]
    if with_output:
        o_shape = jax.ShapeDtypeStruct((bsz, l, HEADS * HEAD_DIM), F32)
        out_specs = [pl.BlockSpec((None, tblk, LANE), lambda b, hh, t: (b, t, hh)),
                     pl.BlockSpec((None, tblk, LANE), lambda b, hh, t: (b, nt - 1 - t, hh))] + out_specs
        out_shape = [o_shape, o_shape] + out_shape
    outs = pl.pallas_call(
        functools.partial(_dn_kernel, nchunk=tblk // CHUNK, with_output=with_output),
        name="deltanet_scan",
        grid=(bsz, HEADS, nt),
        in_specs=in_specs,
        out_specs=out_specs,
        out_shape=out_shape,
        scratch_shapes=[pltpu.VMEM((HEAD_DIM, HEAD_DIM), F32), pltpu.VMEM((HEAD_DIM, HEAD_DIM), F32)],
        compiler_params=_cparams(("parallel", "parallel", "arbitrary")),
    )(*args)
    if with_output:
        return outs[0], outs[1], (outs[2], outs[3])
    return None, None, (outs[0], outs[1])


def _combine_kernel(of_ref, ob_ref, z_ref, w_ref, o_ref, *, use_silu):
    o = of_ref[...] + ob_ref[...]
    o = o * lax.rsqrt(jnp.mean(o * o, axis=-1, keepdims=True) + RMS_EPS) * w_ref[...]
    z = z_ref[...]
    o_ref[...] = (o * (_silu(z) if use_silu else _sigmoid(z))).astype(o_ref.dtype)


def _combine(o_f, o_b, h, col_z, norm_w, *, tl, use_silu):
    bsz, l, c = o_f.shape
    cz = col_z // LANE
    spec = pl.BlockSpec((None, tl, LANE), lambda b, i, hh: (b, i, hh))
    return pl.pallas_call(
        functools.partial(_combine_kernel, use_silu=use_silu),
        name="mixer_norm_gate",
        grid=(bsz, l // tl, c // LANE),
        in_specs=[spec, spec,
                  pl.BlockSpec((None, tl, LANE), lambda b, i, hh: (b, i, cz + hh)),
                  pl.BlockSpec((1, LANE), lambda b, i, hh: (0, 0))],
        out_specs=spec,
        out_shape=jax.ShapeDtypeStruct((bsz, l, c), BF16),
        compiler_params=_cparams(("parallel", "parallel", "arbitrary")),
    )(o_f, o_b, h, norm_w)


def _dot_f32(a, b):
    return jnp.dot(a, b, precision=lax.Precision.HIGHEST, preferred_element_type=F32)


def _hy_filt_kernel(w1_ref, b1_ref, w2_ref, b2_ref, w3_ref, o_ref, hid_ref, *, l_total, tl):
    i = pl.program_id(0)
    j = pl.program_id(1)

    @pl.when(j == 0)
    def _():
        row = (_iota2((tl, LANE), 0) + i * tl).astype(F32)
        lane = _iota2((tl, LANE), 1)
        t = row * (1.0 / (l_total - 1))
        ang = row * (2.0 * math.pi / l_total)
        bidx = jnp.where(lane <= HY_BANDS, lane - 1, lane - 1 - HY_BANDS).astype(F32)
        band = 1e-4 + bidx * ((HY_BANDS - 1 - 1e-4) / (HY_BANDS - 1))
        arg = band * ang
        feats = jnp.where(lane == 0, t,
                          jnp.where(lane <= HY_BANDS, jnp.cos(arg),
                                    jnp.where(lane <= 2 * HY_BANDS, -jnp.sin(arg), 0.0)))
        hid = jnp.sin(_dot_f32(feats, w1_ref[...]) + b1_ref[...])
        hid_ref[...] = jnp.sin(_dot_f32(hid, w2_ref[...]) + b2_ref[...])

    hcol = _dot_f32(hid_ref[...], w3_ref[...])
    width = hcol.shape[1]
    ch = _iota2((tl, width), 1).astype(F32)
    lo = math.log(HY_DECAY_TARGET) / HY_SLOW_DECAY
    hi = math.log(HY_DECAY_TARGET) / HY_FAST_DECAY
    delta = jnp.abs(lo + ch * ((hi - lo) / (width - 1)))
    trow = (_iota2((tl, width), 0) + i * tl)
    out = hcol * jnp.exp(-(trow.astype(F32) * (1.0 / (l_total - 1))) * delta)
    drop = jnp.logical_and(trow == 0, jnp.bitwise_and(j, 1) == 1)
    o_ref[...] = jnp.where(drop, 0.0, out)


def _hy_filters(l_total, w1p, b1, w2, b2, w3):
    tl = min(l_total, 512)
    n = w3.shape[1]
    full = lambda a: pl.BlockSpec(a.shape, lambda i, j: (0,) * a.ndim)
    return pl.pallas_call(
        functools.partial(_hy_filt_kernel, l_total=l_total, tl=tl),
        name="hy_filters",
        grid=(l_total // tl, n // BRANCH_WIDTH),
        in_specs=[full(w1p), full(b1), full(w2), full(b2),
                  pl.BlockSpec((HY_HIDDEN, BRANCH_WIDTH), lambda i, j: (0, j))],
        out_specs=pl.BlockSpec((tl, BRANCH_WIDTH), lambda i, j: (i, j)),
        out_shape=jax.ShapeDtypeStruct((l_total, n), F32),
        scratch_shapes=[pltpu.VMEM((tl, HY_HIDDEN), F32)],
        compiler_params=_cparams(("parallel", "arbitrary")),
    )(w1p, b1, w2, b2, w3)


@functools.lru_cache(maxsize=None)
def _fft_tables(n1, n2):
    n = n1 * n2
    n2h = n2 // 2
    k2 = np.arange(n2, dtype=np.int64)[:, None]
    m = np.arange(n2h, dtype=np.int64)[None, :]
    ta = np.zeros((n1, 2 * n2, 2 * n2h))
    for a in range(n1):
        th = 2.0 * np.pi * ((k2 * (a + n1 * m)) % n) / n
        c, s = np.cos(th), np.sin(th)
        ta[a] = np.block([[c, s], [-s, c]])
    tai = np.transpose(ta, (0, 2, 1)) / n
    k1 = np.arange(n1, dtype=np.int64)
    ph = 2.0 * np.pi * ((k1[:, None] * k1[None, :]) % n1) / n1
    c, s = np.cos(ph), np.sin(ph)
    fb = np.block([[c, s], [-s, c]])
    fs = fb[:n1] + fb[n1:]
    fsr = np.stack([fs[(-k1 - e) % n1] for e in (0, 1)])
    return tuple(np.asarray(t, np.float32) for t in (ta, tai, fb, fb.T, fs, fsr))


def _fft_split(l_total):
    n = 2 * l_total
    n1 = min(128, l_total // 8)
    return n1, n // n1


def _hy_stage_a_kernel(gr_ref, cr_ref, gi_ref, ci_ref, xr_ref, xi_ref, t_ref, o_ref, *, planes, n2):
    del gr_ref, cr_ref, gi_ref, ci_ref
    for j in range(8):
        x = jnp.concatenate([xr_ref[:, j, :], xi_ref[:, j, :]], axis=0).astype(BF16)
        y = jnp.dot(t_ref[j], x, preferred_element_type=F32)
        for p in range(planes):
            o_ref[p, :, j, :] = y[p * n2:(p + 1) * n2].astype(o_ref.dtype)


def _hy_stage_a(x4, pairs, tab, *, ct, planes, n2):
    _, n2h, n1, _ = x4.shape
    npair = len(pairs)
    nct = BRANCH_WIDTH // ct
    gr = jnp.asarray([p[0][0] for p in pairs], jnp.int32)
    cr = jnp.asarray([p[0][1] for p in pairs], jnp.int32)
    gi = jnp.asarray([p[1][0] for p in pairs], jnp.int32)
    ci = jnp.asarray([p[1][1] for p in pairs], jnp.int32)
    grid_spec = pltpu.PrefetchScalarGridSpec(
        num_scalar_prefetch=4,
        grid=(npair, nct, n1 // 8),
        in_specs=[pl.BlockSpec((None, n2h, 8, ct), lambda p, c, g, gr, cr, gi, ci: (gr[p], 0, g, cr[p] * nct + c)),
                  pl.BlockSpec((None, n2h, 8, ct), lambda p, c, g, gr, cr, gi, ci: (gi[p], 0, g, ci[p] * nct + c)),
                  pl.BlockSpec((8, planes * n2, 2 * n2h), lambda p, c, g, *_: (g, 0, 0))],
        out_specs=pl.BlockSpec((None, planes, n2, 8, ct), lambda p, c, g, *_: (p, 0, 0, g, c)),
    )
    return pl.pallas_call(
        functools.partial(_hy_stage_a_kernel, planes=planes, n2=n2),
        name="hy_stage_a",
        grid_spec=grid_spec,
        out_shape=jax.ShapeDtypeStruct((npair, planes, n2, n1, BRANCH_WIDTH), BF16),
        compiler_params=_cparams(("parallel", "parallel", "arbitrary")),
    )(gr, cr, gi, ci, x4, x4, tab)


def _hy_spec_kernel(c_ref, call_ref, fs_ref, fsr_ref, o_ref, *, kg, n2):
    g = pl.program_id(2)
    for k in range(kg):
        s = jnp.dot(fs_ref[...], jnp.concatenate([c_ref[0, k], c_ref[1, k]], axis=0), preferred_element_type=F32)
        k2 = g * kg + k
        kp = jnp.where(k2 == 0, 0, n2 - k2)
        fsr = fsr_ref[1] if k else jnp.where(g == 0, fsr_ref[0], fsr_ref[1])
        t = jnp.dot(fsr, jnp.concatenate([call_ref[0, kp], call_ref[1, kp]], axis=0), preferred_element_type=F32)
        o_ref[0, k] = 0.5 * (s + t)
        o_ref[1, k] = 0.5 * (s - t)


def _hy_spectrum(cs, fs, fsr, *, ct, kg):
    no, _, n2, n1, c = cs.shape
    return pl.pallas_call(
        functools.partial(_hy_spec_kernel, kg=kg, n2=n2),
        name="hy_spectrum",
        grid=(no, c // ct, n2 // kg),
        in_specs=[pl.BlockSpec((None, 2, kg, n1, ct), lambda o, j, g: (o, 0, g, 0, j)),
                  pl.BlockSpec((None, 2, n2, n1, ct), lambda o, j, g: (o, 0, 0, 0, j)),
                  pl.BlockSpec(fs.shape, lambda o, j, g: (0, 0)),
                  pl.BlockSpec(fsr.shape, lambda o, j, g: (0, 0, 0))],
        out_specs=pl.BlockSpec((None, 2, kg, n1, ct), lambda o, j, g: (o, 0, g, 0, j)),
        out_shape=jax.ShapeDtypeStruct((no, 2, n2, n1, c), F32),
        compiler_params=_cparams(("parallel", "parallel", "arbitrary")),
    )(cs, cs, fs, fsr)


def _hy_stage_b_kernel(c_ref, k_ref, fb_ref, fbi_ref, o_ref, *, kg, n1):
    for k in range(kg):
        z = jnp.dot(fb_ref[...], jnp.concatenate([c_ref[0, k], c_ref[1, k]], axis=0), preferred_element_type=F32)
        zr, zi = z[:n1], z[n1:]
        kr, ki = k_ref[0, k], k_ref[1, k]
        y = jnp.concatenate([zr * kr - zi * ki, zr * ki + zi * kr], axis=0).astype(BF16)
        d = jnp.dot(fbi_ref[...], y, preferred_element_type=F32)
        o_ref[0, k] = d[:n1].astype(o_ref.dtype)
        o_ref[1, k] = d[n1:].astype(o_ref.dtype)


def _hy_stage_b(cs, spec, order, fb, fbi, *, ct, kg):
    _, _, n2, n1, c = cs.shape
    return pl.pallas_call(
        functools.partial(_hy_stage_b_kernel, kg=kg, n1=n1),
        name="hy_stage_b",
        grid=(c // ct, n2 // kg),
        in_specs=[pl.BlockSpec((None, 2, kg, n1, ct), lambda j, g: (0, 0, g, 0, j)),
                  pl.BlockSpec((None, 2, kg, n1, ct), lambda j, g: (order, 0, g, 0, j)),
                  pl.BlockSpec(fb.shape, lambda j, g: (0, 0)),
                  pl.BlockSpec(fbi.shape, lambda j, g: (0, 0))],
        out_specs=pl.BlockSpec((2, kg, n1, ct), lambda j, g: (0, g, 0, j)),
        out_shape=jax.ShapeDtypeStruct((2, n2, n1, c), BF16),
        compiler_params=_cparams(("parallel", "arbitrary")),
    )(cs, spec, fb, fbi)


def _hy_stage_c_kernel(d_ref, t_ref, u_ref, g_ref, skip_ref, o_ref, *, n2h):
    skip = skip_ref[...]
    for j in range(8):
        d = jnp.concatenate([d_ref[0, :, j, :], d_ref[1, :, j, :]], axis=0)
        y = jnp.dot(t_ref[j], d, preferred_element_type=F32)
        for b in range(2):
            yb = y[b * n2h:(b + 1) * n2h]
            o_ref[b, :, j, :] = (g_ref[b, :, j, :] * (yb + skip * u_ref[b, :, j, :])).astype(o_ref.dtype)


def _hy_stage_c(d, tai, u4, ucol, z4, gcol, skip, *, ct, out_dtype):
    _, n2, n1, c = d.shape
    n2h = n2 // 2
    uc, gc = ucol // ct, gcol // ct
    return pl.pallas_call(
        functools.partial(_hy_stage_c_kernel, n2h=n2h),
        name="hy_stage_c",
        grid=(c // ct, n1 // 8),
        in_specs=[pl.BlockSpec((2, n2, 8, ct), lambda j, g: (0, 0, g, j)),
                  pl.BlockSpec((8, 2 * n2h, 2 * n2), lambda j, g: (g, 0, 0)),
                  pl.BlockSpec((2, n2h, 8, ct), lambda j, g: (0, 0, g, uc + j)),
                  pl.BlockSpec((2, n2h, 8, ct), lambda j, g: (0, 0, g, gc + j)),
                  pl.BlockSpec((1, ct), lambda j, g: (0, j))],
        out_specs=pl.BlockSpec((2, n2h, 8, ct), lambda j, g: (0, 0, g, j)),
        out_shape=jax.ShapeDtypeStruct((2, n2h, n1, c), out_dtype),
        compiler_params=_cparams(("parallel", "arbitrary")),
    )(d, tai, u4, z4, skip)


def _hyena(h, filt_params, skip):
    bsz, l, _ = h.shape
    assert bsz == 2
    n1, n2 = _fft_split(l)
    n2h = n2 // 2
    ta, tai, fb, fbi, fs, fsr = (jnp.asarray(t).astype(BF16) for t in _fft_tables(n1, n2))
    ct = 512
    kg = 8

    hfilt = _hy_filters(l, *filt_params)
    cs_k = _hy_stage_a(hfilt.reshape(1, n2h, n1, -1), [((0, 2 * o), (0, 2 * o + 1)) for o in range(HY_ORDER)],
                       ta, ct=ct, planes=2, n2=n2)
    spec = _hy_spectrum(cs_k, fs, fsr, ct=256, kg=kg)

    z4 = h.reshape(bsz, n2h, n1, -1)
    u4, ucol = z4, COL_HY
    for o in range(HY_ORDER):
        last = o == HY_ORDER - 1
        cs = _hy_stage_a(u4, [((0, ucol // BRANCH_WIDTH), (1, ucol // BRANCH_WIDTH))], ta, ct=ct, planes=2, n2=n2)
        d = _hy_stage_b(cs, spec, o, fb, fbi, ct=ct, kg=kg)
        u4 = _hy_stage_c(d, tai, u4, ucol, z4, COL_HY + (o + 1) * BRANCH_WIDTH, skip[o:o + 1], ct=ct,
                         out_dtype=BF16 if last else F32)
        ucol = 0
    return u4.reshape(bsz, l, BRANCH_WIDTH)


def _conv_tables(hy_conv_w, hy_conv_b, dn_conv_q, dn_conv_k, dn_conv_v):
    depth = hy_conv_w.shape[0]
    ident = lambda n: jnp.broadcast_to(jnp.asarray([0.0, 1.0, 0.0], F32)[None, :, None], (depth, 3, n))
    taps = jnp.concatenate([ident(COL_HY), hy_conv_w, ident(COL_DK - COL_FF), dn_conv_k, dn_conv_v,
                            ident(COL_DQ - COL_AB), dn_conv_q, ident(COL_PAD - COL_DQ - dn_conv_q.shape[2])], axis=2)
    bias = jnp.concatenate([jnp.zeros((depth, 1, COL_HY), F32), hy_conv_b[:, None, :],
                            jnp.zeros((depth, 1, COL_PAD - COL_FF), F32)], axis=2)
    return taps, bias


def _token_mixers(u_src, sh, sc, w_t, conv_t, layer, lp, states, *, is_grid, with_output, tm):
    bsz, l, _ = u_src.shape
    tblk = min(l, 512)
    taps, bias = (a[layer:layer + 1] for a in conv_t)
    if with_output:
        h = _in_proj(u_src, sh, sc, w_t, layer, range(COL_PAD // IN_TN), taps, bias, tm=tm, is_grid=is_grid)
        base = 0
    else:
        lo, hi = STATE_BLOCKS[0] * IN_TN, (STATE_BLOCKS[-1] + 1) * IN_TN
        h = _in_proj(u_src, sh, sc, w_t, layer, STATE_BLOCKS, taps[:, :, lo:hi], bias[:, :, lo:hi],
                     tm=tm, is_grid=is_grid)
        base = COL_FF
    col = lambda c: c - base
    ab_t = jnp.swapaxes(h[:, :, col(COL_AB):col(COL_AB) + 4 * HEADS], 1, 2)
    hg_f, hg_b, hg_states = _hgrn2_scan(h, (col(COL_FF), col(COL_FB), col(COL_HI), col(COL_HQ)),
                                        lp["lb_f"], lp["lb_b"], states[0], tblk=tblk, with_output=with_output)
    dn_f, dn_b, dn_states = _deltanet_scan(h, ab_t, (col(COL_DK), col(COL_DV), col(COL_DQ)),
                                           lp["dn_a_log"], lp["dn_dt_bias"], states[1],
                                           tblk=tblk, with_output=with_output)
    new_states = (hg_states, dn_states)
    if not with_output:
        return None, new_states
    tl = min(l, 2048)
    hg_out = _combine(hg_f, hg_b, h, COL_HGATE, lp["hg_norm_w"], tl=tl, use_silu=False)
    dn_out = _combine(dn_f, dn_b, h, COL_DZ, lp["dn_norm_w"], tl=tl, use_silu=True)
    hy_out = _hyena(h, lp["hy_filt"], lp["hy_skip"])
    ysum = _branch_merge([hy_out, hg_out, dn_out], lp["w_branch"], layer, h, tm=min(l, 1024), tn=512)
    return ysum, new_states


def kernel(x, c, ctx, c_ctx, w_ada, b_ada, w_in, hy_conv_w, hy_conv_b, hy_filt_w1, hy_filt_b1, hy_filt_w2, hy_filt_b2, hy_filt_w3, hy_skip, hg_lb_logits, hg_norm_w, dn_conv_q, dn_conv_k, dn_conv_v, dn_a_log, dn_dt_bias, dn_norm_w, w_branch, w_out, ln1_g, ln1_b, w_ff1, w_ff2, ln2_g, ln2_b):
    depth = w_in.shape[0]
    bsz, _, d = x.shape
    alpha = (2 * depth) ** 0.25
    p = jax.nn.softmax(hg_lb_logits.astype(F32), axis=1)
    lower = jnp.cumsum(p, axis=1) - p[:, :1]
    cs = jnp.concatenate([c, c_ctx[None], jnp.zeros((8 - bsz - 1, d), F32)], axis=0)
    w_r = jnp.swapaxes(w_in, 1, 2)
    w_b, w_o, w_1, w_2 = w_branch, w_out, w_ff1, w_ff2.astype(BF16)
    b_ada3 = b_ada[:, None, :]
    conv_t = _conv_tables(hy_conv_w, hy_conv_b, dn_conv_q, dn_conv_k, dn_conv_v)
    h_ctx = ctx
    for l in range(depth):
        last = l == depth - 1
        row = lambda v: v.reshape(1, -1)
        lp = {
            "lb_f": row(lower[0, l]), "lb_b": row(lower[1, l]),
            "hg_norm_w": row(hg_norm_w[l]), "dn_norm_w": row(dn_norm_w[l]),
            "dn_a_log": dn_a_log[l], "dn_dt_bias": dn_dt_bias[l], "hy_skip": hy_skip[l],
            "hy_filt": (jnp.pad(hy_filt_w1[l], ((0, LANE - HY_EMB), (0, 0))), row(hy_filt_b1[l]),
                        hy_filt_w2[l], row(hy_filt_b2[l]), hy_filt_w3[l]),
            "w_branch": w_b,
        }
        g1n, b1n, g2n, b2n = row(ln1_g[l]), row(ln1_b[l]), row(ln2_g[l]), row(ln2_b[l])
        mod = _ada_mod(cs, w_ada, b_ada3, l)
        m_lat = [mod[:bsz, i * d:(i + 1) * d][:, None, :] for i in range(6)]
        m_ctx = [jnp.broadcast_to(mod[bsz, i * d:(i + 1) * d][None, None, :], (bsz, 1, d)) for i in range(6)]
        zero = jnp.zeros((bsz, HEADS, HEAD_DIM, HEAD_DIM), F32)
        init = ((zero, zero), (zero, zero))

        lc = h_ctx.shape[1]
        y_ctx, ctx_states = _token_mixers(h_ctx, m_ctx[0], m_ctx[1], w_r, conv_t, l, lp, init,
                                          is_grid=False, with_output=not last, tm=lc)
        if not last:
            h_ctx = _mm_ln(y_ctx, w_o, l, h_ctx, m_ctx[2], g1n, b1n, tm=lc, tk=1024, alpha=alpha)
            mid = _mlp_up(h_ctx, m_ctx[3], m_ctx[4], w_1, l, tm=lc, tn=1024)
            h_ctx = _mm_ln(mid, w_2, l, h_ctx, m_ctx[5], g2n, b2n, tm=lc, tk=1024, alpha=alpha)

        lx = x.shape[1]
        y, _ = _token_mixers(x, m_lat[0], m_lat[1], w_r, conv_t, l, lp, ctx_states,
                             is_grid=True, with_output=True, tm=min(lx, 1024))
        x = _mm_ln(y, w_o, l, x, m_lat[2], g1n, b1n, tm=min(lx, 512), tk=1024, alpha=alpha)
        mid = _mlp_up(x, m_lat[3], m_lat[4], w_1, l, tm=min(lx, 1024), tn=1024)
        x = _mm_ln(mid, w_2, l, x, m_lat[5], g2n, b2n, tm=min(lx, 1024), tk=512, alpha=alpha)
    return x
```

```python
import functools
import math

import numpy as np
import jax
import jax.numpy as jnp
from jax import lax
from jax.experimental import pallas as pl
from jax.experimental.pallas import tpu as pltpu

F32 = jnp.float32
BF16 = jnp.bfloat16

D_MODEL = 2048
GRID_W = 64
BRANCH_WIDTH = 1024
N_BRANCH = 3
HY_ORDER = 2
HY_EMB = 33
HY_BANDS = (HY_EMB - 1) // 2
HY_HIDDEN = 64
HY_DECAY_TARGET = 1e-2
HY_FAST_DECAY = 0.3
HY_SLOW_DECAY = 1.5
HEADS = 8
HEAD_DIM = 128
DN_QK_HEADS = 4
HG_F_MIN = 1e-30
CHUNK = 64
LN_EPS = 1e-5
RMS_EPS = 1e-6
LANE = 128

IN_TN = 1024
COL_GATE = 0
COL_HY = 6144
COL_FF = 9216
COL_FB = 10240
COL_HI = 11264
COL_DK = 12288
COL_DV = 12800
COL_AB = 13824
COL_HQ = 14336
COL_HGATE = 15360
COL_DQ = 16384
COL_DZ = 17408
COL_PAD = 18432
SOURCE_ROWS = np.asarray([11296 + IN_TN * k for k in range(6)] + [8224 + IN_TN * k for k in range(3)]
                         + [0, 1024, 2048, 3072, 4096, 4640, 5664, 6688, 7200], np.int32)
CONV_WINDOWS = np.asarray([COL_HY <= IN_TN * j < COL_FF or COL_DK <= IN_TN * j < COL_HQ or IN_TN * j == COL_DQ
                           for j in range(COL_PAD // IN_TN)], np.int32)
STATE_BLOCKS = tuple(range(COL_FF // IN_TN, COL_HQ // IN_TN))

VMEM_LIMIT = 56 * 1024 * 1024


def _cparams(sem):
    return pltpu.CompilerParams(dimension_semantics=sem, vmem_limit_bytes=VMEM_LIMIT)


def _dot(a, b):
    return jnp.dot(a.astype(BF16), b.astype(BF16), preferred_element_type=F32)


def _dot_nt(a, b):
    return lax.dot_general(a.astype(BF16), b.astype(BF16), (((1,), (1,)), ((), ())),
                           preferred_element_type=F32)


def _dot_tn(a, b):
    return lax.dot_general(a.astype(BF16), b.astype(BF16), (((0,), (0,)), ((), ())),
                           preferred_element_type=F32)


def _sigmoid(x):
    return 1.0 / (1.0 + jnp.exp(-x))


def _silu(x):
    return x * _sigmoid(x)


def _ada_kernel(c_ref, w_ref, b_ref, o_ref):
    a = _silu(c_ref[...])
    o_ref[...] = _dot(a, w_ref[...]) + b_ref[...]


def _ada_mod(cs, w, b, layer):
    _, d, n = w.shape
    tn = 1024
    return pl.pallas_call(
        _ada_kernel,
        grid=(n // tn,),
        in_specs=[pl.BlockSpec((8, d), lambda j: (0, 0)),
                  pl.BlockSpec((None, d, tn), lambda j: (layer, 0, j)),
                  pl.BlockSpec((None, 1, tn), lambda j: (layer, 0, j))],
        out_specs=pl.BlockSpec((8, tn), lambda j: (0, j)),
        out_shape=jax.ShapeDtypeStruct((8, n), F32),
        compiler_params=_cparams(("arbitrary",)),
        name="ada_mod",
    )(cs, w, b)


def _conv3(x, w, is_grid):
    t = x.shape[0]
    row = _iota2((t, 1), 0)
    if is_grid:
        pos = jnp.bitwise_and(row, GRID_W - 1)
        has_l, has_r = pos != 0, pos != GRID_W - 1
    else:
        has_l, has_r = row != 0, row != t - 1
    xl = jnp.where(has_l, pltpu.roll(x, 1, 0), 0.0)
    xr = jnp.where(has_r, pltpu.roll(x, t - 1, 0), 0.0)
    return xl * w[0:1] + x * w[1:2] + xr * w[2:3]


def _mlp_up_kernel(x_ref, sh_ref, sc_ref, w_ref, o_ref, a_scr):
    @pl.when(pl.program_id(2) == 0)
    def _():
        a_scr[...] = (x_ref[...] * (1.0 + sc_ref[...]) + sh_ref[...]).astype(BF16)

    acc = jnp.dot(a_scr[...], w_ref[...].astype(BF16), preferred_element_type=F32)
    o_ref[...] = jnp.square(jnp.maximum(acc, 0.0)).astype(o_ref.dtype)


def _mlp_up(x, sh, sc, w, layer, *, tm, tn):
    bsz, l, d = x.shape
    n = w.shape[2]
    return pl.pallas_call(
        _mlp_up_kernel,
        grid=(bsz, l // tm, n // tn),
        in_specs=[pl.BlockSpec((None, tm, d), lambda b, i, j: (b, i, 0)),
                  pl.BlockSpec((None, 1, d), lambda b, i, j: (b, 0, 0)),
                  pl.BlockSpec((None, 1, d), lambda b, i, j: (b, 0, 0)),
                  pl.BlockSpec((None, d, tn), lambda b, i, j: (layer, 0, j))],
        out_specs=pl.BlockSpec((None, tm, tn), lambda b, i, j: (b, i, j)),
        out_shape=jax.ShapeDtypeStruct((bsz, l, n), BF16),
        scratch_shapes=[pltpu.VMEM((tm, d), BF16)],
        compiler_params=_cparams(("parallel", "parallel", "arbitrary")),
        name="mm_relu2",
    )(x, sh, sc, w)


def _in_proj_kernel(off_ref, conv_ref, x_ref, sh_ref, sc_ref, w_ref, cw_ref, cb_ref, o_ref, a_scr, *, conv_grid):
    del off_ref
    has_conv = conv_ref[pl.program_id(2)] != 0

    @pl.when(pl.program_id(2) == 0)
    def _():
        a_scr[...] = (x_ref[...] * (1.0 + sc_ref[...]) + sh_ref[...]).astype(BF16)

    acc = lax.dot_general(a_scr[...], w_ref[0].astype(BF16), (((1,), (1,)), ((), ())),
                          preferred_element_type=F32)

    @pl.when(has_conv)
    def _():
        o_ref[...] = _conv3(acc, cw_ref[...], conv_grid) + cb_ref[...]

    @pl.when(jnp.logical_not(has_conv))
    def _():
        o_ref[...] = acc


def _in_proj(x, sh, sc, w_t, layer, windows, taps, bias, *, tm, is_grid):
    bsz, l, d = x.shape
    nblk = len(windows)
    assert tm == l or (is_grid and tm % GRID_W == 0)
    row_off = jnp.asarray(SOURCE_ROWS[list(windows)])
    has_conv = jnp.asarray(CONV_WINDOWS[list(windows)])
    grid_spec = pltpu.PrefetchScalarGridSpec(
        num_scalar_prefetch=2,
        grid=(bsz, l // tm, nblk),
        in_specs=[pl.BlockSpec((None, tm, d), lambda b, i, j, off, cv: (b, i, 0)),
                  pl.BlockSpec((None, 1, d), lambda b, i, j, off, cv: (b, 0, 0)),
                  pl.BlockSpec((None, 1, d), lambda b, i, j, off, cv: (b, 0, 0)),
                  pl.BlockSpec((pl.Element(1), pl.Element(IN_TN), pl.Element(d)),
                               lambda b, i, j, off, cv: (layer, pl.multiple_of(off[j], 4 * HEADS), 0)),
                  pl.BlockSpec((None, 3, IN_TN), lambda b, i, j, off, cv: (0, 0, j)),
                  pl.BlockSpec((None, 1, IN_TN), lambda b, i, j, off, cv: (0, 0, j))],
        out_specs=pl.BlockSpec((None, tm, IN_TN), lambda b, i, j, off, cv: (b, i, j)),
        scratch_shapes=[pltpu.VMEM((tm, d), BF16)],
    )
    return pl.pallas_call(
        functools.partial(_in_proj_kernel, conv_grid=is_grid),
        name="mm_in_proj",
        grid_spec=grid_spec,
        out_shape=jax.ShapeDtypeStruct((bsz, l, nblk * IN_TN), F32),
        compiler_params=_cparams(("parallel", "parallel", "arbitrary")),
    )(row_off, has_conv, x, sh, sc, w_t, taps, bias)


def _mm_ln_kernel(a_ref, w_ref, x_ref, gate_ref, g_ref, b_ref, o_ref, acc_ref, *, nk, alpha):
    k = pl.program_id(2)

    @pl.when(k == 0)
    def _():
        acc_ref[...] = jnp.zeros_like(acc_ref)

    acc_ref[...] += jnp.dot(a_ref[...], w_ref[...].astype(BF16), preferred_element_type=F32)

    @pl.when(k == nk - 1)
    def _():
        y = alpha * x_ref[...] + gate_ref[...] * acc_ref[...]
        mu = jnp.mean(y, axis=-1, keepdims=True)
        yc = y - mu
        var = jnp.mean(yc * yc, axis=-1, keepdims=True)
        o_ref[...] = yc * lax.rsqrt(var + LN_EPS) * g_ref[...] + b_ref[...]


def _mm_ln(a, w, layer, x, gate, ln_g, ln_b, *, tm, tk, alpha):
    bsz, l, kdim = a.shape
    d = w.shape[2]
    nk = kdim // tk
    return pl.pallas_call(
        functools.partial(_mm_ln_kernel, nk=nk, alpha=alpha),
        grid=(bsz, l // tm, nk),
        in_specs=[pl.BlockSpec((None, tm, tk), lambda b, i, k: (b, i, k)),
                  pl.BlockSpec((None, tk, d), lambda b, i, k: (layer, k, 0)),
                  pl.BlockSpec((None, tm, d), lambda b, i, k: (b, i, 0)),
                  pl.BlockSpec((None, 1, d), lambda b, i, k: (b, 0, 0)),
                  pl.BlockSpec((1, d), lambda b, i, k: (0, 0)),
                  pl.BlockSpec((1, d), lambda b, i, k: (0, 0))],
        out_specs=pl.BlockSpec((None, tm, d), lambda b, i, k: (b, i, 0)),
        out_shape=jax.ShapeDtypeStruct((bsz, l, d), F32),
        scratch_shapes=[pltpu.VMEM((tm, d), F32)],
        compiler_params=_cparams(("parallel", "parallel", "arbitrary")),
        name="mm_resid_ln",
    )(a, w, x, gate, ln_g, ln_b)


def _branch_kernel(b0_ref, b1_ref, b2_ref, w_ref, g0_ref, g1_ref, g2_ref, o_ref):
    acc = _sigmoid(g0_ref[...]) * jnp.dot(b0_ref[...], w_ref[0].astype(BF16), preferred_element_type=F32)
    acc += _sigmoid(g1_ref[...]) * jnp.dot(b1_ref[...], w_ref[1].astype(BF16), preferred_element_type=F32)
    acc += _sigmoid(g2_ref[...]) * jnp.dot(b2_ref[...], w_ref[2].astype(BF16), preferred_element_type=F32)
    o_ref[...] = acc.astype(o_ref.dtype)


def _branch_merge(branches, w, layer, h, *, tm, tn):
    bsz, l, c = branches[0].shape
    d = w.shape[3]
    nj = d // tn
    br_spec = pl.BlockSpec((None, tm, c), lambda b, i, j: (b, i, 0))

    def gate_spec(g):
        return pl.BlockSpec((None, tm, tn), lambda b, i, j: (b, i, COL_GATE // tn + g * nj + j))

    return pl.pallas_call(
        _branch_kernel,
        grid=(bsz, l // tm, nj),
        in_specs=[br_spec, br_spec, br_spec,
                  pl.BlockSpec((None, N_BRANCH, c, tn), lambda b, i, j: (layer, 0, 0, j)),
                  gate_spec(0), gate_spec(1), gate_spec(2)],
        out_specs=pl.BlockSpec((None, tm, tn), lambda b, i, j: (b, i, j)),
        out_shape=jax.ShapeDtypeStruct((bsz, l, d), BF16),
        compiler_params=_cparams(("parallel", "parallel", "arbitrary")),
        name="branch_merge",
    )(*branches, w, h, h, h)


def _iota2(shape, axis):
    return lax.broadcasted_iota(jnp.int32, shape, axis)


def _split3(g):
    g1 = g.astype(BF16)
    r = g - g1.astype(F32)
    g2 = r.astype(BF16)
    g3 = (r - g2.astype(F32)).astype(BF16)
    return g1, g2, g3


def _cumsum_rows(g, rev):
    row = _iota2((CHUNK, 3 * CHUNK), 0)
    col = jnp.bitwise_and(_iota2((CHUNK, 3 * CHUNK), 1), CHUNK - 1)
    tri = jnp.where((col >= row) if rev else (col <= row), 1.0, 0.0).astype(BF16)
    g3 = jnp.concatenate(_split3(g), axis=0)
    return jnp.dot(tri, g3, preferred_element_type=F32)


def _hg_prepare(items):
    for it in items:
        lb = it["lb"]
        sg = _sigmoid(it["fz"])
        it["g"] = jnp.log(jnp.maximum(lb + (1.0 - lb) * sg, HG_F_MIN))
        it["kk"] = (1.0 - lb) * (1.0 - sg)
    for it in items:
        it["bc"] = _cumsum_rows(it["g"], it["rev"])
    half = CHUNK // 2
    for it in items:
        bc = it["bc"]
        if it["rev"]:
            mid, last = bc[half:half + 1], bc[0:1]
        else:
            mid, last = bc[half - 1:half], bc[CHUNK - 1:CHUNK]
        it["e_last"] = jnp.exp(last)
        kd = it["kk"] * jnp.exp(mid - bc)
        it["ks"] = kd * jnp.exp(last - mid)
        if it["q"] is not None:
            qd = _silu(it["q"]) * jnp.exp(bc - mid)
            it["qd"], it["kd"] = qd, kd
            it["qs"] = qd * jnp.exp(mid)
    r = _iota2((CHUNK, CHUNK), 0)
    c = _iota2((CHUNK, CHUNK), 1)
    for it in items:
        it["kv"] = _dot_tn(it["v"], it["ks"])
        if it["q"] is not None:
            att = _dot_nt(it["qd"], it["kd"])
            it["att"] = jnp.where((c >= r) if it["rev"] else (c <= r), att, 0.0)
    for it in items:
        if it["q"] is not None:
            it["o_c"] = _dot(it["att"], it["v"])


def _hg_kernel(*refs, nchunk, with_output):
    if with_output:
        (ff_ref, vf_ref, qf_ref, fb_ref, vb_ref, qb_ref, lbf_ref, lbb_ref, s0f_ref, s0b_ref,
         of_ref, ob_ref, sf_ref, sb_ref, stf, stb) = refs
    else:
        (ff_ref, vf_ref, fb_ref, vb_ref, lbf_ref, lbb_ref, s0f_ref, s0b_ref,
         sf_ref, sb_ref, stf, stb) = refs
        qf_ref = qb_ref = of_ref = ob_ref = None

    @pl.when(pl.program_id(2) == 0)
    def _():
        stf[...] = s0f_ref[...]
        stb[...] = s0b_ref[...]

    def chunk_item(f_ref, v_ref, q_ref, lb_ref, ci, rev):
        sl = pl.ds(ci * CHUNK, CHUNK)
        return {"fz": f_ref[sl, :], "v": v_ref[sl, :], "q": q_ref[sl, :] if with_output else None,
                "lb": lb_ref[...], "rev": rev, "sl": sl}

    items = []
    for ci in range(nchunk):
        items.append(chunk_item(ff_ref, vf_ref, qf_ref, lbf_ref, ci, False))
        items.append(chunk_item(fb_ref, vb_ref, qb_ref, lbb_ref, nchunk - 1 - ci, True))
    _hg_prepare(items)
    state = {False: stf[...], True: stb[...]}
    for it in items:
        st = state[it["rev"]]
        if with_output:
            o_ref = ob_ref if it["rev"] else of_ref
            o_ref[it["sl"], :] = it["o_c"] + _dot_nt(it["qs"], st)
        state[it["rev"]] = st * it["e_last"] + it["kv"]
    stf[...] = state[False]
    stb[...] = state[True]
    sf_ref[...] = state[False]
    sb_ref[...] = state[True]


def _hgrn2_scan(h, cols, lb_f, lb_b, s0, *, tblk, with_output):
    bsz, l, _ = h.shape
    nt = l // tblk
    c_ff, c_fb, c_i, c_q = (c // LANE for c in cols)

    def fwd(cb):
        return pl.BlockSpec((None, tblk, LANE), lambda b, hh, t: (b, t, cb + hh))

    def bwd(cb):
        return pl.BlockSpec((None, tblk, LANE), lambda b, hh, t: (b, nt - 1 - t, cb + hh))

    lb_spec = pl.BlockSpec((1, LANE), lambda b, hh, t: (0, hh))
    st_spec = pl.BlockSpec((None, None, HEAD_DIM, HEAD_DIM), lambda b, hh, t: (b, hh, 0, 0))
    if with_output:
        in_specs = [fwd(c_ff), fwd(c_i), fwd(c_q), bwd(c_fb), bwd(c_i), bwd(c_q)]
        args = [h] * 6
    else:
        in_specs = [fwd(c_ff), fwd(c_i), bwd(c_fb), bwd(c_i)]
        args = [h] * 4
    in_specs += [lb_spec, lb_spec, st_spec, st_spec]
    args += [lb_f, lb_b, s0[0], s0[1]]
    st_shape = jax.ShapeDtypeStruct((bsz, HEADS, HEAD_DIM, HEAD_DIM), F32)
    out_specs = [st_spec, st_spec]
    out_shape = [st_shape, st_shape]
    if with_output:
        o_shape = jax.ShapeDtypeStruct((bsz, l, HEADS * HEAD_DIM), F32)
        out_specs = [pl.BlockSpec((None, tblk, LANE), lambda b, hh, t: (b, t, hh)),
                     pl.BlockSpec((None, tblk, LANE), lambda b, hh, t: (b, nt - 1 - t, hh))] + out_specs
        out_shape = [o_shape, o_shape] + out_shape
    outs = pl.pallas_call(
        functools.partial(_hg_kernel, nchunk=tblk // CHUNK, with_output=with_output),
        name="hgrn2_scan",
        grid=(bsz, HEADS, nt),
        in_specs=in_specs,
        out_specs=out_specs,
        out_shape=out_shape,
        scratch_shapes=[pltpu.VMEM((HEAD_DIM, HEAD_DIM), F32), pltpu.VMEM((HEAD_DIM, HEAD_DIM), F32)],
        compiler_params=_cparams(("parallel", "parallel", "arbitrary")),
    )(*args)
    if with_output:
        return outs[0], outs[1], (outs[2], outs[3])
    return None, None, (outs[0], outs[1])


def _l2norm(t):
    return t * lax.rsqrt(jnp.sum(t * t, axis=-1, keepdims=True) + 1e-6)


def _softplus(x):
    return jnp.maximum(x, 0.0) + jnp.log(1.0 + jnp.exp(-jnp.abs(x)))


def _dn_prepare(items):
    r = _iota2((CHUNK, CHUNK), 0)
    c = _iota2((CHUNK, CHUNK), 1)
    eye = r == c
    for it in items:
        rev = it["rev"]
        incl = (c >= r) if rev else (c <= r)
        incl_t = (r >= c) if rev else (r <= c)
        g_b = jnp.broadcast_to(it["g"], (CHUNK, CHUNK))
        gc_col = jnp.sum(jnp.where(incl, g_b, 0.0), axis=1, keepdims=True)
        g_col = jnp.sum(jnp.where(eye, g_b, 0.0), axis=1, keepdims=True)
        gc_row = jnp.sum(jnp.where(incl_t, jnp.broadcast_to(g_col, (CHUNK, CHUNK)), 0.0), axis=0, keepdims=True)
        beta_col = jnp.sum(jnp.where(eye, jnp.broadcast_to(it["beta"], (CHUNK, CHUNK)), 0.0), axis=1,
                           keepdims=True)
        it["decay"] = jnp.where(incl, jnp.exp(jnp.where(incl, gc_col - gc_row, 0.0)), 0.0)
        g_last = gc_col[0:1] if rev else gc_col[CHUNK - 1:CHUNK]
        it["e_last"] = jnp.exp(g_last)
        egc = jnp.exp(gc_col)
        kb = it["k"] * beta_col
        it["kb"] = kb
        it["rhs"] = jnp.concatenate([it["v"] * beta_col, kb * egc], axis=1)
        it["kd"] = it["k"] * jnp.exp(g_last - gc_col)
        if it["q"] is not None:
            it["qe"] = it["q"] * egc
    for it in items:
        it["kk"] = _dot_nt(it["kb"], it["k"])
        if it["q"] is not None:
            it["qk"] = _dot_nt(it["q"], it["k"]) * it["decay"]
    for it in items:
        strict = (c > r) if it["rev"] else (c < r)
        n = jnp.where(strict, it["kk"] * it["decay"], 0.0)
        it["n"] = n
        it["p"] = jnp.where(eye, 1.0, 0.0) - n
    for it in items:
        it["m"] = _dot(it["n"], it["n"])
    for i in range(5):
        for it in items:
            if i < 4:
                pm = _dot(jnp.concatenate([it["p"], it["m"]], axis=0), it["m"])
                it["p"] = it["p"] + pm[:CHUNK]
                it["m"] = pm[CHUNK:]
            else:
                it["p"] = it["p"] + _dot(it["p"], it["m"])
    for it in items:
        it["uw"] = _dot(it["p"], it["rhs"])
    for it in items:
        bm = _dot_tn(it["kd"], it["uw"])
        it["b_c"], it["m_c"] = bm[:, :HEAD_DIM], bm[:, HEAD_DIM:]
        if it["q"] is not None:
            ow = _dot(it["qk"], it["uw"])
            it["o_c"], it["q_eff"] = ow[:, :HEAD_DIM], it["qe"] - ow[:, HEAD_DIM:]


def _dn_kernel(*refs, nchunk, with_output):
    if with_output:
        (alog_ref, dtb_ref, kf_ref, vf_ref, qf_ref, abf_ref, kb_ref, vb_ref, qb_ref, abb_ref,
         s0f_ref, s0b_ref, of_ref, ob_ref, sf_ref, sb_ref, stf, stb) = refs
    else:
        (alog_ref, dtb_ref, kf_ref, vf_ref, abf_ref, kb_ref, vb_ref, abb_ref,
         s0f_ref, s0b_ref, sf_ref, sb_ref, stf, stb) = refs
        qf_ref = qb_ref = of_ref = ob_ref = None
    hh = pl.program_id(1)

    @pl.when(pl.program_id(2) == 0)
    def _():
        stf[...] = s0f_ref[...]
        stb[...] = s0b_ref[...]

    def prep(k_ref, v_ref, q_ref, ab_ref, d):
        k = _l2norm(_silu(k_ref[...]))
        v = _silu(v_ref[...])
        q = None
        if with_output:
            q = _l2norm(_silu(q_ref[...])) * (HEAD_DIM ** -0.5)
        a = ab_ref[pl.ds(d * HEADS + hh, 1), :]
        bt = ab_ref[pl.ds(2 * HEADS + d * HEADS + hh, 1), :]
        g = -jnp.exp(alog_ref[d, hh]) * _softplus(a + dtb_ref[d, hh])
        return k, v, q, g, _sigmoid(bt)

    kf, vf, qf, gf, bf = prep(kf_ref, vf_ref, qf_ref, abf_ref, 0)
    kb, vb, qb, gb, bb = prep(kb_ref, vb_ref, qb_ref, abb_ref, 1)
    def chunk_item(k, v, q, g, beta, ci, rev):
        lo = ci * CHUNK
        return {"k": k[lo:lo + CHUNK], "v": v[lo:lo + CHUNK], "q": q[lo:lo + CHUNK] if with_output else None,
                "g": g[:, lo:lo + CHUNK], "beta": beta[:, lo:lo + CHUNK], "rev": rev, "lo": lo}

    items = []
    for ci in range(nchunk):
        items.append(chunk_item(kf, vf, qf, gf, bf, ci, False))
        items.append(chunk_item(kb, vb, qb, gb, bb, nchunk - 1 - ci, True))
    _dn_prepare(items)
    state = {False: stf[...], True: stb[...]}
    for it in items:
        s = state[it["rev"]]
        if with_output:
            o_ref = ob_ref if it["rev"] else of_ref
            o_ref[pl.ds(it["lo"], CHUNK), :] = it["o_c"] + _dot(it["q_eff"], s)
        state[it["rev"]] = s * it["e_last"] + it["b_c"] - _dot(it["m_c"], s)
    stf[...] = state[False]
    stb[...] = state[True]
    sf_ref[...] = state[False]
    sb_ref[...] = state[True]


def _deltanet_scan(h, ab_t, cols, a_log, dt_bias, s0, *, tblk, with_output):
    bsz, l, _ = h.shape
    nt = l // tblk
    c_k, c_v, c_q = (c // LANE for c in cols)
    rep = HEADS // DN_QK_HEADS

    def tok(cb, shared, rev):
        def imap(b, hh, t):
            return (b, (nt - 1 - t) if rev else t, cb + (hh // rep if shared else hh))
        return pl.BlockSpec((None, tblk, LANE), imap)

    def ab_spec(rev):
        return pl.BlockSpec((None, 4 * HEADS, tblk), lambda b, hh, t: (b, 0, (nt - 1 - t) if rev else t))

    smem = pl.BlockSpec(memory_space=pltpu.SMEM)
    st_spec = pl.BlockSpec((None, None, HEAD_DIM, HEAD_DIM), lambda b, hh, t: (b, hh, 0, 0))
    if with_output:
        in_specs = [smem, smem, tok(c_k, True, False), tok(c_v, False, False), tok(c_q, True, False), ab_spec(False),
                    tok(c_k, True, True), tok(c_v, False, True), tok(c_q, True, True), ab_spec(True),
                    st_spec, st_spec]
        args = [a_log, dt_bias, h, h, h, ab_t, h, h, h, ab_t, s0[0], s0[1]]
    else:
        in_specs = [smem, smem, tok(c_k, True, False), tok(c_v, False, False), ab_spec(False),
                    tok(c_k, True, True), tok(c_v, False, True), ab_spec(True),
                    st_spec, st_spec]
        args = [a_log, dt_bias, h, h, ab_t, h, h, ab_t, s0[0], s0[1]]
    st_shape = jax.ShapeDtypeStruct((bsz, HEADS, HEAD_DIM, HEAD_DIM), F32)
    out_specs = [st_spec, st_spec]
    out_shape = [st_shape, st_shape]
    if with_output:
        o_shape = jax.ShapeDtypeStruct((bsz, l, HEADS * HEAD_DIM), F32)
        out_specs = [pl.BlockSpec((None, tblk, LANE), lambda b, hh, t: (b, t, hh)),
                     pl.BlockSpec((None, tblk, LANE), lambda b, hh, t: (b, nt - 1 - t, hh))] + out_specs
        out_shape = [o_shape, o_shape] + out_shape
    outs = pl.pallas_call(
        functools.partial(_dn_kernel, nchunk=tblk // CHUNK, with_output=with_output),
        name="deltanet_scan",
        grid=(bsz, HEADS, nt),
        in_specs=in_specs,
        out_specs=out_specs,
        out_shape=out_shape,
        scratch_shapes=[pltpu.VMEM((HEAD_DIM, HEAD_DIM), F32), pltpu.VMEM((HEAD_DIM, HEAD_DIM), F32)],
        compiler_params=_cparams(("parallel", "parallel", "arbitrary")),
    )(*args)
    if with_output:
        return outs[0], outs[1], (outs[2], outs[3])
    return None, None, (outs[0], outs[1])


def _combine_kernel(of_ref, ob_ref, z_ref, w_ref, o_ref, *, use_silu):
    o = of_ref[...] + ob_ref[...]
    o = o * lax.rsqrt(jnp.mean(o * o, axis=-1, keepdims=True) + RMS_EPS) * w_ref[...]
    z = z_ref[...]
    o_ref[...] = (o * (_silu(z) if use_silu else _sigmoid(z))).astype(o_ref.dtype)


def _combine(o_f, o_b, h, col_z, norm_w, *, tl, use_silu):
    bsz, l, c = o_f.shape
    cz = col_z // LANE
    spec = pl.BlockSpec((None, tl, LANE), lambda b, i, hh: (b, i, hh))
    return pl.pallas_call(
        functools.partial(_combine_kernel, use_silu=use_silu),
        name="mixer_norm_gate",
        grid=(bsz, l // tl, c // LANE),
        in_specs=[spec, spec,
                  pl.BlockSpec((None, tl, LANE), lambda b, i, hh: (b, i, cz + hh)),
                  pl.BlockSpec((1, LANE), lambda b, i, hh: (0, 0))],
        out_specs=spec,
        out_shape=jax.ShapeDtypeStruct((bsz, l, c), BF16),
        compiler_params=_cparams(("parallel", "parallel", "arbitrary")),
    )(o_f, o_b, h, norm_w)


def _dot_f32(a, b):
    return jnp.dot(a, b, precision=lax.Precision.HIGHEST, preferred_element_type=F32)


def _hy_filt_kernel(w1_ref, b1_ref, w2_ref, b2_ref, w3_ref, o_ref, hid_ref, *, l_total, tl):
    i = pl.program_id(0)
    j = pl.program_id(1)

    @pl.when(j == 0)
    def _():
        row = (_iota2((tl, LANE), 0) + i * tl).astype(F32)
        lane = _iota2((tl, LANE), 1)
        t = row * (1.0 / (l_total - 1))
        ang = row * (2.0 * math.pi / l_total)
        bidx = jnp.where(lane <= HY_BANDS, lane - 1, lane - 1 - HY_BANDS).astype(F32)
        band = 1e-4 + bidx * ((HY_BANDS - 1 - 1e-4) / (HY_BANDS - 1))
        arg = band * ang
        feats = jnp.where(lane == 0, t,
                          jnp.where(lane <= HY_BANDS, jnp.cos(arg),
                                    jnp.where(lane <= 2 * HY_BANDS, -jnp.sin(arg), 0.0)))
        hid = jnp.sin(_dot_f32(feats, w1_ref[...]) + b1_ref[...])
        hid_ref[...] = jnp.sin(_dot_f32(hid, w2_ref[...]) + b2_ref[...])

    hcol = _dot_f32(hid_ref[...], w3_ref[...])
    width = hcol.shape[1]
    ch = _iota2((tl, width), 1).astype(F32)
    lo = math.log(HY_DECAY_TARGET) / HY_SLOW_DECAY
    hi = math.log(HY_DECAY_TARGET) / HY_FAST_DECAY
    delta = jnp.abs(lo + ch * ((hi - lo) / (width - 1)))
    trow = (_iota2((tl, width), 0) + i * tl)
    out = hcol * jnp.exp(-(trow.astype(F32) * (1.0 / (l_total - 1))) * delta)
    drop = jnp.logical_and(trow == 0, jnp.bitwise_and(j, 1) == 1)
    o_ref[...] = jnp.where(drop, 0.0, out)


def _hy_filters(l_total, w1p, b1, w2, b2, w3):
    tl = min(l_total, 512)
    n = w3.shape[1]
    full = lambda a: pl.BlockSpec(a.shape, lambda i, j: (0,) * a.ndim)
    return pl.pallas_call(
        functools.partial(_hy_filt_kernel, l_total=l_total, tl=tl),
        name="hy_filters",
        grid=(l_total // tl, n // BRANCH_WIDTH),
        in_specs=[full(w1p), full(b1), full(w2), full(b2),
                  pl.BlockSpec((HY_HIDDEN, BRANCH_WIDTH), lambda i, j: (0, j))],
        out_specs=pl.BlockSpec((tl, BRANCH_WIDTH), lambda i, j: (i, j)),
        out_shape=jax.ShapeDtypeStruct((l_total, n), F32),
        scratch_shapes=[pltpu.VMEM((tl, HY_HIDDEN), F32)],
        compiler_params=_cparams(("parallel", "arbitrary")),
    )(w1p, b1, w2, b2, w3)


@functools.lru_cache(maxsize=None)
def _fft_tables(n1, n2):
    n = n1 * n2
    n2h = n2 // 2
    k2 = np.arange(n2, dtype=np.int64)[:, None]
    m = np.arange(n2h, dtype=np.int64)[None, :]
    ta = np.zeros((n1, 2 * n2, 2 * n2h))
    for a in range(n1):
        th = 2.0 * np.pi * ((k2 * (a + n1 * m)) % n) / n
        c, s = np.cos(th), np.sin(th)
        ta[a] = np.block([[c, s], [-s, c]])
    tai = np.transpose(ta, (0, 2, 1)) / n
    k1 = np.arange(n1, dtype=np.int64)
    ph = 2.0 * np.pi * ((k1[:, None] * k1[None, :]) % n1) / n1
    c, s = np.cos(ph), np.sin(ph)
    fb = np.block([[c, s], [-s, c]])
    fs = fb[:n1] + fb[n1:]
    fsr = np.stack([fs[(-k1 - e) % n1] for e in (0, 1)])
    return tuple(np.asarray(t, np.float32) for t in (ta, tai, fb, fb.T, fs, fsr))


def _fft_split(l_total):
    n = 2 * l_total
    n1 = min(128, l_total // 8)
    return n1, n // n1


def _hy_stage_a_kernel(gr_ref, cr_ref, gi_ref, ci_ref, xr_ref, xi_ref, t_ref, o_ref, *, planes, n2):
    del gr_ref, cr_ref, gi_ref, ci_ref
    for j in range(8):
        x = jnp.concatenate([xr_ref[:, j, :], xi_ref[:, j, :]], axis=0).astype(BF16)
        y = jnp.dot(t_ref[j], x, preferred_element_type=F32)
        for p in range(planes):
            o_ref[p, :, j, :] = y[p * n2:(p + 1) * n2].astype(o_ref.dtype)


def _hy_stage_a(x4, pairs, tab, *, ct, planes, n2):
    _, n2h, n1, _ = x4.shape
    npair = len(pairs)
    nct = BRANCH_WIDTH // ct
    gr = jnp.asarray([p[0][0] for p in pairs], jnp.int32)
    cr = jnp.asarray([p[0][1] for p in pairs], jnp.int32)
    gi = jnp.asarray([p[1][0] for p in pairs], jnp.int32)
    ci = jnp.asarray([p[1][1] for p in pairs], jnp.int32)
    grid_spec = pltpu.PrefetchScalarGridSpec(
        num_scalar_prefetch=4,
        grid=(npair, nct, n1 // 8),
        in_specs=[pl.BlockSpec((None, n2h, 8, ct), lambda p, c, g, gr, cr, gi, ci: (gr[p], 0, g, cr[p] * nct + c)),
                  pl.BlockSpec((None, n2h, 8, ct), lambda p, c, g, gr, cr, gi, ci: (gi[p], 0, g, ci[p] * nct + c)),
                  pl.BlockSpec((8, planes * n2, 2 * n2h), lambda p, c, g, *_: (g, 0, 0))],
        out_specs=pl.BlockSpec((None, planes, n2, 8, ct), lambda p, c, g, *_: (p, 0, 0, g, c)),
    )
    return pl.pallas_call(
        functools.partial(_hy_stage_a_kernel, planes=planes, n2=n2),
        name="hy_stage_a",
        grid_spec=grid_spec,
        out_shape=jax.ShapeDtypeStruct((npair, planes, n2, n1, BRANCH_WIDTH), BF16),
        compiler_params=_cparams(("parallel", "parallel", "arbitrary")),
    )(gr, cr, gi, ci, x4, x4, tab)


def _hy_spec_kernel(c_ref, call_ref, fs_ref, fsr_ref, o_ref, *, kg, n2):
    g = pl.program_id(2)
    for k in range(kg):
        s = jnp.dot(fs_ref[...], jnp.concatenate([c_ref[0, k], c_ref[1, k]], axis=0), preferred_element_type=F32)
        k2 = g * kg + k
        kp = jnp.where(k2 == 0, 0, n2 - k2)
        fsr = fsr_ref[1] if k else jnp.where(g == 0, fsr_ref[0], fsr_ref[1])
        t = jnp.dot(fsr, jnp.concatenate([call_ref[0, kp], call_ref[1, kp]], axis=0), preferred_element_type=F32)
        o_ref[0, k] = 0.5 * (s + t)
        o_ref[1, k] = 0.5 * (s - t)


def _hy_spectrum(cs, fs, fsr, *, ct, kg):
    no, _, n2, n1, c = cs.shape
    return pl.pallas_call(
        functools.partial(_hy_spec_kernel, kg=kg, n2=n2),
        name="hy_spectrum",
        grid=(no, c // ct, n2 // kg),
        in_specs=[pl.BlockSpec((None, 2, kg, n1, ct), lambda o, j, g: (o, 0, g, 0, j)),
                  pl.BlockSpec((None, 2, n2, n1, ct), lambda o, j, g: (o, 0, 0, 0, j)),
                  pl.BlockSpec(fs.shape, lambda o, j, g: (0, 0)),
                  pl.BlockSpec(fsr.shape, lambda o, j, g: (0, 0, 0))],
        out_specs=pl.BlockSpec((None, 2, kg, n1, ct), lambda o, j, g: (o, 0, g, 0, j)),
        out_shape=jax.ShapeDtypeStruct((no, 2, n2, n1, c), F32),
        compiler_params=_cparams(("parallel", "parallel", "arbitrary")),
    )(cs, cs, fs, fsr)


def _hy_stage_b_kernel(c_ref, k_ref, fb_ref, fbi_ref, o_ref, *, kg, n1):
    for k in range(kg):
        z = jnp.dot(fb_ref[...], jnp.concatenate([c_ref[0, k], c_ref[1, k]], axis=0), preferred_element_type=F32)
        zr, zi = z[:n1], z[n1:]
        kr, ki = k_ref[0, k], k_ref[1, k]
        y = jnp.concatenate([zr * kr - zi * ki, zr * ki + zi * kr], axis=0).astype(BF16)
        d = jnp.dot(fbi_ref[...], y, preferred_element_type=F32)
        o_ref[0, k] = d[:n1].astype(o_ref.dtype)
        o_ref[1, k] = d[n1:].astype(o_ref.dtype)


def _hy_stage_b(cs, spec, order, fb, fbi, *, ct, kg):
    _, _, n2, n1, c = cs.shape
    return pl.pallas_call(
        functools.partial(_hy_stage_b_kernel, kg=kg, n1=n1),
        name="hy_stage_b",
        grid=(c // ct, n2 // kg),
        in_specs=[pl.BlockSpec((None, 2, kg, n1, ct), lambda j, g: (0, 0, g, 0, j)),
                  pl.BlockSpec((None, 2, kg, n1, ct), lambda j, g: (order, 0, g, 0, j)),
                  pl.BlockSpec(fb.shape, lambda j, g: (0, 0)),
                  pl.BlockSpec(fbi.shape, lambda j, g: (0, 0))],
        out_specs=pl.BlockSpec((2, kg, n1, ct), lambda j, g: (0, g, 0, j)),
        out_shape=jax.ShapeDtypeStruct((2, n2, n1, c), BF16),
        compiler_params=_cparams(("parallel", "arbitrary")),
    )(cs, spec, fb, fbi)


def _hy_stage_c_kernel(d_ref, t_ref, u_ref, g_ref, skip_ref, o_ref, *, n2h):
    skip = skip_ref[...]
    for j in range(8):
        d = jnp.concatenate([d_ref[0, :, j, :], d_ref[1, :, j, :]], axis=0)
        y = jnp.dot(t_ref[j], d, preferred_element_type=F32)
        for b in range(2):
            yb = y[b * n2h:(b + 1) * n2h]
            o_ref[b, :, j, :] = (g_ref[b, :, j, :] * (yb + skip * u_ref[b, :, j, :])).astype(o_ref.dtype)


def _hy_stage_c(d, tai, u4, ucol, z4, gcol, skip, *, ct, out_dtype):
    _, n2, n1, c = d.shape
    n2h = n2 // 2
    uc, gc = ucol // ct, gcol // ct
    return pl.pallas_call(
        functools.partial(_hy_stage_c_kernel, n2h=n2h),
        name="hy_stage_c",
        grid=(c // ct, n1 // 8),
        in_specs=[pl.BlockSpec((2, n2, 8, ct), lambda j, g: (0, 0, g, j)),
                  pl.BlockSpec((8, 2 * n2h, 2 * n2), lambda j, g: (g, 0, 0)),
                  pl.BlockSpec((2, n2h, 8, ct), lambda j, g: (0, 0, g, uc + j)),
                  pl.BlockSpec((2, n2h, 8, ct), lambda j, g: (0, 0, g, gc + j)),
                  pl.BlockSpec((1, ct), lambda j, g: (0, j))],
        out_specs=pl.BlockSpec((2, n2h, 8, ct), lambda j, g: (0, 0, g, j)),
        out_shape=jax.ShapeDtypeStruct((2, n2h, n1, c), out_dtype),
        compiler_params=_cparams(("parallel", "arbitrary")),
    )(d, tai, u4, z4, skip)


def _hyena(h, filt_params, skip):
    bsz, l, _ = h.shape
    assert bsz == 2
    n1, n2 = _fft_split(l)
    n2h = n2 // 2
    ta, tai, fb, fbi, fs, fsr = (jnp.asarray(t).astype(BF16) for t in _fft_tables(n1, n2))
    ct = 512
    kg = 8

    hfilt = _hy_filters(l, *filt_params)
    cs_k = _hy_stage_a(hfilt.reshape(1, n2h, n1, -1), [((0, 2 * o), (0, 2 * o + 1)) for o in range(HY_ORDER)],
                       ta, ct=ct, planes=2, n2=n2)
    spec = _hy_spectrum(cs_k, fs, fsr, ct=256, kg=kg)

    z4 = h.reshape(bsz, n2h, n1, -1)
    u4, ucol = z4, COL_HY
    for o in range(HY_ORDER):
        last = o == HY_ORDER - 1
        cs = _hy_stage_a(u4, [((0, ucol // BRANCH_WIDTH), (1, ucol // BRANCH_WIDTH))], ta, ct=ct, planes=2, n2=n2)
        d = _hy_stage_b(cs, spec, o, fb, fbi, ct=ct, kg=kg)
        u4 = _hy_stage_c(d, tai, u4, ucol, z4, COL_HY + (o + 1) * BRANCH_WIDTH, skip[o:o + 1], ct=ct,
                         out_dtype=BF16 if last else F32)
        ucol = 0
    return u4.reshape(bsz, l, BRANCH_WIDTH)


def _conv_tables(hy_conv_w, hy_conv_b, dn_conv_q, dn_conv_k, dn_conv_v):
    depth = hy_conv_w.shape[0]
    ident = lambda n: jnp.broadcast_to(jnp.asarray([0.0, 1.0, 0.0], F32)[None, :, None], (depth, 3, n))
    taps = jnp.concatenate([ident(COL_HY), hy_conv_w, ident(COL_DK - COL_FF), dn_conv_k, dn_conv_v,
                            ident(COL_DQ - COL_AB), dn_conv_q, ident(COL_PAD - COL_DQ - dn_conv_q.shape[2])], axis=2)
    bias = jnp.concatenate([jnp.zeros((depth, 1, COL_HY), F32), hy_conv_b[:, None, :],
                            jnp.zeros((depth, 1, COL_PAD - COL_FF), F32)], axis=2)
    return taps, bias


def _token_mixers(u_src, sh, sc, w_t, conv_t, layer, lp, states, *, is_grid, with_output, tm):
    bsz, l, _ = u_src.shape
    tblk = min(l, 512)
    taps, bias = (a[layer:layer + 1] for a in conv_t)
    if with_output:
        h = _in_proj(u_src, sh, sc, w_t, layer, range(COL_PAD // IN_TN), taps, bias, tm=tm, is_grid=is_grid)
        base = 0
    else:
        lo, hi = STATE_BLOCKS[0] * IN_TN, (STATE_BLOCKS[-1] + 1) * IN_TN
        h = _in_proj(u_src, sh, sc, w_t, layer, STATE_BLOCKS, taps[:, :, lo:hi], bias[:, :, lo:hi],
                     tm=tm, is_grid=is_grid)
        base = COL_FF
    col = lambda c: c - base
    ab_t = jnp.swapaxes(h[:, :, col(COL_AB):col(COL_AB) + 4 * HEADS], 1, 2)
    hg_f, hg_b, hg_states = _hgrn2_scan(h, (col(COL_FF), col(COL_FB), col(COL_HI), col(COL_HQ)),
                                        lp["lb_f"], lp["lb_b"], states[0], tblk=tblk, with_output=with_output)
    dn_f, dn_b, dn_states = _deltanet_scan(h, ab_t, (col(COL_DK), col(COL_DV), col(COL_DQ)),
                                           lp["dn_a_log"], lp["dn_dt_bias"], states[1],
                                           tblk=tblk, with_output=with_output)
    new_states = (hg_states, dn_states)
    if not with_output:
        return None, new_states
    tl = min(l, 2048)
    hg_out = _combine(hg_f, hg_b, h, COL_HGATE, lp["hg_norm_w"], tl=tl, use_silu=False)
    dn_out = _combine(dn_f, dn_b, h, COL_DZ, lp["dn_norm_w"], tl=tl, use_silu=True)
    hy_out = _hyena(h, lp["hy_filt"], lp["hy_skip"])
    ysum = _branch_merge([hy_out, hg_out, dn_out], lp["w_branch"], layer, h, tm=min(l, 1024), tn=512)
    return ysum, new_states


def kernel(x, c, ctx, c_ctx, w_ada, b_ada, w_in, hy_conv_w, hy_conv_b, hy_filt_w1, hy_filt_b1, hy_filt_w2, hy_filt_b2, hy_filt_w3, hy_skip, hg_lb_logits, hg_norm_w, dn_conv_q, dn_conv_k, dn_conv_v, dn_a_log, dn_dt_bias, dn_norm_w, w_branch, w_out, ln1_g, ln1_b, w_ff1, w_ff2, ln2_g, ln2_b):
    depth = w_in.shape[0]
    bsz, _, d = x.shape
    alpha = (2 * depth) ** 0.25
    p = jax.nn.softmax(hg_lb_logits.astype(F32), axis=1)
    lower = jnp.cumsum(p, axis=1) - p[:, :1]
    cs = jnp.concatenate([c, c_ctx[None], jnp.zeros((8 - bsz - 1, d), F32)], axis=0)
    w_r = jnp.swapaxes(w_in, 1, 2).astype(BF16)
    w_b, w_o, w_1, w_2 = w_branch.astype(BF16), w_out.astype(BF16), w_ff1, w_ff2.astype(BF16)
    b_ada3 = b_ada[:, None, :]
    conv_t = _conv_tables(hy_conv_w, hy_conv_b, dn_conv_q, dn_conv_k, dn_conv_v)
    h_ctx = ctx
    for l in range(depth):
        last = l == depth - 1
        row = lambda v: v.reshape(1, -1)
        lp = {
            "lb_f": row(lower[0, l]), "lb_b": row(lower[1, l]),
            "hg_norm_w": row(hg_norm_w[l]), "dn_norm_w": row(dn_norm_w[l]),
            "dn_a_log": dn_a_log[l], "dn_dt_bias": dn_dt_bias[l], "hy_skip": hy_skip[l],
            "hy_filt": (jnp.pad(hy_filt_w1[l], ((0, LANE - HY_EMB), (0, 0))), row(hy_filt_b1[l]),
                        hy_filt_w2[l], row(hy_filt_b2[l]), hy_filt_w3[l]),
            "w_branch": w_b,
        }
        g1n, b1n, g2n, b2n = row(ln1_g[l]), row(ln1_b[l]), row(ln2_g[l]), row(ln2_b[l])
        mod = _ada_mod(cs, w_ada, b_ada3, l)
        m_lat = [mod[:bsz, i * d:(i + 1) * d][:, None, :] for i in range(6)]
        m_ctx = [jnp.broadcast_to(mod[bsz, i * d:(i + 1) * d][None, None, :], (bsz, 1, d)) for i in range(6)]
        zero = jnp.zeros((bsz, HEADS, HEAD_DIM, HEAD_DIM), F32)
        init = ((zero, zero), (zero, zero))

        lc = h_ctx.shape[1]
        y_ctx, ctx_states = _token_mixers(h_ctx, m_ctx[0], m_ctx[1], w_r, conv_t, l, lp, init,
                                          is_grid=False, with_output=not last, tm=lc)
        if not last:
            h_ctx = _mm_ln(y_ctx, w_o, l, h_ctx, m_ctx[2], g1n, b1n, tm=lc, tk=d, alpha=alpha)
            mid = _mlp_up(h_ctx, m_ctx[3], m_ctx[4], w_1, l, tm=lc, tn=1024)
            h_ctx = _mm_ln(mid, w_2, l, h_ctx, m_ctx[5], g2n, b2n, tm=lc, tk=1024, alpha=alpha)

        lx = x.shape[1]
        y, _ = _token_mixers(x, m_lat[0], m_lat[1], w_r, conv_t, l, lp, ctx_states,
                             is_grid=True, with_output=True, tm=min(lx, 1024))
        x = _mm_ln(y, w_o, l, x, m_lat[2], g1n, b1n, tm=min(lx, 512), tk=d, alpha=alpha)
        mid = _mlp_up(x, m_lat[3], m_lat[4], w_1, l, tm=min(lx, 1024), tn=1024)
        x = _mm_ln(mid, w_2, l, x, m_lat[5], g2n, b2n, tm=min(lx, 1024), tk=512, alpha=alpha)
    return x
```

```python
import functools
import math

import numpy as np
import jax
import jax.numpy as jnp
from jax import lax
from jax.experimental import pallas as pl
from jax.experimental.pallas import tpu as pltpu

F32 = jnp.float32
BF16 = jnp.bfloat16

D_MODEL = 2048
GRID_W = 64
BRANCH_WIDTH = 1024
N_BRANCH = 3
HY_ORDER = 2
HY_EMB = 33
HY_BANDS = (HY_EMB - 1) // 2
HY_HIDDEN = 64
HY_DECAY_TARGET = 1e-2
HY_FAST_DECAY = 0.3
HY_SLOW_DECAY = 1.5
HEADS = 8
HEAD_DIM = 128
DN_QK_HEADS = 4
HG_F_MIN = 1e-30
CHUNK = 64
LN_EPS = 1e-5
RMS_EPS = 1e-6
LANE = 128

IN_TN = 1024
COL_GATE = 0
COL_HY = 6144
COL_FF = 9216
COL_FB = 10240
COL_HI = 11264
COL_DK = 12288
COL_DV = 12800
COL_AB = 13824
COL_HQ = 14336
COL_HGATE = 15360
COL_DQ = 16384
COL_DZ = 17408
COL_PAD = 18432
SOURCE_ROWS = np.asarray([11296 + IN_TN * k for k in range(6)] + [8224 + IN_TN * k for k in range(3)]
                         + [0, 1024, 2048, 3072, 4096, 4640, 5664, 6688, 7200], np.int32)
CONV_WINDOWS = np.asarray([COL_HY <= IN_TN * j < COL_FF or COL_DK <= IN_TN * j < COL_HQ or IN_TN * j == COL_DQ
                           for j in range(COL_PAD // IN_TN)], np.int32)
STATE_BLOCKS = tuple(range(COL_FF // IN_TN, COL_HQ // IN_TN))

VMEM_LIMIT = 56 * 1024 * 1024


def _cparams(sem):
    return pltpu.CompilerParams(dimension_semantics=sem, vmem_limit_bytes=VMEM_LIMIT)


def _dot(a, b):
    return jnp.dot(a.astype(BF16), b.astype(BF16), preferred_element_type=F32)


def _dot_nt(a, b):
    return lax.dot_general(a.astype(BF16), b.astype(BF16), (((1,), (1,)), ((), ())),
                           preferred_element_type=F32)


def _dot_tn(a, b):
    return lax.dot_general(a.astype(BF16), b.astype(BF16), (((0,), (0,)), ((), ())),
                           preferred_element_type=F32)


def _sigmoid(x):
    return 1.0 / (1.0 + jnp.exp(-x))


def _silu(x):
    return x * _sigmoid(x)


def _ada_kernel(c_ref, w_ref, b_ref, o_ref):
    a = _silu(c_ref[...])
    o_ref[...] = _dot(a, w_ref[...]) + b_ref[...]


def _ada_mod(cs, w, b, layer):
    _, d, n = w.shape
    tn = 1024
    return pl.pallas_call(
        _ada_kernel,
        grid=(n // tn,),
        in_specs=[pl.BlockSpec((8, d), lambda j: (0, 0)),
                  pl.BlockSpec((None, d, tn), lambda j: (layer, 0, j)),
                  pl.BlockSpec((None, 1, tn), lambda j: (layer, 0, j))],
        out_specs=pl.BlockSpec((8, tn), lambda j: (0, j)),
        out_shape=jax.ShapeDtypeStruct((8, n), F32),
        compiler_params=_cparams(("arbitrary",)),
        name="ada_mod",
    )(cs, w, b)


def _conv3(x, w, is_grid):
    t = x.shape[0]
    row = _iota2((t, 1), 0)
    if is_grid:
        pos = jnp.bitwise_and(row, GRID_W - 1)
        has_l, has_r = pos != 0, pos != GRID_W - 1
    else:
        has_l, has_r = row != 0, row != t - 1
    xl = jnp.where(has_l, pltpu.roll(x, 1, 0), 0.0)
    xr = jnp.where(has_r, pltpu.roll(x, t - 1, 0), 0.0)
    return xl * w[0:1] + x * w[1:2] + xr * w[2:3]


def _mlp_up_kernel(x_ref, sh_ref, sc_ref, w_ref, o_ref, a_scr):
    @pl.when(pl.program_id(2) == 0)
    def _():
        a_scr[...] = (x_ref[...] * (1.0 + sc_ref[...]) + sh_ref[...]).astype(BF16)

    acc = jnp.dot(a_scr[...], w_ref[...].astype(BF16), preferred_element_type=F32)
    o_ref[...] = jnp.square(jnp.maximum(acc, 0.0)).astype(o_ref.dtype)


def _mlp_up(x, sh, sc, w, layer, *, tm, tn):
    bsz, l, d = x.shape
    n = w.shape[2]
    return pl.pallas_call(
        _mlp_up_kernel,
        grid=(bsz, l // tm, n // tn),
        in_specs=[pl.BlockSpec((None, tm, d), lambda b, i, j: (b, i, 0)),
                  pl.BlockSpec((None, 1, d), lambda b, i, j: (b, 0, 0)),
                  pl.BlockSpec((None, 1, d), lambda b, i, j: (b, 0, 0)),
                  pl.BlockSpec((None, d, tn), lambda b, i, j: (layer, 0, j))],
        out_specs=pl.BlockSpec((None, tm, tn), lambda b, i, j: (b, i, j)),
        out_shape=jax.ShapeDtypeStruct((bsz, l, n), BF16),
        scratch_shapes=[pltpu.VMEM((tm, d), BF16)],
        compiler_params=_cparams(("parallel", "parallel", "arbitrary")),
        name="mm_relu2",
    )(x, sh, sc, w)


def _in_proj_kernel(off_ref, conv_ref, x_ref, sh_ref, sc_ref, w_ref, cw_ref, cb_ref, o_ref, a_scr, *, conv_grid):
    del off_ref
    has_conv = conv_ref[pl.program_id(2)] != 0

    @pl.when(pl.program_id(2) == 0)
    def _():
        a_scr[...] = (x_ref[...] * (1.0 + sc_ref[...]) + sh_ref[...]).astype(BF16)

    acc = lax.dot_general(a_scr[...], w_ref[0].astype(BF16), (((1,), (1,)), ((), ())),
                          preferred_element_type=F32)

    @pl.when(has_conv)
    def _():
        o_ref[...] = _conv3(acc, cw_ref[...], conv_grid) + cb_ref[...]

    @pl.when(jnp.logical_not(has_conv))
    def _():
        o_ref[...] = acc


def _in_proj(x, sh, sc, w_t, layer, windows, taps, bias, *, tm, is_grid):
    bsz, l, d = x.shape
    nblk = len(windows)
    assert tm == l or (is_grid and tm % GRID_W == 0)
    row_off = jnp.asarray(SOURCE_ROWS[list(windows)])
    has_conv = jnp.asarray(CONV_WINDOWS[list(windows)])
    grid_spec = pltpu.PrefetchScalarGridSpec(
        num_scalar_prefetch=2,
        grid=(bsz, l // tm, nblk),
        in_specs=[pl.BlockSpec((None, tm, d), lambda b, i, j, off, cv: (b, i, 0)),
                  pl.BlockSpec((None, 1, d), lambda b, i, j, off, cv: (b, 0, 0)),
                  pl.BlockSpec((None, 1, d), lambda b, i, j, off, cv: (b, 0, 0)),
                  pl.BlockSpec((pl.Element(1), pl.Element(IN_TN), pl.Element(d)),
                               lambda b, i, j, off, cv: (layer, pl.multiple_of(off[j], 4 * HEADS), 0)),
                  pl.BlockSpec((None, 3, IN_TN), lambda b, i, j, off, cv: (0, 0, j)),
                  pl.BlockSpec((None, 1, IN_TN), lambda b, i, j, off, cv: (0, 0, j))],
        out_specs=pl.BlockSpec((None, tm, IN_TN), lambda b, i, j, off, cv: (b, i, j)),
        scratch_shapes=[pltpu.VMEM((tm, d), BF16)],
    )
    return pl.pallas_call(
        functools.partial(_in_proj_kernel, conv_grid=is_grid),
        name="mm_in_proj",
        grid_spec=grid_spec,
        out_shape=jax.ShapeDtypeStruct((bsz, l, nblk * IN_TN), F32),
        compiler_params=_cparams(("parallel", "parallel", "arbitrary")),
    )(row_off, has_conv, x, sh, sc, w_t, taps, bias)


def _mm_ln_kernel(a_ref, w_ref, x_ref, gate_ref, g_ref, b_ref, o_ref, acc_ref, *, nk, alpha):
    k = pl.program_id(2)

    @pl.when(k == 0)
    def _():
        acc_ref[...] = jnp.zeros_like(acc_ref)

    acc_ref[...] += jnp.dot(a_ref[...], w_ref[...].astype(BF16), preferred_element_type=F32)

    @pl.when(k == nk - 1)
    def _():
        y = alpha * x_ref[...] + gate_ref[...] * acc_ref[...]
        mu = jnp.mean(y, axis=-1, keepdims=True)
        yc = y - mu
        var = jnp.mean(yc * yc, axis=-1, keepdims=True)
        o_ref[...] = yc * lax.rsqrt(var + LN_EPS) * g_ref[...] + b_ref[...]


def _mm_ln(a, w, layer, x, gate, ln_g, ln_b, *, tm, tk, alpha):
    bsz, l, kdim = a.shape
    d = w.shape[2]
    nk = kdim // tk
    return pl.pallas_call(
        functools.partial(_mm_ln_kernel, nk=nk, alpha=alpha),
        grid=(bsz, l // tm, nk),
        in_specs=[pl.BlockSpec((None, tm, tk), lambda b, i, k: (b, i, k)),
                  pl.BlockSpec((None, tk, d), lambda b, i, k: (layer, k, 0)),
                  pl.BlockSpec((None, tm, d), lambda b, i, k: (b, i, 0)),
                  pl.BlockSpec((None, 1, d), lambda b, i, k: (b, 0, 0)),
                  pl.BlockSpec((1, d), lambda b, i, k: (0, 0)),
                  pl.BlockSpec((1, d), lambda b, i, k: (0, 0))],
        out_specs=pl.BlockSpec((None, tm, d), lambda b, i, k: (b, i, 0)),
        out_shape=jax.ShapeDtypeStruct((bsz, l, d), F32),
        scratch_shapes=[pltpu.VMEM((tm, d), F32)],
        compiler_params=_cparams(("parallel", "parallel", "arbitrary")),
        name="mm_resid_ln",
    )(a, w, x, gate, ln_g, ln_b)


def _branch_kernel(b0_ref, b1_ref, b2_ref, w_ref, g0_ref, g1_ref, g2_ref, o_ref):
    b0 = jnp.concatenate([b0_ref[s] for s in range(b0_ref.shape[0])], axis=1).astype(BF16)
    acc = _sigmoid(g0_ref[...]) * jnp.dot(b0, w_ref[0].astype(BF16), preferred_element_type=F32)
    acc += _sigmoid(g1_ref[...]) * jnp.dot(b1_ref[...], w_ref[1].astype(BF16), preferred_element_type=F32)
    acc += _sigmoid(g2_ref[...]) * jnp.dot(b2_ref[...], w_ref[2].astype(BF16), preferred_element_type=F32)
    o_ref[...] = acc.astype(o_ref.dtype)


def _branch_merge(branches, w, layer, h, *, tm, tn):
    bsz, l, c = branches[1].shape
    d = w.shape[3]
    nj = d // tn
    br_spec = pl.BlockSpec((None, tm, c), lambda b, i, j: (b, i, 0))
    hy_spec = pl.BlockSpec((None, c // LANE, tm, LANE), lambda b, i, j: (b, 0, i, 0))

    def gate_spec(g):
        return pl.BlockSpec((None, tm, tn), lambda b, i, j: (b, i, COL_GATE // tn + g * nj + j))

    return pl.pallas_call(
        _branch_kernel,
        grid=(bsz, l // tm, nj),
        in_specs=[hy_spec, br_spec, br_spec,
                  pl.BlockSpec((None, N_BRANCH, c, tn), lambda b, i, j: (layer, 0, 0, j)),
                  gate_spec(0), gate_spec(1), gate_spec(2)],
        out_specs=pl.BlockSpec((None, tm, tn), lambda b, i, j: (b, i, j)),
        out_shape=jax.ShapeDtypeStruct((bsz, l, d), BF16),
        compiler_params=_cparams(("parallel", "parallel", "arbitrary")),
        name="branch_merge",
    )(*branches, w, h, h, h)


def _iota2(shape, axis):
    return lax.broadcasted_iota(jnp.int32, shape, axis)


def _split3(g):
    g1 = g.astype(BF16)
    r = g - g1.astype(F32)
    g2 = r.astype(BF16)
    g3 = (r - g2.astype(F32)).astype(BF16)
    return g1, g2, g3


def _cumsum_rows(g, rev):
    row = _iota2((CHUNK, 3 * CHUNK), 0)
    col = jnp.bitwise_and(_iota2((CHUNK, 3 * CHUNK), 1), CHUNK - 1)
    tri = jnp.where((col >= row) if rev else (col <= row), 1.0, 0.0).astype(BF16)
    g3 = jnp.concatenate(_split3(g), axis=0)
    return jnp.dot(tri, g3, preferred_element_type=F32)


def _hg_prepare(items):
    for it in items:
        lb = it["lb"]
        sg = _sigmoid(it["fz"])
        it["g"] = jnp.log(jnp.maximum(lb + (1.0 - lb) * sg, HG_F_MIN))
        it["kk"] = (1.0 - lb) * (1.0 - sg)
    for it in items:
        it["bc"] = _cumsum_rows(it["g"], it["rev"])
    half = CHUNK // 2
    for it in items:
        bc = it["bc"]
        if it["rev"]:
            mid, last = bc[half:half + 1], bc[0:1]
        else:
            mid, last = bc[half - 1:half], bc[CHUNK - 1:CHUNK]
        it["e_last"] = jnp.exp(last)
        kd = it["kk"] * jnp.exp(mid - bc)
        it["ks"] = kd * jnp.exp(last - mid)
        if it["q"] is not None:
            qd = _silu(it["q"]) * jnp.exp(bc - mid)
            it["qd"], it["kd"] = qd, kd
            it["qs"] = qd * jnp.exp(mid)
    r = _iota2((CHUNK, CHUNK), 0)
    c = _iota2((CHUNK, CHUNK), 1)
    for it in items:
        it["kv"] = _dot_tn(it["v"], it["ks"])
        if it["q"] is not None:
            att = _dot_nt(it["qd"], it["kd"])
            it["att"] = jnp.where((c >= r) if it["rev"] else (c <= r), att, 0.0)
    for it in items:
        if it["q"] is not None:
            it["o_c"] = _dot(it["att"], it["v"])


def _hg_kernel(*refs, nchunk, with_output):
    if with_output:
        (ff_ref, vf_ref, qf_ref, fb_ref, vb_ref, qb_ref, lbf_ref, lbb_ref, s0f_ref, s0b_ref,
         of_ref, ob_ref, sf_ref, sb_ref, stf, stb) = refs
    else:
        (ff_ref, vf_ref, fb_ref, vb_ref, lbf_ref, lbb_ref, s0f_ref, s0b_ref,
         sf_ref, sb_ref, stf, stb) = refs
        qf_ref = qb_ref = of_ref = ob_ref = None

    @pl.when(pl.program_id(2) == 0)
    def _():
        stf[...] = s0f_ref[...]
        stb[...] = s0b_ref[...]

    def chunk_item(f_ref, v_ref, q_ref, lb_ref, ci, rev):
        sl = pl.ds(ci * CHUNK, CHUNK)
        return {"fz": f_ref[sl, :], "v": v_ref[sl, :], "q": q_ref[sl, :] if with_output else None,
                "lb": lb_ref[...], "rev": rev, "sl": sl}

    items = []
    for ci in range(nchunk):
        items.append(chunk_item(ff_ref, vf_ref, qf_ref, lbf_ref, ci, False))
        items.append(chunk_item(fb_ref, vb_ref, qb_ref, lbb_ref, nchunk - 1 - ci, True))
    _hg_prepare(items)
    state = {False: stf[...], True: stb[...]}
    for it in items:
        st = state[it["rev"]]
        if with_output:
            o_ref = ob_ref if it["rev"] else of_ref
            o_ref[it["sl"], :] = it["o_c"] + _dot_nt(it["qs"], st)
        state[it["rev"]] = st * it["e_last"] + it["kv"]
    stf[...] = state[False]
    stb[...] = state[True]
    sf_ref[...] = state[False]
    sb_ref[...] = state[True]


def _hgrn2_scan(h, cols, lb_f, lb_b, s0, *, tblk, with_output):
    bsz, l, _ = h.shape
    nt = l // tblk
    c_ff, c_fb, c_i, c_q = (c // LANE for c in cols)

    def fwd(cb):
        return pl.BlockSpec((None, tblk, LANE), lambda b, hh, t: (b, t, cb + hh))

    def bwd(cb):
        return pl.BlockSpec((None, tblk, LANE), lambda b, hh, t: (b, nt - 1 - t, cb + hh))

    lb_spec = pl.BlockSpec((1, LANE), lambda b, hh, t: (0, hh))
    st_spec = pl.BlockSpec((None, None, HEAD_DIM, HEAD_DIM), lambda b, hh, t: (b, hh, 0, 0))
    if with_output:
        in_specs = [fwd(c_ff), fwd(c_i), fwd(c_q), bwd(c_fb), bwd(c_i), bwd(c_q)]
        args = [h] * 6
    else:
        in_specs = [fwd(c_ff), fwd(c_i), bwd(c_fb), bwd(c_i)]
        args = [h] * 4
    in_specs += [lb_spec, lb_spec, st_spec, st_spec]
    args += [lb_f, lb_b, s0[0], s0[1]]
    st_shape = jax.ShapeDtypeStruct((bsz, HEADS, HEAD_DIM, HEAD_DIM), F32)
    out_specs = [st_spec, st_spec]
    out_shape = [st_shape, st_shape]
    if with_output:
        o_shape = jax.ShapeDtypeStruct((bsz, l, HEADS * HEAD_DIM), F32)
        out_specs = [pl.BlockSpec((None, tblk, LANE), lambda b, hh, t: (b, t, hh)),
                     pl.BlockSpec((None, tblk, LANE), lambda b, hh, t: (b, nt - 1 - t, hh))] + out_specs
        out_shape = [o_shape, o_shape] + out_shape
    outs = pl.pallas_call(
        functools.partial(_hg_kernel, nchunk=tblk // CHUNK, with_output=with_output),
        name="hgrn2_scan",
        grid=(bsz, HEADS, nt),
        in_specs=in_specs,
        out_specs=out_specs,
        out_shape=out_shape,
        scratch_shapes=[pltpu.VMEM((HEAD_DIM, HEAD_DIM), F32), pltpu.VMEM((HEAD_DIM, HEAD_DIM), F32)],
        compiler_params=_cparams(("parallel", "parallel", "arbitrary")),
    )(*args)
    if with_output:
        return outs[0], outs[1], (outs[2], outs[3])
    return None, None, (outs[0], outs[1])


def _l2norm(t):
    return t * lax.rsqrt(jnp.sum(t * t, axis=-1, keepdims=True) + 1e-6)


def _softplus(x):
    return jnp.maximum(x, 0.0) + jnp.log(1.0 + jnp.exp(-jnp.abs(x)))


def _dn_prepare(items):
    r = _iota2((CHUNK, CHUNK), 0)
    c = _iota2((CHUNK, CHUNK), 1)
    eye = r == c
    for it in items:
        rev = it["rev"]
        incl = (c >= r) if rev else (c <= r)
        incl_t = (r >= c) if rev else (r <= c)
        g_b = jnp.broadcast_to(it["g"], (CHUNK, CHUNK))
        gc_col = jnp.sum(jnp.where(incl, g_b, 0.0), axis=1, keepdims=True)
        g_col = jnp.sum(jnp.where(eye, g_b, 0.0), axis=1, keepdims=True)
        gc_row = jnp.sum(jnp.where(incl_t, jnp.broadcast_to(g_col, (CHUNK, CHUNK)), 0.0), axis=0, keepdims=True)
        beta_col = jnp.sum(jnp.where(eye, jnp.broadcast_to(it["beta"], (CHUNK, CHUNK)), 0.0), axis=1,
                           keepdims=True)
        it["decay"] = jnp.where(incl, jnp.exp(jnp.where(incl, gc_col - gc_row, 0.0)), 0.0)
        g_last = gc_col[0:1] if rev else gc_col[CHUNK - 1:CHUNK]
        it["e_last"] = jnp.exp(g_last)
        egc = jnp.exp(gc_col)
        kb = it["k"] * beta_col
        it["kb"] = kb
        it["rhs"] = jnp.concatenate([it["v"] * beta_col, kb * egc], axis=1)
        it["kd"] = it["k"] * jnp.exp(g_last - gc_col)
        if it["q"] is not None:
            it["qe"] = it["q"] * egc
    for it in items:
        it["kk"] = _dot_nt(it["kb"], it["k"])
        if it["q"] is not None:
            it["qk"] = _dot_nt(it["q"], it["k"]) * it["decay"]
    for it in items:
        strict = (c > r) if it["rev"] else (c < r)
        n = jnp.where(strict, it["kk"] * it["decay"], 0.0)
        it["n"] = n
        it["p"] = jnp.where(eye, 1.0, 0.0) - n
    for it in items:
        it["m"] = _dot(it["n"], it["n"])
    for i in range(5):
        for it in items:
            if i < 4:
                pm = _dot(jnp.concatenate([it["p"], it["m"]], axis=0), it["m"])
                it["p"] = it["p"] + pm[:CHUNK]
                it["m"] = pm[CHUNK:]
            else:
                it["p"] = it["p"] + _dot(it["p"], it["m"])
    for it in items:
        it["uw"] = _dot(it["p"], it["rhs"])
    for it in items:
        bm = _dot_tn(it["kd"], it["uw"])
        it["b_c"], it["m_c"] = bm[:, :HEAD_DIM], bm[:, HEAD_DIM:]
        if it["q"] is not None:
            ow = _dot(it["qk"], it["uw"])
            it["o_c"], it["q_eff"] = ow[:, :HEAD_DIM], it["qe"] - ow[:, HEAD_DIM:]


def _dn_kernel(*refs, nchunk, with_output):
    if with_output:
        (alog_ref, dtb_ref, kf_ref, vf_ref, qf_ref, abf_ref, kb_ref, vb_ref, qb_ref, abb_ref,
         s0f_ref, s0b_ref, of_ref, ob_ref, sf_ref, sb_ref, stf, stb) = refs
    else:
        (alog_ref, dtb_ref, kf_ref, vf_ref, abf_ref, kb_ref, vb_ref, abb_ref,
         s0f_ref, s0b_ref, sf_ref, sb_ref, stf, stb) = refs
        qf_ref = qb_ref = of_ref = ob_ref = None
    hh = pl.program_id(1)

    @pl.when(pl.program_id(2) == 0)
    def _():
        stf[...] = s0f_ref[...]
        stb[...] = s0b_ref[...]

    def prep(k_ref, v_ref, q_ref, ab_ref, d):
        k = _l2norm(_silu(k_ref[...]))
        v = _silu(v_ref[...])
        q = None
        if with_output:
            q = _l2norm(_silu(q_ref[...])) * (HEAD_DIM ** -0.5)
        a = ab_ref[pl.ds(d * HEADS + hh, 1), :]
        bt = ab_ref[pl.ds(2 * HEADS + d * HEADS + hh, 1), :]
        g = -jnp.exp(alog_ref[d, hh]) * _softplus(a + dtb_ref[d, hh])
        return k, v, q, g, _sigmoid(bt)

    kf, vf, qf, gf, bf = prep(kf_ref, vf_ref, qf_ref, abf_ref, 0)
    kb, vb, qb, gb, bb = prep(kb_ref, vb_ref, qb_ref, abb_ref, 1)
    def chunk_item(k, v, q, g, beta, ci, rev):
        lo = ci * CHUNK
        return {"k": k[lo:lo + CHUNK], "v": v[lo:lo + CHUNK], "q": q[lo:lo + CHUNK] if with_output else None,
                "g": g[:, lo:lo + CHUNK], "beta": beta[:, lo:lo + CHUNK], "rev": rev, "lo": lo}

    items = []
    for ci in range(nchunk):
        items.append(chunk_item(kf, vf, qf, gf, bf, ci, False))
        items.append(chunk_item(kb, vb, qb, gb, bb, nchunk - 1 - ci, True))
    _dn_prepare(items)
    state = {False: stf[...], True: stb[...]}
    for it in items:
        s = state[it["rev"]]
        if with_output:
            o_ref = ob_ref if it["rev"] else of_ref
            o_ref[pl.ds(it["lo"], CHUNK), :] = it["o_c"] + _dot(it["q_eff"], s)
        state[it["rev"]] = s * it["e_last"] + it["b_c"] - _dot(it["m_c"], s)
    stf[...] = state[False]
    stb[...] = state[True]
    sf_ref[...] = state[False]
    sb_ref[...] = state[True]


def _deltanet_scan(h, ab_t, cols, a_log, dt_bias, s0, *, tblk, with_output):
    bsz, l, _ = h.shape
    nt = l // tblk
    c_k, c_v, c_q = (c // LANE for c in cols)
    rep = HEADS // DN_QK_HEADS

    def tok(cb, shared, rev):
        def imap(b, hh, t):
            return (b, (nt - 1 - t) if rev else t, cb + (hh // rep if shared else hh))
        return pl.BlockSpec((None, tblk, LANE), imap)

    def ab_spec(rev):
        return pl.BlockSpec((None, 4 * HEADS, tblk), lambda b, hh, t: (b, 0, (nt - 1 - t) if rev else t))

    smem = pl.BlockSpec(memory_space=pltpu.SMEM)
    st_spec = pl.BlockSpec((None, None, HEAD_DIM, HEAD_DIM), lambda b, hh, t: (b, hh, 0, 0))
    if with_output:
        in_specs = [smem, smem, tok(c_k, True, False), tok(c_v, False, False), tok(c_q, True, False), ab_spec(False),
                    tok(c_k, True, True), tok(c_v, False, True), tok(c_q, True, True), ab_spec(True),
                    st_spec, st_spec]
        args = [a_log, dt_bias, h, h, h, ab_t, h, h, h, ab_t, s0[0], s0[1]]
    else:
        in_specs = [smem, smem, tok(c_k, True, False), tok(c_v, False, False), ab_spec(False),
                    tok(c_k, True, True), tok(c_v, False, True), ab_spec(True),
                    st_spec, st_spec]
        args = [a_log, dt_bias, h, h, ab_t, h, h, ab_t, s0[0], s0[1]]
    st_shape = jax.ShapeDtypeStruct((bsz, HEADS, HEAD_DIM, HEAD_DIM), F32)
    out_specs = [st_spec, st_spec]
    out_shape = [st_shape, st_shape]
    if with_output:
        o_shape = jax.ShapeDtypeStruct((bsz, l, HEADS * HEAD_DIM), F32)
        out_specs = [pl.BlockSpec((None, tblk, LANE), lambda b, hh, t: (b, t, hh)),
                     pl.BlockSpec((None, tblk, LANE), lambda b, hh, t: (b, nt - 1 - t, hh))] + out_specs
        out_shape = [o_shape, o_shape] + out_shape
    outs = pl.pallas_call(
        functools.partial(_dn_kernel, nchunk=tblk // CHUNK, with_output=with_output),
        name="deltanet_scan",
        grid=(bsz, HEADS, nt),
        in_specs=in_specs,
        out_specs=out_specs,
        out_shape=out_shape,
        scratch_shapes=[pltpu.VMEM((HEAD_DIM, HEAD_DIM), F32), pltpu.VMEM((HEAD_DIM, HEAD_DIM), F32)],
        compiler_params=_cparams(("parallel", "parallel", "arbitrary")),
    )(*args)
    if with_output:
        return outs[0], outs[1], (outs[2], outs[3])
    return None, None, (outs[0], outs[1])


def _combine_kernel(of_ref, ob_ref, z_ref, w_ref, o_ref, *, use_silu):
    o = of_ref[...] + ob_ref[...]
    o = o * lax.rsqrt(jnp.mean(o * o, axis=-1, keepdims=True) + RMS_EPS) * w_ref[...]
    z = z_ref[...]
    o_ref[...] = (o * (_silu(z) if use_silu else _sigmoid(z))).astype(o_ref.dtype)


def _combine(o_f, o_b, h, col_z, norm_w, *, tl, use_silu):
    bsz, l, c = o_f.shape
    cz = col_z // LANE
    spec = pl.BlockSpec((None, tl, LANE), lambda b, i, hh: (b, i, hh))
    return pl.pallas_call(
        functools.partial(_combine_kernel, use_silu=use_silu),
        name="mixer_norm_gate",
        grid=(bsz, l // tl, c // LANE),
        in_specs=[spec, spec,
                  pl.BlockSpec((None, tl, LANE), lambda b, i, hh: (b, i, cz + hh)),
                  pl.BlockSpec((1, LANE), lambda b, i, hh: (0, 0))],
        out_specs=spec,
        out_shape=jax.ShapeDtypeStruct((bsz, l, c), BF16),
        compiler_params=_cparams(("parallel", "parallel", "arbitrary")),
    )(o_f, o_b, h, norm_w)


def _dot_f32(a, b):
    return jnp.dot(a, b, precision=lax.Precision.HIGHEST, preferred_element_type=F32)


def _hy_filt_kernel(w1_ref, b1_ref, w2_ref, b2_ref, w3_ref, o_ref, hid_ref, *, l_total, tl):
    i = pl.program_id(0)
    j = pl.program_id(1)

    @pl.when(j == 0)
    def _():
        row = (_iota2((tl, LANE), 0) + i * tl).astype(F32)
        lane = _iota2((tl, LANE), 1)
        t = row * (1.0 / (l_total - 1))
        ang = row * (2.0 * math.pi / l_total)
        bidx = jnp.where(lane <= HY_BANDS, lane - 1, lane - 1 - HY_BANDS).astype(F32)
        band = 1e-4 + bidx * ((HY_BANDS - 1 - 1e-4) / (HY_BANDS - 1))
        arg = band * ang
        feats = jnp.where(lane == 0, t,
                          jnp.where(lane <= HY_BANDS, jnp.cos(arg),
                                    jnp.where(lane <= 2 * HY_BANDS, -jnp.sin(arg), 0.0)))
        hid = jnp.sin(_dot_f32(feats, w1_ref[...]) + b1_ref[...])
        hid_ref[...] = jnp.sin(_dot_f32(hid, w2_ref[...]) + b2_ref[...])

    hcol = _dot_f32(hid_ref[...], w3_ref[...])
    width = hcol.shape[1]
    ch = _iota2((tl, width), 1).astype(F32)
    lo = math.log(HY_DECAY_TARGET) / HY_SLOW_DECAY
    hi = math.log(HY_DECAY_TARGET) / HY_FAST_DECAY
    delta = jnp.abs(lo + ch * ((hi - lo) / (width - 1)))
    trow = (_iota2((tl, width), 0) + i * tl)
    out = hcol * jnp.exp(-(trow.astype(F32) * (1.0 / (l_total - 1))) * delta)
    drop = jnp.logical_and(trow == 0, jnp.bitwise_and(j, 1) == 1)
    o_ref[...] = jnp.where(drop, 0.0, out)


def _hy_filters(l_total, w1p, b1, w2, b2, w3):
    tl = min(l_total, 512)
    n = w3.shape[1]
    full = lambda a: pl.BlockSpec(a.shape, lambda i, j: (0,) * a.ndim)
    return pl.pallas_call(
        functools.partial(_hy_filt_kernel, l_total=l_total, tl=tl),
        name="hy_filters",
        grid=(l_total // tl, n // BRANCH_WIDTH),
        in_specs=[full(w1p), full(b1), full(w2), full(b2),
                  pl.BlockSpec((HY_HIDDEN, BRANCH_WIDTH), lambda i, j: (0, j))],
        out_specs=pl.BlockSpec((tl, BRANCH_WIDTH), lambda i, j: (i, j)),
        out_shape=jax.ShapeDtypeStruct((l_total, n), F32),
        scratch_shapes=[pltpu.VMEM((tl, HY_HIDDEN), F32)],
        compiler_params=_cparams(("parallel", "arbitrary")),
    )(w1p, b1, w2, b2, w3)


@functools.lru_cache(maxsize=None)
def _fft_tables(n1, n2):
    n = n1 * n2
    n2h = n2 // 2
    k2 = np.arange(n2, dtype=np.int64)[:, None]
    m = np.arange(n2h, dtype=np.int64)[None, :]
    ta = np.zeros((n1, 2 * n2, 2 * n2h))
    for a in range(n1):
        th = 2.0 * np.pi * ((k2 * (a + n1 * m)) % n) / n
        c, s = np.cos(th), np.sin(th)
        ta[a] = np.block([[c, s], [-s, c]])
    tai = np.transpose(ta, (0, 2, 1)) / n
    k1 = np.arange(n1, dtype=np.int64)
    ph = 2.0 * np.pi * ((k1[:, None] * k1[None, :]) % n1) / n1
    c, s = np.cos(ph), np.sin(ph)
    fb = np.block([[c, s], [-s, c]])
    fs = fb[:n1] + fb[n1:]
    fsr = np.stack([fs[(-k1 - e) % n1] for e in (0, 1)])
    return tuple(np.asarray(t, np.float32) for t in (ta, tai, fb, fb.T, fs, fsr))


def _fft_split(l_total):
    n = 2 * l_total
    n1 = min(128, l_total // 8)
    return n1, n // n1


def _rows2d(ref):
    rows = 1
    for s in ref.shape[:-1]:
        rows *= s
    return ref.reshape(rows, ref.shape[-1])


def _lane_slabs(shape, imap, ns):
    def slab(s):
        def index_map(*a):
            idx = imap(*a)
            return idx[:-1] + (idx[-1] * ns + s,)
        return pl.BlockSpec(shape + (LANE,), index_map)
    return [slab(s) for s in range(ns)]


def _strided_rows(ref2, group, n, j):
    return ref2[pl.ds(group * 8 + j, n, stride=8), :]


def _hy_stage_a_kernel(gr_ref, cr_ref, gi_ref, ci_ref, *refs, ns, n2, n2h, slab_major_in):
    del gr_ref, cr_ref, gi_ref, ci_ref
    if slab_major_in:
        xs = [[(_rows2d(r), s * n2h) for s in range(ns)] for r in refs[:2]]
        refs = refs[2:]
    else:
        xs = [[(_rows2d(r), 0) for r in refs[q * ns:(q + 1) * ns]] for q in range(2)]
        refs = refs[2 * ns:]
    t_ref, o_ref = refs
    o2 = _rows2d(o_ref)
    for j in range(8):
        x = jnp.concatenate([jnp.concatenate([_strided_rows(r2, grp, n2h, j) for r2, grp in part], axis=1)
                             for part in xs], axis=0).astype(BF16)
        y = jnp.dot(t_ref[j], x, preferred_element_type=F32)
        for p in range(2):
            for s in range(ns):
                o2[pl.ds(((p * ns + s) * n2) * 8 + j, n2, stride=8), :] = y[p * n2:(p + 1) * n2, s * LANE:(s + 1) * LANE]


def _hy_stage_a(x, pairs, tab, *, ct, n2, slab_major_in):
    n2h = n2 // 2
    n1 = x.shape[-2] if slab_major_in else x.shape[2]
    npair = len(pairs)
    nct = BRANCH_WIDTH // ct
    ns = ct // LANE
    gr = jnp.asarray([p[0][0] for p in pairs], jnp.int32)
    cr = jnp.asarray([p[0][1] for p in pairs], jnp.int32)
    gi = jnp.asarray([p[1][0] for p in pairs], jnp.int32)
    ci = jnp.asarray([p[1][1] for p in pairs], jnp.int32)
    if slab_major_in:
        in_specs = [pl.BlockSpec((None, ns, n2h, 8, LANE), lambda p, c, g, *_: (0, c, 0, g, 0)),
                    pl.BlockSpec((None, ns, n2h, 8, LANE), lambda p, c, g, *_: (1, c, 0, g, 0))]
        args = [x, x]
    else:
        in_specs = (_lane_slabs((None, n2h, 8), lambda p, c, g, gr, cr, gi, ci: (gr[p], 0, g, cr[p] * nct + c), ns)
                    + _lane_slabs((None, n2h, 8), lambda p, c, g, gr, cr, gi, ci: (gi[p], 0, g, ci[p] * nct + c), ns))
        args = [x] * (2 * ns)
    grid_spec = pltpu.PrefetchScalarGridSpec(
        num_scalar_prefetch=4,
        grid=(npair, nct, n1 // 8),
        in_specs=in_specs + [pl.BlockSpec((8, 2 * n2, 2 * n2h), lambda p, c, g, *_: (g, 0, 0))],
        out_specs=pl.BlockSpec((None, 2, ns, n2, 8, LANE), lambda p, c, g, *_: (p, 0, c, 0, g, 0)),
    )
    return pl.pallas_call(
        functools.partial(_hy_stage_a_kernel, ns=ns, n2=n2, n2h=n2h, slab_major_in=slab_major_in),
        name="hy_stage_a",
        grid_spec=grid_spec,
        out_shape=jax.ShapeDtypeStruct((npair, 2, BRANCH_WIDTH // LANE, n2, n1, LANE), F32),
        compiler_params=_cparams(("parallel", "parallel", "arbitrary")),
    )(gr, cr, gi, ci, *args, tab)


def _hy_spec_kernel(c_ref, cm_ref, c0_ref, fs_ref, fsr_ref, o_ref, *, kg):
    g = pl.program_id(2)
    for k in range(kg):
        s = jnp.dot(fs_ref[...], _complex_slab(c_ref, k), preferred_element_type=F32)
        if k:
            mirror, fsr = _complex_slab(cm_ref, kg - k), fsr_ref[1]
        else:
            mirror, fsr = _complex_slab(c0_ref, 0), jnp.where(g == 0, fsr_ref[0], fsr_ref[1])
        t = jnp.dot(fsr, mirror, preferred_element_type=F32)
        o_ref[0, k] = 0.5 * (s + t)
        o_ref[1, k] = 0.5 * (s - t)


def _complex_slab(ref, k):
    ns = ref.shape[1]
    return jnp.concatenate([jnp.concatenate([ref[p, s, k] for s in range(ns)], axis=1) for p in range(2)],
                           axis=0).astype(BF16)


def _hy_spectrum(cs, fs, fsr, *, ct, kg):
    no, _, nslab, n2, n1, _ = cs.shape
    ns = ct // LANE
    ng = n2 // kg
    return pl.pallas_call(
        functools.partial(_hy_spec_kernel, kg=kg),
        name="hy_spectrum",
        grid=(no, nslab // ns, ng),
        in_specs=[pl.BlockSpec((None, 2, ns, kg, n1, LANE), lambda o, j, g: (o, 0, j, g, 0, 0)),
                  pl.BlockSpec((None, 2, ns, kg, n1, LANE), lambda o, j, g: (o, 0, j, ng - 1 - g, 0, 0)),
                  pl.BlockSpec((None, 2, ns, 1, n1, LANE), lambda o, j, g: (o, 0, j, (kg * (ng - g)) % n2, 0, 0)),
                  pl.BlockSpec(fs.shape, lambda o, j, g: (0, 0)),
                  pl.BlockSpec(fsr.shape, lambda o, j, g: (0, 0, 0))],
        out_specs=pl.BlockSpec((None, 2, kg, n1, ct), lambda o, j, g: (o, 0, g, 0, j)),
        out_shape=jax.ShapeDtypeStruct((no, 2, n2, n1, nslab * LANE), F32),
        compiler_params=_cparams(("parallel", "parallel", "arbitrary")),
    )(cs, cs, cs, fs, fsr)


def _hy_stage_b_kernel(c_ref, k_ref, fb_ref, fbi_ref, o_ref, *, kg, n1):
    ns = c_ref.shape[1]
    for k in range(kg):
        z = jnp.dot(fb_ref[...], _complex_slab(c_ref, k), preferred_element_type=F32)
        zr, zi = z[:n1], z[n1:]
        kr, ki = k_ref[0, k], k_ref[1, k]
        y = jnp.concatenate([zr * kr - zi * ki, zr * ki + zi * kr], axis=0).astype(BF16)
        d = jnp.dot(fbi_ref[...], y, preferred_element_type=F32)
        for p in range(2):
            for s in range(ns):
                o_ref[p, s, k] = d[p * n1:(p + 1) * n1, s * LANE:(s + 1) * LANE]


def _hy_stage_b(cs, spec, order, fb, fbi, *, ct, kg):
    _, _, nslab, n2, n1, _ = cs.shape
    ns = ct // LANE
    c = nslab * LANE
    return pl.pallas_call(
        functools.partial(_hy_stage_b_kernel, kg=kg, n1=n1),
        name="hy_stage_b",
        grid=(c // ct, n2 // kg),
        in_specs=[pl.BlockSpec((None, 2, ns, kg, n1, LANE), lambda j, g: (0, 0, j, g, 0, 0)),
                  pl.BlockSpec((None, 2, kg, n1, ct), lambda j, g: (order, 0, g, 0, j)),
                  pl.BlockSpec(fb.shape, lambda j, g: (0, 0)),
                  pl.BlockSpec(fbi.shape, lambda j, g: (0, 0))],
        out_specs=pl.BlockSpec((2, ns, kg, n1, LANE), lambda j, g: (0, j, g, 0, 0)),
        out_shape=jax.ShapeDtypeStruct((2, nslab, n2, n1, LANE), F32),
        compiler_params=_cparams(("parallel", "arbitrary")),
    )(cs, spec, fb, fbi)


def _hy_stage_c_kernel(*refs, ns, n2, n2h, slab_major_u):
    d2 = _rows2d(refs[0])
    t_ref = refs[1]
    if slab_major_u:
        u2 = _rows2d(refs[2])
        u_parts = [[(u2, (b * ns + s) * n2h) for s in range(ns)] for b in range(2)]
        refs = refs[3:]
    else:
        u_parts = [[(_rows2d(r), b * n2h) for r in refs[2:2 + ns]] for b in range(2)]
        refs = refs[2 + ns:]
    g_parts = [[(_rows2d(r), b * n2h) for r in refs[:ns]] for b in range(2)]
    skip_ref, o_ref = refs[ns:]
    o2 = _rows2d(o_ref)
    skip = skip_ref[...]

    def rows(parts, j):
        return jnp.concatenate([jnp.concatenate([_strided_rows(r2, grp, n2h, j) for r2, grp in part], axis=1)
                                for part in parts], axis=0)

    for j in range(8):
        d = jnp.concatenate([jnp.concatenate([_strided_rows(d2, (p * ns + s) * n2, n2, j) for s in range(ns)], axis=1)
                             for p in range(2)], axis=0).astype(BF16)
        y = jnp.dot(t_ref[j], d, preferred_element_type=F32)
        out = rows(g_parts, j) * (y + skip * rows(u_parts, j))
        for b in range(2):
            for s in range(ns):
                o2[pl.ds(((b * ns + s) * n2h) * 8 + j, n2h, stride=8), :] = (
                    out[b * n2h:(b + 1) * n2h, s * LANE:(s + 1) * LANE])


def _hy_stage_c(d, tai, u, ucol, z4, gcol, skip, *, ct, slab_major_u):
    _, nslab, n2, n1, _ = d.shape
    n2h = n2 // 2
    ns = ct // LANE
    uc, gc = ucol // ct, gcol // ct
    if slab_major_u:
        u_specs = [pl.BlockSpec((2, ns, n2h, 8, LANE), lambda j, g: (0, j, 0, g, 0))]
        u_args = [u]
    else:
        u_specs = _lane_slabs((2, n2h, 8), lambda j, g: (0, 0, g, uc + j), ns)
        u_args = [u] * ns
    return pl.pallas_call(
        functools.partial(_hy_stage_c_kernel, ns=ns, n2=n2, n2h=n2h, slab_major_u=slab_major_u),
        name="hy_stage_c",
        grid=(nslab // ns, n1 // 8),
        in_specs=([pl.BlockSpec((2, ns, n2, 8, LANE), lambda j, g: (0, j, 0, g, 0)),
                   pl.BlockSpec((8, 2 * n2h, 2 * n2), lambda j, g: (g, 0, 0))]
                  + u_specs + _lane_slabs((2, n2h, 8), lambda j, g: (0, 0, g, gc + j), ns)
                  + [pl.BlockSpec((1, ct), lambda j, g: (0, j))]),
        out_specs=pl.BlockSpec((2, ns, n2h, 8, LANE), lambda j, g: (0, j, 0, g, 0)),
        out_shape=jax.ShapeDtypeStruct((2, nslab, n2h, n1, LANE), F32),
        compiler_params=_cparams(("parallel", "arbitrary")),
    )(d, tai, *u_args, *([z4] * ns), skip)


def _hyena(h, filt_params, skip):
    bsz, l, _ = h.shape
    assert bsz == 2
    n1, n2 = _fft_split(l)
    n2h = n2 // 2
    ta, tai, fb, fbi, fs, fsr = (jnp.asarray(t).astype(BF16) for t in _fft_tables(n1, n2))
    ct = 512
    kg = 8

    hfilt = _hy_filters(l, *filt_params)
    cs_k = _hy_stage_a(hfilt.reshape(1, n2h, n1, -1), [((0, 2 * o), (0, 2 * o + 1)) for o in range(HY_ORDER)],
                       ta, ct=ct, n2=n2, slab_major_in=False)
    spec = _hy_spectrum(cs_k, fs, fsr, ct=ct, kg=kg)

    z4 = h.reshape(bsz, n2h, n1, -1)
    u, ucol = z4, COL_HY
    for o in range(HY_ORDER):
        first = o == 0
        cs = _hy_stage_a(u, [((0, ucol // BRANCH_WIDTH), (1, ucol // BRANCH_WIDTH))], ta, ct=ct, n2=n2,
                         slab_major_in=not first)
        d = _hy_stage_b(cs, spec, o, fb, fbi, ct=ct, kg=kg)
        u = _hy_stage_c(d, tai, u, ucol, z4, COL_HY + (o + 1) * BRANCH_WIDTH, skip[o:o + 1], ct=ct,
                        slab_major_u=not first)
        ucol = 0
    return u.reshape(bsz, BRANCH_WIDTH // LANE, l, LANE)


def _conv_tables(hy_conv_w, hy_conv_b, dn_conv_q, dn_conv_k, dn_conv_v):
    depth = hy_conv_w.shape[0]
    ident = lambda n: jnp.broadcast_to(jnp.asarray([0.0, 1.0, 0.0], F32)[None, :, None], (depth, 3, n))
    taps = jnp.concatenate([ident(COL_HY), hy_conv_w, ident(COL_DK - COL_FF), dn_conv_k, dn_conv_v,
                            ident(COL_DQ - COL_AB), dn_conv_q, ident(COL_PAD - COL_DQ - dn_conv_q.shape[2])], axis=2)
    bias = jnp.concatenate([jnp.zeros((depth, 1, COL_HY), F32), hy_conv_b[:, None, :],
                            jnp.zeros((depth, 1, COL_PAD - COL_FF), F32)], axis=2)
    return taps, bias


def _token_mixers(u_src, sh, sc, w_t, conv_t, layer, lp, states, *, is_grid, with_output, tm):
    bsz, l, _ = u_src.shape
    tblk = min(l, 512)
    taps, bias = (a[layer:layer + 1] for a in conv_t)
    if with_output:
        h = _in_proj(u_src, sh, sc, w_t, layer, range(COL_PAD // IN_TN), taps, bias, tm=tm, is_grid=is_grid)
        base = 0
    else:
        lo, hi = STATE_BLOCKS[0] * IN_TN, (STATE_BLOCKS[-1] + 1) * IN_TN
        h = _in_proj(u_src, sh, sc, w_t, layer, STATE_BLOCKS, taps[:, :, lo:hi], bias[:, :, lo:hi],
                     tm=tm, is_grid=is_grid)
        base = COL_FF
    col = lambda c: c - base
    ab_t = jnp.swapaxes(h[:, :, col(COL_AB):col(COL_AB) + 4 * HEADS], 1, 2)
    hg_f, hg_b, hg_states = _hgrn2_scan(h, (col(COL_FF), col(COL_FB), col(COL_HI), col(COL_HQ)),
                                        lp["lb_f"], lp["lb_b"], states[0], tblk=tblk, with_output=with_output)
    dn_f, dn_b, dn_states = _deltanet_scan(h, ab_t, (col(COL_DK), col(COL_DV), col(COL_DQ)),
                                           lp["dn_a_log"], lp["dn_dt_bias"], states[1],
                                           tblk=tblk, with_output=with_output)
    new_states = (hg_states, dn_states)
    if not with_output:
        return None, new_states
    tl = min(l, 2048)
    hg_out = _combine(hg_f, hg_b, h, COL_HGATE, lp["hg_norm_w"], tl=tl, use_silu=False)
    dn_out = _combine(dn_f, dn_b, h, COL_DZ, lp["dn_norm_w"], tl=tl, use_silu=True)
    hy_out = _hyena(h, lp["hy_filt"], lp["hy_skip"])
    ysum = _branch_merge([hy_out, hg_out, dn_out], lp["w_branch"], layer, h, tm=min(l, 1024), tn=512)
    return ysum, new_states


def kernel(x, c, ctx, c_ctx, w_ada, b_ada, w_in, hy_conv_w, hy_conv_b, hy_filt_w1, hy_filt_b1, hy_filt_w2, hy_filt_b2, hy_filt_w3, hy_skip, hg_lb_logits, hg_norm_w, dn_conv_q, dn_conv_k, dn_conv_v, dn_a_log, dn_dt_bias, dn_norm_w, w_branch, w_out, ln1_g, ln1_b, w_ff1, w_ff2, ln2_g, ln2_b):
    depth = w_in.shape[0]
    bsz, _, d = x.shape
    alpha = (2 * depth) ** 0.25
    p = jax.nn.softmax(hg_lb_logits.astype(F32), axis=1)
    lower = jnp.cumsum(p, axis=1) - p[:, :1]
    cs = jnp.concatenate([c, c_ctx[None], jnp.zeros((8 - bsz - 1, d), F32)], axis=0)
    w_r = jnp.swapaxes(w_in, 1, 2)
    w_b, w_o, w_1, w_2 = w_branch.astype(BF16), w_out.astype(BF16), w_ff1, w_ff2.astype(BF16)
    b_ada3 = b_ada[:, None, :]
    conv_t = _conv_tables(hy_conv_w, hy_conv_b, dn_conv_q, dn_conv_k, dn_conv_v)
    h_ctx = ctx
    for l in range(depth):
        last = l == depth - 1
        row = lambda v: v.reshape(1, -1)
        lp = {
            "lb_f": row(lower[0, l]), "lb_b": row(lower[1, l]),
            "hg_norm_w": row(hg_norm_w[l]), "dn_norm_w": row(dn_norm_w[l]),
            "dn_a_log": dn_a_log[l], "dn_dt_bias": dn_dt_bias[l], "hy_skip": hy_skip[l],
            "hy_filt": (jnp.pad(hy_filt_w1[l], ((0, LANE - HY_EMB), (0, 0))), row(hy_filt_b1[l]),
                        hy_filt_w2[l], row(hy_filt_b2[l]), hy_filt_w3[l]),
            "w_branch": w_b,
        }
        g1n, b1n, g2n, b2n = row(ln1_g[l]), row(ln1_b[l]), row(ln2_g[l]), row(ln2_b[l])
        mod = _ada_mod(cs, w_ada, b_ada3, l)
        m_lat = [mod[:bsz, i * d:(i + 1) * d][:, None, :] for i in range(6)]
        m_ctx = [jnp.broadcast_to(mod[bsz, i * d:(i + 1) * d][None, None, :], (bsz, 1, d)) for i in range(6)]
        zero = jnp.zeros((bsz, HEADS, HEAD_DIM, HEAD_DIM), F32)
        init = ((zero, zero), (zero, zero))

        lc = h_ctx.shape[1]
        y_ctx, ctx_states = _token_mixers(h_ctx, m_ctx[0], m_ctx[1], w_r, conv_t, l, lp, init,
                                          is_grid=False, with_output=not last, tm=lc)
        if not last:
            h_ctx = _mm_ln(y_ctx, w_o, l, h_ctx, m_ctx[2], g1n, b1n, tm=lc, tk=d, alpha=alpha)
            mid = _mlp_up(h_ctx, m_ctx[3], m_ctx[4], w_1, l, tm=lc, tn=1024)
            h_ctx = _mm_ln(mid, w_2, l, h_ctx, m_ctx[5], g2n, b2n, tm=lc, tk=1024, alpha=alpha)

        lx = x.shape[1]
        y, _ = _token_mixers(x, m_lat[0], m_lat[1], w_r, conv_t, l, lp, ctx_states,
                             is_grid=True, with_output=True, tm=min(lx, 1024))
        x = _mm_ln(y, w_o, l, x, m_lat[2], g1n, b1n, tm=min(lx, 512), tk=d, alpha=alpha)
        mid = _mlp_up(x, m_lat[3], m_lat[4], w_1, l, tm=min(lx, 1024), tn=1024)
        x = _mm_ln(mid, w_2, l, x, m_lat[5], g2n, b2n, tm=min(lx, 1024), tk=512, alpha=alpha)
    return x
```

```python
import functools
import math

import numpy as np
import jax
import jax.numpy as jnp
from jax import lax
from jax.experimental import pallas as pl
from jax.experimental.pallas import tpu as pltpu

F32 = jnp.float32
BF16 = jnp.bfloat16

D_MODEL = 2048
GRID_W = 64
BRANCH_WIDTH = 1024
N_BRANCH = 3
HY_ORDER = 2
HY_EMB = 33
HY_BANDS = (HY_EMB - 1) // 2
HY_HIDDEN = 64
HY_DECAY_TARGET = 1e-2
HY_FAST_DECAY = 0.3
HY_SLOW_DECAY = 1.5
HEADS = 8
HEAD_DIM = 128
DN_QK_HEADS = 4
HG_F_MIN = 1e-30
CHUNK = 64
LN_EPS = 1e-5
RMS_EPS = 1e-6
LANE = 128

IN_TN = 1024
COL_GATE = 0
COL_HY = 6144
COL_FF = 9216
COL_FB = 10240
COL_HI = 11264
COL_DK = 12288
COL_DV = 12800
COL_AB = 13824
COL_HQ = 14336
COL_HGATE = 15360
COL_DQ = 16384
COL_DZ = 17408
COL_PAD = 18432
SOURCE_ROWS = np.asarray([11296 + IN_TN * k for k in range(6)] + [8224 + IN_TN * k for k in range(3)]
                         + [0, 1024, 2048, 3072, 4096, 4640, 5664, 6688, 7200], np.int32)
CONV_WINDOWS = np.asarray([COL_HY <= IN_TN * j < COL_FF or COL_DK <= IN_TN * j < COL_HQ or IN_TN * j == COL_DQ
                           for j in range(COL_PAD // IN_TN)], np.int32)
STATE_BLOCKS = tuple(range(COL_FF // IN_TN, COL_HQ // IN_TN))

VMEM_LIMIT = 56 * 1024 * 1024


def _cparams(sem):
    return pltpu.CompilerParams(dimension_semantics=sem, vmem_limit_bytes=VMEM_LIMIT)


def _dot(a, b):
    return jnp.dot(a.astype(BF16), b.astype(BF16), preferred_element_type=F32)


def _dot_nt(a, b):
    return lax.dot_general(a.astype(BF16), b.astype(BF16), (((1,), (1,)), ((), ())),
                           preferred_element_type=F32)


def _dot_tn(a, b):
    return lax.dot_general(a.astype(BF16), b.astype(BF16), (((0,), (0,)), ((), ())),
                           preferred_element_type=F32)


def _sigmoid(x):
    return 1.0 / (1.0 + jnp.exp(-x))


def _silu(x):
    return x * _sigmoid(x)


def _ada_kernel(c_ref, w_ref, b_ref, o_ref):
    a = _silu(c_ref[...])
    o_ref[...] = _dot(a, w_ref[...]) + b_ref[...]


def _ada_mod(cs, w, b, layer):
    _, d, n = w.shape
    tn = 1024
    return pl.pallas_call(
        _ada_kernel,
        grid=(n // tn,),
        in_specs=[pl.BlockSpec((8, d), lambda j: (0, 0)),
                  pl.BlockSpec((None, d, tn), lambda j: (layer, 0, j)),
                  pl.BlockSpec((None, 1, tn), lambda j: (layer, 0, j))],
        out_specs=pl.BlockSpec((8, tn), lambda j: (0, j)),
        out_shape=jax.ShapeDtypeStruct((8, n), F32),
        compiler_params=_cparams(("arbitrary",)),
        name="ada_mod",
    )(cs, w, b)


def _conv3(x, w, is_grid):
    t = x.shape[0]
    row = _iota2((t, 1), 0)
    if is_grid:
        pos = jnp.bitwise_and(row, GRID_W - 1)
        has_l, has_r = pos != 0, pos != GRID_W - 1
    else:
        has_l, has_r = row != 0, row != t - 1
    xl = jnp.where(has_l, pltpu.roll(x, 1, 0), 0.0)
    xr = jnp.where(has_r, pltpu.roll(x, t - 1, 0), 0.0)
    return xl * w[0:1] + x * w[1:2] + xr * w[2:3]


def _mlp_up_kernel(x_ref, sh_ref, sc_ref, w_ref, o_ref, a_scr):
    @pl.when(pl.program_id(2) == 0)
    def _():
        a_scr[...] = (x_ref[...] * (1.0 + sc_ref[...]) + sh_ref[...]).astype(BF16)

    acc = jnp.dot(a_scr[...], w_ref[...].astype(BF16), preferred_element_type=F32)
    o_ref[...] = jnp.square(jnp.maximum(acc, 0.0)).astype(o_ref.dtype)


def _mlp_up(x, sh, sc, w, layer, *, tm, tn):
    bsz, l, d = x.shape
    n = w.shape[2]
    return pl.pallas_call(
        _mlp_up_kernel,
        grid=(bsz, l // tm, n // tn),
        in_specs=[pl.BlockSpec((None, tm, d), lambda b, i, j: (b, i, 0)),
                  pl.BlockSpec((None, 1, d), lambda b, i, j: (b, 0, 0)),
                  pl.BlockSpec((None, 1, d), lambda b, i, j: (b, 0, 0)),
                  pl.BlockSpec((None, d, tn), lambda b, i, j: (layer, 0, j))],
        out_specs=pl.BlockSpec((None, tm, tn), lambda b, i, j: (b, i, j)),
        out_shape=jax.ShapeDtypeStruct((bsz, l, n), BF16),
        scratch_shapes=[pltpu.VMEM((tm, d), BF16)],
        compiler_params=_cparams(("parallel", "parallel", "arbitrary")),
        name="mm_relu2",
    )(x, sh, sc, w)


def _in_proj_kernel(off_ref, conv_ref, x_ref, sh_ref, sc_ref, w_ref, cw_ref, cb_ref, o_ref, a_scr, *, conv_grid):
    del off_ref
    has_conv = conv_ref[pl.program_id(2)] != 0

    @pl.when(pl.program_id(2) == 0)
    def _():
        a_scr[...] = (x_ref[...] * (1.0 + sc_ref[...]) + sh_ref[...]).astype(BF16)

    o_ref[...] = lax.dot_general(a_scr[...], w_ref[0].astype(BF16), (((1,), (1,)), ((), ())),
                                 preferred_element_type=F32)

    @pl.when(has_conv)
    def _():
        o_ref[...] = _conv3(o_ref[...], cw_ref[...], conv_grid) + cb_ref[...]


def _in_proj(x, sh, sc, w_t, layer, windows, taps, bias, *, tm, is_grid):
    bsz, l, d = x.shape
    nblk = len(windows)
    assert tm == l or (is_grid and tm % GRID_W == 0)
    row_off = jnp.asarray(SOURCE_ROWS[list(windows)])
    has_conv = jnp.asarray(CONV_WINDOWS[list(windows)])
    grid_spec = pltpu.PrefetchScalarGridSpec(
        num_scalar_prefetch=2,
        grid=(bsz, l // tm, nblk),
        in_specs=[pl.BlockSpec((None, tm, d), lambda b, i, j, off, cv: (b, i, 0)),
                  pl.BlockSpec((None, 1, d), lambda b, i, j, off, cv: (b, 0, 0)),
                  pl.BlockSpec((None, 1, d), lambda b, i, j, off, cv: (b, 0, 0)),
                  pl.BlockSpec((pl.Element(1), pl.Element(IN_TN), pl.Element(d)),
                               lambda b, i, j, off, cv: (layer, pl.multiple_of(off[j], 4 * HEADS), 0)),
                  pl.BlockSpec((None, 3, IN_TN), lambda b, i, j, off, cv: (0, 0, j)),
                  pl.BlockSpec((None, 1, IN_TN), lambda b, i, j, off, cv: (0, 0, j))],
        out_specs=pl.BlockSpec((None, tm, IN_TN), lambda b, i, j, off, cv: (b, i, j)),
        scratch_shapes=[pltpu.VMEM((tm, d), BF16)],
    )
    return pl.pallas_call(
        functools.partial(_in_proj_kernel, conv_grid=is_grid),
        name="mm_in_proj",
        grid_spec=grid_spec,
        out_shape=jax.ShapeDtypeStruct((bsz, l, nblk * IN_TN), F32),
        compiler_params=_cparams(("parallel", "parallel", "arbitrary")),
    )(row_off, has_conv, x, sh, sc, w_t, taps, bias)


def _mm_ln_kernel(a_ref, w_ref, x_ref, gate_ref, g_ref, b_ref, o_ref, acc_ref, *, nk, alpha):
    k = pl.program_id(2)

    @pl.when(k == 0)
    def _():
        acc_ref[...] = jnp.zeros_like(acc_ref)

    acc_ref[...] += jnp.dot(a_ref[...], w_ref[...].astype(BF16), preferred_element_type=F32)

    @pl.when(k == nk - 1)
    def _():
        y = alpha * x_ref[...] + gate_ref[...] * acc_ref[...]
        mu = jnp.mean(y, axis=-1, keepdims=True)
        yc = y - mu
        var = jnp.mean(yc * yc, axis=-1, keepdims=True)
        o_ref[...] = yc * lax.rsqrt(var + LN_EPS) * g_ref[...] + b_ref[...]


def _mm_ln(a, w, layer, x, gate, ln_g, ln_b, *, tm, tk, alpha):
    bsz, l, kdim = a.shape
    d = w.shape[2]
    nk = kdim // tk
    return pl.pallas_call(
        functools.partial(_mm_ln_kernel, nk=nk, alpha=alpha),
        grid=(bsz, l // tm, nk),
        in_specs=[pl.BlockSpec((None, tm, tk), lambda b, i, k: (b, i, k)),
                  pl.BlockSpec((None, tk, d), lambda b, i, k: (layer, k, 0)),
                  pl.BlockSpec((None, tm, d), lambda b, i, k: (b, i, 0)),
                  pl.BlockSpec((None, 1, d), lambda b, i, k: (b, 0, 0)),
                  pl.BlockSpec((1, d), lambda b, i, k: (0, 0)),
                  pl.BlockSpec((1, d), lambda b, i, k: (0, 0))],
        out_specs=pl.BlockSpec((None, tm, d), lambda b, i, k: (b, i, 0)),
        out_shape=jax.ShapeDtypeStruct((bsz, l, d), F32),
        scratch_shapes=[pltpu.VMEM((tm, d), F32)],
        compiler_params=_cparams(("parallel", "parallel", "arbitrary")),
        name="mm_resid_ln",
    )(a, w, x, gate, ln_g, ln_b)


def _branch_kernel(b0_ref, b1_ref, b2_ref, w_ref, g0_ref, g1_ref, g2_ref, o_ref):
    b0 = jnp.concatenate([b0_ref[s] for s in range(b0_ref.shape[0])], axis=1).astype(BF16)
    acc = _sigmoid(g0_ref[...]) * jnp.dot(b0, w_ref[0].astype(BF16), preferred_element_type=F32)
    acc += _sigmoid(g1_ref[...]) * jnp.dot(b1_ref[...], w_ref[1].astype(BF16), preferred_element_type=F32)
    acc += _sigmoid(g2_ref[...]) * jnp.dot(b2_ref[...], w_ref[2].astype(BF16), preferred_element_type=F32)
    o_ref[...] = acc.astype(o_ref.dtype)


def _branch_merge(branches, w, layer, h, *, tm, tn):
    bsz, l, c = branches[1].shape
    d = w.shape[3]
    nj = d // tn
    br_spec = pl.BlockSpec((None, tm, c), lambda b, i, j: (b, i, 0))
    hy_spec = pl.BlockSpec((None, c // LANE, tm, LANE), lambda b, i, j: (b, 0, i, 0))

    def gate_spec(g):
        return pl.BlockSpec((None, tm, tn), lambda b, i, j: (b, i, COL_GATE // tn + g * nj + j))

    return pl.pallas_call(
        _branch_kernel,
        grid=(bsz, l // tm, nj),
        in_specs=[hy_spec, br_spec, br_spec,
                  pl.BlockSpec((None, N_BRANCH, c, tn), lambda b, i, j: (layer, 0, 0, j)),
                  gate_spec(0), gate_spec(1), gate_spec(2)],
        out_specs=pl.BlockSpec((None, tm, tn), lambda b, i, j: (b, i, j)),
        out_shape=jax.ShapeDtypeStruct((bsz, l, d), BF16),
        compiler_params=_cparams(("parallel", "parallel", "arbitrary")),
        name="branch_merge",
    )(*branches, w, h, h, h)


def _iota2(shape, axis):
    return lax.broadcasted_iota(jnp.int32, shape, axis)


def _split3(g):
    g1 = g.astype(BF16)
    r = g - g1.astype(F32)
    g2 = r.astype(BF16)
    g3 = (r - g2.astype(F32)).astype(BF16)
    return g1, g2, g3


def _cumsum_rows(g, rev):
    row = _iota2((CHUNK, 3 * CHUNK), 0)
    col = jnp.bitwise_and(_iota2((CHUNK, 3 * CHUNK), 1), CHUNK - 1)
    tri = jnp.where((col >= row) if rev else (col <= row), 1.0, 0.0).astype(BF16)
    g3 = jnp.concatenate(_split3(g), axis=0)
    return jnp.dot(tri, g3, preferred_element_type=F32)


def _hg_prepare(items):
    for it in items:
        lb = it["lb"]
        sg = _sigmoid(it["fz"])
        it["g"] = jnp.log(jnp.maximum(lb + (1.0 - lb) * sg, HG_F_MIN))
        it["kk"] = (1.0 - lb) * (1.0 - sg)
    for it in items:
        it["bc"] = _cumsum_rows(it["g"], it["rev"])
    half = CHUNK // 2
    for it in items:
        bc = it["bc"]
        if it["rev"]:
            mid, last = bc[half:half + 1], bc[0:1]
        else:
            mid, last = bc[half - 1:half], bc[CHUNK - 1:CHUNK]
        it["e_last"] = jnp.exp(last)
        kd = it["kk"] * jnp.exp(mid - bc)
        it["ks"] = kd * jnp.exp(last - mid)
        if it["q"] is not None:
            qd = _silu(it["q"]) * jnp.exp(bc - mid)
            it["qd"], it["kd"] = qd, kd
            it["qs"] = qd * jnp.exp(mid)
    r = _iota2((CHUNK, CHUNK), 0)
    c = _iota2((CHUNK, CHUNK), 1)
    for it in items:
        it["kv"] = _dot_tn(it["v"], it["ks"])
        if it["q"] is not None:
            att = _dot_nt(it["qd"], it["kd"])
            it["att"] = jnp.where((c >= r) if it["rev"] else (c <= r), att, 0.0)
    for it in items:
        if it["q"] is not None:
            it["o_c"] = _dot(it["att"], it["v"])


def _hg_kernel(*refs, nchunk, with_output):
    if with_output:
        (ff_ref, vf_ref, qf_ref, fb_ref, vb_ref, qb_ref, lbf_ref, lbb_ref, s0f_ref, s0b_ref,
         of_ref, ob_ref, sf_ref, sb_ref, stf, stb) = refs
    else:
        (ff_ref, vf_ref, fb_ref, vb_ref, lbf_ref, lbb_ref, s0f_ref, s0b_ref,
         sf_ref, sb_ref, stf, stb) = refs
        qf_ref = qb_ref = of_ref = ob_ref = None

    @pl.when(pl.program_id(2) == 0)
    def _():
        stf[...] = s0f_ref[...]
        stb[...] = s0b_ref[...]

    def chunk_item(f_ref, v_ref, q_ref, lb_ref, ci, rev):
        sl = pl.ds(ci * CHUNK, CHUNK)
        return {"fz": f_ref[sl, :], "v": v_ref[sl, :], "q": q_ref[sl, :] if with_output else None,
                "lb": lb_ref[...], "rev": rev, "sl": sl}

    items = []
    for ci in range(nchunk):
        items.append(chunk_item(ff_ref, vf_ref, qf_ref, lbf_ref, ci, False))
        items.append(chunk_item(fb_ref, vb_ref, qb_ref, lbb_ref, nchunk - 1 - ci, True))
    _hg_prepare(items)
    state = {False: stf[...], True: stb[...]}
    for it in items:
        st = state[it["rev"]]
        if with_output:
            o_ref = ob_ref if it["rev"] else of_ref
            o_ref[it["sl"], :] = it["o_c"] + _dot_nt(it["qs"], st)
        state[it["rev"]] = st * it["e_last"] + it["kv"]
    stf[...] = state[False]
    stb[...] = state[True]
    sf_ref[...] = state[False]
    sb_ref[...] = state[True]


def _hgrn2_scan(h, cols, lb_f, lb_b, s0, *, tblk, with_output):
    bsz, l, _ = h.shape
    nt = l // tblk
    c_ff, c_fb, c_i, c_q = (c // LANE for c in cols)

    def fwd(cb):
        return pl.BlockSpec((None, tblk, LANE), lambda b, hh, t: (b, t, cb + hh))

    def bwd(cb):
        return pl.BlockSpec((None, tblk, LANE), lambda b, hh, t: (b, nt - 1 - t, cb + hh))

    lb_spec = pl.BlockSpec((1, LANE), lambda b, hh, t: (0, hh))
    st_spec = pl.BlockSpec((None, None, HEAD_DIM, HEAD_DIM), lambda b, hh, t: (b, hh, 0, 0))
    if with_output:
        in_specs = [fwd(c_ff), fwd(c_i), fwd(c_q), bwd(c_fb), bwd(c_i), bwd(c_q)]
        args = [h] * 6
    else:
        in_specs = [fwd(c_ff), fwd(c_i), bwd(c_fb), bwd(c_i)]
        args = [h] * 4
    in_specs += [lb_spec, lb_spec, st_spec, st_spec]
    args += [lb_f, lb_b, s0[0], s0[1]]
    st_shape = jax.ShapeDtypeStruct((bsz, HEADS, HEAD_DIM, HEAD_DIM), F32)
    out_specs = [st_spec, st_spec]
    out_shape = [st_shape, st_shape]
    if with_output:
        o_shape = jax.ShapeDtypeStruct((bsz, l, HEADS * HEAD_DIM), F32)
        out_specs = [pl.BlockSpec((None, tblk, LANE), lambda b, hh, t: (b, t, hh)),
                     pl.BlockSpec((None, tblk, LANE), lambda b, hh, t: (b, nt - 1 - t, hh))] + out_specs
        out_shape = [o_shape, o_shape] + out_shape
    outs = pl.pallas_call(
        functools.partial(_hg_kernel, nchunk=tblk // CHUNK, with_output=with_output),
        name="hgrn2_scan",
        grid=(bsz, HEADS, nt),
        in_specs=in_specs,
        out_specs=out_specs,
        out_shape=out_shape,
        scratch_shapes=[pltpu.VMEM((HEAD_DIM, HEAD_DIM), F32), pltpu.VMEM((HEAD_DIM, HEAD_DIM), F32)],
        compiler_params=_cparams(("parallel", "parallel", "arbitrary")),
    )(*args)
    if with_output:
        return outs[0], outs[1], (outs[2], outs[3])
    return None, None, (outs[0], outs[1])


def _l2norm(t):
    return t * lax.rsqrt(jnp.sum(t * t, axis=-1, keepdims=True) + 1e-6)


def _softplus(x):
    return jnp.maximum(x, 0.0) + jnp.log(1.0 + jnp.exp(-jnp.abs(x)))


def _dn_prepare(items):
    r = _iota2((CHUNK, CHUNK), 0)
    c = _iota2((CHUNK, CHUNK), 1)
    eye = r == c
    for it in items:
        rev = it["rev"]
        incl = (c >= r) if rev else (c <= r)
        incl_t = (r >= c) if rev else (r <= c)
        g_b = jnp.broadcast_to(it["g"], (CHUNK, CHUNK))
        gc_col = jnp.sum(jnp.where(incl, g_b, 0.0), axis=1, keepdims=True)
        g_col = jnp.sum(jnp.where(eye, g_b, 0.0), axis=1, keepdims=True)
        gc_row = jnp.sum(jnp.where(incl_t, jnp.broadcast_to(g_col, (CHUNK, CHUNK)), 0.0), axis=0, keepdims=True)
        beta_col = jnp.sum(jnp.where(eye, jnp.broadcast_to(it["beta"], (CHUNK, CHUNK)), 0.0), axis=1,
                           keepdims=True)
        it["decay"] = jnp.where(incl, jnp.exp(jnp.where(incl, gc_col - gc_row, 0.0)), 0.0)
        g_last = gc_col[0:1] if rev else gc_col[CHUNK - 1:CHUNK]
        it["e_last"] = jnp.exp(g_last)
        egc = jnp.exp(gc_col)
        kb = it["k"] * beta_col
        it["kb"] = kb
        it["rhs"] = jnp.concatenate([it["v"] * beta_col, kb * egc], axis=1)
        it["kd"] = it["k"] * jnp.exp(g_last - gc_col)
        if it["q"] is not None:
            it["qe"] = it["q"] * egc
    for it in items:
        if it["q"] is not None:
            both = _dot_nt(jnp.concatenate([it["kb"], it["q"]], axis=0), it["k"])
            it["kk"], it["qk"] = both[:CHUNK], both[CHUNK:] * it["decay"]
        else:
            it["kk"] = _dot_nt(it["kb"], it["k"])
    for it in items:
        strict = (c > r) if it["rev"] else (c < r)
        n = jnp.where(strict, it["kk"] * it["decay"], 0.0)
        it["n"] = n
        it["p"] = jnp.where(eye, 1.0, 0.0) - n
    for it in items:
        it["m"] = _dot(it["n"], it["n"])
    for i in range(5):
        for it in items:
            if i < 4:
                pm = _dot(jnp.concatenate([it["p"], it["m"]], axis=0), it["m"])
                it["p"] = it["p"] + pm[:CHUNK]
                it["m"] = pm[CHUNK:]
            else:
                it["p"] = it["p"] + _dot(it["p"], it["m"])
    for it in items:
        it["uw"] = _dot(it["p"], it["rhs"])
    for it in items:
        bm = _dot_tn(it["kd"], it["uw"])
        it["b_c"], it["m_c"] = bm[:, :HEAD_DIM], bm[:, HEAD_DIM:]
        if it["q"] is not None:
            ow = _dot(it["qk"], it["uw"])
            it["o_c"], it["q_eff"] = ow[:, :HEAD_DIM], it["qe"] - ow[:, HEAD_DIM:]


def _dn_kernel(*refs, nchunk, with_output):
    if with_output:
        (alog_ref, dtb_ref, kf_ref, vf_ref, qf_ref, abf_ref, kb_ref, vb_ref, qb_ref, abb_ref,
         s0f_ref, s0b_ref, of_ref, ob_ref, sf_ref, sb_ref, stf, stb) = refs
    else:
        (alog_ref, dtb_ref, kf_ref, vf_ref, abf_ref, kb_ref, vb_ref, abb_ref,
         s0f_ref, s0b_ref, sf_ref, sb_ref, stf, stb) = refs
        qf_ref = qb_ref = of_ref = ob_ref = None
    hh = pl.program_id(1)

    @pl.when(pl.program_id(2) == 0)
    def _():
        stf[...] = s0f_ref[...]
        stb[...] = s0b_ref[...]

    def prep(k_ref, v_ref, q_ref, ab_ref, d):
        k = _l2norm(_silu(k_ref[...]))
        v = _silu(v_ref[...])
        q = None
        if with_output:
            q = _l2norm(_silu(q_ref[...])) * (HEAD_DIM ** -0.5)
        a = ab_ref[pl.ds(d * HEADS + hh, 1), :]
        bt = ab_ref[pl.ds(2 * HEADS + d * HEADS + hh, 1), :]
        g = -jnp.exp(alog_ref[d, hh]) * _softplus(a + dtb_ref[d, hh])
        return k, v, q, g, _sigmoid(bt)

    kf, vf, qf, gf, bf = prep(kf_ref, vf_ref, qf_ref, abf_ref, 0)
    kb, vb, qb, gb, bb = prep(kb_ref, vb_ref, qb_ref, abb_ref, 1)
    def chunk_item(k, v, q, g, beta, ci, rev):
        lo = ci * CHUNK
        return {"k": k[lo:lo + CHUNK], "v": v[lo:lo + CHUNK], "q": q[lo:lo + CHUNK] if with_output else None,
                "g": g[:, lo:lo + CHUNK], "beta": beta[:, lo:lo + CHUNK], "rev": rev, "lo": lo}

    items = []
    for ci in range(nchunk):
        items.append(chunk_item(kf, vf, qf, gf, bf, ci, False))
        items.append(chunk_item(kb, vb, qb, gb, bb, nchunk - 1 - ci, True))
    _dn_prepare(items)
    state = {False: stf[...], True: stb[...]}
    for it in items:
        s = state[it["rev"]]
        if with_output:
            o_ref = ob_ref if it["rev"] else of_ref
            o_ref[pl.ds(it["lo"], CHUNK), :] = it["o_c"] + _dot(it["q_eff"], s)
        state[it["rev"]] = s * it["e_last"] + it["b_c"] - _dot(it["m_c"], s)
    stf[...] = state[False]
    stb[...] = state[True]
    sf_ref[...] = state[False]
    sb_ref[...] = state[True]


def _deltanet_scan(h, ab_t, cols, a_log, dt_bias, s0, *, tblk, with_output):
    bsz, l, _ = h.shape
    nt = l // tblk
    c_k, c_v, c_q = (c // LANE for c in cols)
    rep = HEADS // DN_QK_HEADS

    def tok(cb, shared, rev):
        def imap(b, hh, t):
            return (b, (nt - 1 - t) if rev else t, cb + (hh // rep if shared else hh))
        return pl.BlockSpec((None, tblk, LANE), imap)

    def ab_spec(rev):
        return pl.BlockSpec((None, 4 * HEADS, tblk), lambda b, hh, t: (b, 0, (nt - 1 - t) if rev else t))

    smem = pl.BlockSpec(memory_space=pltpu.SMEM)
    st_spec = pl.BlockSpec((None, None, HEAD_DIM, HEAD_DIM), lambda b, hh, t: (b, hh, 0, 0))
    if with_output:
        in_specs = [smem, smem, tok(c_k, True, False), tok(c_v, False, False), tok(c_q, True, False), ab_spec(False),
                    tok(c_k, True, True), tok(c_v, False, True), tok(c_q, True, True), ab_spec(True),
                    st_spec, st_spec]
        args = [a_log, dt_bias, h, h, h, ab_t, h, h, h, ab_t, s0[0], s0[1]]
    else:
        in_specs = [smem, smem, tok(c_k, True, False), tok(c_v, False, False), ab_spec(False),
                    tok(c_k, True, True), tok(c_v, False, True), ab_spec(True),
                    st_spec, st_spec]
        args = [a_log, dt_bias, h, h, ab_t, h, h, ab_t, s0[0], s0[1]]
    st_shape = jax.ShapeDtypeStruct((bsz, HEADS, HEAD_DIM, HEAD_DIM), F32)
    out_specs = [st_spec, st_spec]
    out_shape = [st_shape, st_shape]
    if with_output:
        o_shape = jax.ShapeDtypeStruct((bsz, l, HEADS * HEAD_DIM), F32)
        out_specs = [pl.BlockSpec((None, tblk, LANE), lambda b, hh, t: (b, t, hh)),
                     pl.BlockSpec((None, tblk, LANE), lambda b, hh, t: (b, nt - 1 - t, hh))] + out_specs
        out_shape = [o_shape, o_shape] + out_shape
    outs = pl.pallas_call(
        functools.partial(_dn_kernel, nchunk=tblk // CHUNK, with_output=with_output),
        name="deltanet_scan",
        grid=(bsz, HEADS, nt),
        in_specs=in_specs,
        out_specs=out_specs,
        out_shape=out_shape,
        scratch_shapes=[pltpu.VMEM((HEAD_DIM, HEAD_DIM), F32), pltpu.VMEM((HEAD_DIM, HEAD_DIM), F32)],
        compiler_params=_cparams(("parallel", "parallel", "arbitrary")),
    )(*args)
    if with_output:
        return outs[0], outs[1], (outs[2], outs[3])
    return None, None, (outs[0], outs[1])


def _combine_kernel(of_ref, ob_ref, z_ref, w_ref, o_ref, *, use_silu):
    o = of_ref[...] + ob_ref[...]
    o = o * lax.rsqrt(jnp.mean(o * o, axis=-1, keepdims=True) + RMS_EPS) * w_ref[...]
    z = z_ref[...]
    o_ref[...] = (o * (_silu(z) if use_silu else _sigmoid(z))).astype(o_ref.dtype)


def _combine(o_f, o_b, h, col_z, norm_w, *, tl, use_silu):
    bsz, l, c = o_f.shape
    cz = col_z // LANE
    spec = pl.BlockSpec((None, tl, LANE), lambda b, i, hh: (b, i, hh))
    return pl.pallas_call(
        functools.partial(_combine_kernel, use_silu=use_silu),
        name="mixer_norm_gate",
        grid=(bsz, l // tl, c // LANE),
        in_specs=[spec, spec,
                  pl.BlockSpec((None, tl, LANE), lambda b, i, hh: (b, i, cz + hh)),
                  pl.BlockSpec((1, LANE), lambda b, i, hh: (0, 0))],
        out_specs=spec,
        out_shape=jax.ShapeDtypeStruct((bsz, l, c), BF16),
        compiler_params=_cparams(("parallel", "parallel", "arbitrary")),
    )(o_f, o_b, h, norm_w)


def _dot_f32(a, b):
    return jnp.dot(a, b, precision=lax.Precision.HIGHEST, preferred_element_type=F32)


def _hy_filt_kernel(w1_ref, b1_ref, w2_ref, b2_ref, w3_ref, o_ref, hid_ref, *, l_total, tl):
    i = pl.program_id(0)
    j = pl.program_id(1)

    @pl.when(j == 0)
    def _():
        row = (_iota2((tl, LANE), 0) + i * tl).astype(F32)
        lane = _iota2((tl, LANE), 1)
        t = row * (1.0 / (l_total - 1))
        ang = row * (2.0 * math.pi / l_total)
        bidx = jnp.where(lane <= HY_BANDS, lane - 1, lane - 1 - HY_BANDS).astype(F32)
        band = 1e-4 + bidx * ((HY_BANDS - 1 - 1e-4) / (HY_BANDS - 1))
        arg = band * ang
        feats = jnp.where(lane == 0, t,
                          jnp.where(lane <= HY_BANDS, jnp.cos(arg),
                                    jnp.where(lane <= 2 * HY_BANDS, -jnp.sin(arg), 0.0)))
        hid = jnp.sin(_dot_f32(feats, w1_ref[...]) + b1_ref[...])
        hid_ref[...] = jnp.sin(_dot_f32(hid, w2_ref[...]) + b2_ref[...])

    hcol = _dot_f32(hid_ref[...], w3_ref[...])
    width = hcol.shape[1]
    ch = _iota2((tl, width), 1).astype(F32)
    lo = math.log(HY_DECAY_TARGET) / HY_SLOW_DECAY
    hi = math.log(HY_DECAY_TARGET) / HY_FAST_DECAY
    delta = jnp.abs(lo + ch * ((hi - lo) / (width - 1)))
    trow = (_iota2((tl, width), 0) + i * tl)
    out = hcol * jnp.exp(-(trow.astype(F32) * (1.0 / (l_total - 1))) * delta)
    drop = jnp.logical_and(trow == 0, jnp.bitwise_and(j, 1) == 1)
    o_ref[...] = jnp.where(drop, 0.0, out)


def _hy_filters(l_total, w1p, b1, w2, b2, w3):
    tl = min(l_total, 512)
    n = w3.shape[1]
    full = lambda a: pl.BlockSpec(a.shape, lambda i, j: (0,) * a.ndim)
    return pl.pallas_call(
        functools.partial(_hy_filt_kernel, l_total=l_total, tl=tl),
        name="hy_filters",
        grid=(l_total // tl, n // BRANCH_WIDTH),
        in_specs=[full(w1p), full(b1), full(w2), full(b2),
                  pl.BlockSpec((HY_HIDDEN, BRANCH_WIDTH), lambda i, j: (0, j))],
        out_specs=pl.BlockSpec((tl, BRANCH_WIDTH), lambda i, j: (i, j)),
        out_shape=jax.ShapeDtypeStruct((l_total, n), F32),
        scratch_shapes=[pltpu.VMEM((tl, HY_HIDDEN), F32)],
        compiler_params=_cparams(("parallel", "arbitrary")),
    )(w1p, b1, w2, b2, w3)


@functools.lru_cache(maxsize=None)
def _fft_tables(n1, n2):
    n = n1 * n2
    n2h = n2 // 2
    k2 = np.arange(n2, dtype=np.int64)[:, None]
    m = np.arange(n2h, dtype=np.int64)[None, :]
    ta = np.zeros((n1, 2 * n2, 2 * n2h))
    for a in range(n1):
        th = 2.0 * np.pi * ((k2 * (a + n1 * m)) % n) / n
        c, s = np.cos(th), np.sin(th)
        ta[a] = np.block([[c, s], [-s, c]])
    tai = np.transpose(ta, (0, 2, 1)) / n
    k1 = np.arange(n1, dtype=np.int64)
    ph = 2.0 * np.pi * ((k1[:, None] * k1[None, :]) % n1) / n1
    c, s = np.cos(ph), np.sin(ph)
    fb = np.block([[c, s], [-s, c]])
    fs = fb[:n1] + fb[n1:]
    fsr = np.stack([fs[(-k1 - e) % n1] for e in (0, 1)])
    return tuple(np.asarray(t, np.float32) for t in (ta, tai, fb, fb.T, fs, fsr))


def _fft_split(l_total):
    n = 2 * l_total
    n1 = min(128, l_total // 8)
    return n1, n // n1


def _rows2d(ref):
    rows = 1
    for s in ref.shape[:-1]:
        rows *= s
    return ref.reshape(rows, ref.shape[-1])


def _lane_slabs(shape, imap, ns):
    def slab(s):
        def index_map(*a):
            idx = imap(*a)
            return idx[:-1] + (idx[-1] * ns + s,)
        return pl.BlockSpec(shape + (LANE,), index_map)
    return [slab(s) for s in range(ns)]


def _strided_rows(ref2, group, n, j):
    return ref2[pl.ds(group * 8 + j, n, stride=8), :]


def _hy_stage_a_kernel(gr_ref, cr_ref, gi_ref, ci_ref, *refs, ns, n2, n2h, slab_major_in):
    del gr_ref, cr_ref, gi_ref, ci_ref
    if slab_major_in:
        xs = [[(_rows2d(r), s * n2h) for s in range(ns)] for r in refs[:2]]
        refs = refs[2:]
    else:
        xs = [[(_rows2d(r), 0) for r in refs[q * ns:(q + 1) * ns]] for q in range(2)]
        refs = refs[2 * ns:]
    t_ref, o_ref = refs
    o2 = _rows2d(o_ref)
    for j in range(8):
        x = jnp.concatenate([jnp.concatenate([_strided_rows(r2, grp, n2h, j) for r2, grp in part], axis=1)
                             for part in xs], axis=0).astype(BF16)
        y = jnp.dot(t_ref[j], x, preferred_element_type=F32)
        for p in range(2):
            for s in range(ns):
                o2[pl.ds(((p * ns + s) * n2) * 8 + j, n2, stride=8), :] = y[p * n2:(p + 1) * n2, s * LANE:(s + 1) * LANE]


def _hy_stage_a(x, pairs, tab, *, ct, n2, slab_major_in):
    n2h = n2 // 2
    n1 = x.shape[-2] if slab_major_in else x.shape[2]
    npair = len(pairs)
    nct = BRANCH_WIDTH // ct
    ns = ct // LANE
    gr = jnp.asarray([p[0][0] for p in pairs], jnp.int32)
    cr = jnp.asarray([p[0][1] for p in pairs], jnp.int32)
    gi = jnp.asarray([p[1][0] for p in pairs], jnp.int32)
    ci = jnp.asarray([p[1][1] for p in pairs], jnp.int32)
    if slab_major_in:
        in_specs = [pl.BlockSpec((None, ns, n2h, 8, LANE), lambda p, c, g, *_: (0, c, 0, g, 0)),
                    pl.BlockSpec((None, ns, n2h, 8, LANE), lambda p, c, g, *_: (1, c, 0, g, 0))]
        args = [x, x]
    else:
        in_specs = (_lane_slabs((None, n2h, 8), lambda p, c, g, gr, cr, gi, ci: (gr[p], 0, g, cr[p] * nct + c), ns)
                    + _lane_slabs((None, n2h, 8), lambda p, c, g, gr, cr, gi, ci: (gi[p], 0, g, ci[p] * nct + c), ns))
        args = [x] * (2 * ns)
    grid_spec = pltpu.PrefetchScalarGridSpec(
        num_scalar_prefetch=4,
        grid=(npair, nct, n1 // 8),
        in_specs=in_specs + [pl.BlockSpec((8, 2 * n2, 2 * n2h), lambda p, c, g, *_: (g, 0, 0))],
        out_specs=pl.BlockSpec((None, 2, ns, n2, 8, LANE), lambda p, c, g, *_: (p, 0, c, 0, g, 0)),
    )
    return pl.pallas_call(
        functools.partial(_hy_stage_a_kernel, ns=ns, n2=n2, n2h=n2h, slab_major_in=slab_major_in),
        name="hy_stage_a",
        grid_spec=grid_spec,
        out_shape=jax.ShapeDtypeStruct((npair, 2, BRANCH_WIDTH // LANE, n2, n1, LANE), F32),
        compiler_params=_cparams(("parallel", "parallel", "arbitrary")),
    )(gr, cr, gi, ci, *args, tab)


def _hy_spec_kernel(c_ref, cm_ref, c0_ref, fs_ref, fsr_ref, o_ref, *, kg):
    g = pl.program_id(2)
    for k in range(kg):
        s = jnp.dot(fs_ref[...], _complex_slab(c_ref, k), preferred_element_type=F32)
        if k:
            mirror, fsr = _complex_slab(cm_ref, kg - k), fsr_ref[1]
        else:
            mirror, fsr = _complex_slab(c0_ref, 0), jnp.where(g == 0, fsr_ref[0], fsr_ref[1])
        t = jnp.dot(fsr, mirror, preferred_element_type=F32)
        o_ref[0, k] = 0.5 * (s + t)
        o_ref[1, k] = 0.5 * (s - t)


def _complex_slab(ref, k):
    ns = ref.shape[1]
    return jnp.concatenate([jnp.concatenate([ref[p, s, k] for s in range(ns)], axis=1) for p in range(2)],
                           axis=0).astype(BF16)


def _hy_spectrum(cs, fs, fsr, *, ct, kg):
    no, _, nslab, n2, n1, _ = cs.shape
    ns = ct // LANE
    ng = n2 // kg
    return pl.pallas_call(
        functools.partial(_hy_spec_kernel, kg=kg),
        name="hy_spectrum",
        grid=(no, nslab // ns, ng),
        in_specs=[pl.BlockSpec((None, 2, ns, kg, n1, LANE), lambda o, j, g: (o, 0, j, g, 0, 0)),
                  pl.BlockSpec((None, 2, ns, kg, n1, LANE), lambda o, j, g: (o, 0, j, ng - 1 - g, 0, 0)),
                  pl.BlockSpec((None, 2, ns, 1, n1, LANE), lambda o, j, g: (o, 0, j, (kg * (ng - g)) % n2, 0, 0)),
                  pl.BlockSpec(fs.shape, lambda o, j, g: (0, 0)),
                  pl.BlockSpec(fsr.shape, lambda o, j, g: (0, 0, 0))],
        out_specs=pl.BlockSpec((None, 2, kg, n1, ct), lambda o, j, g: (o, 0, g, 0, j)),
        out_shape=jax.ShapeDtypeStruct((no, 2, n2, n1, nslab * LANE), F32),
        compiler_params=_cparams(("parallel", "parallel", "arbitrary")),
    )(cs, cs, cs, fs, fsr)


def _hy_stage_b_kernel(c_ref, k_ref, fb_ref, fbi_ref, o_ref, *, kg, n1):
    ns = c_ref.shape[1]
    for k in range(kg):
        z = jnp.dot(fb_ref[...], _complex_slab(c_ref, k), preferred_element_type=F32)
        zr, zi = z[:n1], z[n1:]
        kr, ki = k_ref[0, k], k_ref[1, k]
        y = jnp.concatenate([zr * kr - zi * ki, zr * ki + zi * kr], axis=0).astype(BF16)
        d = jnp.dot(fbi_ref[...], y, preferred_element_type=F32)
        for p in range(2):
            for s in range(ns):
                o_ref[p, s, k] = d[p * n1:(p + 1) * n1, s * LANE:(s + 1) * LANE]


def _hy_stage_b(cs, spec, order, fb, fbi, *, ct, kg):
    _, _, nslab, n2, n1, _ = cs.shape
    ns = ct // LANE
    c = nslab * LANE
    return pl.pallas_call(
        functools.partial(_hy_stage_b_kernel, kg=kg, n1=n1),
        name="hy_stage_b",
        grid=(c // ct, n2 // kg),
        in_specs=[pl.BlockSpec((None, 2, ns, kg, n1, LANE), lambda j, g: (0, 0, j, g, 0, 0)),
                  pl.BlockSpec((None, 2, kg, n1, ct), lambda j, g: (order, 0, g, 0, j)),
                  pl.BlockSpec(fb.shape, lambda j, g: (0, 0)),
                  pl.BlockSpec(fbi.shape, lambda j, g: (0, 0))],
        out_specs=pl.BlockSpec((2, ns, kg, n1, LANE), lambda j, g: (0, j, g, 0, 0)),
        out_shape=jax.ShapeDtypeStruct((2, nslab, n2, n1, LANE), F32),
        compiler_params=_cparams(("parallel", "arbitrary")),
    )(cs, spec, fb, fbi)


def _hy_stage_c_kernel(*refs, ns, n2, n2h, slab_major_u):
    d2 = _rows2d(refs[0])
    t_ref = refs[1]
    if slab_major_u:
        u2 = _rows2d(refs[2])
        u_parts = [[(u2, (b * ns + s) * n2h) for s in range(ns)] for b in range(2)]
        refs = refs[3:]
    else:
        u_parts = [[(_rows2d(r), b * n2h) for r in refs[2:2 + ns]] for b in range(2)]
        refs = refs[2 + ns:]
    g_parts = [[(_rows2d(r), b * n2h) for r in refs[:ns]] for b in range(2)]
    skip_ref, o_ref = refs[ns:]
    o2 = _rows2d(o_ref)
    skip = skip_ref[...]

    def rows(parts, j):
        return jnp.concatenate([jnp.concatenate([_strided_rows(r2, grp, n2h, j) for r2, grp in part], axis=1)
                                for part in parts], axis=0)

    for j in range(8):
        d = jnp.concatenate([jnp.concatenate([_strided_rows(d2, (p * ns + s) * n2, n2, j) for s in range(ns)], axis=1)
                             for p in range(2)], axis=0).astype(BF16)
        y = jnp.dot(t_ref[j], d, preferred_element_type=F32)
        out = rows(g_parts, j) * (y + skip * rows(u_parts, j))
        for b in range(2):
            for s in range(ns):
                o2[pl.ds(((b * ns + s) * n2h) * 8 + j, n2h, stride=8), :] = (
                    out[b * n2h:(b + 1) * n2h, s * LANE:(s + 1) * LANE])


def _hy_stage_c(d, tai, u, ucol, z4, gcol, skip, *, ct, slab_major_u):
    _, nslab, n2, n1, _ = d.shape
    n2h = n2 // 2
    ns = ct // LANE
    uc, gc = ucol // ct, gcol // ct
    if slab_major_u:
        u_specs = [pl.BlockSpec((2, ns, n2h, 8, LANE), lambda j, g: (0, j, 0, g, 0))]
        u_args = [u]
    else:
        u_specs = _lane_slabs((2, n2h, 8), lambda j, g: (0, 0, g, uc + j), ns)
        u_args = [u] * ns
    return pl.pallas_call(
        functools.partial(_hy_stage_c_kernel, ns=ns, n2=n2, n2h=n2h, slab_major_u=slab_major_u),
        name="hy_stage_c",
        grid=(nslab // ns, n1 // 8),
        in_specs=([pl.BlockSpec((2, ns, n2, 8, LANE), lambda j, g: (0, j, 0, g, 0)),
                   pl.BlockSpec((8, 2 * n2h, 2 * n2), lambda j, g: (g, 0, 0))]
                  + u_specs + _lane_slabs((2, n2h, 8), lambda j, g: (0, 0, g, gc + j), ns)
                  + [pl.BlockSpec((1, ct), lambda j, g: (0, j))]),
        out_specs=pl.BlockSpec((2, ns, n2h, 8, LANE), lambda j, g: (0, j, 0, g, 0)),
        out_shape=jax.ShapeDtypeStruct((2, nslab, n2h, n1, LANE), F32),
        compiler_params=_cparams(("parallel", "arbitrary")),
    )(d, tai, *u_args, *([z4] * ns), skip)


def _hyena(h, filt_params, skip):
    bsz, l, _ = h.shape
    assert bsz == 2
    n1, n2 = _fft_split(l)
    n2h = n2 // 2
    ta, tai, fb, fbi, fs, fsr = (jnp.asarray(t).astype(BF16) for t in _fft_tables(n1, n2))
    ct = 512
    kg = 8

    hfilt = _hy_filters(l, *filt_params)
    cs_k = _hy_stage_a(hfilt.reshape(1, n2h, n1, -1), [((0, 2 * o), (0, 2 * o + 1)) for o in range(HY_ORDER)],
                       ta, ct=ct, n2=n2, slab_major_in=False)
    spec = _hy_spectrum(cs_k, fs, fsr, ct=ct, kg=kg)

    z4 = h.reshape(bsz, n2h, n1, -1)
    u, ucol = z4, COL_HY
    for o in range(HY_ORDER):
        first = o == 0
        cs = _hy_stage_a(u, [((0, ucol // BRANCH_WIDTH), (1, ucol // BRANCH_WIDTH))], ta, ct=ct, n2=n2,
                         slab_major_in=not first)
        d = _hy_stage_b(cs, spec, o, fb, fbi, ct=ct, kg=kg)
        u = _hy_stage_c(d, tai, u, ucol, z4, COL_HY + (o + 1) * BRANCH_WIDTH, skip[o:o + 1], ct=ct,
                        slab_major_u=not first)
        ucol = 0
    return u.reshape(bsz, BRANCH_WIDTH // LANE, l, LANE)


def _conv_tables(hy_conv_w, hy_conv_b, dn_conv_q, dn_conv_k, dn_conv_v):
    depth = hy_conv_w.shape[0]
    ident = lambda n: jnp.broadcast_to(jnp.asarray([0.0, 1.0, 0.0], F32)[None, :, None], (depth, 3, n))
    taps = jnp.concatenate([ident(COL_HY), hy_conv_w, ident(COL_DK - COL_FF), dn_conv_k, dn_conv_v,
                            ident(COL_DQ - COL_AB), dn_conv_q, ident(COL_PAD - COL_DQ - dn_conv_q.shape[2])], axis=2)
    bias = jnp.concatenate([jnp.zeros((depth, 1, COL_HY), F32), hy_conv_b[:, None, :],
                            jnp.zeros((depth, 1, COL_PAD - COL_FF), F32)], axis=2)
    return taps, bias


def _token_mixers(u_src, sh, sc, w_t, conv_t, layer, lp, states, *, is_grid, with_output, tm):
    bsz, l, _ = u_src.shape
    tblk = min(l, 1024)
    taps, bias = (a[layer:layer + 1] for a in conv_t)
    if with_output:
        h = _in_proj(u_src, sh, sc, w_t, layer, range(COL_PAD // IN_TN), taps, bias, tm=tm, is_grid=is_grid)
        base = 0
    else:
        lo, hi = STATE_BLOCKS[0] * IN_TN, (STATE_BLOCKS[-1] + 1) * IN_TN
        h = _in_proj(u_src, sh, sc, w_t, layer, STATE_BLOCKS, taps[:, :, lo:hi], bias[:, :, lo:hi],
                     tm=tm, is_grid=is_grid)
        base = COL_FF
    col = lambda c: c - base
    ab_t = jnp.swapaxes(h[:, :, col(COL_AB):col(COL_AB) + 4 * HEADS], 1, 2)
    hg_f, hg_b, hg_states = _hgrn2_scan(h, (col(COL_FF), col(COL_FB), col(COL_HI), col(COL_HQ)),
                                        lp["lb_f"], lp["lb_b"], states[0], tblk=tblk, with_output=with_output)
    dn_f, dn_b, dn_states = _deltanet_scan(h, ab_t, (col(COL_DK), col(COL_DV), col(COL_DQ)),
                                           lp["dn_a_log"], lp["dn_dt_bias"], states[1],
                                           tblk=tblk, with_output=with_output)
    new_states = (hg_states, dn_states)
    if not with_output:
        return None, new_states
    tl = min(l, 2048)
    hg_out = _combine(hg_f, hg_b, h, COL_HGATE, lp["hg_norm_w"], tl=tl, use_silu=False)
    dn_out = _combine(dn_f, dn_b, h, COL_DZ, lp["dn_norm_w"], tl=tl, use_silu=True)
    hy_out = _hyena(h, lp["hy_filt"], lp["hy_skip"])
    ysum = _branch_merge([hy_out, hg_out, dn_out], lp["w_branch"], layer, h, tm=min(l, 1024), tn=512)
    return ysum, new_states


def kernel(x, c, ctx, c_ctx, w_ada, b_ada, w_in, hy_conv_w, hy_conv_b, hy_filt_w1, hy_filt_b1, hy_filt_w2, hy_filt_b2, hy_filt_w3, hy_skip, hg_lb_logits, hg_norm_w, dn_conv_q, dn_conv_k, dn_conv_v, dn_a_log, dn_dt_bias, dn_norm_w, w_branch, w_out, ln1_g, ln1_b, w_ff1, w_ff2, ln2_g, ln2_b):
    depth = w_in.shape[0]
    bsz, _, d = x.shape
    alpha = (2 * depth) ** 0.25
    p = jax.nn.softmax(hg_lb_logits.astype(F32), axis=1)
    lower = jnp.cumsum(p, axis=1) - p[:, :1]
    cs = jnp.concatenate([c, c_ctx[None], jnp.zeros((8 - bsz - 1, d), F32)], axis=0)
    w_r = jnp.swapaxes(w_in, 1, 2)
    w_b, w_o, w_1, w_2 = w_branch.astype(BF16), w_out.astype(BF16), w_ff1, w_ff2.astype(BF16)
    b_ada3 = b_ada[:, None, :]
    conv_t = _conv_tables(hy_conv_w, hy_conv_b, dn_conv_q, dn_conv_k, dn_conv_v)
    h_ctx = ctx
    for l in range(depth):
        last = l == depth - 1
        row = lambda v: v.reshape(1, -1)
        lp = {
            "lb_f": row(lower[0, l]), "lb_b": row(lower[1, l]),
            "hg_norm_w": row(hg_norm_w[l]), "dn_norm_w": row(dn_norm_w[l]),
            "dn_a_log": dn_a_log[l], "dn_dt_bias": dn_dt_bias[l], "hy_skip": hy_skip[l],
            "hy_filt": (jnp.pad(hy_filt_w1[l], ((0, LANE - HY_EMB), (0, 0))), row(hy_filt_b1[l]),
                        hy_filt_w2[l], row(hy_filt_b2[l]), hy_filt_w3[l]),
            "w_branch": w_b,
        }
        g1n, b1n, g2n, b2n = row(ln1_g[l]), row(ln1_b[l]), row(ln2_g[l]), row(ln2_b[l])
        mod = _ada_mod(cs, w_ada, b_ada3, l)
        m_lat = [mod[:bsz, i * d:(i + 1) * d][:, None, :] for i in range(6)]
        m_ctx = [jnp.broadcast_to(mod[bsz, i * d:(i + 1) * d][None, None, :], (bsz, 1, d)) for i in range(6)]
        zero = jnp.zeros((bsz, HEADS, HEAD_DIM, HEAD_DIM), F32)
        init = ((zero, zero), (zero, zero))

        lc = h_ctx.shape[1]
        y_ctx, ctx_states = _token_mixers(h_ctx, m_ctx[0], m_ctx[1], w_r, conv_t, l, lp, init,
                                          is_grid=False, with_output=not last, tm=lc)
        if not last:
            h_ctx = _mm_ln(y_ctx, w_o, l, h_ctx, m_ctx[2], g1n, b1n, tm=lc, tk=d, alpha=alpha)
            mid = _mlp_up(h_ctx, m_ctx[3], m_ctx[4], w_1, l, tm=lc, tn=1024)
            h_ctx = _mm_ln(mid, w_2, l, h_ctx, m_ctx[5], g2n, b2n, tm=lc, tk=1024, alpha=alpha)

        lx = x.shape[1]
        y, _ = _token_mixers(x, m_lat[0], m_lat[1], w_r, conv_t, l, lp, ctx_states,
                             is_grid=True, with_output=True, tm=min(lx, 1024))
        x = _mm_ln(y, w_o, l, x, m_lat[2], g1n, b1n, tm=min(lx, 512), tk=d, alpha=alpha)
        mid = _mlp_up(x, m_lat[3], m_lat[4], w_1, l, tm=min(lx, 1024), tn=1024)
        x = _mm_ln(mid, w_2, l, x, m_lat[5], g2n, b2n, tm=min(lx, 1024), tk=512, alpha=alpha)
    return x
```

```python
import functools
import math

import numpy as np
import jax
import jax.numpy as jnp
from jax import lax
from jax.experimental import pallas as pl
from jax.experimental.pallas import tpu as pltpu

F32 = jnp.float32
BF16 = jnp.bfloat16

D_MODEL = 2048
GRID_W = 64
BRANCH_WIDTH = 1024
N_BRANCH = 3
HY_ORDER = 2
HY_EMB = 33
HY_BANDS = (HY_EMB - 1) // 2
HY_HIDDEN = 64
HY_DECAY_TARGET = 1e-2
HY_FAST_DECAY = 0.3
HY_SLOW_DECAY = 1.5
HEADS = 8
HEAD_DIM = 128
DN_QK_HEADS = 4
HG_F_MIN = 1e-30
CHUNK = 64
LN_EPS = 1e-5
RMS_EPS = 1e-6
LANE = 128

IN_TN = 1024
COL_GATE = 0
COL_HY = 6144
COL_FF = 9216
COL_FB = 10240
COL_HI = 11264
COL_DK = 12288
COL_DV = 12800
COL_AB = 13824
COL_HQ = 14336
COL_HGATE = 15360
COL_DQ = 16384
COL_DZ = 17408
COL_PAD = 18432
SOURCE_ROWS = np.asarray([11296 + IN_TN * k for k in range(6)] + [8224 + IN_TN * k for k in range(3)]
                         + [0, 1024, 2048, 3072, 4096, 4640, 5664, 6688, 7200], np.int32)
CONV_WINDOWS = np.asarray([COL_HY <= IN_TN * j < COL_FF or COL_DK <= IN_TN * j < COL_HQ or IN_TN * j == COL_DQ
                           for j in range(COL_PAD // IN_TN)], np.int32)
STATE_BLOCKS = tuple(range(COL_FF // IN_TN, COL_HQ // IN_TN))

VMEM_LIMIT = 56 * 1024 * 1024


def _cparams(sem):
    return pltpu.CompilerParams(dimension_semantics=sem, vmem_limit_bytes=VMEM_LIMIT)


def _dot(a, b):
    return jnp.dot(a.astype(BF16), b.astype(BF16), preferred_element_type=F32)


def _dot_nt(a, b):
    return lax.dot_general(a.astype(BF16), b.astype(BF16), (((1,), (1,)), ((), ())),
                           preferred_element_type=F32)


def _dot_tn(a, b):
    return lax.dot_general(a.astype(BF16), b.astype(BF16), (((0,), (0,)), ((), ())),
                           preferred_element_type=F32)


def _sigmoid(x):
    return 1.0 / (1.0 + jnp.exp(-x))


def _silu(x):
    return x * _sigmoid(x)


def _ada_kernel(c_ref, w_ref, b_ref, o_ref):
    a = _silu(c_ref[...])
    o_ref[...] = _dot(a, w_ref[...]) + b_ref[...]


def _ada_mod(cs, w, b, layer):
    _, d, n = w.shape
    tn = 1024
    return pl.pallas_call(
        _ada_kernel,
        grid=(n // tn,),
        in_specs=[pl.BlockSpec((8, d), lambda j: (0, 0)),
                  pl.BlockSpec((None, d, tn), lambda j: (layer, 0, j)),
                  pl.BlockSpec((None, 1, tn), lambda j: (layer, 0, j))],
        out_specs=pl.BlockSpec((8, tn), lambda j: (0, j)),
        out_shape=jax.ShapeDtypeStruct((8, n), F32),
        compiler_params=_cparams(("arbitrary",)),
        name="ada_mod",
    )(cs, w, b)


def _conv3(x, w, is_grid):
    t = x.shape[0]
    row = _iota2((t, 1), 0)
    if is_grid:
        pos = jnp.bitwise_and(row, GRID_W - 1)
        has_l, has_r = pos != 0, pos != GRID_W - 1
    else:
        has_l, has_r = row != 0, row != t - 1
    xl = jnp.where(has_l, pltpu.roll(x, 1, 0), 0.0)
    xr = jnp.where(has_r, pltpu.roll(x, t - 1, 0), 0.0)
    return xl * w[0:1] + x * w[1:2] + xr * w[2:3]


def _mlp_up_kernel(x_ref, sh_ref, sc_ref, w_ref, o_ref, a_scr):
    @pl.when(pl.program_id(2) == 0)
    def _():
        a_scr[...] = (x_ref[...] * (1.0 + sc_ref[...]) + sh_ref[...]).astype(BF16)

    acc = jnp.dot(a_scr[...], w_ref[...].astype(BF16), preferred_element_type=F32)
    o_ref[...] = jnp.square(jnp.maximum(acc, 0.0)).astype(o_ref.dtype)


def _mlp_up(x, sh, sc, w, layer, *, tm, tn):
    bsz, l, d = x.shape
    n = w.shape[2]
    return pl.pallas_call(
        _mlp_up_kernel,
        grid=(bsz, l // tm, n // tn),
        in_specs=[pl.BlockSpec((None, tm, d), lambda b, i, j: (b, i, 0)),
                  pl.BlockSpec((None, 1, d), lambda b, i, j: (b, 0, 0)),
                  pl.BlockSpec((None, 1, d), lambda b, i, j: (b, 0, 0)),
                  pl.BlockSpec((None, d, tn), lambda b, i, j: (layer, 0, j))],
        out_specs=pl.BlockSpec((None, tm, tn), lambda b, i, j: (b, i, j)),
        out_shape=jax.ShapeDtypeStruct((bsz, l, n), BF16),
        scratch_shapes=[pltpu.VMEM((tm, d), BF16)],
        compiler_params=_cparams(("parallel", "parallel", "arbitrary")),
        name="mm_relu2",
    )(x, sh, sc, w)


def _in_proj_kernel(off_ref, conv_ref, x_ref, sh_ref, sc_ref, w_ref, cw_ref, cb_ref, o_ref, a_scr, *, conv_grid):
    del off_ref
    has_conv = conv_ref[pl.program_id(2)] != 0

    @pl.when(pl.program_id(2) == 0)
    def _():
        a_scr[...] = (x_ref[...] * (1.0 + sc_ref[...]) + sh_ref[...]).astype(BF16)

    o_ref[...] = lax.dot_general(a_scr[...], w_ref[0].astype(BF16), (((1,), (1,)), ((), ())),
                                 preferred_element_type=F32)

    @pl.when(has_conv)
    def _():
        o_ref[...] = _conv3(o_ref[...], cw_ref[...], conv_grid) + cb_ref[...]


def _in_proj(x, sh, sc, w_t, layer, windows, taps, bias, *, tm, is_grid):
    bsz, l, d = x.shape
    nblk = len(windows)
    assert tm == l or (is_grid and tm % GRID_W == 0)
    row_off = jnp.asarray(SOURCE_ROWS[list(windows)])
    has_conv = jnp.asarray(CONV_WINDOWS[list(windows)])
    grid_spec = pltpu.PrefetchScalarGridSpec(
        num_scalar_prefetch=2,
        grid=(bsz, l // tm, nblk),
        in_specs=[pl.BlockSpec((None, tm, d), lambda b, i, j, off, cv: (b, i, 0)),
                  pl.BlockSpec((None, 1, d), lambda b, i, j, off, cv: (b, 0, 0)),
                  pl.BlockSpec((None, 1, d), lambda b, i, j, off, cv: (b, 0, 0)),
                  pl.BlockSpec((pl.Element(1), pl.Element(IN_TN), pl.Element(d)),
                               lambda b, i, j, off, cv: (layer, pl.multiple_of(off[j], 4 * HEADS), 0)),
                  pl.BlockSpec((None, 3, IN_TN), lambda b, i, j, off, cv: (0, 0, j)),
                  pl.BlockSpec((None, 1, IN_TN), lambda b, i, j, off, cv: (0, 0, j))],
        out_specs=pl.BlockSpec((None, tm, IN_TN), lambda b, i, j, off, cv: (b, i, j)),
        scratch_shapes=[pltpu.VMEM((tm, d), BF16)],
    )
    return pl.pallas_call(
        functools.partial(_in_proj_kernel, conv_grid=is_grid),
        name="mm_in_proj",
        grid_spec=grid_spec,
        out_shape=jax.ShapeDtypeStruct((bsz, l, nblk * IN_TN), F32),
        compiler_params=_cparams(("parallel", "parallel", "arbitrary")),
    )(row_off, has_conv, x, sh, sc, w_t, taps, bias)


def _mm_ln_kernel(a_ref, w_ref, x_ref, gate_ref, g_ref, b_ref, o_ref, acc_ref, *, nk, alpha):
    k = pl.program_id(2)

    @pl.when(k == 0)
    def _():
        acc_ref[...] = jnp.zeros_like(acc_ref)

    acc_ref[...] += jnp.dot(a_ref[...], w_ref[...].astype(BF16), preferred_element_type=F32)

    @pl.when(k == nk - 1)
    def _():
        y = alpha * x_ref[...] + gate_ref[...] * acc_ref[...]
        mu = jnp.mean(y, axis=-1, keepdims=True)
        yc = y - mu
        var = jnp.mean(yc * yc, axis=-1, keepdims=True)
        o_ref[...] = yc * lax.rsqrt(var + LN_EPS) * g_ref[...] + b_ref[...]


def _mm_ln(a, w, layer, x, gate, ln_g, ln_b, *, tm, tk, alpha):
    bsz, l, kdim = a.shape
    d = w.shape[2]
    nk = kdim // tk
    return pl.pallas_call(
        functools.partial(_mm_ln_kernel, nk=nk, alpha=alpha),
        grid=(bsz, l // tm, nk),
        in_specs=[pl.BlockSpec((None, tm, tk), lambda b, i, k: (b, i, k)),
                  pl.BlockSpec((None, tk, d), lambda b, i, k: (layer, k, 0)),
                  pl.BlockSpec((None, tm, d), lambda b, i, k: (b, i, 0)),
                  pl.BlockSpec((None, 1, d), lambda b, i, k: (b, 0, 0)),
                  pl.BlockSpec((1, d), lambda b, i, k: (0, 0)),
                  pl.BlockSpec((1, d), lambda b, i, k: (0, 0))],
        out_specs=pl.BlockSpec((None, tm, d), lambda b, i, k: (b, i, 0)),
        out_shape=jax.ShapeDtypeStruct((bsz, l, d), F32),
        scratch_shapes=[pltpu.VMEM((tm, d), F32)],
        compiler_params=_cparams(("parallel", "parallel", "arbitrary")),
        name="mm_resid_ln",
    )(a, w, x, gate, ln_g, ln_b)


def _branch_kernel(b0_ref, b1_ref, b2_ref, w_ref, g0_ref, g1_ref, g2_ref, o_ref):
    b0 = jnp.concatenate([b0_ref[s] for s in range(b0_ref.shape[0])], axis=1).astype(BF16)
    acc = _sigmoid(g0_ref[...]) * jnp.dot(b0, w_ref[0].astype(BF16), preferred_element_type=F32)
    acc += _sigmoid(g1_ref[...]) * jnp.dot(b1_ref[...], w_ref[1].astype(BF16), preferred_element_type=F32)
    acc += _sigmoid(g2_ref[...]) * jnp.dot(b2_ref[...], w_ref[2].astype(BF16), preferred_element_type=F32)
    o_ref[...] = acc.astype(o_ref.dtype)


def _branch_merge(branches, w, layer, h, *, tm, tn):
    bsz, l, c = branches[1].shape
    d = w.shape[3]
    nj = d // tn
    br_spec = pl.BlockSpec((None, tm, c), lambda b, i, j: (b, i, 0))
    hy_spec = pl.BlockSpec((None, c // LANE, tm, LANE), lambda b, i, j: (b, 0, i, 0))

    def gate_spec(g):
        return pl.BlockSpec((None, tm, tn), lambda b, i, j: (b, i, COL_GATE // tn + g * nj + j))

    return pl.pallas_call(
        _branch_kernel,
        grid=(bsz, l // tm, nj),
        in_specs=[hy_spec, br_spec, br_spec,
                  pl.BlockSpec((None, N_BRANCH, c, tn), lambda b, i, j: (layer, 0, 0, j)),
                  gate_spec(0), gate_spec(1), gate_spec(2)],
        out_specs=pl.BlockSpec((None, tm, tn), lambda b, i, j: (b, i, j)),
        out_shape=jax.ShapeDtypeStruct((bsz, l, d), BF16),
        compiler_params=_cparams(("parallel", "parallel", "arbitrary")),
        name="branch_merge",
    )(*branches, w, h, h, h)


def _iota2(shape, axis):
    return lax.broadcasted_iota(jnp.int32, shape, axis)


def _split3(g):
    g1 = g.astype(BF16)
    r = g - g1.astype(F32)
    g2 = r.astype(BF16)
    g3 = (r - g2.astype(F32)).astype(BF16)
    return g1, g2, g3


def _waves(items, nwaves):
    n = min(nwaves, max(1, len(items) // 8))
    per = len(items) // n
    return [items[w * per:(w + 1) * per] for w in range(n)]


def _cumsum_rows(g, rev):
    row = _iota2((CHUNK, 3 * CHUNK), 0)
    col = jnp.bitwise_and(_iota2((CHUNK, 3 * CHUNK), 1), CHUNK - 1)
    tri = jnp.where((col >= row) if rev else (col <= row), 1.0, 0.0).astype(BF16)
    g3 = jnp.concatenate(_split3(g), axis=0)
    return jnp.dot(tri, g3, preferred_element_type=F32)


def _hg_prepare(items):
    for it in items:
        lb = it["lb"]
        sg = _sigmoid(it["fz"])
        it["g"] = jnp.log(jnp.maximum(lb + (1.0 - lb) * sg, HG_F_MIN))
        it["kk"] = (1.0 - lb) * (1.0 - sg)
    for it in items:
        it["bc"] = _cumsum_rows(it["g"], it["rev"])
    half = CHUNK // 2
    for it in items:
        bc = it["bc"]
        if it["rev"]:
            mid, last = bc[half:half + 1], bc[0:1]
        else:
            mid, last = bc[half - 1:half], bc[CHUNK - 1:CHUNK]
        it["e_last"] = jnp.exp(last)
        kd = it["kk"] * jnp.exp(mid - bc)
        it["ks"] = kd * jnp.exp(last - mid)
        if it["q"] is not None:
            qd = _silu(it["q"]) * jnp.exp(bc - mid)
            it["qd"], it["kd"] = qd, kd
            it["qs"] = qd * jnp.exp(mid)
    r = _iota2((CHUNK, CHUNK), 0)
    c = _iota2((CHUNK, CHUNK), 1)
    for it in items:
        it["kv"] = _dot_tn(it["v"], it["ks"])
        if it["q"] is not None:
            att = _dot_nt(it["qd"], it["kd"])
            it["att"] = jnp.where((c >= r) if it["rev"] else (c <= r), att, 0.0)
    for it in items:
        if it["q"] is not None:
            it["o_c"] = _dot(it["att"], it["v"])


def _hg_kernel(*refs, nchunk, with_output):
    if with_output:
        (ff_ref, vf_ref, qf_ref, fb_ref, vb_ref, qb_ref, lbf_ref, lbb_ref, s0f_ref, s0b_ref,
         of_ref, ob_ref, sf_ref, sb_ref, stf, stb) = refs
    else:
        (ff_ref, vf_ref, fb_ref, vb_ref, lbf_ref, lbb_ref, s0f_ref, s0b_ref,
         sf_ref, sb_ref, stf, stb) = refs
        qf_ref = qb_ref = of_ref = ob_ref = None

    @pl.when(pl.program_id(2) == 0)
    def _():
        stf[...] = s0f_ref[...]
        stb[...] = s0b_ref[...]

    def chunk_item(f_ref, v_ref, q_ref, lb_ref, ci, rev):
        sl = pl.ds(ci * CHUNK, CHUNK)
        return {"fz": f_ref[sl, :], "v": v_ref[sl, :], "q": q_ref[sl, :] if with_output else None,
                "lb": lb_ref[...], "rev": rev, "sl": sl}

    items = []
    for ci in range(nchunk):
        items.append(chunk_item(ff_ref, vf_ref, qf_ref, lbf_ref, ci, False))
        items.append(chunk_item(fb_ref, vb_ref, qb_ref, lbb_ref, nchunk - 1 - ci, True))
    state = {False: stf[...], True: stb[...]}
    for wave in _waves(items, 8):
        _hg_prepare(wave)
        for it in wave:
            st = state[it["rev"]]
            if with_output:
                o_ref = ob_ref if it["rev"] else of_ref
                o_ref[it["sl"], :] = it["o_c"] + _dot_nt(it["qs"], st)
            state[it["rev"]] = st * it["e_last"] + it["kv"]
    stf[...] = state[False]
    stb[...] = state[True]
    sf_ref[...] = state[False]
    sb_ref[...] = state[True]


def _hgrn2_scan(h, cols, lb_f, lb_b, s0, *, tblk, with_output):
    bsz, l, _ = h.shape
    nt = l // tblk
    c_ff, c_fb, c_i, c_q = (c // LANE for c in cols)

    def fwd(cb):
        return pl.BlockSpec((None, tblk, LANE), lambda b, hh, t: (b, t, cb + hh))

    def bwd(cb):
        return pl.BlockSpec((None, tblk, LANE), lambda b, hh, t: (b, nt - 1 - t, cb + hh))

    lb_spec = pl.BlockSpec((1, LANE), lambda b, hh, t: (0, hh))
    st_spec = pl.BlockSpec((None, None, HEAD_DIM, HEAD_DIM), lambda b, hh, t: (b, hh, 0, 0))
    if with_output:
        in_specs = [fwd(c_ff), fwd(c_i), fwd(c_q), bwd(c_fb), bwd(c_i), bwd(c_q)]
        args = [h] * 6
    else:
        in_specs = [fwd(c_ff), fwd(c_i), bwd(c_fb), bwd(c_i)]
        args = [h] * 4
    in_specs += [lb_spec, lb_spec, st_spec, st_spec]
    args += [lb_f, lb_b, s0[0], s0[1]]
    st_shape = jax.ShapeDtypeStruct((bsz, HEADS, HEAD_DIM, HEAD_DIM), F32)
    out_specs = [st_spec, st_spec]
    out_shape = [st_shape, st_shape]
    if with_output:
        o_shape = jax.ShapeDtypeStruct((bsz, l, HEADS * HEAD_DIM), F32)
        out_specs = [pl.BlockSpec((None, tblk, LANE), lambda b, hh, t: (b, t, hh)),
                     pl.BlockSpec((None, tblk, LANE), lambda b, hh, t: (b, nt - 1 - t, hh))] + out_specs
        out_shape = [o_shape, o_shape] + out_shape
    outs = pl.pallas_call(
        functools.partial(_hg_kernel, nchunk=tblk // CHUNK, with_output=with_output),
        name="hgrn2_scan",
        grid=(bsz, HEADS, nt),
        in_specs=in_specs,
        out_specs=out_specs,
        out_shape=out_shape,
        scratch_shapes=[pltpu.VMEM((HEAD_DIM, HEAD_DIM), F32), pltpu.VMEM((HEAD_DIM, HEAD_DIM), F32)],
        compiler_params=_cparams(("parallel", "parallel", "arbitrary")),
    )(*args)
    if with_output:
        return outs[0], outs[1], (outs[2], outs[3])
    return None, None, (outs[0], outs[1])


def _l2norm(t):
    return t * lax.rsqrt(jnp.sum(t * t, axis=-1, keepdims=True) + 1e-6)


def _softplus(x):
    return jnp.maximum(x, 0.0) + jnp.log(1.0 + jnp.exp(-jnp.abs(x)))


def _dn_prepare(items):
    r = _iota2((CHUNK, CHUNK), 0)
    c = _iota2((CHUNK, CHUNK), 1)
    eye = r == c
    for it in items:
        rev = it["rev"]
        incl = (c >= r) if rev else (c <= r)
        incl_t = (r >= c) if rev else (r <= c)
        g_b = jnp.broadcast_to(it["g"], (CHUNK, CHUNK))
        gc_col = jnp.sum(jnp.where(incl, g_b, 0.0), axis=1, keepdims=True)
        g_col = jnp.sum(jnp.where(eye, g_b, 0.0), axis=1, keepdims=True)
        gc_row = jnp.sum(jnp.where(incl_t, jnp.broadcast_to(g_col, (CHUNK, CHUNK)), 0.0), axis=0, keepdims=True)
        beta_col = jnp.sum(jnp.where(eye, jnp.broadcast_to(it["beta"], (CHUNK, CHUNK)), 0.0), axis=1,
                           keepdims=True)
        it["decay"] = jnp.where(incl, jnp.exp(jnp.where(incl, gc_col - gc_row, 0.0)), 0.0)
        g_last = gc_col[0:1] if rev else gc_col[CHUNK - 1:CHUNK]
        it["e_last"] = jnp.exp(g_last)
        egc = jnp.exp(gc_col)
        kb = it["k"] * beta_col
        it["kb"] = kb
        it["rhs"] = jnp.concatenate([it["v"] * beta_col, kb * egc], axis=1)
        it["kd"] = it["k"] * jnp.exp(g_last - gc_col)
        if it["q"] is not None:
            it["qe"] = it["q"] * egc
    for it in items:
        if it["q"] is not None:
            both = _dot_nt(jnp.concatenate([it["kb"], it["q"]], axis=0), it["k"])
            it["kk"], it["qk"] = both[:CHUNK], both[CHUNK:] * it["decay"]
        else:
            it["kk"] = _dot_nt(it["kb"], it["k"])
    for it in items:
        strict = (c > r) if it["rev"] else (c < r)
        n = jnp.where(strict, it["kk"] * it["decay"], 0.0)
        it["n"] = n
        it["p"] = jnp.where(eye, 1.0, 0.0) - n
    for it in items:
        it["m"] = _dot(it["n"], it["n"])
    for i in range(5):
        for it in items:
            if i < 4:
                pm = _dot(jnp.concatenate([it["p"], it["m"]], axis=0), it["m"])
                it["p"] = it["p"] + pm[:CHUNK]
                it["m"] = pm[CHUNK:]
            else:
                it["p"] = it["p"] + _dot(it["p"], it["m"])
    for it in items:
        it["uw"] = _dot(it["p"], it["rhs"])
    for it in items:
        bm = _dot_tn(it["kd"], it["uw"])
        it["b_c"], it["m_c"] = bm[:, :HEAD_DIM], bm[:, HEAD_DIM:]
        if it["q"] is not None:
            ow = _dot(it["qk"], it["uw"])
            it["o_c"], it["q_eff"] = ow[:, :HEAD_DIM], it["qe"] - ow[:, HEAD_DIM:]


def _dn_kernel(*refs, nchunk, with_output):
    if with_output:
        (alog_ref, dtb_ref, kf_ref, vf_ref, qf_ref, abf_ref, kb_ref, vb_ref, qb_ref, abb_ref,
         s0f_ref, s0b_ref, of_ref, ob_ref, sf_ref, sb_ref, stf, stb) = refs
    else:
        (alog_ref, dtb_ref, kf_ref, vf_ref, abf_ref, kb_ref, vb_ref, abb_ref,
         s0f_ref, s0b_ref, sf_ref, sb_ref, stf, stb) = refs
        qf_ref = qb_ref = of_ref = ob_ref = None
    hh = pl.program_id(1)

    @pl.when(pl.program_id(2) == 0)
    def _():
        stf[...] = s0f_ref[...]
        stb[...] = s0b_ref[...]

    def prep(k_ref, v_ref, q_ref, ab_ref, d):
        k = _l2norm(_silu(k_ref[...]))
        v = _silu(v_ref[...])
        q = None
        if with_output:
            q = _l2norm(_silu(q_ref[...])) * (HEAD_DIM ** -0.5)
        a = ab_ref[pl.ds(d * HEADS + hh, 1), :]
        bt = ab_ref[pl.ds(2 * HEADS + d * HEADS + hh, 1), :]
        g = -jnp.exp(alog_ref[d, hh]) * _softplus(a + dtb_ref[d, hh])
        return k, v, q, g, _sigmoid(bt)

    kf, vf, qf, gf, bf = prep(kf_ref, vf_ref, qf_ref, abf_ref, 0)
    kb, vb, qb, gb, bb = prep(kb_ref, vb_ref, qb_ref, abb_ref, 1)
    def chunk_item(k, v, q, g, beta, ci, rev):
        lo = ci * CHUNK
        return {"k": k[lo:lo + CHUNK], "v": v[lo:lo + CHUNK], "q": q[lo:lo + CHUNK] if with_output else None,
                "g": g[:, lo:lo + CHUNK], "beta": beta[:, lo:lo + CHUNK], "rev": rev, "lo": lo}

    items = []
    for ci in range(nchunk):
        items.append(chunk_item(kf, vf, qf, gf, bf, ci, False))
        items.append(chunk_item(kb, vb, qb, gb, bb, nchunk - 1 - ci, True))
    state = {False: stf[...], True: stb[...]}
    for wave in _waves(items, 4):
        _dn_prepare(wave)
        for it in wave:
            s = state[it["rev"]]
            if with_output:
                o_ref = ob_ref if it["rev"] else of_ref
                o_ref[pl.ds(it["lo"], CHUNK), :] = it["o_c"] + _dot(it["q_eff"], s)
            state[it["rev"]] = s * it["e_last"] + it["b_c"] - _dot(it["m_c"], s)
    stf[...] = state[False]
    stb[...] = state[True]
    sf_ref[...] = state[False]
    sb_ref[...] = state[True]


def _deltanet_scan(h, ab_t, cols, a_log, dt_bias, s0, *, tblk, with_output):
    bsz, l, _ = h.shape
    nt = l // tblk
    c_k, c_v, c_q = (c // LANE for c in cols)
    rep = HEADS // DN_QK_HEADS

    def tok(cb, shared, rev):
        def imap(b, hh, t):
            return (b, (nt - 1 - t) if rev else t, cb + (hh // rep if shared else hh))
        return pl.BlockSpec((None, tblk, LANE), imap)

    def ab_spec(rev):
        return pl.BlockSpec((None, 4 * HEADS, tblk), lambda b, hh, t: (b, 0, (nt - 1 - t) if rev else t))

    smem = pl.BlockSpec(memory_space=pltpu.SMEM)
    st_spec = pl.BlockSpec((None, None, HEAD_DIM, HEAD_DIM), lambda b, hh, t: (b, hh, 0, 0))
    if with_output:
        in_specs = [smem, smem, tok(c_k, True, False), tok(c_v, False, False), tok(c_q, True, False), ab_spec(False),
                    tok(c_k, True, True), tok(c_v, False, True), tok(c_q, True, True), ab_spec(True),
                    st_spec, st_spec]
        args = [a_log, dt_bias, h, h, h, ab_t, h, h, h, ab_t, s0[0], s0[1]]
    else:
        in_specs = [smem, smem, tok(c_k, True, False), tok(c_v, False, False), ab_spec(False),
                    tok(c_k, True, True), tok(c_v, False, True), ab_spec(True),
                    st_spec, st_spec]
        args = [a_log, dt_bias, h, h, ab_t, h, h, ab_t, s0[0], s0[1]]
    st_shape = jax.ShapeDtypeStruct((bsz, HEADS, HEAD_DIM, HEAD_DIM), F32)
    out_specs = [st_spec, st_spec]
    out_shape = [st_shape, st_shape]
    if with_output:
        o_shape = jax.ShapeDtypeStruct((bsz, l, HEADS * HEAD_DIM), F32)
        out_specs = [pl.BlockSpec((None, tblk, LANE), lambda b, hh, t: (b, t, hh)),
                     pl.BlockSpec((None, tblk, LANE), lambda b, hh, t: (b, nt - 1 - t, hh))] + out_specs
        out_shape = [o_shape, o_shape] + out_shape
    outs = pl.pallas_call(
        functools.partial(_dn_kernel, nchunk=tblk // CHUNK, with_output=with_output),
        name="deltanet_scan",
        grid=(bsz, HEADS, nt),
        in_specs=in_specs,
        out_specs=out_specs,
        out_shape=out_shape,
        scratch_shapes=[pltpu.VMEM((HEAD_DIM, HEAD_DIM), F32), pltpu.VMEM((HEAD_DIM, HEAD_DIM), F32)],
        compiler_params=_cparams(("parallel", "parallel", "arbitrary")),
    )(*args)
    if with_output:
        return outs[0], outs[1], (outs[2], outs[3])
    return None, None, (outs[0], outs[1])


def _combine_kernel(of_ref, ob_ref, z_ref, w_ref, o_ref, *, use_silu):
    o = of_ref[...] + ob_ref[...]
    o = o * lax.rsqrt(jnp.mean(o * o, axis=-1, keepdims=True) + RMS_EPS) * w_ref[...]
    z = z_ref[...]
    o_ref[...] = (o * (_silu(z) if use_silu else _sigmoid(z))).astype(o_ref.dtype)


def _combine(o_f, o_b, h, col_z, norm_w, *, tl, use_silu):
    bsz, l, c = o_f.shape
    cz = col_z // LANE
    spec = pl.BlockSpec((None, tl, LANE), lambda b, i, hh: (b, i, hh))
    return pl.pallas_call(
        functools.partial(_combine_kernel, use_silu=use_silu),
        name="mixer_norm_gate",
        grid=(bsz, l // tl, c // LANE),
        in_specs=[spec, spec,
                  pl.BlockSpec((None, tl, LANE), lambda b, i, hh: (b, i, cz + hh)),
                  pl.BlockSpec((1, LANE), lambda b, i, hh: (0, 0))],
        out_specs=spec,
        out_shape=jax.ShapeDtypeStruct((bsz, l, c), BF16),
        compiler_params=_cparams(("parallel", "parallel", "arbitrary")),
    )(o_f, o_b, h, norm_w)


def _dot_f32(a, b):
    return jnp.dot(a, b, precision=lax.Precision.HIGHEST, preferred_element_type=F32)


def _hy_filt_kernel(w1_ref, b1_ref, w2_ref, b2_ref, w3_ref, o_ref, hid_ref, *, l_total, tl):
    i = pl.program_id(0)
    j = pl.program_id(1)

    @pl.when(j == 0)
    def _():
        row = (_iota2((tl, LANE), 0) + i * tl).astype(F32)
        lane = _iota2((tl, LANE), 1)
        t = row * (1.0 / (l_total - 1))
        ang = row * (2.0 * math.pi / l_total)
        bidx = jnp.where(lane <= HY_BANDS, lane - 1, lane - 1 - HY_BANDS).astype(F32)
        band = 1e-4 + bidx * ((HY_BANDS - 1 - 1e-4) / (HY_BANDS - 1))
        arg = band * ang
        feats = jnp.where(lane == 0, t,
                          jnp.where(lane <= HY_BANDS, jnp.cos(arg),
                                    jnp.where(lane <= 2 * HY_BANDS, -jnp.sin(arg), 0.0)))
        hid = jnp.sin(_dot_f32(feats, w1_ref[...]) + b1_ref[...])
        hid_ref[...] = jnp.sin(_dot_f32(hid, w2_ref[...]) + b2_ref[...])

    hcol = _dot_f32(hid_ref[...], w3_ref[...])
    width = hcol.shape[1]
    ch = _iota2((tl, width), 1).astype(F32)
    lo = math.log(HY_DECAY_TARGET) / HY_SLOW_DECAY
    hi = math.log(HY_DECAY_TARGET) / HY_FAST_DECAY
    delta = jnp.abs(lo + ch * ((hi - lo) / (width - 1)))
    trow = (_iota2((tl, width), 0) + i * tl)
    out = hcol * jnp.exp(-(trow.astype(F32) * (1.0 / (l_total - 1))) * delta)
    drop = jnp.logical_and(trow == 0, jnp.bitwise_and(j, 1) == 1)
    o_ref[...] = jnp.where(drop, 0.0, out)


def _hy_filters(l_total, w1p, b1, w2, b2, w3):
    tl = min(l_total, 512)
    n = w3.shape[1]
    full = lambda a: pl.BlockSpec(a.shape, lambda i, j: (0,) * a.ndim)
    return pl.pallas_call(
        functools.partial(_hy_filt_kernel, l_total=l_total, tl=tl),
        name="hy_filters",
        grid=(l_total // tl, n // BRANCH_WIDTH),
        in_specs=[full(w1p), full(b1), full(w2), full(b2),
                  pl.BlockSpec((HY_HIDDEN, BRANCH_WIDTH), lambda i, j: (0, j))],
        out_specs=pl.BlockSpec((tl, BRANCH_WIDTH), lambda i, j: (i, j)),
        out_shape=jax.ShapeDtypeStruct((l_total, n), F32),
        scratch_shapes=[pltpu.VMEM((tl, HY_HIDDEN), F32)],
        compiler_params=_cparams(("parallel", "arbitrary")),
    )(w1p, b1, w2, b2, w3)


@functools.lru_cache(maxsize=None)
def _fft_tables(n1, n2):
    n = n1 * n2
    n2h = n2 // 2
    k2 = np.arange(n2, dtype=np.int64)[:, None]
    m = np.arange(n2h, dtype=np.int64)[None, :]
    ta = np.zeros((n1, 2 * n2, 2 * n2h))
    for a in range(n1):
        th = 2.0 * np.pi * ((k2 * (a + n1 * m)) % n) / n
        c, s = np.cos(th), np.sin(th)
        ta[a] = np.block([[c, s], [-s, c]])
    tai = np.transpose(ta, (0, 2, 1)) / n
    k1 = np.arange(n1, dtype=np.int64)
    ph = 2.0 * np.pi * ((k1[:, None] * k1[None, :]) % n1) / n1
    c, s = np.cos(ph), np.sin(ph)
    fb = np.block([[c, s], [-s, c]])
    fs = fb[:n1] + fb[n1:]
    fsr = np.stack([fs[(-k1 - e) % n1] for e in (0, 1)])
    return tuple(np.asarray(t, np.float32) for t in (ta, tai, fb, fb.T, fs, fsr))


def _fft_split(l_total):
    n = 2 * l_total
    n1 = min(128, l_total // 8)
    return n1, n // n1


def _rows2d(ref):
    rows = 1
    for s in ref.shape[:-1]:
        rows *= s
    return ref.reshape(rows, ref.shape[-1])


def _lane_slabs(shape, imap, ns):
    def slab(s):
        def index_map(*a):
            idx = imap(*a)
            return idx[:-1] + (idx[-1] * ns + s,)
        return pl.BlockSpec(shape + (LANE,), index_map)
    return [slab(s) for s in range(ns)]


def _strided_rows(ref2, group, n, j):
    return ref2[pl.ds(group * 8 + j, n, stride=8), :]


def _hy_stage_a_kernel(gr_ref, cr_ref, gi_ref, ci_ref, *refs, ns, n2, n2h, slab_major_in):
    del gr_ref, cr_ref, gi_ref, ci_ref
    if slab_major_in:
        xs = [[(_rows2d(r), s * n2h) for s in range(ns)] for r in refs[:2]]
        refs = refs[2:]
    else:
        xs = [[(_rows2d(r), 0) for r in refs[q * ns:(q + 1) * ns]] for q in range(2)]
        refs = refs[2 * ns:]
    t_ref, o_ref = refs
    o2 = _rows2d(o_ref)
    for j in range(8):
        x = jnp.concatenate([jnp.concatenate([_strided_rows(r2, grp, n2h, j) for r2, grp in part], axis=1)
                             for part in xs], axis=0).astype(BF16)
        y = jnp.dot(t_ref[j], x, preferred_element_type=F32)
        for p in range(2):
            for s in range(ns):
                o2[pl.ds(((p * ns + s) * n2) * 8 + j, n2, stride=8), :] = y[p * n2:(p + 1) * n2, s * LANE:(s + 1) * LANE]


def _hy_stage_a(x, pairs, tab, *, ct, n2, slab_major_in):
    n2h = n2 // 2
    n1 = x.shape[-2] if slab_major_in else x.shape[2]
    npair = len(pairs)
    nct = BRANCH_WIDTH // ct
    ns = ct // LANE
    gr = jnp.asarray([p[0][0] for p in pairs], jnp.int32)
    cr = jnp.asarray([p[0][1] for p in pairs], jnp.int32)
    gi = jnp.asarray([p[1][0] for p in pairs], jnp.int32)
    ci = jnp.asarray([p[1][1] for p in pairs], jnp.int32)
    if slab_major_in:
        in_specs = [pl.BlockSpec((None, ns, n2h, 8, LANE), lambda p, c, g, *_: (0, c, 0, g, 0)),
                    pl.BlockSpec((None, ns, n2h, 8, LANE), lambda p, c, g, *_: (1, c, 0, g, 0))]
        args = [x, x]
    else:
        in_specs = (_lane_slabs((None, n2h, 8), lambda p, c, g, gr, cr, gi, ci: (gr[p], 0, g, cr[p] * nct + c), ns)
                    + _lane_slabs((None, n2h, 8), lambda p, c, g, gr, cr, gi, ci: (gi[p], 0, g, ci[p] * nct + c), ns))
        args = [x] * (2 * ns)
    grid_spec = pltpu.PrefetchScalarGridSpec(
        num_scalar_prefetch=4,
        grid=(npair, nct, n1 // 8),
        in_specs=in_specs + [pl.BlockSpec((8, 2 * n2, 2 * n2h), lambda p, c, g, *_: (g, 0, 0))],
        out_specs=pl.BlockSpec((None, 2, ns, n2, 8, LANE), lambda p, c, g, *_: (p, 0, c, 0, g, 0)),
    )
    return pl.pallas_call(
        functools.partial(_hy_stage_a_kernel, ns=ns, n2=n2, n2h=n2h, slab_major_in=slab_major_in),
        name="hy_stage_a",
        grid_spec=grid_spec,
        out_shape=jax.ShapeDtypeStruct((npair, 2, BRANCH_WIDTH // LANE, n2, n1, LANE), F32),
        compiler_params=_cparams(("parallel", "parallel", "arbitrary")),
    )(gr, cr, gi, ci, *args, tab)


def _hy_spec_kernel(c_ref, cm_ref, c0_ref, fs_ref, fsr_ref, o_ref, *, kg):
    g = pl.program_id(2)
    for k in range(kg):
        s = jnp.dot(fs_ref[...], _complex_slab(c_ref, k), preferred_element_type=F32)
        if k:
            mirror, fsr = _complex_slab(cm_ref, kg - k), fsr_ref[1]
        else:
            mirror, fsr = _complex_slab(c0_ref, 0), jnp.where(g == 0, fsr_ref[0], fsr_ref[1])
        t = jnp.dot(fsr, mirror, preferred_element_type=F32)
        o_ref[0, k] = 0.5 * (s + t)
        o_ref[1, k] = 0.5 * (s - t)


def _complex_slab(ref, k):
    ns = ref.shape[1]
    return jnp.concatenate([jnp.concatenate([ref[p, s, k] for s in range(ns)], axis=1) for p in range(2)],
                           axis=0).astype(BF16)


def _hy_spectrum(cs, fs, fsr, *, ct, kg):
    no, _, nslab, n2, n1, _ = cs.shape
    ns = ct // LANE
    ng = n2 // kg
    return pl.pallas_call(
        functools.partial(_hy_spec_kernel, kg=kg),
        name="hy_spectrum",
        grid=(no, nslab // ns, ng),
        in_specs=[pl.BlockSpec((None, 2, ns, kg, n1, LANE), lambda o, j, g: (o, 0, j, g, 0, 0)),
                  pl.BlockSpec((None, 2, ns, kg, n1, LANE), lambda o, j, g: (o, 0, j, ng - 1 - g, 0, 0)),
                  pl.BlockSpec((None, 2, ns, 1, n1, LANE), lambda o, j, g: (o, 0, j, (kg * (ng - g)) % n2, 0, 0)),
                  pl.BlockSpec(fs.shape, lambda o, j, g: (0, 0)),
                  pl.BlockSpec(fsr.shape, lambda o, j, g: (0, 0, 0))],
        out_specs=pl.BlockSpec((None, 2, kg, n1, ct), lambda o, j, g: (o, 0, g, 0, j)),
        out_shape=jax.ShapeDtypeStruct((no, 2, n2, n1, nslab * LANE), F32),
        compiler_params=_cparams(("parallel", "parallel", "arbitrary")),
    )(cs, cs, cs, fs, fsr)


def _hy_stage_b_kernel(c_ref, k_ref, fb_ref, fbi_ref, o_ref, *, kg, n1):
    ns = c_ref.shape[1]
    for k in range(kg):
        z = jnp.dot(fb_ref[...], _complex_slab(c_ref, k), preferred_element_type=F32)
        zr, zi = z[:n1], z[n1:]
        kr, ki = k_ref[0, k], k_ref[1, k]
        y = jnp.concatenate([zr * kr - zi * ki, zr * ki + zi * kr], axis=0).astype(BF16)
        d = jnp.dot(fbi_ref[...], y, preferred_element_type=F32)
        for p in range(2):
            for s in range(ns):
                o_ref[p, s, k] = d[p * n1:(p + 1) * n1, s * LANE:(s + 1) * LANE]


def _hy_stage_b(cs, spec, order, fb, fbi, *, ct, kg):
    _, _, nslab, n2, n1, _ = cs.shape
    ns = ct // LANE
    c = nslab * LANE
    return pl.pallas_call(
        functools.partial(_hy_stage_b_kernel, kg=kg, n1=n1),
        name="hy_stage_b",
        grid=(c // ct, n2 // kg),
        in_specs=[pl.BlockSpec((None, 2, ns, kg, n1, LANE), lambda j, g: (0, 0, j, g, 0, 0)),
                  pl.BlockSpec((None, 2, kg, n1, ct), lambda j, g: (order, 0, g, 0, j)),
                  pl.BlockSpec(fb.shape, lambda j, g: (0, 0)),
                  pl.BlockSpec(fbi.shape, lambda j, g: (0, 0))],
        out_specs=pl.BlockSpec((2, ns, kg, n1, LANE), lambda j, g: (0, j, g, 0, 0)),
        out_shape=jax.ShapeDtypeStruct((2, nslab, n2, n1, LANE), F32),
        compiler_params=_cparams(("parallel", "arbitrary")),
    )(cs, spec, fb, fbi)


def _hy_stage_c_kernel(*refs, ns, n2, n2h, slab_major_u):
    d2 = _rows2d(refs[0])
    t_ref = refs[1]
    if slab_major_u:
        u2 = _rows2d(refs[2])
        u_parts = [[(u2, (b * ns + s) * n2h) for s in range(ns)] for b in range(2)]
        refs = refs[3:]
    else:
        u_parts = [[(_rows2d(r), b * n2h) for r in refs[2:2 + ns]] for b in range(2)]
        refs = refs[2 + ns:]
    g_parts = [[(_rows2d(r), b * n2h) for r in refs[:ns]] for b in range(2)]
    skip_ref, o_ref = refs[ns:]
    o2 = _rows2d(o_ref)
    skip = skip_ref[...]

    def rows(parts, j):
        return jnp.concatenate([jnp.concatenate([_strided_rows(r2, grp, n2h, j) for r2, grp in part], axis=1)
                                for part in parts], axis=0)

    for j in range(8):
        d = jnp.concatenate([jnp.concatenate([_strided_rows(d2, (p * ns + s) * n2, n2, j) for s in range(ns)], axis=1)
                             for p in range(2)], axis=0).astype(BF16)
        y = jnp.dot(t_ref[j], d, preferred_element_type=F32)
        out = rows(g_parts, j) * (y + skip * rows(u_parts, j))
        for b in range(2):
            for s in range(ns):
                o2[pl.ds(((b * ns + s) * n2h) * 8 + j, n2h, stride=8), :] = (
                    out[b * n2h:(b + 1) * n2h, s * LANE:(s + 1) * LANE])


def _hy_stage_c(d, tai, u, ucol, z4, gcol, skip, *, ct, slab_major_u):
    _, nslab, n2, n1, _ = d.shape
    n2h = n2 // 2
    ns = ct // LANE
    uc, gc = ucol // ct, gcol // ct
    if slab_major_u:
        u_specs = [pl.BlockSpec((2, ns, n2h, 8, LANE), lambda j, g: (0, j, 0, g, 0))]
        u_args = [u]
    else:
        u_specs = _lane_slabs((2, n2h, 8), lambda j, g: (0, 0, g, uc + j), ns)
        u_args = [u] * ns
    return pl.pallas_call(
        functools.partial(_hy_stage_c_kernel, ns=ns, n2=n2, n2h=n2h, slab_major_u=slab_major_u),
        name="hy_stage_c",
        grid=(nslab // ns, n1 // 8),
        in_specs=([pl.BlockSpec((2, ns, n2, 8, LANE), lambda j, g: (0, j, 0, g, 0)),
                   pl.BlockSpec((8, 2 * n2h, 2 * n2), lambda j, g: (g, 0, 0))]
                  + u_specs + _lane_slabs((2, n2h, 8), lambda j, g: (0, 0, g, gc + j), ns)
                  + [pl.BlockSpec((1, ct), lambda j, g: (0, j))]),
        out_specs=pl.BlockSpec((2, ns, n2h, 8, LANE), lambda j, g: (0, j, 0, g, 0)),
        out_shape=jax.ShapeDtypeStruct((2, nslab, n2h, n1, LANE), F32),
        compiler_params=_cparams(("parallel", "arbitrary")),
    )(d, tai, *u_args, *([z4] * ns), skip)


def _hyena(h, filt_params, skip):
    bsz, l, _ = h.shape
    assert bsz == 2
    n1, n2 = _fft_split(l)
    n2h = n2 // 2
    ta, tai, fb, fbi, fs, fsr = (jnp.asarray(t).astype(BF16) for t in _fft_tables(n1, n2))
    ct = 512
    kg = 8

    hfilt = _hy_filters(l, *filt_params)
    cs_k = _hy_stage_a(hfilt.reshape(1, n2h, n1, -1), [((0, 2 * o), (0, 2 * o + 1)) for o in range(HY_ORDER)],
                       ta, ct=ct, n2=n2, slab_major_in=False)
    spec = _hy_spectrum(cs_k, fs, fsr, ct=ct, kg=kg)

    z4 = h.reshape(bsz, n2h, n1, -1)
    u, ucol = z4, COL_HY
    for o in range(HY_ORDER):
        first = o == 0
        cs = _hy_stage_a(u, [((0, ucol // BRANCH_WIDTH), (1, ucol // BRANCH_WIDTH))], ta, ct=ct, n2=n2,
                         slab_major_in=not first)
        d = _hy_stage_b(cs, spec, o, fb, fbi, ct=ct, kg=kg)
        u = _hy_stage_c(d, tai, u, ucol, z4, COL_HY + (o + 1) * BRANCH_WIDTH, skip[o:o + 1], ct=ct,
                        slab_major_u=not first)
        ucol = 0
    return u.reshape(bsz, BRANCH_WIDTH // LANE, l, LANE)


def _conv_tables(hy_conv_w, hy_conv_b, dn_conv_q, dn_conv_k, dn_conv_v):
    depth = hy_conv_w.shape[0]
    ident = lambda n: jnp.broadcast_to(jnp.asarray([0.0, 1.0, 0.0], F32)[None, :, None], (depth, 3, n))
    taps = jnp.concatenate([ident(COL_HY), hy_conv_w, ident(COL_DK - COL_FF), dn_conv_k, dn_conv_v,
                            ident(COL_DQ - COL_AB), dn_conv_q, ident(COL_PAD - COL_DQ - dn_conv_q.shape[2])], axis=2)
    bias = jnp.concatenate([jnp.zeros((depth, 1, COL_HY), F32), hy_conv_b[:, None, :],
                            jnp.zeros((depth, 1, COL_PAD - COL_FF), F32)], axis=2)
    return taps, bias


def _token_mixers(u_src, sh, sc, w_t, conv_t, layer, lp, states, *, is_grid, with_output, tm):
    bsz, l, _ = u_src.shape
    tblk = min(l, 2048)
    taps, bias = (a[layer:layer + 1] for a in conv_t)
    if with_output:
        h = _in_proj(u_src, sh, sc, w_t, layer, range(COL_PAD // IN_TN), taps, bias, tm=tm, is_grid=is_grid)
        base = 0
    else:
        lo, hi = STATE_BLOCKS[0] * IN_TN, (STATE_BLOCKS[-1] + 1) * IN_TN
        h = _in_proj(u_src, sh, sc, w_t, layer, STATE_BLOCKS, taps[:, :, lo:hi], bias[:, :, lo:hi],
                     tm=tm, is_grid=is_grid)
        base = COL_FF
    col = lambda c: c - base
    ab_t = jnp.swapaxes(h[:, :, col(COL_AB):col(COL_AB) + 4 * HEADS], 1, 2)
    hg_f, hg_b, hg_states = _hgrn2_scan(h, (col(COL_FF), col(COL_FB), col(COL_HI), col(COL_HQ)),
                                        lp["lb_f"], lp["lb_b"], states[0], tblk=tblk, with_output=with_output)
    dn_f, dn_b, dn_states = _deltanet_scan(h, ab_t, (col(COL_DK), col(COL_DV), col(COL_DQ)),
                                           lp["dn_a_log"], lp["dn_dt_bias"], states[1],
                                           tblk=min(l, 2048), with_output=with_output)
    new_states = (hg_states, dn_states)
    if not with_output:
        return None, new_states
    tl = min(l, 2048)
    hg_out = _combine(hg_f, hg_b, h, COL_HGATE, lp["hg_norm_w"], tl=tl, use_silu=False)
    dn_out = _combine(dn_f, dn_b, h, COL_DZ, lp["dn_norm_w"], tl=tl, use_silu=True)
    hy_out = _hyena(h, lp["hy_filt"], lp["hy_skip"])
    ysum = _branch_merge([hy_out, hg_out, dn_out], lp["w_branch"], layer, h, tm=min(l, 1024), tn=512)
    return ysum, new_states


def kernel(x, c, ctx, c_ctx, w_ada, b_ada, w_in, hy_conv_w, hy_conv_b, hy_filt_w1, hy_filt_b1, hy_filt_w2, hy_filt_b2, hy_filt_w3, hy_skip, hg_lb_logits, hg_norm_w, dn_conv_q, dn_conv_k, dn_conv_v, dn_a_log, dn_dt_bias, dn_norm_w, w_branch, w_out, ln1_g, ln1_b, w_ff1, w_ff2, ln2_g, ln2_b):
    depth = w_in.shape[0]
    bsz, _, d = x.shape
    alpha = (2 * depth) ** 0.25
    p = jax.nn.softmax(hg_lb_logits.astype(F32), axis=1)
    lower = jnp.cumsum(p, axis=1) - p[:, :1]
    cs = jnp.concatenate([c, c_ctx[None], jnp.zeros((8 - bsz - 1, d), F32)], axis=0)
    w_r = jnp.swapaxes(w_in, 1, 2)
    w_b, w_o, w_1, w_2 = w_branch.astype(BF16), w_out.astype(BF16), w_ff1, w_ff2.astype(BF16)
    b_ada3 = b_ada[:, None, :]
    conv_t = _conv_tables(hy_conv_w, hy_conv_b, dn_conv_q, dn_conv_k, dn_conv_v)
    h_ctx = ctx
    for l in range(depth):
        last = l == depth - 1
        row = lambda v: v.reshape(1, -1)
        lp = {
            "lb_f": row(lower[0, l]), "lb_b": row(lower[1, l]),
            "hg_norm_w": row(hg_norm_w[l]), "dn_norm_w": row(dn_norm_w[l]),
            "dn_a_log": dn_a_log[l], "dn_dt_bias": dn_dt_bias[l], "hy_skip": hy_skip[l],
            "hy_filt": (jnp.pad(hy_filt_w1[l], ((0, LANE - HY_EMB), (0, 0))), row(hy_filt_b1[l]),
                        hy_filt_w2[l], row(hy_filt_b2[l]), hy_filt_w3[l]),
            "w_branch": w_b,
        }
        g1n, b1n, g2n, b2n = row(ln1_g[l]), row(ln1_b[l]), row(ln2_g[l]), row(ln2_b[l])
        mod = _ada_mod(cs, w_ada, b_ada3, l)
        m_lat = [mod[:bsz, i * d:(i + 1) * d][:, None, :] for i in range(6)]
        m_ctx = [jnp.broadcast_to(mod[bsz, i * d:(i + 1) * d][None, None, :], (bsz, 1, d)) for i in range(6)]
        zero = jnp.zeros((bsz, HEADS, HEAD_DIM, HEAD_DIM), F32)
        init = ((zero, zero), (zero, zero))

        lc = h_ctx.shape[1]
        y_ctx, ctx_states = _token_mixers(h_ctx, m_ctx[0], m_ctx[1], w_r, conv_t, l, lp, init,
                                          is_grid=False, with_output=not last, tm=lc)
        if not last:
            h_ctx = _mm_ln(y_ctx, w_o, l, h_ctx, m_ctx[2], g1n, b1n, tm=lc, tk=d, alpha=alpha)
            mid = _mlp_up(h_ctx, m_ctx[3], m_ctx[4], w_1, l, tm=lc, tn=1024)
            h_ctx = _mm_ln(mid, w_2, l, h_ctx, m_ctx[5], g2n, b2n, tm=lc, tk=1024, alpha=alpha)

        lx = x.shape[1]
        y, _ = _token_mixers(x, m_lat[0], m_lat[1], w_r, conv_t, l, lp, ctx_states,
                             is_grid=True, with_output=True, tm=min(lx, 1024))
        x = _mm_ln(y, w_o, l, x, m_lat[2], g1n, b1n, tm=min(lx, 512), tk=d, alpha=alpha)
        mid = _mlp_up(x, m_lat[3], m_lat[4], w_1, l, tm=min(lx, 1024), tn=1024)
        x = _mm_ln(mid, w_2, l, x, m_lat[5], g2n, b2n, tm=min(lx, 1024), tk=512, alpha=alpha)
    return x
```

```python
import functools
import math

import numpy as np
import jax
import jax.numpy as jnp
from jax import lax
from jax.experimental import pallas as pl
from jax.experimental.pallas import tpu as pltpu

F32 = jnp.float32
BF16 = jnp.bfloat16

D_MODEL = 2048
GRID_W = 64
BRANCH_WIDTH = 1024
N_BRANCH = 3
HY_ORDER = 2
HY_EMB = 33
HY_BANDS = (HY_EMB - 1) // 2
HY_HIDDEN = 64
HY_DECAY_TARGET = 1e-2
HY_FAST_DECAY = 0.3
HY_SLOW_DECAY = 1.5
HEADS = 8
HEAD_DIM = 128
DN_QK_HEADS = 4
HG_F_MIN = 1e-30
CHUNK = 64
LN_EPS = 1e-5
RMS_EPS = 1e-6
LANE = 128

IN_TN = 1024
COL_GATE = 0
COL_HY = 6144
COL_FF = 9216
COL_FB = 10240
COL_HI = 11264
COL_DK = 12288
COL_DV = 12800
COL_AB = 13824
COL_HQ = 14336
COL_HGATE = 15360
COL_DQ = 16384
COL_DZ = 17408
COL_PAD = 18432
SOURCE_ROWS = np.asarray([11296 + IN_TN * k for k in range(6)] + [8224 + IN_TN * k for k in range(3)]
                         + [0, 1024, 2048, 3072, 4096, 4640, 5664, 6688, 7200], np.int32)
CONV_WINDOWS = np.asarray([COL_HY <= IN_TN * j < COL_FF or COL_DK <= IN_TN * j < COL_HQ or IN_TN * j == COL_DQ
                           for j in range(COL_PAD // IN_TN)], np.int32)
STATE_BLOCKS = tuple(range(COL_FF // IN_TN, COL_HQ // IN_TN))

VMEM_LIMIT = 56 * 1024 * 1024


def _cparams(sem):
    return pltpu.CompilerParams(dimension_semantics=sem, vmem_limit_bytes=VMEM_LIMIT)


def _dot(a, b):
    return jnp.dot(a.astype(BF16), b.astype(BF16), preferred_element_type=F32)


def _dot_nt(a, b):
    return lax.dot_general(a.astype(BF16), b.astype(BF16), (((1,), (1,)), ((), ())),
                           preferred_element_type=F32)


def _dot_tn(a, b):
    return lax.dot_general(a.astype(BF16), b.astype(BF16), (((0,), (0,)), ((), ())),
                           preferred_element_type=F32)


def _sigmoid(x):
    return 1.0 / (1.0 + jnp.exp(-x))


def _silu(x):
    return x * _sigmoid(x)


def _ada_kernel(c_ref, w_ref, b_ref, o_ref):
    a = _silu(c_ref[...])
    o_ref[...] = _dot(a, w_ref[...]) + b_ref[...]


def _ada_mod(cs, w, b, layer):
    _, d, n = w.shape
    tn = 1024
    return pl.pallas_call(
        _ada_kernel,
        grid=(n // tn,),
        in_specs=[pl.BlockSpec((8, d), lambda j: (0, 0)),
                  pl.BlockSpec((None, d, tn), lambda j: (layer, 0, j)),
                  pl.BlockSpec((None, 1, tn), lambda j: (layer, 0, j))],
        out_specs=pl.BlockSpec((8, tn), lambda j: (0, j)),
        out_shape=jax.ShapeDtypeStruct((8, n), F32),
        compiler_params=_cparams(("arbitrary",)),
        name="ada_mod",
    )(cs, w, b)


def _conv3(x, w, period):
    t = x.shape[0]
    pos = jnp.bitwise_and(_iota2((t, 1), 0), period - 1)
    has_l, has_r = pos != 0, pos != period - 1
    xl = jnp.where(has_l, pltpu.roll(x, 1, 0), 0.0)
    xr = jnp.where(has_r, pltpu.roll(x, t - 1, 0), 0.0)
    return xl * w[0:1] + x * w[1:2] + xr * w[2:3]


def _mlp_up_kernel(x_ref, sh_ref, sc_ref, w_ref, o_ref, a_scr):
    @pl.when(pl.program_id(2) == 0)
    def _():
        a_scr[...] = (x_ref[...] * (1.0 + sc_ref[...]) + sh_ref[...]).astype(BF16)

    acc = jnp.dot(a_scr[...], w_ref[...].astype(BF16), preferred_element_type=F32)
    o_ref[...] = jnp.square(jnp.maximum(acc, 0.0)).astype(o_ref.dtype)


def _mlp_up(x, sh, sc, w, layer, *, tm, tn):
    bsz, l, d = x.shape
    n = w.shape[2]
    return pl.pallas_call(
        _mlp_up_kernel,
        grid=(bsz, l // tm, n // tn),
        in_specs=[pl.BlockSpec((None, tm, d), lambda b, i, j: (b, i, 0)),
                  pl.BlockSpec((None, 1, d), lambda b, i, j: (b, 0, 0)),
                  pl.BlockSpec((None, 1, d), lambda b, i, j: (b, 0, 0)),
                  pl.BlockSpec((None, d, tn), lambda b, i, j: (layer, 0, j))],
        out_specs=pl.BlockSpec((None, tm, tn), lambda b, i, j: (b, i, j)),
        out_shape=jax.ShapeDtypeStruct((bsz, l, n), BF16),
        scratch_shapes=[pltpu.VMEM((tm, d), BF16)],
        compiler_params=_cparams(("parallel", "parallel", "arbitrary")),
        name="mm_relu2",
    )(x, sh, sc, w)


def _in_proj_kernel(off_ref, conv_ref, x_ref, sh_ref, sc_ref, w_ref, cw_ref, cb_ref, o_ref, a_scr, *, period):
    del off_ref
    has_conv = conv_ref[pl.program_id(2)] != 0

    @pl.when(pl.program_id(2) == 0)
    def _():
        a_scr[...] = (x_ref[...] * (1.0 + sc_ref[...]) + sh_ref[...]).astype(BF16)

    o_ref[...] = lax.dot_general(a_scr[...], w_ref[0].astype(BF16), (((1,), (1,)), ((), ())),
                                 preferred_element_type=F32)

    @pl.when(has_conv)
    def _():
        o_ref[...] = _conv3(o_ref[...], cw_ref[...], period) + cb_ref[...]


def _in_proj(x, sh, sc, w_t, layer, windows, taps, bias, *, tm, period):
    bsz, l, d = x.shape
    nblk = len(windows)
    assert tm % period == 0 and period & (period - 1) == 0
    row_off = jnp.asarray(SOURCE_ROWS[list(windows)])
    has_conv = jnp.asarray(CONV_WINDOWS[list(windows)])
    grid_spec = pltpu.PrefetchScalarGridSpec(
        num_scalar_prefetch=2,
        grid=(bsz, l // tm, nblk),
        in_specs=[pl.BlockSpec((None, tm, d), lambda b, i, j, off, cv: (b, i, 0)),
                  pl.BlockSpec((None, 1, d), lambda b, i, j, off, cv: (b, 0, 0)),
                  pl.BlockSpec((None, 1, d), lambda b, i, j, off, cv: (b, 0, 0)),
                  pl.BlockSpec((pl.Element(1), pl.Element(IN_TN), pl.Element(d)),
                               lambda b, i, j, off, cv: (layer, pl.multiple_of(off[j], 4 * HEADS), 0)),
                  pl.BlockSpec((None, 3, IN_TN), lambda b, i, j, off, cv: (0, 0, j)),
                  pl.BlockSpec((None, 1, IN_TN), lambda b, i, j, off, cv: (0, 0, j))],
        out_specs=pl.BlockSpec((None, tm, IN_TN), lambda b, i, j, off, cv: (b, i, j)),
        scratch_shapes=[pltpu.VMEM((tm, d), BF16)],
    )
    return pl.pallas_call(
        functools.partial(_in_proj_kernel, period=period),
        name="mm_in_proj",
        grid_spec=grid_spec,
        out_shape=jax.ShapeDtypeStruct((bsz, l, nblk * IN_TN), F32),
        compiler_params=_cparams(("parallel", "parallel", "arbitrary")),
    )(row_off, has_conv, x, sh, sc, w_t, taps, bias)


def _mm_ln_kernel(a_ref, w_ref, x_ref, gate_ref, g_ref, b_ref, o_ref, acc_ref, *, nk, alpha):
    k = pl.program_id(2)

    @pl.when(k == 0)
    def _():
        acc_ref[...] = jnp.zeros_like(acc_ref)

    acc_ref[...] += jnp.dot(a_ref[...], w_ref[...].astype(BF16), preferred_element_type=F32)

    @pl.when(k == nk - 1)
    def _():
        y = alpha * x_ref[...] + gate_ref[...] * acc_ref[...]
        mu = jnp.mean(y, axis=-1, keepdims=True)
        yc = y - mu
        var = jnp.mean(yc * yc, axis=-1, keepdims=True)
        o_ref[...] = yc * lax.rsqrt(var + LN_EPS) * g_ref[...] + b_ref[...]


def _mm_ln(a, w, layer, x, gate, ln_g, ln_b, *, tm, tk, alpha):
    bsz, l, kdim = a.shape
    d = w.shape[2]
    nk = kdim // tk
    return pl.pallas_call(
        functools.partial(_mm_ln_kernel, nk=nk, alpha=alpha),
        grid=(bsz, l // tm, nk),
        in_specs=[pl.BlockSpec((None, tm, tk), lambda b, i, k: (b, i, k)),
                  pl.BlockSpec((None, tk, d), lambda b, i, k: (layer, k, 0)),
                  pl.BlockSpec((None, tm, d), lambda b, i, k: (b, i, 0)),
                  pl.BlockSpec((None, 1, d), lambda b, i, k: (b, 0, 0)),
                  pl.BlockSpec((1, d), lambda b, i, k: (0, 0)),
                  pl.BlockSpec((1, d), lambda b, i, k: (0, 0))],
        out_specs=pl.BlockSpec((None, tm, d), lambda b, i, k: (b, i, 0)),
        out_shape=jax.ShapeDtypeStruct((bsz, l, d), F32),
        scratch_shapes=[pltpu.VMEM((tm, d), F32)],
        compiler_params=_cparams(("parallel", "parallel", "arbitrary")),
        name="mm_resid_ln",
    )(a, w, x, gate, ln_g, ln_b)


def _branch_kernel(b0_ref, b1_ref, b2_ref, w_ref, g0_ref, g1_ref, g2_ref, o_ref):
    b0 = jnp.concatenate([b0_ref[s] for s in range(b0_ref.shape[0])], axis=1).astype(BF16)
    acc = _sigmoid(g0_ref[...]) * jnp.dot(b0, w_ref[0].astype(BF16), preferred_element_type=F32)
    acc += _sigmoid(g1_ref[...]) * jnp.dot(b1_ref[...], w_ref[1].astype(BF16), preferred_element_type=F32)
    acc += _sigmoid(g2_ref[...]) * jnp.dot(b2_ref[...], w_ref[2].astype(BF16), preferred_element_type=F32)
    o_ref[...] = acc.astype(o_ref.dtype)


def _branch_merge(branches, w, layer, h, *, tm, tn):
    bsz, l, c = branches[1].shape
    d = w.shape[3]
    nj = d // tn
    br_spec = pl.BlockSpec((None, tm, c), lambda b, i, j: (b, i, 0))
    hy_spec = pl.BlockSpec((None, c // LANE, tm, LANE), lambda b, i, j: (b, 0, i, 0))

    def gate_spec(g):
        return pl.BlockSpec((None, tm, tn), lambda b, i, j: (b, i, COL_GATE // tn + g * nj + j))

    return pl.pallas_call(
        _branch_kernel,
        grid=(bsz, l // tm, nj),
        in_specs=[hy_spec, br_spec, br_spec,
                  pl.BlockSpec((None, N_BRANCH, c, tn), lambda b, i, j: (layer, 0, 0, j)),
                  gate_spec(0), gate_spec(1), gate_spec(2)],
        out_specs=pl.BlockSpec((None, tm, tn), lambda b, i, j: (b, i, j)),
        out_shape=jax.ShapeDtypeStruct((bsz, l, d), BF16),
        compiler_params=_cparams(("parallel", "parallel", "arbitrary")),
        name="branch_merge",
    )(*branches, w, h, h, h)


def _iota2(shape, axis):
    return lax.broadcasted_iota(jnp.int32, shape, axis)


def _split3(g):
    g1 = g.astype(BF16)
    r = g - g1.astype(F32)
    g2 = r.astype(BF16)
    g3 = (r - g2.astype(F32)).astype(BF16)
    return g1, g2, g3


def _waves(items, nwaves):
    n = min(nwaves, max(1, len(items) // 8))
    per = len(items) // n
    return [items[w * per:(w + 1) * per] for w in range(n)]


def _cumsum_rows(g, rev):
    row = _iota2((CHUNK, 3 * CHUNK), 0)
    col = jnp.bitwise_and(_iota2((CHUNK, 3 * CHUNK), 1), CHUNK - 1)
    tri = jnp.where((col >= row) if rev else (col <= row), 1.0, 0.0).astype(BF16)
    g3 = jnp.concatenate(_split3(g), axis=0)
    return jnp.dot(tri, g3, preferred_element_type=F32)


def _hg_prepare(items):
    for it in items:
        lb = it["lb"]
        sg = _sigmoid(it["fz"])
        it["g"] = jnp.log(jnp.maximum(lb + (1.0 - lb) * sg, HG_F_MIN))
        it["kk"] = (1.0 - lb) * (1.0 - sg)
    for it in items:
        it["bc"] = _cumsum_rows(it["g"], it["rev"])
    half = CHUNK // 2
    for it in items:
        bc = it["bc"]
        if it["rev"]:
            mid, last = bc[half:half + 1], bc[0:1]
        else:
            mid, last = bc[half - 1:half], bc[CHUNK - 1:CHUNK]
        it["e_last"] = jnp.exp(last)
        kd = it["kk"] * jnp.exp(mid - bc)
        it["ks"] = kd * jnp.exp(last - mid)
        if it["q"] is not None:
            qd = _silu(it["q"]) * jnp.exp(bc - mid)
            it["qd"], it["kd"] = qd, kd
            it["qs"] = qd * jnp.exp(mid)
    r = _iota2((CHUNK, CHUNK), 0)
    c = _iota2((CHUNK, CHUNK), 1)
    for it in items:
        it["kv"] = _dot_tn(it["v"], it["ks"])
        if it["q"] is not None:
            att = _dot_nt(it["qd"], it["kd"])
            it["att"] = jnp.where((c >= r) if it["rev"] else (c <= r), att, 0.0)
    for it in items:
        if it["q"] is not None:
            it["o_c"] = _dot(it["att"], it["v"])


def _hg_kernel(*refs, nchunk, with_output):
    if with_output:
        (ff_ref, vf_ref, qf_ref, fb_ref, vb_ref, qb_ref, lbf_ref, lbb_ref, s0f_ref, s0b_ref,
         of_ref, ob_ref, sf_ref, sb_ref, stf, stb) = refs
    else:
        (ff_ref, vf_ref, fb_ref, vb_ref, lbf_ref, lbb_ref, s0f_ref, s0b_ref,
         sf_ref, sb_ref, stf, stb) = refs
        qf_ref = qb_ref = of_ref = ob_ref = None

    @pl.when(pl.program_id(2) == 0)
    def _():
        stf[...] = s0f_ref[...]
        stb[...] = s0b_ref[...]

    def chunk_item(f_ref, v_ref, q_ref, lb_ref, ci, rev):
        sl = pl.ds(ci * CHUNK, CHUNK)
        return {"fz": f_ref[sl, :], "v": v_ref[sl, :], "q": q_ref[sl, :] if with_output else None,
                "lb": lb_ref[...], "rev": rev, "sl": sl}

    items = []
    for ci in range(nchunk):
        items.append(chunk_item(ff_ref, vf_ref, qf_ref, lbf_ref, ci, False))
        items.append(chunk_item(fb_ref, vb_ref, qb_ref, lbb_ref, nchunk - 1 - ci, True))
    state = {False: stf[...], True: stb[...]}
    for wave in _waves(items, 8):
        _hg_prepare(wave)
        for it in wave:
            st = state[it["rev"]]
            if with_output:
                o_ref = ob_ref if it["rev"] else of_ref
                o_ref[it["sl"], :] = it["o_c"] + _dot_nt(it["qs"], st)
            state[it["rev"]] = st * it["e_last"] + it["kv"]
    stf[...] = state[False]
    stb[...] = state[True]
    sf_ref[...] = state[False]
    sb_ref[...] = state[True]


def _hgrn2_scan(h, cols, lb_f, lb_b, s0, *, tblk, with_output):
    bsz, l, _ = h.shape
    nt = l // tblk
    c_ff, c_fb, c_i, c_q = (c // LANE for c in cols)

    def fwd(cb):
        return pl.BlockSpec((None, tblk, LANE), lambda b, hh, t: (b, t, cb + hh))

    def bwd(cb):
        return pl.BlockSpec((None, tblk, LANE), lambda b, hh, t: (b, nt - 1 - t, cb + hh))

    lb_spec = pl.BlockSpec((1, LANE), lambda b, hh, t: (0, hh))
    st_spec = pl.BlockSpec((None, None, HEAD_DIM, HEAD_DIM), lambda b, hh, t: (b, hh, 0, 0))
    if with_output:
        in_specs = [fwd(c_ff), fwd(c_i), fwd(c_q), bwd(c_fb), bwd(c_i), bwd(c_q)]
        args = [h] * 6
    else:
        in_specs = [fwd(c_ff), fwd(c_i), bwd(c_fb), bwd(c_i)]
        args = [h] * 4
    in_specs += [lb_spec, lb_spec, st_spec, st_spec]
    args += [lb_f, lb_b, s0[0], s0[1]]
    st_shape = jax.ShapeDtypeStruct((bsz, HEADS, HEAD_DIM, HEAD_DIM), F32)
    out_specs = [st_spec, st_spec]
    out_shape = [st_shape, st_shape]
    if with_output:
        o_shape = jax.ShapeDtypeStruct((bsz, l, HEADS * HEAD_DIM), F32)
        out_specs = [pl.BlockSpec((None, tblk, LANE), lambda b, hh, t: (b, t, hh)),
                     pl.BlockSpec((None, tblk, LANE), lambda b, hh, t: (b, nt - 1 - t, hh))] + out_specs
        out_shape = [o_shape, o_shape] + out_shape
    outs = pl.pallas_call(
        functools.partial(_hg_kernel, nchunk=tblk // CHUNK, with_output=with_output),
        name="hgrn2_scan",
        grid=(bsz, HEADS, nt),
        in_specs=in_specs,
        out_specs=out_specs,
        out_shape=out_shape,
        scratch_shapes=[pltpu.VMEM((HEAD_DIM, HEAD_DIM), F32), pltpu.VMEM((HEAD_DIM, HEAD_DIM), F32)],
        compiler_params=_cparams(("parallel", "parallel", "arbitrary")),
    )(*args)
    if with_output:
        return outs[0], outs[1], (outs[2], outs[3])
    return None, None, (outs[0], outs[1])


def _l2norm(t):
    return t * lax.rsqrt(jnp.sum(t * t, axis=-1, keepdims=True) + 1e-6)


def _softplus(x):
    return jnp.maximum(x, 0.0) + jnp.log(1.0 + jnp.exp(-jnp.abs(x)))


def _dn_prepare(items):
    r = _iota2((CHUNK, CHUNK), 0)
    c = _iota2((CHUNK, CHUNK), 1)
    eye = r == c
    for it in items:
        rev = it["rev"]
        incl = (c >= r) if rev else (c <= r)
        incl_t = (r >= c) if rev else (r <= c)
        g_b = jnp.broadcast_to(it["g"], (CHUNK, CHUNK))
        gc_col = jnp.sum(jnp.where(incl, g_b, 0.0), axis=1, keepdims=True)
        g_col = jnp.sum(jnp.where(eye, g_b, 0.0), axis=1, keepdims=True)
        gc_row = jnp.sum(jnp.where(incl_t, jnp.broadcast_to(g_col, (CHUNK, CHUNK)), 0.0), axis=0, keepdims=True)
        beta_col = jnp.sum(jnp.where(eye, jnp.broadcast_to(it["beta"], (CHUNK, CHUNK)), 0.0), axis=1,
                           keepdims=True)
        it["decay"] = jnp.where(incl, jnp.exp(jnp.where(incl, gc_col - gc_row, 0.0)), 0.0)
        g_last = gc_col[0:1] if rev else gc_col[CHUNK - 1:CHUNK]
        it["e_last"] = jnp.exp(g_last)
        egc = jnp.exp(gc_col)
        kb = it["k"] * beta_col
        it["kb"] = kb
        it["rhs"] = jnp.concatenate([it["v"] * beta_col, kb * egc], axis=1)
        it["kd"] = it["k"] * jnp.exp(g_last - gc_col)
        if it["q"] is not None:
            it["qe"] = it["q"] * egc
    for it in items:
        if it["q"] is not None:
            both = _dot_nt(jnp.concatenate([it["kb"], it["q"]], axis=0), it["k"])
            it["kk"], it["qk"] = both[:CHUNK], both[CHUNK:] * it["decay"]
        else:
            it["kk"] = _dot_nt(it["kb"], it["k"])
    for it in items:
        strict = (c > r) if it["rev"] else (c < r)
        n = jnp.where(strict, it["kk"] * it["decay"], 0.0)
        it["n"] = n
        it["p"] = jnp.where(eye, 1.0, 0.0) - n
    for it in items:
        it["m"] = _dot(it["n"], it["n"])
    for i in range(5):
        for it in items:
            if i < 4:
                pm = _dot(jnp.concatenate([it["p"], it["m"]], axis=0), it["m"])
                it["p"] = it["p"] + pm[:CHUNK]
                it["m"] = pm[CHUNK:]
            else:
                it["p"] = it["p"] + _dot(it["p"], it["m"])
    for it in items:
        it["uw"] = _dot(it["p"], it["rhs"])
    for it in items:
        bm = _dot_tn(it["kd"], it["uw"])
        it["b_c"], it["m_c"] = bm[:, :HEAD_DIM], bm[:, HEAD_DIM:]
        if it["q"] is not None:
            ow = _dot(it["qk"], it["uw"])
            it["o_c"], it["q_eff"] = ow[:, :HEAD_DIM], it["qe"] - ow[:, HEAD_DIM:]


def _dn_kernel(*refs, nchunk, with_output):
    if with_output:
        (alog_ref, dtb_ref, kf_ref, vf_ref, qf_ref, abf_ref, kb_ref, vb_ref, qb_ref, abb_ref,
         s0f_ref, s0b_ref, of_ref, ob_ref, sf_ref, sb_ref, stf, stb) = refs
    else:
        (alog_ref, dtb_ref, kf_ref, vf_ref, abf_ref, kb_ref, vb_ref, abb_ref,
         s0f_ref, s0b_ref, sf_ref, sb_ref, stf, stb) = refs
        qf_ref = qb_ref = of_ref = ob_ref = None
    hh = pl.program_id(1)

    @pl.when(pl.program_id(2) == 0)
    def _():
        stf[...] = s0f_ref[...]
        stb[...] = s0b_ref[...]

    def prep(k_ref, v_ref, q_ref, ab_ref, d):
        k = _l2norm(_silu(k_ref[...]))
        v = _silu(v_ref[...])
        q = None
        if with_output:
            q = _l2norm(_silu(q_ref[...])) * (HEAD_DIM ** -0.5)
        a = ab_ref[pl.ds(d * HEADS + hh, 1), :]
        bt = ab_ref[pl.ds(2 * HEADS + d * HEADS + hh, 1), :]
        g = -jnp.exp(alog_ref[d, hh]) * _softplus(a + dtb_ref[d, hh])
        return k, v, q, g, _sigmoid(bt)

    kf, vf, qf, gf, bf = prep(kf_ref, vf_ref, qf_ref, abf_ref, 0)
    kb, vb, qb, gb, bb = prep(kb_ref, vb_ref, qb_ref, abb_ref, 1)
    def chunk_item(k, v, q, g, beta, ci, rev):
        lo = ci * CHUNK
        return {"k": k[lo:lo + CHUNK], "v": v[lo:lo + CHUNK], "q": q[lo:lo + CHUNK] if with_output else None,
                "g": g[:, lo:lo + CHUNK], "beta": beta[:, lo:lo + CHUNK], "rev": rev, "lo": lo}

    items = []
    for ci in range(nchunk):
        items.append(chunk_item(kf, vf, qf, gf, bf, ci, False))
        items.append(chunk_item(kb, vb, qb, gb, bb, nchunk - 1 - ci, True))
    state = {False: stf[...], True: stb[...]}
    for wave in _waves(items, 4):
        _dn_prepare(wave)
        for it in wave:
            s = state[it["rev"]]
            if with_output:
                o_ref = ob_ref if it["rev"] else of_ref
                o_ref[pl.ds(it["lo"], CHUNK), :] = it["o_c"] + _dot(it["q_eff"], s)
            state[it["rev"]] = s * it["e_last"] + it["b_c"] - _dot(it["m_c"], s)
    stf[...] = state[False]
    stb[...] = state[True]
    sf_ref[...] = state[False]
    sb_ref[...] = state[True]


def _deltanet_scan(h, ab_t, cols, a_log, dt_bias, s0, *, tblk, with_output):
    bsz, l, _ = h.shape
    nt = l // tblk
    c_k, c_v, c_q = (c // LANE for c in cols)
    rep = HEADS // DN_QK_HEADS

    def tok(cb, shared, rev):
        def imap(b, hh, t):
            return (b, (nt - 1 - t) if rev else t, cb + (hh // rep if shared else hh))
        return pl.BlockSpec((None, tblk, LANE), imap)

    def ab_spec(rev):
        return pl.BlockSpec((None, 4 * HEADS, tblk), lambda b, hh, t: (b, 0, (nt - 1 - t) if rev else t))

    smem = pl.BlockSpec(memory_space=pltpu.SMEM)
    st_spec = pl.BlockSpec((None, None, HEAD_DIM, HEAD_DIM), lambda b, hh, t: (b, hh, 0, 0))
    if with_output:
        in_specs = [smem, smem, tok(c_k, True, False), tok(c_v, False, False), tok(c_q, True, False), ab_spec(False),
                    tok(c_k, True, True), tok(c_v, False, True), tok(c_q, True, True), ab_spec(True),
                    st_spec, st_spec]
        args = [a_log, dt_bias, h, h, h, ab_t, h, h, h, ab_t, s0[0], s0[1]]
    else:
        in_specs = [smem, smem, tok(c_k, True, False), tok(c_v, False, False), ab_spec(False),
                    tok(c_k, True, True), tok(c_v, False, True), ab_spec(True),
                    st_spec, st_spec]
        args = [a_log, dt_bias, h, h, ab_t, h, h, ab_t, s0[0], s0[1]]
    st_shape = jax.ShapeDtypeStruct((bsz, HEADS, HEAD_DIM, HEAD_DIM), F32)
    out_specs = [st_spec, st_spec]
    out_shape = [st_shape, st_shape]
    if with_output:
        o_shape = jax.ShapeDtypeStruct((bsz, l, HEADS * HEAD_DIM), F32)
        out_specs = [pl.BlockSpec((None, tblk, LANE), lambda b, hh, t: (b, t, hh)),
                     pl.BlockSpec((None, tblk, LANE), lambda b, hh, t: (b, nt - 1 - t, hh))] + out_specs
        out_shape = [o_shape, o_shape] + out_shape
    outs = pl.pallas_call(
        functools.partial(_dn_kernel, nchunk=tblk // CHUNK, with_output=with_output),
        name="deltanet_scan",
        grid=(bsz, HEADS, nt),
        in_specs=in_specs,
        out_specs=out_specs,
        out_shape=out_shape,
        scratch_shapes=[pltpu.VMEM((HEAD_DIM, HEAD_DIM), F32), pltpu.VMEM((HEAD_DIM, HEAD_DIM), F32)],
        compiler_params=_cparams(("parallel", "parallel", "arbitrary")),
    )(*args)
    if with_output:
        return outs[0], outs[1], (outs[2], outs[3])
    return None, None, (outs[0], outs[1])


def _combine_kernel(of_ref, ob_ref, z_ref, w_ref, o_ref, *, use_silu):
    o = of_ref[...] + ob_ref[...]
    o = o * lax.rsqrt(jnp.mean(o * o, axis=-1, keepdims=True) + RMS_EPS) * w_ref[...]
    z = z_ref[...]
    o_ref[...] = (o * (_silu(z) if use_silu else _sigmoid(z))).astype(o_ref.dtype)


def _combine(o_f, o_b, h, col_z, norm_w, *, tl, use_silu):
    bsz, l, c = o_f.shape
    cz = col_z // LANE
    spec = pl.BlockSpec((None, tl, LANE), lambda b, i, hh: (b, i, hh))
    return pl.pallas_call(
        functools.partial(_combine_kernel, use_silu=use_silu),
        name="mixer_norm_gate",
        grid=(bsz, l // tl, c // LANE),
        in_specs=[spec, spec,
                  pl.BlockSpec((None, tl, LANE), lambda b, i, hh: (b, i, cz + hh)),
                  pl.BlockSpec((1, LANE), lambda b, i, hh: (0, 0))],
        out_specs=spec,
        out_shape=jax.ShapeDtypeStruct((bsz, l, c), BF16),
        compiler_params=_cparams(("parallel", "parallel", "arbitrary")),
    )(o_f, o_b, h, norm_w)


def _dot_f32(a, b):
    return jnp.dot(a, b, precision=lax.Precision.HIGHEST, preferred_element_type=F32)


def _dot_3pass(a, b):
    a1 = a.astype(BF16)
    a2 = (a - a1.astype(F32)).astype(BF16)
    b1 = b.astype(BF16)
    b2 = (b - b1.astype(F32)).astype(BF16)
    return jnp.dot(jnp.concatenate([a1, a1, a2], axis=1), jnp.concatenate([b1, b2, b1], axis=0),
                   preferred_element_type=F32)


def _hy_filter_stage_a_kernel(w1_ref, b1_ref, w2_ref, b2_ref, w3_ref, t_ref, o_ref, *, l_total, n1, n2, ct):
    n2h = n2 // 2
    rows = 8 * n2h
    ns = ct // LANE
    g = pl.program_id(0)

    def lag(shape):
        r = _iota2(shape, 0)
        return 8 * g + jnp.right_shift(r, n2h.bit_length() - 1) + n1 * jnp.bitwise_and(r, n2h - 1)

    pos = lag((rows, LANE)).astype(F32)
    lane = _iota2((rows, LANE), 1)
    bidx = jnp.where(lane <= HY_BANDS, lane - 1, lane - 1 - HY_BANDS).astype(F32)
    arg = (1e-4 + bidx * ((HY_BANDS - 1 - 1e-4) / (HY_BANDS - 1))) * (pos * (2.0 * math.pi / l_total))
    feats = jnp.where(lane == 0, pos * (1.0 / (l_total - 1)),
                      jnp.where(lane <= HY_BANDS, jnp.cos(arg),
                                jnp.where(lane <= 2 * HY_BANDS, -jnp.sin(arg), 0.0)))
    hid = jnp.sin(_dot_f32(feats, w1_ref[...]) + b1_ref[...])
    hid = jnp.sin(_dot_f32(hid, w2_ref[...]) + b2_ref[...])
    lag_c = lag((rows, 1))
    t_c = lag_c.astype(F32) * (1.0 / (l_total - 1))
    lo = math.log(HY_DECAY_TARGET) / HY_SLOW_DECAY
    hi = math.log(HY_DECAY_TARGET) / HY_FAST_DECAY
    nslab = BRANCH_WIDTH // LANE
    o2 = _rows2d(o_ref)
    for o in range(HY_ORDER):
        for c in range(BRANCH_WIDTH // ct):
            ch = (_iota2((1, ct), 1) + c * ct).astype(F32)
            window = jnp.exp(-t_c * jnp.abs(lo + ch * ((hi - lo) / (BRANCH_WIDTH - 1))))
            col_f = (2 * o) * BRANCH_WIDTH + c * ct
            col_b = col_f + BRANCH_WIDTH
            h_f = _dot_3pass(hid, w3_ref[:, col_f:col_f + ct]) * window
            h_b = jnp.where(lag_c == 0, 0.0, _dot_3pass(hid, w3_ref[:, col_b:col_b + ct]) * window)
            for j in range(8):
                x = jnp.concatenate([h_f[j * n2h:(j + 1) * n2h], h_b[j * n2h:(j + 1) * n2h]], axis=0).astype(BF16)
                y = jnp.dot(t_ref[j], x, preferred_element_type=F32)
                for p in range(2):
                    for s in range(ns):
                        group = (((o * 2 + p) * nslab + c * ns + s) * n2)
                        o2[pl.ds(group * 8 + j, n2, stride=8), :] = y[p * n2:(p + 1) * n2, s * LANE:(s + 1) * LANE]


def _hy_filter_stage_a(l_total, w1p, b1, w2, b2, w3, tab, *, ct, n1, n2):
    n2h = n2 // 2
    nslab = BRANCH_WIDTH // LANE
    full = lambda a: pl.BlockSpec(a.shape, lambda g: (0,) * a.ndim)
    return pl.pallas_call(
        functools.partial(_hy_filter_stage_a_kernel, l_total=l_total, n1=n1, n2=n2, ct=ct),
        name="hy_filter_stage_a",
        grid=(n1 // 8,),
        in_specs=[full(w1p), full(b1), full(w2), full(b2), full(w3),
                  pl.BlockSpec((8, 2 * n2, 2 * n2h), lambda g: (g, 0, 0))],
        out_specs=pl.BlockSpec((HY_ORDER, 2, nslab, n2, 8, LANE), lambda g: (0, 0, 0, 0, g, 0)),
        out_shape=jax.ShapeDtypeStruct((HY_ORDER, 2, nslab, n2, n1, LANE), F32),
        compiler_params=_cparams(("arbitrary",)),
    )(w1p, b1, w2, b2, w3, tab)


@functools.lru_cache(maxsize=None)
def _fft_tables(n1, n2):
    n = n1 * n2
    n2h = n2 // 2
    k2 = np.arange(n2, dtype=np.int64)[:, None]
    m = np.arange(n2h, dtype=np.int64)[None, :]
    ta = np.zeros((n1, 2 * n2, 2 * n2h))
    for a in range(n1):
        th = 2.0 * np.pi * ((k2 * (a + n1 * m)) % n) / n
        c, s = np.cos(th), np.sin(th)
        ta[a] = np.block([[c, s], [-s, c]])
    tai = np.transpose(ta, (0, 2, 1)) / n
    k1 = np.arange(n1, dtype=np.int64)
    ph = 2.0 * np.pi * ((k1[:, None] * k1[None, :]) % n1) / n1
    c, s = np.cos(ph), np.sin(ph)
    fb = np.block([[c, s], [-s, c]])
    fs = fb[:n1] + fb[n1:]
    fsr = np.stack([fs[(-k1 - e) % n1] for e in (0, 1)])
    return tuple(np.asarray(t, np.float32) for t in (ta, tai, fb, fb.T, fs, fsr))


def _fft_split(l_total):
    n = 2 * l_total
    n1 = min(128, l_total // 8)
    return n1, n // n1


def _rows2d(ref):
    rows = 1
    for s in ref.shape[:-1]:
        rows *= s
    return ref.reshape(rows, ref.shape[-1])


def _lane_slabs(shape, imap, ns):
    def slab(s):
        def index_map(*a):
            idx = imap(*a)
            return idx[:-1] + (idx[-1] * ns + s,)
        return pl.BlockSpec(shape + (LANE,), index_map)
    return [slab(s) for s in range(ns)]


def _strided_rows(ref2, group, n, j):
    return ref2[pl.ds(group * 8 + j, n, stride=8), :]


def _hy_stage_a_kernel(gr_ref, cr_ref, gi_ref, ci_ref, *refs, ns, n2, n2h, slab_major_in):
    del gr_ref, cr_ref, gi_ref, ci_ref
    if slab_major_in:
        xs = [[(_rows2d(r), s * n2h) for s in range(ns)] for r in refs[:2]]
        refs = refs[2:]
    else:
        xs = [[(_rows2d(r), 0) for r in refs[q * ns:(q + 1) * ns]] for q in range(2)]
        refs = refs[2 * ns:]
    t_ref, o_ref = refs
    o2 = _rows2d(o_ref)
    for j in range(8):
        x = jnp.concatenate([jnp.concatenate([_strided_rows(r2, grp, n2h, j) for r2, grp in part], axis=1)
                             for part in xs], axis=0).astype(BF16)
        y = jnp.dot(t_ref[j], x, preferred_element_type=F32)
        for p in range(2):
            for s in range(ns):
                o2[pl.ds(((p * ns + s) * n2) * 8 + j, n2, stride=8), :] = y[p * n2:(p + 1) * n2, s * LANE:(s + 1) * LANE]


def _hy_stage_a(x, pairs, tab, *, ct, n2, slab_major_in):
    n2h = n2 // 2
    n1 = x.shape[-2] if slab_major_in else x.shape[2]
    npair = len(pairs)
    nct = BRANCH_WIDTH // ct
    ns = ct // LANE
    gr = jnp.asarray([p[0][0] for p in pairs], jnp.int32)
    cr = jnp.asarray([p[0][1] for p in pairs], jnp.int32)
    gi = jnp.asarray([p[1][0] for p in pairs], jnp.int32)
    ci = jnp.asarray([p[1][1] for p in pairs], jnp.int32)
    if slab_major_in:
        in_specs = [pl.BlockSpec((None, ns, n2h, 8, LANE), lambda p, c, g, *_: (0, c, 0, g, 0)),
                    pl.BlockSpec((None, ns, n2h, 8, LANE), lambda p, c, g, *_: (1, c, 0, g, 0))]
        args = [x, x]
    else:
        in_specs = (_lane_slabs((None, n2h, 8), lambda p, c, g, gr, cr, gi, ci: (gr[p], 0, g, cr[p] * nct + c), ns)
                    + _lane_slabs((None, n2h, 8), lambda p, c, g, gr, cr, gi, ci: (gi[p], 0, g, ci[p] * nct + c), ns))
        args = [x] * (2 * ns)
    grid_spec = pltpu.PrefetchScalarGridSpec(
        num_scalar_prefetch=4,
        grid=(npair, nct, n1 // 8),
        in_specs=in_specs + [pl.BlockSpec((8, 2 * n2, 2 * n2h), lambda p, c, g, *_: (g, 0, 0))],
        out_specs=pl.BlockSpec((None, 2, ns, n2, 8, LANE), lambda p, c, g, *_: (p, 0, c, 0, g, 0)),
    )
    return pl.pallas_call(
        functools.partial(_hy_stage_a_kernel, ns=ns, n2=n2, n2h=n2h, slab_major_in=slab_major_in),
        name="hy_stage_a",
        grid_spec=grid_spec,
        out_shape=jax.ShapeDtypeStruct((npair, 2, BRANCH_WIDTH // LANE, n2, n1, LANE), F32),
        compiler_params=_cparams(("parallel", "parallel", "arbitrary")),
    )(gr, cr, gi, ci, *args, tab)


def _hy_spec_kernel(c_ref, cm_ref, c0_ref, fs_ref, fsr_ref, o_ref, *, kg):
    g = pl.program_id(2)
    for k in range(kg):
        s = jnp.dot(fs_ref[...], _complex_slab(c_ref, k), preferred_element_type=F32)
        if k:
            mirror, fsr = _complex_slab(cm_ref, kg - k), fsr_ref[1]
        else:
            mirror, fsr = _complex_slab(c0_ref, 0), jnp.where(g == 0, fsr_ref[0], fsr_ref[1])
        t = jnp.dot(fsr, mirror, preferred_element_type=F32)
        o_ref[0, k] = 0.5 * (s + t)
        o_ref[1, k] = 0.5 * (s - t)


def _complex_slab(ref, k):
    ns = ref.shape[1]
    return jnp.concatenate([jnp.concatenate([ref[p, s, k] for s in range(ns)], axis=1) for p in range(2)],
                           axis=0).astype(BF16)


def _hy_spectrum(cs, fs, fsr, *, ct, kg):
    no, _, nslab, n2, n1, _ = cs.shape
    ns = ct // LANE
    ng = n2 // kg
    return pl.pallas_call(
        functools.partial(_hy_spec_kernel, kg=kg),
        name="hy_spectrum",
        grid=(no, nslab // ns, ng),
        in_specs=[pl.BlockSpec((None, 2, ns, kg, n1, LANE), lambda o, j, g: (o, 0, j, g, 0, 0)),
                  pl.BlockSpec((None, 2, ns, kg, n1, LANE), lambda o, j, g: (o, 0, j, ng - 1 - g, 0, 0)),
                  pl.BlockSpec((None, 2, ns, 1, n1, LANE), lambda o, j, g: (o, 0, j, (kg * (ng - g)) % n2, 0, 0)),
                  pl.BlockSpec(fs.shape, lambda o, j, g: (0, 0)),
                  pl.BlockSpec(fsr.shape, lambda o, j, g: (0, 0, 0))],
        out_specs=pl.BlockSpec((None, 2, kg, n1, ct), lambda o, j, g: (o, 0, g, 0, j)),
        out_shape=jax.ShapeDtypeStruct((no, 2, n2, n1, nslab * LANE), F32),
        compiler_params=_cparams(("parallel", "parallel", "arbitrary")),
    )(cs, cs, cs, fs, fsr)


def _hy_stage_b_kernel(c_ref, k_ref, fb_ref, fbi_ref, o_ref, *, kg, n1):
    ns = c_ref.shape[1]
    for k in range(kg):
        z = jnp.dot(fb_ref[...], _complex_slab(c_ref, k), preferred_element_type=F32)
        zr, zi = z[:n1], z[n1:]
        kr, ki = k_ref[0, k], k_ref[1, k]
        y = jnp.concatenate([zr * kr - zi * ki, zr * ki + zi * kr], axis=0).astype(BF16)
        d = jnp.dot(fbi_ref[...], y, preferred_element_type=F32)
        for p in range(2):
            for s in range(ns):
                o_ref[p, s, k] = d[p * n1:(p + 1) * n1, s * LANE:(s + 1) * LANE]


def _hy_stage_b(cs, spec, order, fb, fbi, *, ct, kg):
    _, _, nslab, n2, n1, _ = cs.shape
    ns = ct // LANE
    c = nslab * LANE
    return pl.pallas_call(
        functools.partial(_hy_stage_b_kernel, kg=kg, n1=n1),
        name="hy_stage_b",
        grid=(c // ct, n2 // kg),
        in_specs=[pl.BlockSpec((None, 2, ns, kg, n1, LANE), lambda j, g: (0, 0, j, g, 0, 0)),
                  pl.BlockSpec((None, 2, kg, n1, ct), lambda j, g: (order, 0, g, 0, j)),
                  pl.BlockSpec(fb.shape, lambda j, g: (0, 0)),
                  pl.BlockSpec(fbi.shape, lambda j, g: (0, 0))],
        out_specs=pl.BlockSpec((2, ns, kg, n1, LANE), lambda j, g: (0, j, g, 0, 0)),
        out_shape=jax.ShapeDtypeStruct((2, nslab, n2, n1, LANE), F32),
        compiler_params=_cparams(("parallel", "arbitrary")),
    )(cs, spec, fb, fbi)


def _hy_stage_c_kernel(*refs, ns, n2, n2h, slab_major_u):
    d2 = _rows2d(refs[0])
    t_ref = refs[1]
    if slab_major_u:
        u2 = _rows2d(refs[2])
        u_parts = [[(u2, (b * ns + s) * n2h) for s in range(ns)] for b in range(2)]
        refs = refs[3:]
    else:
        u_parts = [[(_rows2d(r), b * n2h) for r in refs[2:2 + ns]] for b in range(2)]
        refs = refs[2 + ns:]
    g_parts = [[(_rows2d(r), b * n2h) for r in refs[:ns]] for b in range(2)]
    skip_ref, o_ref = refs[ns:]
    o2 = _rows2d(o_ref)
    skip = skip_ref[...]

    def rows(parts, j):
        return jnp.concatenate([jnp.concatenate([_strided_rows(r2, grp, n2h, j) for r2, grp in part], axis=1)
                                for part in parts], axis=0)

    for j in range(8):
        d = jnp.concatenate([jnp.concatenate([_strided_rows(d2, (p * ns + s) * n2, n2, j) for s in range(ns)], axis=1)
                             for p in range(2)], axis=0).astype(BF16)
        y = jnp.dot(t_ref[j], d, preferred_element_type=F32)
        out = rows(g_parts, j) * (y + skip * rows(u_parts, j))
        for b in range(2):
            for s in range(ns):
                o2[pl.ds(((b * ns + s) * n2h) * 8 + j, n2h, stride=8), :] = (
                    out[b * n2h:(b + 1) * n2h, s * LANE:(s + 1) * LANE])


def _hy_stage_c(d, tai, u, ucol, z4, gcol, skip, *, ct, slab_major_u):
    _, nslab, n2, n1, _ = d.shape
    n2h = n2 // 2
    ns = ct // LANE
    uc, gc = ucol // ct, gcol // ct
    if slab_major_u:
        u_specs = [pl.BlockSpec((2, ns, n2h, 8, LANE), lambda j, g: (0, j, 0, g, 0))]
        u_args = [u]
    else:
        u_specs = _lane_slabs((2, n2h, 8), lambda j, g: (0, 0, g, uc + j), ns)
        u_args = [u] * ns
    return pl.pallas_call(
        functools.partial(_hy_stage_c_kernel, ns=ns, n2=n2, n2h=n2h, slab_major_u=slab_major_u),
        name="hy_stage_c",
        grid=(nslab // ns, n1 // 8),
        in_specs=([pl.BlockSpec((2, ns, n2, 8, LANE), lambda j, g: (0, j, 0, g, 0)),
                   pl.BlockSpec((8, 2 * n2h, 2 * n2), lambda j, g: (g, 0, 0))]
                  + u_specs + _lane_slabs((2, n2h, 8), lambda j, g: (0, 0, g, gc + j), ns)
                  + [pl.BlockSpec((1, ct), lambda j, g: (0, j))]),
        out_specs=pl.BlockSpec((2, ns, n2h, 8, LANE), lambda j, g: (0, j, 0, g, 0)),
        out_shape=jax.ShapeDtypeStruct((2, nslab, n2h, n1, LANE), F32),
        compiler_params=_cparams(("parallel", "arbitrary")),
    )(d, tai, *u_args, *([z4] * ns), skip)


def _hyena(h, filt_params, skip):
    bsz, l, _ = h.shape
    assert bsz == 2
    n1, n2 = _fft_split(l)
    n2h = n2 // 2
    ta, tai, fb, fbi, fs, fsr = (jnp.asarray(t).astype(BF16) for t in _fft_tables(n1, n2))
    ct = 512
    kg = 8

    cs_k = _hy_filter_stage_a(l, *filt_params, ta, ct=ct, n1=n1, n2=n2)
    spec = _hy_spectrum(cs_k, fs, fsr, ct=ct, kg=kg)

    z4 = h.reshape(bsz, n2h, n1, -1)
    u, ucol = z4, COL_HY
    for o in range(HY_ORDER):
        first = o == 0
        cs = _hy_stage_a(u, [((0, ucol // BRANCH_WIDTH), (1, ucol // BRANCH_WIDTH))], ta, ct=ct, n2=n2,
                         slab_major_in=not first)
        d = _hy_stage_b(cs, spec, o, fb, fbi, ct=ct, kg=kg)
        u = _hy_stage_c(d, tai, u, ucol, z4, COL_HY + (o + 1) * BRANCH_WIDTH, skip[o:o + 1], ct=ct,
                        slab_major_u=not first)
        ucol = 0
    return u.reshape(bsz, BRANCH_WIDTH // LANE, l, LANE)


def _conv_tables(hy_conv_w, hy_conv_b, dn_conv_q, dn_conv_k, dn_conv_v):
    depth = hy_conv_w.shape[0]
    ident = lambda n: jnp.broadcast_to(jnp.asarray([0.0, 1.0, 0.0], F32)[None, :, None], (depth, 3, n))
    taps = jnp.concatenate([ident(COL_HY), hy_conv_w, ident(COL_DK - COL_FF), dn_conv_k, dn_conv_v,
                            ident(COL_DQ - COL_AB), dn_conv_q, ident(COL_PAD - COL_DQ - dn_conv_q.shape[2])], axis=2)
    bias = jnp.concatenate([jnp.zeros((depth, 1, COL_HY), F32), hy_conv_b[:, None, :],
                            jnp.zeros((depth, 1, COL_PAD - COL_FF), F32)], axis=2)
    return taps, bias


def _token_mixers(u_src, sh, sc, w_t, conv_t, layer, lp, states, *, period, with_output, tm):
    l = lp["seq_len"]
    bsz = u_src.shape[0] * u_src.shape[1] // l
    tblk = min(l, 2048)
    taps, bias = (a[layer:layer + 1] for a in conv_t)
    if with_output:
        h = _in_proj(u_src, sh, sc, w_t, layer, range(COL_PAD // IN_TN), taps, bias, tm=tm, period=period)
        base = 0
    else:
        lo, hi = STATE_BLOCKS[0] * IN_TN, (STATE_BLOCKS[-1] + 1) * IN_TN
        h = _in_proj(u_src, sh, sc, w_t, layer, STATE_BLOCKS, taps[:, :, lo:hi], bias[:, :, lo:hi],
                     tm=tm, period=period)
        base = COL_FF
    h = h.reshape(bsz, l, h.shape[-1])
    col = lambda c: c - base
    ab_t = jnp.swapaxes(h[:, :, col(COL_AB):col(COL_AB) + 4 * HEADS], 1, 2)
    hg_f, hg_b, hg_states = _hgrn2_scan(h, (col(COL_FF), col(COL_FB), col(COL_HI), col(COL_HQ)),
                                        lp["lb_f"], lp["lb_b"], states[0], tblk=tblk, with_output=with_output)
    dn_f, dn_b, dn_states = _deltanet_scan(h, ab_t, (col(COL_DK), col(COL_DV), col(COL_DQ)),
                                           lp["dn_a_log"], lp["dn_dt_bias"], states[1],
                                           tblk=min(l, 2048), with_output=with_output)
    new_states = (hg_states, dn_states)
    if not with_output:
        return None, new_states
    tl = min(l, 2048)
    hg_out = _combine(hg_f, hg_b, h, COL_HGATE, lp["hg_norm_w"], tl=tl, use_silu=False)
    dn_out = _combine(dn_f, dn_b, h, COL_DZ, lp["dn_norm_w"], tl=tl, use_silu=True)
    hy_out = _hyena(h, lp["hy_filt"], lp["hy_skip"])
    ysum = _branch_merge([hy_out, hg_out, dn_out], lp["w_branch"], layer, h, tm=min(l, 1024), tn=512)
    return ysum, new_states


def kernel(x, c, ctx, c_ctx, w_ada, b_ada, w_in, hy_conv_w, hy_conv_b, hy_filt_w1, hy_filt_b1, hy_filt_w2, hy_filt_b2, hy_filt_w3, hy_skip, hg_lb_logits, hg_norm_w, dn_conv_q, dn_conv_k, dn_conv_v, dn_a_log, dn_dt_bias, dn_norm_w, w_branch, w_out, ln1_g, ln1_b, w_ff1, w_ff2, ln2_g, ln2_b):
    depth = w_in.shape[0]
    bsz, _, d = x.shape
    alpha = (2 * depth) ** 0.25
    p = jax.nn.softmax(hg_lb_logits.astype(F32), axis=1)
    lower = jnp.cumsum(p, axis=1) - p[:, :1]
    cs = jnp.concatenate([c, c_ctx[None], jnp.zeros((8 - bsz - 1, d), F32)], axis=0)
    w_r = jnp.swapaxes(w_in, 1, 2)
    w_b, w_o, w_1, w_2 = w_branch.astype(BF16), w_out.astype(BF16), w_ff1, w_ff2.astype(BF16)
    b_ada3 = b_ada[:, None, :]
    conv_t = _conv_tables(hy_conv_w, hy_conv_b, dn_conv_q, dn_conv_k, dn_conv_v)
    lc = ctx.shape[1]
    h_ctx = ctx.reshape(1, bsz * lc, d)
    for l in range(depth):
        last = l == depth - 1
        row = lambda v: v.reshape(1, -1)
        lp = {
            "lb_f": row(lower[0, l]), "lb_b": row(lower[1, l]),
            "hg_norm_w": row(hg_norm_w[l]), "dn_norm_w": row(dn_norm_w[l]),
            "dn_a_log": dn_a_log[l], "dn_dt_bias": dn_dt_bias[l], "hy_skip": hy_skip[l],
            "hy_filt": (jnp.pad(hy_filt_w1[l], ((0, LANE - HY_EMB), (0, 0))), row(hy_filt_b1[l]),
                        hy_filt_w2[l], row(hy_filt_b2[l]), hy_filt_w3[l]),
            "w_branch": w_b,
        }
        g1n, b1n, g2n, b2n = row(ln1_g[l]), row(ln1_b[l]), row(ln2_g[l]), row(ln2_b[l])
        mod = _ada_mod(cs, w_ada, b_ada3, l)
        m_lat = [mod[:bsz, i * d:(i + 1) * d][:, None, :] for i in range(6)]
        m_ctx = [mod[bsz, i * d:(i + 1) * d][None, None, :] for i in range(6)]
        zero = jnp.zeros((bsz, HEADS, HEAD_DIM, HEAD_DIM), F32)
        init = ((zero, zero), (zero, zero))

        rows_c = bsz * lc
        y_ctx, ctx_states = _token_mixers(h_ctx, m_ctx[0], m_ctx[1], w_r, conv_t, l, dict(lp, seq_len=lc), init,
                                          period=lc, with_output=not last, tm=rows_c)
        if not last:
            y_ctx = y_ctx.reshape(1, rows_c, d)
            h_ctx = _mm_ln(y_ctx, w_o, l, h_ctx, m_ctx[2], g1n, b1n, tm=rows_c, tk=d, alpha=alpha)
            mid = _mlp_up(h_ctx, m_ctx[3], m_ctx[4], w_1, l, tm=rows_c, tn=1024)
            h_ctx = _mm_ln(mid, w_2, l, h_ctx, m_ctx[5], g2n, b2n, tm=rows_c, tk=1024, alpha=alpha)

        lx = x.shape[1]
        y, _ = _token_mixers(x, m_lat[0], m_lat[1], w_r, conv_t, l, dict(lp, seq_len=lx), ctx_states,
                             period=GRID_W, with_output=True, tm=min(lx, 1024))
        x = _mm_ln(y, w_o, l, x, m_lat[2], g1n, b1n, tm=min(lx, 512), tk=d, alpha=alpha)
        mid = _mlp_up(x, m_lat[3], m_lat[4], w_1, l, tm=min(lx, 1024), tn=1024)
        x = _mm_ln(mid, w_2, l, x, m_lat[5], g2n, b2n, tm=min(lx, 1024), tk=512, alpha=alpha)
    return x
```

```python
import functools
import math

import numpy as np
import jax
import jax.numpy as jnp
from jax import lax
from jax.experimental import pallas as pl
from jax.experimental.pallas import tpu as pltpu

F32 = jnp.float32
BF16 = jnp.bfloat16

D_MODEL = 2048
GRID_W = 64
BRANCH_WIDTH = 1024
N_BRANCH = 3
HY_ORDER = 2
HY_EMB = 33
HY_BANDS = (HY_EMB - 1) // 2
HY_HIDDEN = 64
HY_DECAY_TARGET = 1e-2
HY_FAST_DECAY = 0.3
HY_SLOW_DECAY = 1.5
HEADS = 8
HEAD_DIM = 128
DN_QK_HEADS = 4
HG_F_MIN = 1e-30
CHUNK = 64
LN_EPS = 1e-5
RMS_EPS = 1e-6
LANE = 128

IN_TN = 1024
COL_GATE = 0
COL_HY = 6144
COL_FF = 9216
COL_FB = 10240
COL_HI = 11264
COL_DK = 12288
COL_DV = 12800
COL_AB = 13824
COL_HQ = 14336
COL_HGATE = 15360
COL_DQ = 16384
COL_DZ = 17408
COL_PAD = 18432
SOURCE_ROWS = np.asarray([11296 + IN_TN * k for k in range(6)] + [8224 + IN_TN * k for k in range(3)]
                         + [0, 1024, 2048, 3072, 4096, 4640, 5664, 6688, 7200], np.int32)
CONV_WINDOWS = np.asarray([COL_HY <= IN_TN * j < COL_FF or COL_DK <= IN_TN * j < COL_HQ or IN_TN * j == COL_DQ
                           for j in range(COL_PAD // IN_TN)], np.int32)
STATE_BLOCKS = tuple(range(COL_FF // IN_TN, COL_HQ // IN_TN))

VMEM_LIMIT = 56 * 1024 * 1024


def _cparams(sem):
    return pltpu.CompilerParams(dimension_semantics=sem, vmem_limit_bytes=VMEM_LIMIT)


def _dot(a, b):
    return jnp.dot(a.astype(BF16), b.astype(BF16), preferred_element_type=F32)


def _dot_nt(a, b):
    return lax.dot_general(a.astype(BF16), b.astype(BF16), (((1,), (1,)), ((), ())),
                           preferred_element_type=F32)


def _dot_tn(a, b):
    return lax.dot_general(a.astype(BF16), b.astype(BF16), (((0,), (0,)), ((), ())),
                           preferred_element_type=F32)


def _sigmoid(x):
    return 1.0 / (1.0 + jnp.exp(-x))


def _silu(x):
    return x * _sigmoid(x)


def _ada_kernel(c_ref, w_ref, b_ref, o_ref):
    a = _silu(c_ref[...])
    o_ref[...] = _dot(a, w_ref[...]) + b_ref[...]


def _ada_mod(cs, w, b, layer):
    _, d, n = w.shape
    tn = 1024
    return pl.pallas_call(
        _ada_kernel,
        grid=(n // tn,),
        in_specs=[pl.BlockSpec((8, d), lambda j: (0, 0)),
                  pl.BlockSpec((None, d, tn), lambda j: (layer, 0, j)),
                  pl.BlockSpec((None, 1, tn), lambda j: (layer, 0, j))],
        out_specs=pl.BlockSpec((8, tn), lambda j: (0, j)),
        out_shape=jax.ShapeDtypeStruct((8, n), F32),
        compiler_params=_cparams(("arbitrary",)),
        name="ada_mod",
    )(cs, w, b)


def _conv3(x, w, period):
    t = x.shape[0]
    pos = jnp.bitwise_and(_iota2((t, 1), 0), period - 1)
    has_l, has_r = pos != 0, pos != period - 1
    xl = jnp.where(has_l, pltpu.roll(x, 1, 0), 0.0)
    xr = jnp.where(has_r, pltpu.roll(x, t - 1, 0), 0.0)
    return xl * w[0:1] + x * w[1:2] + xr * w[2:3]


def _mlp_up_kernel(x_ref, sh_ref, sc_ref, w_ref, o_ref, a_scr):
    @pl.when(pl.program_id(2) == 0)
    def _():
        a_scr[...] = (x_ref[...] * (1.0 + sc_ref[...]) + sh_ref[...]).astype(BF16)

    acc = jnp.dot(a_scr[...], w_ref[...].astype(BF16), preferred_element_type=F32)
    o_ref[...] = jnp.square(jnp.maximum(acc, 0.0)).astype(o_ref.dtype)


def _mlp_up(x, sh, sc, w, layer, *, tm, tn):
    bsz, l, d = x.shape
    n = w.shape[2]
    return pl.pallas_call(
        _mlp_up_kernel,
        grid=(bsz, l // tm, n // tn),
        in_specs=[pl.BlockSpec((None, tm, d), lambda b, i, j: (b, i, 0)),
                  pl.BlockSpec((None, 1, d), lambda b, i, j: (b, 0, 0)),
                  pl.BlockSpec((None, 1, d), lambda b, i, j: (b, 0, 0)),
                  pl.BlockSpec((None, d, tn), lambda b, i, j: (layer, 0, j))],
        out_specs=pl.BlockSpec((None, tm, tn), lambda b, i, j: (b, i, j)),
        out_shape=jax.ShapeDtypeStruct((bsz, l, n), BF16),
        scratch_shapes=[pltpu.VMEM((tm, d), BF16)],
        compiler_params=_cparams(("parallel", "parallel", "arbitrary")),
        name="mm_relu2",
    )(x, sh, sc, w)


def _in_proj_kernel(off_ref, conv_ref, x_ref, sh_ref, sc_ref, w_ref, cw_ref, cb_ref, o_ref, a_scr, *, period):
    del off_ref
    has_conv = conv_ref[pl.program_id(2)] != 0

    @pl.when(pl.program_id(2) == 0)
    def _():
        a_scr[...] = (x_ref[...] * (1.0 + sc_ref[...]) + sh_ref[...]).astype(BF16)

    o_ref[...] = lax.dot_general(a_scr[...], w_ref[0].astype(BF16), (((1,), (1,)), ((), ())),
                                 preferred_element_type=F32)

    @pl.when(has_conv)
    def _():
        o_ref[...] = _conv3(o_ref[...], cw_ref[...], period) + cb_ref[...]


def _in_proj(x, sh, sc, w_t, layer, windows, taps, bias, *, tm, period):
    bsz, l, d = x.shape
    nblk = len(windows)
    assert tm % period == 0 and period & (period - 1) == 0
    row_off = jnp.asarray(SOURCE_ROWS[list(windows)])
    has_conv = jnp.asarray(CONV_WINDOWS[list(windows)])
    grid_spec = pltpu.PrefetchScalarGridSpec(
        num_scalar_prefetch=2,
        grid=(bsz, l // tm, nblk),
        in_specs=[pl.BlockSpec((None, tm, d), lambda b, i, j, off, cv: (b, i, 0)),
                  pl.BlockSpec((None, 1, d), lambda b, i, j, off, cv: (b, 0, 0)),
                  pl.BlockSpec((None, 1, d), lambda b, i, j, off, cv: (b, 0, 0)),
                  pl.BlockSpec((pl.Element(1), pl.Element(IN_TN), pl.Element(d)),
                               lambda b, i, j, off, cv: (layer, pl.multiple_of(off[j], 4 * HEADS), 0)),
                  pl.BlockSpec((None, 3, IN_TN), lambda b, i, j, off, cv: (0, 0, j)),
                  pl.BlockSpec((None, 1, IN_TN), lambda b, i, j, off, cv: (0, 0, j))],
        out_specs=pl.BlockSpec((None, tm, IN_TN), lambda b, i, j, off, cv: (b, i, j)),
        scratch_shapes=[pltpu.VMEM((tm, d), BF16)],
    )
    return pl.pallas_call(
        functools.partial(_in_proj_kernel, period=period),
        name="mm_in_proj",
        grid_spec=grid_spec,
        out_shape=jax.ShapeDtypeStruct((bsz, l, nblk * IN_TN), F32),
        compiler_params=_cparams(("parallel", "parallel", "arbitrary")),
    )(row_off, has_conv, x, sh, sc, w_t, taps, bias)


def _mm_ln_kernel(a_ref, w_ref, x_ref, gate_ref, g_ref, b_ref, o_ref, acc_ref, *, nk, alpha):
    k = pl.program_id(2)

    @pl.when(k == 0)
    def _():
        acc_ref[...] = jnp.zeros_like(acc_ref)

    acc_ref[...] += jnp.dot(a_ref[...], w_ref[...].astype(BF16), preferred_element_type=F32)

    @pl.when(k == nk - 1)
    def _():
        y = alpha * x_ref[...] + gate_ref[...] * acc_ref[...]
        mu = jnp.mean(y, axis=-1, keepdims=True)
        yc = y - mu
        var = jnp.mean(yc * yc, axis=-1, keepdims=True)
        o_ref[...] = yc * lax.rsqrt(var + LN_EPS) * g_ref[...] + b_ref[...]


def _mm_ln(a, w, layer, x, gate, ln_g, ln_b, *, tm, tk, alpha):
    bsz, l, kdim = a.shape
    d = w.shape[2]
    nk = kdim // tk
    return pl.pallas_call(
        functools.partial(_mm_ln_kernel, nk=nk, alpha=alpha),
        grid=(bsz, l // tm, nk),
        in_specs=[pl.BlockSpec((None, tm, tk), lambda b, i, k: (b, i, k)),
                  pl.BlockSpec((None, tk, d), lambda b, i, k: (layer, k, 0)),
                  pl.BlockSpec((None, tm, d), lambda b, i, k: (b, i, 0)),
                  pl.BlockSpec((None, 1, d), lambda b, i, k: (b, 0, 0)),
                  pl.BlockSpec((1, d), lambda b, i, k: (0, 0)),
                  pl.BlockSpec((1, d), lambda b, i, k: (0, 0))],
        out_specs=pl.BlockSpec((None, tm, d), lambda b, i, k: (b, i, 0)),
        out_shape=jax.ShapeDtypeStruct((bsz, l, d), F32),
        scratch_shapes=[pltpu.VMEM((tm, d), F32)],
        compiler_params=_cparams(("parallel", "parallel", "arbitrary")),
        name="mm_resid_ln",
    )(a, w, x, gate, ln_g, ln_b)


def _branch_kernel(b0_ref, b1_ref, b2_ref, w_ref, g0_ref, g1_ref, g2_ref, o_ref):
    b0 = jnp.concatenate([b0_ref[s] for s in range(b0_ref.shape[0])], axis=1).astype(BF16)
    acc = _sigmoid(g0_ref[...]) * jnp.dot(b0, w_ref[0].astype(BF16), preferred_element_type=F32)
    acc += _sigmoid(g1_ref[...]) * jnp.dot(b1_ref[...], w_ref[1].astype(BF16), preferred_element_type=F32)
    acc += _sigmoid(g2_ref[...]) * jnp.dot(b2_ref[...], w_ref[2].astype(BF16), preferred_element_type=F32)
    o_ref[...] = acc.astype(o_ref.dtype)


def _branch_merge(branches, w, layer, h, *, tm, tn):
    bsz, l, c = branches[1].shape
    d = w.shape[3]
    nj = d // tn
    br_spec = pl.BlockSpec((None, tm, c), lambda b, i, j: (b, i, 0))
    hy_spec = pl.BlockSpec((None, c // LANE, tm, LANE), lambda b, i, j: (b, 0, i, 0))

    def gate_spec(g):
        return pl.BlockSpec((None, tm, tn), lambda b, i, j: (b, i, COL_GATE // tn + g * nj + j))

    return pl.pallas_call(
        _branch_kernel,
        grid=(bsz, l // tm, nj),
        in_specs=[hy_spec, br_spec, br_spec,
                  pl.BlockSpec((None, N_BRANCH, c, tn), lambda b, i, j: (layer, 0, 0, j)),
                  gate_spec(0), gate_spec(1), gate_spec(2)],
        out_specs=pl.BlockSpec((None, tm, tn), lambda b, i, j: (b, i, j)),
        out_shape=jax.ShapeDtypeStruct((bsz, l, d), BF16),
        compiler_params=_cparams(("parallel", "parallel", "arbitrary")),
        name="branch_merge",
    )(*branches, w, h, h, h)


def _iota2(shape, axis):
    return lax.broadcasted_iota(jnp.int32, shape, axis)


def _split3(g):
    g1 = g.astype(BF16)
    r = g - g1.astype(F32)
    g2 = r.astype(BF16)
    g3 = (r - g2.astype(F32)).astype(BF16)
    return g1, g2, g3


def _waves(items, nwaves):
    n = min(nwaves, max(1, len(items) // 8))
    per = len(items) // n
    return [items[w * per:(w + 1) * per] for w in range(n)]


def _cumsum_rows(g, rev):
    row = _iota2((CHUNK, 3 * CHUNK), 0)
    col = jnp.bitwise_and(_iota2((CHUNK, 3 * CHUNK), 1), CHUNK - 1)
    tri = jnp.where((col >= row) if rev else (col <= row), 1.0, 0.0).astype(BF16)
    g3 = jnp.concatenate(_split3(g), axis=0)
    return jnp.dot(tri, g3, preferred_element_type=F32)


def _hg_prepare(items):
    for it in items:
        lb = it["lb"]
        sg = _sigmoid(it["fz"])
        it["g"] = jnp.log(jnp.maximum(lb + (1.0 - lb) * sg, HG_F_MIN))
        it["kk"] = (1.0 - lb) * (1.0 - sg)
    for it in items:
        it["bc"] = _cumsum_rows(it["g"], it["rev"])
    half = CHUNK // 2
    for it in items:
        bc = it["bc"]
        if it["rev"]:
            mid, last = bc[half:half + 1], bc[0:1]
        else:
            mid, last = bc[half - 1:half], bc[CHUNK - 1:CHUNK]
        it["e_last"] = jnp.exp(last)
        kd = it["kk"] * jnp.exp(mid - bc)
        it["ks"] = kd * jnp.exp(last - mid)
        if it["q"] is not None:
            qd = _silu(it["q"]) * jnp.exp(bc - mid)
            it["qd"], it["kd"] = qd, kd
            it["qs"] = qd * jnp.exp(mid)
    r = _iota2((CHUNK, CHUNK), 0)
    c = _iota2((CHUNK, CHUNK), 1)
    for it in items:
        it["kv"] = _dot_tn(it["v"], it["ks"])
        if it["q"] is not None:
            att = _dot_nt(it["qd"], it["kd"])
            it["att"] = jnp.where((c >= r) if it["rev"] else (c <= r), att, 0.0)
    for it in items:
        if it["q"] is not None:
            it["o_c"] = _dot(it["att"], it["v"])


def _hg_kernel(*refs, nchunk, with_output):
    if with_output:
        (ff_ref, vf_ref, qf_ref, fb_ref, vb_ref, qb_ref, lbf_ref, lbb_ref, s0f_ref, s0b_ref,
         of_ref, ob_ref, sf_ref, sb_ref, stf, stb) = refs
    else:
        (ff_ref, vf_ref, fb_ref, vb_ref, lbf_ref, lbb_ref, s0f_ref, s0b_ref,
         sf_ref, sb_ref, stf, stb) = refs
        qf_ref = qb_ref = of_ref = ob_ref = None

    @pl.when(pl.program_id(2) == 0)
    def _():
        stf[...] = s0f_ref[...]
        stb[...] = s0b_ref[...]

    def chunk_item(f_ref, v_ref, q_ref, lb_ref, ci, rev):
        sl = pl.ds(ci * CHUNK, CHUNK)
        return {"fz": f_ref[sl, :], "v": v_ref[sl, :], "q": q_ref[sl, :] if with_output else None,
                "lb": lb_ref[...], "rev": rev, "sl": sl}

    items = []
    for ci in range(nchunk):
        items.append(chunk_item(ff_ref, vf_ref, qf_ref, lbf_ref, ci, False))
        items.append(chunk_item(fb_ref, vb_ref, qb_ref, lbb_ref, nchunk - 1 - ci, True))
    state = {False: stf[...], True: stb[...]}
    for wave in _waves(items, 8):
        _hg_prepare(wave)
        for it in wave:
            st = state[it["rev"]]
            if with_output:
                o_ref = ob_ref if it["rev"] else of_ref
                o_ref[it["sl"], :] = it["o_c"] + _dot_nt(it["qs"], st)
            state[it["rev"]] = st * it["e_last"] + it["kv"]
    stf[...] = state[False]
    stb[...] = state[True]
    sf_ref[...] = state[False]
    sb_ref[...] = state[True]


def _hgrn2_scan(h, cols, lb_f, lb_b, s0, *, tblk, with_output):
    bsz, l, _ = h.shape
    nt = l // tblk
    c_ff, c_fb, c_i, c_q = (c // LANE for c in cols)

    def fwd(cb):
        return pl.BlockSpec((None, tblk, LANE), lambda b, hh, t: (b, t, cb + hh))

    def bwd(cb):
        return pl.BlockSpec((None, tblk, LANE), lambda b, hh, t: (b, nt - 1 - t, cb + hh))

    lb_spec = pl.BlockSpec((1, LANE), lambda b, hh, t: (0, hh))
    st_spec = pl.BlockSpec((None, None, HEAD_DIM, HEAD_DIM), lambda b, hh, t: (b, hh, 0, 0))
    if with_output:
        in_specs = [fwd(c_ff), fwd(c_i), fwd(c_q), bwd(c_fb), bwd(c_i), bwd(c_q)]
        args = [h] * 6
    else:
        in_specs = [fwd(c_ff), fwd(c_i), bwd(c_fb), bwd(c_i)]
        args = [h] * 4
    in_specs += [lb_spec, lb_spec, st_spec, st_spec]
    args += [lb_f, lb_b, s0[0], s0[1]]
    st_shape = jax.ShapeDtypeStruct((bsz, HEADS, HEAD_DIM, HEAD_DIM), F32)
    out_specs = [st_spec, st_spec]
    out_shape = [st_shape, st_shape]
    if with_output:
        o_shape = jax.ShapeDtypeStruct((bsz, l, HEADS * HEAD_DIM), F32)
        out_specs = [pl.BlockSpec((None, tblk, LANE), lambda b, hh, t: (b, t, hh)),
                     pl.BlockSpec((None, tblk, LANE), lambda b, hh, t: (b, nt - 1 - t, hh))] + out_specs
        out_shape = [o_shape, o_shape] + out_shape
    outs = pl.pallas_call(
        functools.partial(_hg_kernel, nchunk=tblk // CHUNK, with_output=with_output),
        name="hgrn2_scan",
        grid=(bsz, HEADS, nt),
        in_specs=in_specs,
        out_specs=out_specs,
        out_shape=out_shape,
        scratch_shapes=[pltpu.VMEM((HEAD_DIM, HEAD_DIM), F32), pltpu.VMEM((HEAD_DIM, HEAD_DIM), F32)],
        compiler_params=_cparams(("parallel", "parallel", "arbitrary")),
    )(*args)
    if with_output:
        return outs[0], outs[1], (outs[2], outs[3])
    return None, None, (outs[0], outs[1])


def _l2norm(t):
    return t * lax.rsqrt(jnp.sum(t * t, axis=-1, keepdims=True) + 1e-6)


def _softplus(x):
    return jnp.maximum(x, 0.0) + jnp.log(1.0 + jnp.exp(-jnp.abs(x)))


def _dn_prepare(items):
    r = _iota2((CHUNK, CHUNK), 0)
    c = _iota2((CHUNK, CHUNK), 1)
    eye = r == c
    for it in items:
        rev = it["rev"]
        incl = (c >= r) if rev else (c <= r)
        incl_t = (r >= c) if rev else (r <= c)
        g_b = jnp.broadcast_to(it["g"], (CHUNK, CHUNK))
        gc_col = jnp.sum(jnp.where(incl, g_b, 0.0), axis=1, keepdims=True)
        g_col = jnp.sum(jnp.where(eye, g_b, 0.0), axis=1, keepdims=True)
        gc_row = jnp.sum(jnp.where(incl_t, jnp.broadcast_to(g_col, (CHUNK, CHUNK)), 0.0), axis=0, keepdims=True)
        beta_col = jnp.sum(jnp.where(eye, jnp.broadcast_to(it["beta"], (CHUNK, CHUNK)), 0.0), axis=1,
                           keepdims=True)
        it["decay"] = jnp.where(incl, jnp.exp(jnp.where(incl, gc_col - gc_row, 0.0)), 0.0)
        g_last = gc_col[0:1] if rev else gc_col[CHUNK - 1:CHUNK]
        it["e_last"] = jnp.exp(g_last)
        egc = jnp.exp(gc_col)
        kb = it["k"] * beta_col
        it["kb"] = kb
        it["rhs"] = jnp.concatenate([it["v"] * beta_col, kb * egc], axis=1)
        it["kd"] = it["k"] * jnp.exp(g_last - gc_col)
        if it["q"] is not None:
            it["qe"] = it["q"] * egc
    for it in items:
        if it["q"] is not None:
            both = _dot_nt(jnp.concatenate([it["kb"], it["q"]], axis=0), it["k"])
            it["kk"], it["qk"] = both[:CHUNK], both[CHUNK:] * it["decay"]
        else:
            it["kk"] = _dot_nt(it["kb"], it["k"])
    for it in items:
        strict = (c > r) if it["rev"] else (c < r)
        n = jnp.where(strict, it["kk"] * it["decay"], 0.0)
        it["n"] = n
        it["p"] = jnp.where(eye, 1.0, 0.0) - n
    for it in items:
        it["m"] = _dot(it["n"], it["n"])
    for i in range(5):
        for it in items:
            if i < 4:
                pm = _dot(jnp.concatenate([it["p"], it["m"]], axis=0), it["m"])
                it["p"] = it["p"] + pm[:CHUNK]
                it["m"] = pm[CHUNK:]
            else:
                it["p"] = it["p"] + _dot(it["p"], it["m"])
    for it in items:
        it["uw"] = _dot(it["p"], it["rhs"])
    for it in items:
        bm = _dot_tn(it["kd"], it["uw"])
        it["b_c"], it["m_c"] = bm[:, :HEAD_DIM], bm[:, HEAD_DIM:]
        if it["q"] is not None:
            ow = _dot(it["qk"], it["uw"])
            it["o_c"], it["q_eff"] = ow[:, :HEAD_DIM], it["qe"] - ow[:, HEAD_DIM:]


def _dn_kernel(*refs, nchunk, with_output):
    if with_output:
        (alog_ref, dtb_ref, kf_ref, vf_ref, qf_ref, abf_ref, kb_ref, vb_ref, qb_ref, abb_ref,
         s0f_ref, s0b_ref, of_ref, ob_ref, sf_ref, sb_ref, stf, stb) = refs
    else:
        (alog_ref, dtb_ref, kf_ref, vf_ref, abf_ref, kb_ref, vb_ref, abb_ref,
         s0f_ref, s0b_ref, sf_ref, sb_ref, stf, stb) = refs
        qf_ref = qb_ref = of_ref = ob_ref = None
    hh = pl.program_id(1)

    @pl.when(pl.program_id(2) == 0)
    def _():
        stf[...] = s0f_ref[...]
        stb[...] = s0b_ref[...]

    def prep(k_ref, v_ref, q_ref, ab_ref, d):
        k = _l2norm(_silu(k_ref[...]))
        v = _silu(v_ref[...])
        q = None
        if with_output:
            q = _l2norm(_silu(q_ref[...])) * (HEAD_DIM ** -0.5)
        a = ab_ref[pl.ds(d * HEADS + hh, 1), :]
        bt = ab_ref[pl.ds(2 * HEADS + d * HEADS + hh, 1), :]
        g = -jnp.exp(alog_ref[d, hh]) * _softplus(a + dtb_ref[d, hh])
        return k, v, q, g, _sigmoid(bt)

    kf, vf, qf, gf, bf = prep(kf_ref, vf_ref, qf_ref, abf_ref, 0)
    kb, vb, qb, gb, bb = prep(kb_ref, vb_ref, qb_ref, abb_ref, 1)
    def chunk_item(k, v, q, g, beta, ci, rev):
        lo = ci * CHUNK
        return {"k": k[lo:lo + CHUNK], "v": v[lo:lo + CHUNK], "q": q[lo:lo + CHUNK] if with_output else None,
                "g": g[:, lo:lo + CHUNK], "beta": beta[:, lo:lo + CHUNK], "rev": rev, "lo": lo}

    items = []
    for ci in range(nchunk):
        items.append(chunk_item(kf, vf, qf, gf, bf, ci, False))
        items.append(chunk_item(kb, vb, qb, gb, bb, nchunk - 1 - ci, True))
    state = {False: stf[...], True: stb[...]}
    for wave in _waves(items, 4):
        _dn_prepare(wave)
        for it in wave:
            s = state[it["rev"]]
            if with_output:
                o_ref = ob_ref if it["rev"] else of_ref
                o_ref[pl.ds(it["lo"], CHUNK), :] = it["o_c"] + _dot(it["q_eff"], s)
            state[it["rev"]] = s * it["e_last"] + it["b_c"] - _dot(it["m_c"], s)
    stf[...] = state[False]
    stb[...] = state[True]
    sf_ref[...] = state[False]
    sb_ref[...] = state[True]


def _deltanet_scan(h, ab_t, cols, a_log, dt_bias, s0, *, tblk, with_output):
    bsz, l, _ = h.shape
    nt = l // tblk
    c_k, c_v, c_q = (c // LANE for c in cols)
    rep = HEADS // DN_QK_HEADS

    def tok(cb, shared, rev):
        def imap(b, hh, t):
            return (b, (nt - 1 - t) if rev else t, cb + (hh // rep if shared else hh))
        return pl.BlockSpec((None, tblk, LANE), imap)

    def ab_spec(rev):
        return pl.BlockSpec((None, 4 * HEADS, tblk), lambda b, hh, t: (b, 0, (nt - 1 - t) if rev else t))

    smem = pl.BlockSpec(memory_space=pltpu.SMEM)
    st_spec = pl.BlockSpec((None, None, HEAD_DIM, HEAD_DIM), lambda b, hh, t: (b, hh, 0, 0))
    if with_output:
        in_specs = [smem, smem, tok(c_k, True, False), tok(c_v, False, False), tok(c_q, True, False), ab_spec(False),
                    tok(c_k, True, True), tok(c_v, False, True), tok(c_q, True, True), ab_spec(True),
                    st_spec, st_spec]
        args = [a_log, dt_bias, h, h, h, ab_t, h, h, h, ab_t, s0[0], s0[1]]
    else:
        in_specs = [smem, smem, tok(c_k, True, False), tok(c_v, False, False), ab_spec(False),
                    tok(c_k, True, True), tok(c_v, False, True), ab_spec(True),
                    st_spec, st_spec]
        args = [a_log, dt_bias, h, h, ab_t, h, h, ab_t, s0[0], s0[1]]
    st_shape = jax.ShapeDtypeStruct((bsz, HEADS, HEAD_DIM, HEAD_DIM), F32)
    out_specs = [st_spec, st_spec]
    out_shape = [st_shape, st_shape]
    if with_output:
        o_shape = jax.ShapeDtypeStruct((bsz, l, HEADS * HEAD_DIM), F32)
        out_specs = [pl.BlockSpec((None, tblk, LANE), lambda b, hh, t: (b, t, hh)),
                     pl.BlockSpec((None, tblk, LANE), lambda b, hh, t: (b, nt - 1 - t, hh))] + out_specs
        out_shape = [o_shape, o_shape] + out_shape
    outs = pl.pallas_call(
        functools.partial(_dn_kernel, nchunk=tblk // CHUNK, with_output=with_output),
        name="deltanet_scan",
        grid=(bsz, HEADS, nt),
        in_specs=in_specs,
        out_specs=out_specs,
        out_shape=out_shape,
        scratch_shapes=[pltpu.VMEM((HEAD_DIM, HEAD_DIM), F32), pltpu.VMEM((HEAD_DIM, HEAD_DIM), F32)],
        compiler_params=_cparams(("parallel", "parallel", "arbitrary")),
    )(*args)
    if with_output:
        return outs[0], outs[1], (outs[2], outs[3])
    return None, None, (outs[0], outs[1])


def _combine_kernel(of_ref, ob_ref, z_ref, w_ref, o_ref, *, use_silu):
    o = of_ref[...] + ob_ref[...]
    o = o * lax.rsqrt(jnp.mean(o * o, axis=-1, keepdims=True) + RMS_EPS) * w_ref[...]
    z = z_ref[...]
    o_ref[...] = (o * (_silu(z) if use_silu else _sigmoid(z))).astype(o_ref.dtype)


def _combine(o_f, o_b, h, col_z, norm_w, *, tl, use_silu):
    bsz, l, c = o_f.shape
    cz = col_z // LANE
    spec = pl.BlockSpec((None, tl, LANE), lambda b, i, hh: (b, i, hh))
    return pl.pallas_call(
        functools.partial(_combine_kernel, use_silu=use_silu),
        name="mixer_norm_gate",
        grid=(bsz, l // tl, c // LANE),
        in_specs=[spec, spec,
                  pl.BlockSpec((None, tl, LANE), lambda b, i, hh: (b, i, cz + hh)),
                  pl.BlockSpec((1, LANE), lambda b, i, hh: (0, 0))],
        out_specs=spec,
        out_shape=jax.ShapeDtypeStruct((bsz, l, c), BF16),
        compiler_params=_cparams(("parallel", "parallel", "arbitrary")),
    )(o_f, o_b, h, norm_w)


def _dot_f32(a, b):
    return jnp.dot(a, b, precision=lax.Precision.HIGHEST, preferred_element_type=F32)


def _pack_halves(y, ns):
    nh = ns // 2
    return [pltpu.pack_elementwise([y[:, q * LANE:(q + 1) * LANE], y[:, (q + nh) * LANE:(q + nh + 1) * LANE]],
                                   packed_dtype=BF16) for q in range(nh)]


def _unpack_halves(ws):
    part = lambda i: [pltpu.unpack_elementwise(w, index=i, packed_dtype=BF16, unpacked_dtype=F32) for w in ws]
    return jnp.concatenate(part(0) + part(1), axis=1)


def _dot_3pass(a, b):
    a1 = a.astype(BF16)
    a2 = (a - a1.astype(F32)).astype(BF16)
    b1 = b.astype(BF16)
    b2 = (b - b1.astype(F32)).astype(BF16)
    return jnp.dot(jnp.concatenate([a1, a1, a2], axis=1), jnp.concatenate([b1, b2, b1], axis=0),
                   preferred_element_type=F32)


def _hy_filter_stage_a_kernel(w1_ref, b1_ref, w2_ref, b2_ref, w3_ref, t_ref, o_ref, *, l_total, n1, n2, ct):
    n2h = n2 // 2
    rows = 8 * n2h
    ns = ct // LANE
    g = pl.program_id(0)

    def lag(shape):
        r = _iota2(shape, 0)
        return 8 * g + jnp.right_shift(r, n2h.bit_length() - 1) + n1 * jnp.bitwise_and(r, n2h - 1)

    pos = lag((rows, LANE)).astype(F32)
    lane = _iota2((rows, LANE), 1)
    bidx = jnp.where(lane <= HY_BANDS, lane - 1, lane - 1 - HY_BANDS).astype(F32)
    arg = (1e-4 + bidx * ((HY_BANDS - 1 - 1e-4) / (HY_BANDS - 1))) * (pos * (2.0 * math.pi / l_total))
    feats = jnp.where(lane == 0, pos * (1.0 / (l_total - 1)),
                      jnp.where(lane <= HY_BANDS, jnp.cos(arg),
                                jnp.where(lane <= 2 * HY_BANDS, -jnp.sin(arg), 0.0)))
    hid = jnp.sin(_dot_f32(feats, w1_ref[...]) + b1_ref[...])
    hid = jnp.sin(_dot_f32(hid, w2_ref[...]) + b2_ref[...])
    lag_c = lag((rows, 1))
    t_c = lag_c.astype(F32) * (1.0 / (l_total - 1))
    lo = math.log(HY_DECAY_TARGET) / HY_SLOW_DECAY
    hi = math.log(HY_DECAY_TARGET) / HY_FAST_DECAY
    nslab = BRANCH_WIDTH // LANE // 2
    o2 = _rows2d(o_ref)
    for o in range(HY_ORDER):
        for c in range(BRANCH_WIDTH // ct):
            ch = (_iota2((1, ct), 1) + c * ct).astype(F32)
            window = jnp.exp(-t_c * jnp.abs(lo + ch * ((hi - lo) / (BRANCH_WIDTH - 1))))
            col_f = (2 * o) * BRANCH_WIDTH + c * ct
            col_b = col_f + BRANCH_WIDTH
            h_f = _dot_3pass(hid, w3_ref[:, col_f:col_f + ct]) * window
            h_b = jnp.where(lag_c == 0, 0.0, _dot_3pass(hid, w3_ref[:, col_b:col_b + ct]) * window)
            for j in range(8):
                x = jnp.concatenate([h_f[j * n2h:(j + 1) * n2h], h_b[j * n2h:(j + 1) * n2h]], axis=0).astype(BF16)
                y = jnp.dot(t_ref[j], x, preferred_element_type=F32)
                for p in range(2):
                    for q, word in enumerate(_pack_halves(y[p * n2:(p + 1) * n2], ns)):
                        group = (((o * 2 + p) * nslab + c * (ns // 2) + q) * n2)
                        o2[pl.ds(group * 8 + j, n2, stride=8), :] = word


def _hy_filter_stage_a(l_total, w1p, b1, w2, b2, w3, tab, *, ct, n1, n2):
    n2h = n2 // 2
    nslab = BRANCH_WIDTH // LANE // 2
    full = lambda a: pl.BlockSpec(a.shape, lambda g: (0,) * a.ndim)
    return pl.pallas_call(
        functools.partial(_hy_filter_stage_a_kernel, l_total=l_total, n1=n1, n2=n2, ct=ct),
        name="hy_filter_stage_a",
        grid=(n1 // 8,),
        in_specs=[full(w1p), full(b1), full(w2), full(b2), full(w3),
                  pl.BlockSpec((8, 2 * n2, 2 * n2h), lambda g: (g, 0, 0))],
        out_specs=pl.BlockSpec((HY_ORDER, 2, nslab, n2, 8, LANE), lambda g: (0, 0, 0, 0, g, 0)),
        out_shape=jax.ShapeDtypeStruct((HY_ORDER, 2, nslab, n2, n1, LANE), jnp.int32),
        compiler_params=_cparams(("arbitrary",)),
    )(w1p, b1, w2, b2, w3, tab)


@functools.lru_cache(maxsize=None)
def _fft_tables(n1, n2):
    n = n1 * n2
    n2h = n2 // 2
    k2 = np.arange(n2, dtype=np.int64)[:, None]
    m = np.arange(n2h, dtype=np.int64)[None, :]
    ta = np.zeros((n1, 2 * n2, 2 * n2h))
    for a in range(n1):
        th = 2.0 * np.pi * ((k2 * (a + n1 * m)) % n) / n
        c, s = np.cos(th), np.sin(th)
        ta[a] = np.block([[c, s], [-s, c]])
    tai = np.transpose(ta, (0, 2, 1)) / n
    k1 = np.arange(n1, dtype=np.int64)
    ph = 2.0 * np.pi * ((k1[:, None] * k1[None, :]) % n1) / n1
    c, s = np.cos(ph), np.sin(ph)
    fb = np.block([[c, s], [-s, c]])
    fs = fb[:n1] + fb[n1:]
    fsr = np.stack([fs[(-k1 - e) % n1] for e in (0, 1)])
    return tuple(np.asarray(t, np.float32) for t in (ta, tai, fb, fb.T, fs, fsr))


def _fft_split(l_total):
    n = 2 * l_total
    n1 = min(128, l_total // 8)
    return n1, n // n1


def _rows2d(ref):
    rows = 1
    for s in ref.shape[:-1]:
        rows *= s
    return ref.reshape(rows, ref.shape[-1])


def _lane_slabs(shape, imap, ns):
    def slab(s):
        def index_map(*a):
            idx = imap(*a)
            return idx[:-1] + (idx[-1] * ns + s,)
        return pl.BlockSpec(shape + (LANE,), index_map)
    return [slab(s) for s in range(ns)]


def _strided_rows(ref2, group, n, j):
    return ref2[pl.ds(group * 8 + j, n, stride=8), :]


def _hy_stage_a_kernel(gr_ref, cr_ref, gi_ref, ci_ref, *refs, ns, n2, n2h, slab_major_in):
    del gr_ref, cr_ref, gi_ref, ci_ref
    if slab_major_in:
        xs = [[(_rows2d(r), s * n2h) for s in range(ns)] for r in refs[:2]]
        refs = refs[2:]
    else:
        xs = [[(_rows2d(r), 0) for r in refs[q * ns:(q + 1) * ns]] for q in range(2)]
        refs = refs[2 * ns:]
    t_ref, o_ref = refs
    o2 = _rows2d(o_ref)
    for j in range(8):
        x = jnp.concatenate([jnp.concatenate([_strided_rows(r2, grp, n2h, j) for r2, grp in part], axis=1)
                             for part in xs], axis=0).astype(BF16)
        y = jnp.dot(t_ref[j], x, preferred_element_type=F32)
        for p in range(2):
            for q, word in enumerate(_pack_halves(y[p * n2:(p + 1) * n2], ns)):
                o2[pl.ds(((p * (ns // 2) + q) * n2) * 8 + j, n2, stride=8), :] = word


def _hy_stage_a(x, pairs, tab, *, ct, n2, slab_major_in):
    n2h = n2 // 2
    n1 = x.shape[-2] if slab_major_in else x.shape[2]
    npair = len(pairs)
    nct = BRANCH_WIDTH // ct
    ns = ct // LANE
    gr = jnp.asarray([p[0][0] for p in pairs], jnp.int32)
    cr = jnp.asarray([p[0][1] for p in pairs], jnp.int32)
    gi = jnp.asarray([p[1][0] for p in pairs], jnp.int32)
    ci = jnp.asarray([p[1][1] for p in pairs], jnp.int32)
    if slab_major_in:
        in_specs = [pl.BlockSpec((None, ns, n2h, 8, LANE), lambda p, c, g, *_: (0, c, 0, g, 0)),
                    pl.BlockSpec((None, ns, n2h, 8, LANE), lambda p, c, g, *_: (1, c, 0, g, 0))]
        args = [x, x]
    else:
        in_specs = (_lane_slabs((None, n2h, 8), lambda p, c, g, gr, cr, gi, ci: (gr[p], 0, g, cr[p] * nct + c), ns)
                    + _lane_slabs((None, n2h, 8), lambda p, c, g, gr, cr, gi, ci: (gi[p], 0, g, ci[p] * nct + c), ns))
        args = [x] * (2 * ns)
    grid_spec = pltpu.PrefetchScalarGridSpec(
        num_scalar_prefetch=4,
        grid=(npair, nct, n1 // 8),
        in_specs=in_specs + [pl.BlockSpec((8, 2 * n2, 2 * n2h), lambda p, c, g, *_: (g, 0, 0))],
        out_specs=pl.BlockSpec((None, 2, ns // 2, n2, 8, LANE), lambda p, c, g, *_: (p, 0, c, 0, g, 0)),
    )
    return pl.pallas_call(
        functools.partial(_hy_stage_a_kernel, ns=ns, n2=n2, n2h=n2h, slab_major_in=slab_major_in),
        name="hy_stage_a",
        grid_spec=grid_spec,
        out_shape=jax.ShapeDtypeStruct((npair, 2, BRANCH_WIDTH // LANE // 2, n2, n1, LANE), jnp.int32),
        compiler_params=_cparams(("parallel", "parallel", "arbitrary")),
    )(gr, cr, gi, ci, *args, tab)


def _hy_spec_kernel(c_ref, cm_ref, c0_ref, fs_ref, fsr_ref, o_ref, *, kg):
    g = pl.program_id(2)
    for k in range(kg):
        s = jnp.dot(fs_ref[...], _complex_slab(c_ref, k), preferred_element_type=F32)
        if k:
            mirror, fsr = _complex_slab(cm_ref, kg - k), fsr_ref[1]
        else:
            mirror, fsr = _complex_slab(c0_ref, 0), jnp.where(g == 0, fsr_ref[0], fsr_ref[1])
        t = jnp.dot(fsr, mirror, preferred_element_type=F32)
        o_ref[0, k] = 0.5 * (s + t)
        o_ref[1, k] = 0.5 * (s - t)


def _complex_slab(ref, k):
    nh = ref.shape[1]
    return jnp.concatenate([_unpack_halves([ref[p, q, k] for q in range(nh)]) for p in range(2)],
                           axis=0).astype(BF16)


def _hy_spectrum(cs, fs, fsr, *, ct, kg):
    no, _, nslab, n2, n1, _ = cs.shape
    ns = ct // LANE // 2
    ng = n2 // kg
    return pl.pallas_call(
        functools.partial(_hy_spec_kernel, kg=kg),
        name="hy_spectrum",
        grid=(no, nslab // ns, ng),
        in_specs=[pl.BlockSpec((None, 2, ns, kg, n1, LANE), lambda o, j, g: (o, 0, j, g, 0, 0)),
                  pl.BlockSpec((None, 2, ns, kg, n1, LANE), lambda o, j, g: (o, 0, j, ng - 1 - g, 0, 0)),
                  pl.BlockSpec((None, 2, ns, 1, n1, LANE), lambda o, j, g: (o, 0, j, (kg * (ng - g)) % n2, 0, 0)),
                  pl.BlockSpec(fs.shape, lambda o, j, g: (0, 0)),
                  pl.BlockSpec(fsr.shape, lambda o, j, g: (0, 0, 0))],
        out_specs=pl.BlockSpec((None, 2, kg, n1, ct), lambda o, j, g: (o, 0, g, 0, j)),
        out_shape=jax.ShapeDtypeStruct((no, 2, n2, n1, 2 * nslab * LANE), F32),
        compiler_params=_cparams(("parallel", "parallel", "arbitrary")),
    )(cs, cs, cs, fs, fsr)


def _hy_stage_b_kernel(c_ref, k_ref, fb_ref, fbi_ref, o_ref, *, kg, n1):
    ns = 2 * c_ref.shape[1]
    for k in range(kg):
        z = jnp.dot(fb_ref[...], _complex_slab(c_ref, k), preferred_element_type=F32)
        zr, zi = z[:n1], z[n1:]
        kr, ki = k_ref[0, k], k_ref[1, k]
        y = jnp.concatenate([zr * kr - zi * ki, zr * ki + zi * kr], axis=0).astype(BF16)
        d = jnp.dot(fbi_ref[...], y, preferred_element_type=F32)
        for p in range(2):
            for q, word in enumerate(_pack_halves(d[p * n1:(p + 1) * n1], ns)):
                o_ref[p, q, k] = word


def _hy_stage_b(cs, spec, order, fb, fbi, *, ct, kg):
    _, _, nslab, n2, n1, _ = cs.shape
    ns = ct // LANE // 2
    c = 2 * nslab * LANE
    return pl.pallas_call(
        functools.partial(_hy_stage_b_kernel, kg=kg, n1=n1),
        name="hy_stage_b",
        grid=(c // ct, n2 // kg),
        in_specs=[pl.BlockSpec((None, 2, ns, kg, n1, LANE), lambda j, g: (0, 0, j, g, 0, 0)),
                  pl.BlockSpec((None, 2, kg, n1, ct), lambda j, g: (order, 0, g, 0, j)),
                  pl.BlockSpec(fb.shape, lambda j, g: (0, 0)),
                  pl.BlockSpec(fbi.shape, lambda j, g: (0, 0))],
        out_specs=pl.BlockSpec((2, ns, kg, n1, LANE), lambda j, g: (0, j, g, 0, 0)),
        out_shape=jax.ShapeDtypeStruct((2, nslab, n2, n1, LANE), jnp.int32),
        compiler_params=_cparams(("parallel", "arbitrary")),
    )(cs, spec, fb, fbi)


def _hy_stage_c_kernel(*refs, ns, n2, n2h, slab_major_u):
    d2 = _rows2d(refs[0])
    t_ref = refs[1]
    if slab_major_u:
        u2 = _rows2d(refs[2])
        u_parts = [[(u2, (b * ns + s) * n2h) for s in range(ns)] for b in range(2)]
        refs = refs[3:]
    else:
        u_parts = [[(_rows2d(r), b * n2h) for r in refs[2:2 + ns]] for b in range(2)]
        refs = refs[2 + ns:]
    g_parts = [[(_rows2d(r), b * n2h) for r in refs[:ns]] for b in range(2)]
    skip_ref, o_ref = refs[ns:]
    o2 = _rows2d(o_ref)
    skip = skip_ref[...]

    def rows(parts, j):
        return jnp.concatenate([jnp.concatenate([_strided_rows(r2, grp, n2h, j) for r2, grp in part], axis=1)
                                for part in parts], axis=0)

    for j in range(8):
        nh = ns // 2
        d = jnp.concatenate([_unpack_halves([_strided_rows(d2, (p * nh + q) * n2, n2, j) for q in range(nh)])
                             for p in range(2)], axis=0).astype(BF16)
        y = jnp.dot(t_ref[j], d, preferred_element_type=F32)
        out = rows(g_parts, j) * (y + skip * rows(u_parts, j))
        for b in range(2):
            for s in range(ns):
                o2[pl.ds(((b * ns + s) * n2h) * 8 + j, n2h, stride=8), :] = (
                    out[b * n2h:(b + 1) * n2h, s * LANE:(s + 1) * LANE])


def _hy_stage_c(d, tai, u, ucol, z4, gcol, skip, *, ct, slab_major_u):
    _, npacked, n2, n1, _ = d.shape
    nslab = 2 * npacked
    n2h = n2 // 2
    ns = ct // LANE
    uc, gc = ucol // ct, gcol // ct
    if slab_major_u:
        u_specs = [pl.BlockSpec((2, ns, n2h, 8, LANE), lambda j, g: (0, j, 0, g, 0))]
        u_args = [u]
    else:
        u_specs = _lane_slabs((2, n2h, 8), lambda j, g: (0, 0, g, uc + j), ns)
        u_args = [u] * ns
    return pl.pallas_call(
        functools.partial(_hy_stage_c_kernel, ns=ns, n2=n2, n2h=n2h, slab_major_u=slab_major_u),
        name="hy_stage_c",
        grid=(nslab // ns, n1 // 8),
        in_specs=([pl.BlockSpec((2, ns // 2, n2, 8, LANE), lambda j, g: (0, j, 0, g, 0)),
                   pl.BlockSpec((8, 2 * n2h, 2 * n2), lambda j, g: (g, 0, 0))]
                  + u_specs + _lane_slabs((2, n2h, 8), lambda j, g: (0, 0, g, gc + j), ns)
                  + [pl.BlockSpec((1, ct), lambda j, g: (0, j))]),
        out_specs=pl.BlockSpec((2, ns, n2h, 8, LANE), lambda j, g: (0, j, 0, g, 0)),
        out_shape=jax.ShapeDtypeStruct((2, nslab, n2h, n1, LANE), F32),
        compiler_params=_cparams(("parallel", "arbitrary")),
    )(d, tai, *u_args, *([z4] * ns), skip)


def _hyena(h, filt_params, skip):
    bsz, l, _ = h.shape
    assert bsz == 2
    n1, n2 = _fft_split(l)
    n2h = n2 // 2
    ta, tai, fb, fbi, fs, fsr = (jnp.asarray(t).astype(BF16) for t in _fft_tables(n1, n2))
    ct = 512
    kg = 8

    cs_k = _hy_filter_stage_a(l, *filt_params, ta, ct=ct, n1=n1, n2=n2)
    spec = _hy_spectrum(cs_k, fs, fsr, ct=ct, kg=kg)

    z4 = h.reshape(bsz, n2h, n1, -1)
    u, ucol = z4, COL_HY
    for o in range(HY_ORDER):
        first = o == 0
        cs = _hy_stage_a(u, [((0, ucol // BRANCH_WIDTH), (1, ucol // BRANCH_WIDTH))], ta, ct=ct, n2=n2,
                         slab_major_in=not first)
        d = _hy_stage_b(cs, spec, o, fb, fbi, ct=ct, kg=kg)
        u = _hy_stage_c(d, tai, u, ucol, z4, COL_HY + (o + 1) * BRANCH_WIDTH, skip[o:o + 1], ct=ct,
                        slab_major_u=not first)
        ucol = 0
    return u.reshape(bsz, BRANCH_WIDTH // LANE, l, LANE)


def _conv_tables(hy_conv_w, hy_conv_b, dn_conv_q, dn_conv_k, dn_conv_v):
    depth = hy_conv_w.shape[0]
    ident = lambda n: jnp.broadcast_to(jnp.asarray([0.0, 1.0, 0.0], F32)[None, :, None], (depth, 3, n))
    taps = jnp.concatenate([ident(COL_HY), hy_conv_w, ident(COL_DK - COL_FF), dn_conv_k, dn_conv_v,
                            ident(COL_DQ - COL_AB), dn_conv_q, ident(COL_PAD - COL_DQ - dn_conv_q.shape[2])], axis=2)
    bias = jnp.concatenate([jnp.zeros((depth, 1, COL_HY), F32), hy_conv_b[:, None, :],
                            jnp.zeros((depth, 1, COL_PAD - COL_FF), F32)], axis=2)
    return taps, bias


def _token_mixers(u_src, sh, sc, w_t, conv_t, layer, lp, states, *, period, with_output, tm):
    l = lp["seq_len"]
    bsz = u_src.shape[0] * u_src.shape[1] // l
    tblk = min(l, 2048)
    taps, bias = (a[layer:layer + 1] for a in conv_t)
    if with_output:
        h = _in_proj(u_src, sh, sc, w_t, layer, range(COL_PAD // IN_TN), taps, bias, tm=tm, period=period)
        base = 0
    else:
        lo, hi = STATE_BLOCKS[0] * IN_TN, (STATE_BLOCKS[-1] + 1) * IN_TN
        h = _in_proj(u_src, sh, sc, w_t, layer, STATE_BLOCKS, taps[:, :, lo:hi], bias[:, :, lo:hi],
                     tm=tm, period=period)
        base = COL_FF
    h = h.reshape(bsz, l, h.shape[-1])
    col = lambda c: c - base
    ab_t = jnp.swapaxes(h[:, :, col(COL_AB):col(COL_AB) + 4 * HEADS], 1, 2)
    hg_f, hg_b, hg_states = _hgrn2_scan(h, (col(COL_FF), col(COL_FB), col(COL_HI), col(COL_HQ)),
                                        lp["lb_f"], lp["lb_b"], states[0], tblk=tblk, with_output=with_output)
    dn_f, dn_b, dn_states = _deltanet_scan(h, ab_t, (col(COL_DK), col(COL_DV), col(COL_DQ)),
                                           lp["dn_a_log"], lp["dn_dt_bias"], states[1],
                                           tblk=min(l, 2048), with_output=with_output)
    new_states = (hg_states, dn_states)
    if not with_output:
        return None, new_states
    tl = min(l, 2048)
    hg_out = _combine(hg_f, hg_b, h, COL_HGATE, lp["hg_norm_w"], tl=tl, use_silu=False)
    dn_out = _combine(dn_f, dn_b, h, COL_DZ, lp["dn_norm_w"], tl=tl, use_silu=True)
    hy_out = _hyena(h, lp["hy_filt"], lp["hy_skip"])
    ysum = _branch_merge([hy_out, hg_out, dn_out], lp["w_branch"], layer, h, tm=min(l, 1024), tn=512)
    return ysum, new_states


def kernel(x, c, ctx, c_ctx, w_ada, b_ada, w_in, hy_conv_w, hy_conv_b, hy_filt_w1, hy_filt_b1, hy_filt_w2, hy_filt_b2, hy_filt_w3, hy_skip, hg_lb_logits, hg_norm_w, dn_conv_q, dn_conv_k, dn_conv_v, dn_a_log, dn_dt_bias, dn_norm_w, w_branch, w_out, ln1_g, ln1_b, w_ff1, w_ff2, ln2_g, ln2_b):
    depth = w_in.shape[0]
    bsz, _, d = x.shape
    alpha = (2 * depth) ** 0.25
    p = jax.nn.softmax(hg_lb_logits.astype(F32), axis=1)
    lower = jnp.cumsum(p, axis=1) - p[:, :1]
    cs = jnp.concatenate([c, c_ctx[None], jnp.zeros((8 - bsz - 1, d), F32)], axis=0)
    w_r = jnp.swapaxes(w_in, 1, 2)
    w_b, w_o, w_1, w_2 = w_branch.astype(BF16), w_out.astype(BF16), w_ff1, w_ff2.astype(BF16)
    b_ada3 = b_ada[:, None, :]
    conv_t = _conv_tables(hy_conv_w, hy_conv_b, dn_conv_q, dn_conv_k, dn_conv_v)
    lc = ctx.shape[1]
    h_ctx = ctx.reshape(1, bsz * lc, d)
    for l in range(depth):
        last = l == depth - 1
        row = lambda v: v.reshape(1, -1)
        lp = {
            "lb_f": row(lower[0, l]), "lb_b": row(lower[1, l]),
            "hg_norm_w": row(hg_norm_w[l]), "dn_norm_w": row(dn_norm_w[l]),
            "dn_a_log": dn_a_log[l], "dn_dt_bias": dn_dt_bias[l], "hy_skip": hy_skip[l],
            "hy_filt": (jnp.pad(hy_filt_w1[l], ((0, LANE - HY_EMB), (0, 0))), row(hy_filt_b1[l]),
                        hy_filt_w2[l], row(hy_filt_b2[l]), hy_filt_w3[l]),
            "w_branch": w_b,
        }
        g1n, b1n, g2n, b2n = row(ln1_g[l]), row(ln1_b[l]), row(ln2_g[l]), row(ln2_b[l])
        mod = _ada_mod(cs, w_ada, b_ada3, l)
        m_lat = [mod[:bsz, i * d:(i + 1) * d][:, None, :] for i in range(6)]
        m_ctx = [mod[bsz, i * d:(i + 1) * d][None, None, :] for i in range(6)]
        zero = jnp.zeros((bsz, HEADS, HEAD_DIM, HEAD_DIM), F32)
        init = ((zero, zero), (zero, zero))

        rows_c = bsz * lc
        y_ctx, ctx_states = _token_mixers(h_ctx, m_ctx[0], m_ctx[1], w_r, conv_t, l, dict(lp, seq_len=lc), init,
                                          period=lc, with_output=not last, tm=rows_c)
        if not last:
            y_ctx = y_ctx.reshape(1, rows_c, d)
            h_ctx = _mm_ln(y_ctx, w_o, l, h_ctx, m_ctx[2], g1n, b1n, tm=rows_c, tk=d, alpha=alpha)
            mid = _mlp_up(h_ctx, m_ctx[3], m_ctx[4], w_1, l, tm=rows_c, tn=1024)
            h_ctx = _mm_ln(mid, w_2, l, h_ctx, m_ctx[5], g2n, b2n, tm=rows_c, tk=1024, alpha=alpha)

        lx = x.shape[1]
        y, _ = _token_mixers(x, m_lat[0], m_lat[1], w_r, conv_t, l, dict(lp, seq_len=lx), ctx_states,
                             period=GRID_W, with_output=True, tm=min(lx, 1024))
        x = _mm_ln(y, w_o, l, x, m_lat[2], g1n, b1n, tm=min(lx, 512), tk=d, alpha=alpha)
        mid = _mlp_up(x, m_lat[3], m_lat[4], w_1, l, tm=min(lx, 1024), tn=1024)
        x = _mm_ln(mid, w_2, l, x, m_lat[5], g2n, b2n, tm=min(lx, 1024), tk=512, alpha=alpha)
    return x
```

```python
import functools
import math

import numpy as np
import jax
import jax.numpy as jnp
from jax import lax
from jax.experimental import pallas as pl
from jax.experimental.pallas import tpu as pltpu

F32 = jnp.float32
BF16 = jnp.bfloat16

D_MODEL = 2048
GRID_W = 64
BRANCH_WIDTH = 1024
N_BRANCH = 3
HY_ORDER = 2
HY_EMB = 33
HY_BANDS = (HY_EMB - 1) // 2
HY_HIDDEN = 64
HY_DECAY_TARGET = 1e-2
HY_FAST_DECAY = 0.3
HY_SLOW_DECAY = 1.5
HEADS = 8
HEAD_DIM = 128
DN_QK_HEADS = 4
HG_F_MIN = 1e-30
CHUNK = 64
LN_EPS = 1e-5
RMS_EPS = 1e-6
LANE = 128

IN_TN = 1024
COL_GATE = 0
COL_HY = 6144
COL_FF = 9216
COL_FB = 10240
COL_HI = 11264
COL_DK = 12288
COL_DV = 12800
COL_AB = 13824
COL_HQ = 14336
COL_HGATE = 15360
COL_DQ = 16384
COL_DZ = 17408
COL_PAD = 18432
SOURCE_ROWS = np.asarray([11296 + IN_TN * k for k in range(6)] + [8224 + IN_TN * k for k in range(3)]
                         + [0, 1024, 2048, 3072, 4096, 4640, 5664, 6688, 7200], np.int32)
CONV_WINDOWS = np.asarray([COL_HY <= IN_TN * j < COL_FF or COL_DK <= IN_TN * j < COL_HQ or IN_TN * j == COL_DQ
                           for j in range(COL_PAD // IN_TN)], np.int32)
STATE_BLOCKS = tuple(range(COL_FF // IN_TN, COL_HQ // IN_TN))

VMEM_LIMIT = 56 * 1024 * 1024


def _cparams(sem):
    return pltpu.CompilerParams(dimension_semantics=sem, vmem_limit_bytes=VMEM_LIMIT)


def _dot(a, b):
    return jnp.dot(a.astype(BF16), b.astype(BF16), preferred_element_type=F32)


def _dot_nt(a, b):
    return lax.dot_general(a.astype(BF16), b.astype(BF16), (((1,), (1,)), ((), ())),
                           preferred_element_type=F32)


def _dot_tn(a, b):
    return lax.dot_general(a.astype(BF16), b.astype(BF16), (((0,), (0,)), ((), ())),
                           preferred_element_type=F32)


def _sigmoid(x):
    return 1.0 / (1.0 + jnp.exp(-x))


def _silu(x):
    return x * _sigmoid(x)


def _ada_kernel(c_ref, w_ref, b_ref, o_ref):
    a = _silu(c_ref[...])
    o_ref[...] = _dot(a, w_ref[...]) + b_ref[...]


def _ada_mod(cs, w, b, layer):
    _, d, n = w.shape
    tn = 1024
    return pl.pallas_call(
        _ada_kernel,
        grid=(n // tn,),
        in_specs=[pl.BlockSpec((8, d), lambda j: (0, 0)),
                  pl.BlockSpec((None, d, tn), lambda j: (layer, 0, j)),
                  pl.BlockSpec((None, 1, tn), lambda j: (layer, 0, j))],
        out_specs=pl.BlockSpec((8, tn), lambda j: (0, j)),
        out_shape=jax.ShapeDtypeStruct((8, n), F32),
        compiler_params=_cparams(("arbitrary",)),
        name="ada_mod",
    )(cs, w, b)


def _conv3(x, w, period):
    t = x.shape[0]
    pos = jnp.bitwise_and(_iota2((t, 1), 0), period - 1)
    has_l, has_r = pos != 0, pos != period - 1
    xl = jnp.where(has_l, pltpu.roll(x, 1, 0), 0.0)
    xr = jnp.where(has_r, pltpu.roll(x, t - 1, 0), 0.0)
    return xl * w[0:1] + x * w[1:2] + xr * w[2:3]


def _mlp_up_kernel(x_ref, sh_ref, sc_ref, w_ref, o_ref, a_scr):
    @pl.when(pl.program_id(2) == 0)
    def _():
        a_scr[...] = (x_ref[...] * (1.0 + sc_ref[...]) + sh_ref[...]).astype(BF16)

    acc = jnp.dot(a_scr[...], w_ref[...].astype(BF16), preferred_element_type=F32)
    o_ref[...] = jnp.square(jnp.maximum(acc, 0.0)).astype(o_ref.dtype)


def _mlp_up(x, sh, sc, w, layer, *, tm, tn):
    bsz, l, d = x.shape
    n = w.shape[2]
    return pl.pallas_call(
        _mlp_up_kernel,
        grid=(bsz, l // tm, n // tn),
        in_specs=[pl.BlockSpec((None, tm, d), lambda b, i, j: (b, i, 0)),
                  pl.BlockSpec((None, 1, d), lambda b, i, j: (b, 0, 0)),
                  pl.BlockSpec((None, 1, d), lambda b, i, j: (b, 0, 0)),
                  pl.BlockSpec((None, d, tn), lambda b, i, j: (layer, 0, j))],
        out_specs=pl.BlockSpec((None, tm, tn), lambda b, i, j: (b, i, j)),
        out_shape=jax.ShapeDtypeStruct((bsz, l, n), BF16),
        scratch_shapes=[pltpu.VMEM((tm, d), BF16)],
        compiler_params=_cparams(("parallel", "parallel", "arbitrary")),
        name="mm_relu2",
    )(x, sh, sc, w)


def _in_proj_kernel(off_ref, conv_ref, x_ref, sh_ref, sc_ref, w_ref, cw_ref, cb_ref, o_ref, a_scr, *, period):
    del off_ref
    has_conv = conv_ref[pl.program_id(2)] != 0

    @pl.when(pl.program_id(2) == 0)
    def _():
        a_scr[...] = (x_ref[...] * (1.0 + sc_ref[...]) + sh_ref[...]).astype(BF16)

    o_ref[...] = lax.dot_general(a_scr[...], w_ref[0].astype(BF16), (((1,), (1,)), ((), ())),
                                 preferred_element_type=F32)

    @pl.when(has_conv)
    def _():
        o_ref[...] = _conv3(o_ref[...], cw_ref[...], period) + cb_ref[...]


def _in_proj(x, sh, sc, w_t, layer, windows, taps, bias, *, tm, period):
    bsz, l, d = x.shape
    nblk = len(windows)
    assert tm % period == 0 and period & (period - 1) == 0
    row_off = jnp.asarray(SOURCE_ROWS[list(windows)])
    has_conv = jnp.asarray(CONV_WINDOWS[list(windows)])
    grid_spec = pltpu.PrefetchScalarGridSpec(
        num_scalar_prefetch=2,
        grid=(bsz, l // tm, nblk),
        in_specs=[pl.BlockSpec((None, tm, d), lambda b, i, j, off, cv: (b, i, 0)),
                  pl.BlockSpec((None, 1, d), lambda b, i, j, off, cv: (b, 0, 0)),
                  pl.BlockSpec((None, 1, d), lambda b, i, j, off, cv: (b, 0, 0)),
                  pl.BlockSpec((pl.Element(1), pl.Element(IN_TN), pl.Element(d)),
                               lambda b, i, j, off, cv: (layer, pl.multiple_of(off[j], 4 * HEADS), 0)),
                  pl.BlockSpec((None, 3, IN_TN), lambda b, i, j, off, cv: (0, 0, j)),
                  pl.BlockSpec((None, 1, IN_TN), lambda b, i, j, off, cv: (0, 0, j))],
        out_specs=pl.BlockSpec((None, tm, IN_TN), lambda b, i, j, off, cv: (b, i, j)),
        scratch_shapes=[pltpu.VMEM((tm, d), BF16)],
    )
    return pl.pallas_call(
        functools.partial(_in_proj_kernel, period=period),
        name="mm_in_proj",
        grid_spec=grid_spec,
        out_shape=jax.ShapeDtypeStruct((bsz, l, nblk * IN_TN), F32),
        compiler_params=_cparams(("parallel", "parallel", "arbitrary")),
    )(row_off, has_conv, x, sh, sc, w_t, taps, bias)


def _mm_ln_kernel(a_ref, w_ref, x_ref, gate_ref, g_ref, b_ref, o_ref, acc_ref, *, nk, alpha):
    k = pl.program_id(2)

    @pl.when(k == 0)
    def _():
        acc_ref[...] = jnp.dot(a_ref[...], w_ref[...].astype(BF16), preferred_element_type=F32)

    @pl.when(k > 0)
    def _():
        acc_ref[...] += jnp.dot(a_ref[...], w_ref[...].astype(BF16), preferred_element_type=F32)

    @pl.when(k == nk - 1)
    def _():
        y = alpha * x_ref[...] + gate_ref[...] * acc_ref[...]
        mu = jnp.mean(y, axis=-1, keepdims=True)
        yc = y - mu
        var = jnp.mean(yc * yc, axis=-1, keepdims=True)
        o_ref[...] = yc * lax.rsqrt(var + LN_EPS) * g_ref[...] + b_ref[...]


def _mm_ln(a, w, layer, x, gate, ln_g, ln_b, *, tm, tk, alpha):
    bsz, l, kdim = a.shape
    d = w.shape[2]
    nk = kdim // tk
    return pl.pallas_call(
        functools.partial(_mm_ln_kernel, nk=nk, alpha=alpha),
        grid=(bsz, l // tm, nk),
        in_specs=[pl.BlockSpec((None, tm, tk), lambda b, i, k: (b, i, k)),
                  pl.BlockSpec((None, tk, d), lambda b, i, k: (layer, k, 0)),
                  pl.BlockSpec((None, tm, d), lambda b, i, k: (b, i, 0)),
                  pl.BlockSpec((None, 1, d), lambda b, i, k: (b, 0, 0)),
                  pl.BlockSpec((1, d), lambda b, i, k: (0, 0)),
                  pl.BlockSpec((1, d), lambda b, i, k: (0, 0))],
        out_specs=pl.BlockSpec((None, tm, d), lambda b, i, k: (b, i, 0)),
        out_shape=jax.ShapeDtypeStruct((bsz, l, d), F32),
        scratch_shapes=[pltpu.VMEM((tm, d), F32)],
        compiler_params=_cparams(("parallel", "parallel", "arbitrary")),
        name="mm_resid_ln",
    )(a, w, x, gate, ln_g, ln_b)


def _branch_kernel(b0_ref, b1_ref, b2_ref, w_ref, g0_ref, g1_ref, g2_ref, o_ref):
    b0 = jnp.concatenate([b0_ref[s] for s in range(b0_ref.shape[0])], axis=1).astype(BF16)
    acc = _sigmoid(g0_ref[...]) * jnp.dot(b0, w_ref[0].astype(BF16), preferred_element_type=F32)
    acc += _sigmoid(g1_ref[...]) * jnp.dot(b1_ref[...], w_ref[1].astype(BF16), preferred_element_type=F32)
    acc += _sigmoid(g2_ref[...]) * jnp.dot(b2_ref[...], w_ref[2].astype(BF16), preferred_element_type=F32)
    o_ref[...] = acc.astype(o_ref.dtype)


def _branch_merge(branches, w, layer, h, *, tm, tn):
    bsz, l, c = branches[1].shape
    d = w.shape[3]
    nj = d // tn
    br_spec = pl.BlockSpec((None, tm, c), lambda b, i, j: (b, i, 0))
    hy_spec = pl.BlockSpec((None, c // LANE, tm, LANE), lambda b, i, j: (b, 0, i, 0))

    def gate_spec(g):
        return pl.BlockSpec((None, tm, tn), lambda b, i, j: (b, i, COL_GATE // tn + g * nj + j))

    return pl.pallas_call(
        _branch_kernel,
        grid=(bsz, l // tm, nj),
        in_specs=[hy_spec, br_spec, br_spec,
                  pl.BlockSpec((None, N_BRANCH, c, tn), lambda b, i, j: (layer, 0, 0, j)),
                  gate_spec(0), gate_spec(1), gate_spec(2)],
        out_specs=pl.BlockSpec((None, tm, tn), lambda b, i, j: (b, i, j)),
        out_shape=jax.ShapeDtypeStruct((bsz, l, d), BF16),
        compiler_params=_cparams(("parallel", "parallel", "arbitrary")),
        name="branch_merge",
    )(*branches, w, h, h, h)


def _iota2(shape, axis):
    return lax.broadcasted_iota(jnp.int32, shape, axis)


def _split3(g):
    g1 = g.astype(BF16)
    r = g - g1.astype(F32)
    g2 = r.astype(BF16)
    g3 = (r - g2.astype(F32)).astype(BF16)
    return g1, g2, g3


def _waves(items, nwaves):
    n = min(nwaves, max(1, len(items) // 8))
    per = len(items) // n
    return [items[w * per:(w + 1) * per] for w in range(n)]


def _cumsum_rows(g, rev):
    row = _iota2((CHUNK, 3 * CHUNK), 0)
    col = jnp.bitwise_and(_iota2((CHUNK, 3 * CHUNK), 1), CHUNK - 1)
    tri = jnp.where((col >= row) if rev else (col <= row), 1.0, 0.0).astype(BF16)
    g3 = jnp.concatenate(_split3(g), axis=0)
    return jnp.dot(tri, g3, preferred_element_type=F32)


def _hg_prepare(items):
    for it in items:
        lb = it["lb"]
        sg = _sigmoid(it["fz"])
        it["g"] = jnp.log(jnp.maximum(lb + (1.0 - lb) * sg, HG_F_MIN))
        it["kk"] = (1.0 - lb) * (1.0 - sg)
    for it in items:
        it["bc"] = _cumsum_rows(it["g"], it["rev"])
    half = CHUNK // 2
    for it in items:
        bc = it["bc"]
        if it["rev"]:
            mid, last = bc[half:half + 1], bc[0:1]
        else:
            mid, last = bc[half - 1:half], bc[CHUNK - 1:CHUNK]
        it["e_last"] = jnp.exp(last)
        kd = it["kk"] * jnp.exp(mid - bc)
        it["ks"] = kd * jnp.exp(last - mid)
        if it["q"] is not None:
            qd = _silu(it["q"]) * jnp.exp(bc - mid)
            it["qd"], it["kd"] = qd, kd
            it["qs"] = qd * jnp.exp(mid)
    r = _iota2((CHUNK, CHUNK), 0)
    c = _iota2((CHUNK, CHUNK), 1)
    for it in items:
        it["kv"] = _dot_tn(it["v"], it["ks"])
        if it["q"] is not None:
            att = _dot_nt(it["qd"], it["kd"])
            it["att"] = jnp.where((c >= r) if it["rev"] else (c <= r), att, 0.0)
    for it in items:
        if it["q"] is not None:
            it["o_c"] = _dot(it["att"], it["v"])


def _hg_kernel(*refs, nchunk, with_output):
    if with_output:
        (ff_ref, vf_ref, qf_ref, fb_ref, vb_ref, qb_ref, lbf_ref, lbb_ref, s0f_ref, s0b_ref,
         of_ref, ob_ref, sf_ref, sb_ref, stf, stb) = refs
    else:
        (ff_ref, vf_ref, fb_ref, vb_ref, lbf_ref, lbb_ref, s0f_ref, s0b_ref,
         sf_ref, sb_ref, stf, stb) = refs
        qf_ref = qb_ref = of_ref = ob_ref = None

    @pl.when(pl.program_id(2) == 0)
    def _():
        stf[...] = s0f_ref[...]
        stb[...] = s0b_ref[...]

    def chunk_item(f_ref, v_ref, q_ref, lb_ref, ci, rev):
        sl = pl.ds(ci * CHUNK, CHUNK)
        return {"fz": f_ref[sl, :], "v": v_ref[sl, :], "q": q_ref[sl, :] if with_output else None,
                "lb": lb_ref[...], "rev": rev, "sl": sl}

    items = []
    for ci in range(nchunk):
        items.append(chunk_item(ff_ref, vf_ref, qf_ref, lbf_ref, ci, False))
        items.append(chunk_item(fb_ref, vb_ref, qb_ref, lbb_ref, nchunk - 1 - ci, True))
    state = {False: stf[...], True: stb[...]}
    for wave in _waves(items, 8):
        _hg_prepare(wave)
        for it in wave:
            st = state[it["rev"]]
            if with_output:
                o_ref = ob_ref if it["rev"] else of_ref
                o_ref[it["sl"], :] = it["o_c"] + _dot_nt(it["qs"], st)
            state[it["rev"]] = st * it["e_last"] + it["kv"]
    stf[...] = state[False]
    stb[...] = state[True]
    sf_ref[...] = state[False]
    sb_ref[...] = state[True]


def _hgrn2_scan(h, cols, lb_f, lb_b, s0, *, tblk, with_output):
    bsz, l, _ = h.shape
    nt = l // tblk
    c_ff, c_fb, c_i, c_q = (c // LANE for c in cols)

    def fwd(cb):
        return pl.BlockSpec((None, tblk, LANE), lambda b, hh, t: (b, t, cb + hh))

    def bwd(cb):
        return pl.BlockSpec((None, tblk, LANE), lambda b, hh, t: (b, nt - 1 - t, cb + hh))

    lb_spec = pl.BlockSpec((1, LANE), lambda b, hh, t: (0, hh))
    st_spec = pl.BlockSpec((None, None, HEAD_DIM, HEAD_DIM), lambda b, hh, t: (b, hh, 0, 0))
    if with_output:
        in_specs = [fwd(c_ff), fwd(c_i), fwd(c_q), bwd(c_fb), bwd(c_i), bwd(c_q)]
        args = [h] * 6
    else:
        in_specs = [fwd(c_ff), fwd(c_i), bwd(c_fb), bwd(c_i)]
        args = [h] * 4
    in_specs += [lb_spec, lb_spec, st_spec, st_spec]
    args += [lb_f, lb_b, s0[0], s0[1]]
    st_shape = jax.ShapeDtypeStruct((bsz, HEADS, HEAD_DIM, HEAD_DIM), F32)
    out_specs = [st_spec, st_spec]
    out_shape = [st_shape, st_shape]
    if with_output:
        o_shape = jax.ShapeDtypeStruct((bsz, l, HEADS * HEAD_DIM), F32)
        out_specs = [pl.BlockSpec((None, tblk, LANE), lambda b, hh, t: (b, t, hh)),
                     pl.BlockSpec((None, tblk, LANE), lambda b, hh, t: (b, nt - 1 - t, hh))] + out_specs
        out_shape = [o_shape, o_shape] + out_shape
    outs = pl.pallas_call(
        functools.partial(_hg_kernel, nchunk=tblk // CHUNK, with_output=with_output),
        name="hgrn2_scan",
        grid=(bsz, HEADS, nt),
        in_specs=in_specs,
        out_specs=out_specs,
        out_shape=out_shape,
        scratch_shapes=[pltpu.VMEM((HEAD_DIM, HEAD_DIM), F32), pltpu.VMEM((HEAD_DIM, HEAD_DIM), F32)],
        compiler_params=_cparams(("parallel", "parallel", "arbitrary")),
    )(*args)
    if with_output:
        return outs[0], outs[1], (outs[2], outs[3])
    return None, None, (outs[0], outs[1])


def _l2norm(t):
    return t * lax.rsqrt(jnp.sum(t * t, axis=-1, keepdims=True) + 1e-6)


def _softplus(x):
    return jnp.maximum(x, 0.0) + jnp.log(1.0 + jnp.exp(-jnp.abs(x)))


def _dn_prepare(items):
    r = _iota2((CHUNK, CHUNK), 0)
    c = _iota2((CHUNK, CHUNK), 1)
    eye = r == c
    for it in items:
        rev = it["rev"]
        incl = (c >= r) if rev else (c <= r)
        incl_t = (r >= c) if rev else (r <= c)
        g_b = jnp.broadcast_to(it["g"], (CHUNK, CHUNK))
        gc_col = jnp.sum(jnp.where(incl, g_b, 0.0), axis=1, keepdims=True)
        g_col = jnp.sum(jnp.where(eye, g_b, 0.0), axis=1, keepdims=True)
        gc_row = jnp.sum(jnp.where(incl_t, jnp.broadcast_to(g_col, (CHUNK, CHUNK)), 0.0), axis=0, keepdims=True)
        beta_col = jnp.sum(jnp.where(eye, jnp.broadcast_to(it["beta"], (CHUNK, CHUNK)), 0.0), axis=1,
                           keepdims=True)
        it["decay"] = jnp.where(incl, jnp.exp(jnp.where(incl, gc_col - gc_row, 0.0)), 0.0)
        g_last = gc_col[0:1] if rev else gc_col[CHUNK - 1:CHUNK]
        it["e_last"] = jnp.exp(g_last)
        egc = jnp.exp(gc_col)
        kb = it["k"] * beta_col
        it["kb"] = kb
        it["rhs"] = jnp.concatenate([it["v"] * beta_col, kb * egc], axis=1)
        it["kd"] = it["k"] * jnp.exp(g_last - gc_col)
        if it["q"] is not None:
            it["qe"] = it["q"] * egc
    for it in items:
        if it["q"] is not None:
            both = _dot_nt(jnp.concatenate([it["kb"], it["q"]], axis=0), it["k"])
            it["kk"], it["qk"] = both[:CHUNK], both[CHUNK:] * it["decay"]
        else:
            it["kk"] = _dot_nt(it["kb"], it["k"])
    for it in items:
        strict = (c > r) if it["rev"] else (c < r)
        n = jnp.where(strict, it["kk"] * it["decay"], 0.0)
        it["n"] = n
        it["p"] = jnp.where(eye, 1.0, 0.0) - n
    for it in items:
        it["m"] = _dot(it["n"], it["n"])
    for i in range(5):
        for it in items:
            if i < 4:
                pm = _dot(jnp.concatenate([it["p"], it["m"]], axis=0), it["m"])
                it["p"] = it["p"] + pm[:CHUNK]
                it["m"] = pm[CHUNK:]
            else:
                it["p"] = it["p"] + _dot(it["p"], it["m"])
    for it in items:
        it["uw"] = _dot(it["p"], it["rhs"])
    for it in items:
        bm = _dot_tn(it["kd"], it["uw"])
        it["b_c"], it["m_c"] = bm[:, :HEAD_DIM], bm[:, HEAD_DIM:]
        if it["q"] is not None:
            ow = _dot(it["qk"], it["uw"])
            it["o_c"], it["q_eff"] = ow[:, :HEAD_DIM], it["qe"] - ow[:, HEAD_DIM:]


def _dn_kernel(*refs, nchunk, with_output):
    if with_output:
        (alog_ref, dtb_ref, kf_ref, vf_ref, qf_ref, abf_ref, kb_ref, vb_ref, qb_ref, abb_ref,
         s0f_ref, s0b_ref, of_ref, ob_ref, sf_ref, sb_ref, stf, stb) = refs
    else:
        (alog_ref, dtb_ref, kf_ref, vf_ref, abf_ref, kb_ref, vb_ref, abb_ref,
         s0f_ref, s0b_ref, sf_ref, sb_ref, stf, stb) = refs
        qf_ref = qb_ref = of_ref = ob_ref = None
    hh = pl.program_id(1)

    @pl.when(pl.program_id(2) == 0)
    def _():
        stf[...] = s0f_ref[...]
        stb[...] = s0b_ref[...]

    def prep(k_ref, v_ref, q_ref, ab_ref, d):
        k = _l2norm(_silu(k_ref[...]))
        v = _silu(v_ref[...])
        q = None
        if with_output:
            q = _l2norm(_silu(q_ref[...])) * (HEAD_DIM ** -0.5)
        a = ab_ref[pl.ds(d * HEADS + hh, 1), :]
        bt = ab_ref[pl.ds(2 * HEADS + d * HEADS + hh, 1), :]
        g = -jnp.exp(alog_ref[d, hh]) * _softplus(a + dtb_ref[d, hh])
        return k, v, q, g, _sigmoid(bt)

    kf, vf, qf, gf, bf = prep(kf_ref, vf_ref, qf_ref, abf_ref, 0)
    kb, vb, qb, gb, bb = prep(kb_ref, vb_ref, qb_ref, abb_ref, 1)
    def chunk_item(k, v, q, g, beta, ci, rev):
        lo = ci * CHUNK
        return {"k": k[lo:lo + CHUNK], "v": v[lo:lo + CHUNK], "q": q[lo:lo + CHUNK] if with_output else None,
                "g": g[:, lo:lo + CHUNK], "beta": beta[:, lo:lo + CHUNK], "rev": rev, "lo": lo}

    items = []
    for ci in range(nchunk):
        items.append(chunk_item(kf, vf, qf, gf, bf, ci, False))
        items.append(chunk_item(kb, vb, qb, gb, bb, nchunk - 1 - ci, True))
    state = {False: stf[...], True: stb[...]}
    for wave in _waves(items, 4):
        _dn_prepare(wave)
        for it in wave:
            s = state[it["rev"]]
            if with_output:
                o_ref = ob_ref if it["rev"] else of_ref
                o_ref[pl.ds(it["lo"], CHUNK), :] = it["o_c"] + _dot(it["q_eff"], s)
            state[it["rev"]] = s * it["e_last"] + it["b_c"] - _dot(it["m_c"], s)
    stf[...] = state[False]
    stb[...] = state[True]
    sf_ref[...] = state[False]
    sb_ref[...] = state[True]


def _deltanet_scan(h, ab_t, cols, a_log, dt_bias, s0, *, tblk, with_output):
    bsz, l, _ = h.shape
    nt = l // tblk
    c_k, c_v, c_q = (c // LANE for c in cols)
    rep = HEADS // DN_QK_HEADS

    def tok(cb, shared, rev):
        def imap(b, hh, t):
            return (b, (nt - 1 - t) if rev else t, cb + (hh // rep if shared else hh))
        return pl.BlockSpec((None, tblk, LANE), imap)

    def ab_spec(rev):
        return pl.BlockSpec((None, 4 * HEADS, tblk), lambda b, hh, t: (b, 0, (nt - 1 - t) if rev else t))

    smem = pl.BlockSpec(memory_space=pltpu.SMEM)
    st_spec = pl.BlockSpec((None, None, HEAD_DIM, HEAD_DIM), lambda b, hh, t: (b, hh, 0, 0))
    if with_output:
        in_specs = [smem, smem, tok(c_k, True, False), tok(c_v, False, False), tok(c_q, True, False), ab_spec(False),
                    tok(c_k, True, True), tok(c_v, False, True), tok(c_q, True, True), ab_spec(True),
                    st_spec, st_spec]
        args = [a_log, dt_bias, h, h, h, ab_t, h, h, h, ab_t, s0[0], s0[1]]
    else:
        in_specs = [smem, smem, tok(c_k, True, False), tok(c_v, False, False), ab_spec(False),
                    tok(c_k, True, True), tok(c_v, False, True), ab_spec(True),
                    st_spec, st_spec]
        args = [a_log, dt_bias, h, h, ab_t, h, h, ab_t, s0[0], s0[1]]
    st_shape = jax.ShapeDtypeStruct((bsz, HEADS, HEAD_DIM, HEAD_DIM), F32)
    out_specs = [st_spec, st_spec]
    out_shape = [st_shape, st_shape]
    if with_output:
        o_shape = jax.ShapeDtypeStruct((bsz, l, HEADS * HEAD_DIM), F32)
        out_specs = [pl.BlockSpec((None, tblk, LANE), lambda b, hh, t: (b, t, hh)),
                     pl.BlockSpec((None, tblk, LANE), lambda b, hh, t: (b, nt - 1 - t, hh))] + out_specs
        out_shape = [o_shape, o_shape] + out_shape
    outs = pl.pallas_call(
        functools.partial(_dn_kernel, nchunk=tblk // CHUNK, with_output=with_output),
        name="deltanet_scan",
        grid=(bsz, HEADS, nt),
        in_specs=in_specs,
        out_specs=out_specs,
        out_shape=out_shape,
        scratch_shapes=[pltpu.VMEM((HEAD_DIM, HEAD_DIM), F32), pltpu.VMEM((HEAD_DIM, HEAD_DIM), F32)],
        compiler_params=_cparams(("parallel", "parallel", "arbitrary")),
    )(*args)
    if with_output:
        return outs[0], outs[1], (outs[2], outs[3])
    return None, None, (outs[0], outs[1])


def _combine_kernel(of_ref, ob_ref, z_ref, w_ref, o_ref, *, use_silu):
    o = of_ref[...] + ob_ref[...]
    o = o * lax.rsqrt(jnp.mean(o * o, axis=-1, keepdims=True) + RMS_EPS) * w_ref[...]
    z = z_ref[...]
    o_ref[...] = (o * (_silu(z) if use_silu else _sigmoid(z))).astype(o_ref.dtype)


def _combine(o_f, o_b, h, col_z, norm_w, *, tl, use_silu):
    bsz, l, c = o_f.shape
    cz = col_z // LANE
    spec = pl.BlockSpec((None, tl, LANE), lambda b, i, hh: (b, i, hh))
    return pl.pallas_call(
        functools.partial(_combine_kernel, use_silu=use_silu),
        name="mixer_norm_gate",
        grid=(bsz, l // tl, c // LANE),
        in_specs=[spec, spec,
                  pl.BlockSpec((None, tl, LANE), lambda b, i, hh: (b, i, cz + hh)),
                  pl.BlockSpec((1, LANE), lambda b, i, hh: (0, 0))],
        out_specs=spec,
        out_shape=jax.ShapeDtypeStruct((bsz, l, c), BF16),
        compiler_params=_cparams(("parallel", "parallel", "arbitrary")),
    )(o_f, o_b, h, norm_w)


def _dot_f32(a, b):
    return jnp.dot(a, b, precision=lax.Precision.HIGHEST, preferred_element_type=F32)


def _pack_halves(y, ns):
    nh = ns // 2
    return [pltpu.pack_elementwise([y[:, q * LANE:(q + 1) * LANE], y[:, (q + nh) * LANE:(q + nh + 1) * LANE]],
                                   packed_dtype=BF16) for q in range(nh)]


def _unpack_halves(ws):
    part = lambda i: [pltpu.unpack_elementwise(w, index=i, packed_dtype=BF16, unpacked_dtype=F32) for w in ws]
    return jnp.concatenate(part(0) + part(1), axis=1)


def _dot_3pass(a, b):
    a1 = a.astype(BF16)
    a2 = (a - a1.astype(F32)).astype(BF16)
    b1 = b.astype(BF16)
    b2 = (b - b1.astype(F32)).astype(BF16)
    return jnp.dot(jnp.concatenate([a1, a1, a2], axis=1), jnp.concatenate([b1, b2, b1], axis=0),
                   preferred_element_type=F32)


def _hy_filter_stage_a_kernel(w1_ref, b1_ref, w2_ref, b2_ref, w3_ref, t_ref, o_ref, *, l_total, n1, n2, ct):
    n2h = n2 // 2
    rows = 8 * n2h
    ns = ct // LANE
    g = pl.program_id(0)

    def lag(shape):
        r = _iota2(shape, 0)
        return 8 * g + jnp.right_shift(r, n2h.bit_length() - 1) + n1 * jnp.bitwise_and(r, n2h - 1)

    pos = lag((rows, LANE)).astype(F32)
    lane = _iota2((rows, LANE), 1)
    bidx = jnp.where(lane <= HY_BANDS, lane - 1, lane - 1 - HY_BANDS).astype(F32)
    arg = (1e-4 + bidx * ((HY_BANDS - 1 - 1e-4) / (HY_BANDS - 1))) * (pos * (2.0 * math.pi / l_total))
    feats = jnp.where(lane == 0, pos * (1.0 / (l_total - 1)),
                      jnp.where(lane <= HY_BANDS, jnp.cos(arg),
                                jnp.where(lane <= 2 * HY_BANDS, -jnp.sin(arg), 0.0)))
    hid = jnp.sin(_dot_f32(feats, w1_ref[...]) + b1_ref[...])
    hid = jnp.sin(_dot_f32(hid, w2_ref[...]) + b2_ref[...])
    lag_c = lag((rows, 1))
    t_c = lag_c.astype(F32) * (1.0 / (l_total - 1))
    lo = math.log(HY_DECAY_TARGET) / HY_SLOW_DECAY
    hi = math.log(HY_DECAY_TARGET) / HY_FAST_DECAY
    nslab = BRANCH_WIDTH // LANE // 2
    o2 = _rows2d(o_ref)
    for o in range(HY_ORDER):
        for c in range(BRANCH_WIDTH // ct):
            ch = (_iota2((1, ct), 1) + c * ct).astype(F32)
            window = jnp.exp(-t_c * jnp.abs(lo + ch * ((hi - lo) / (BRANCH_WIDTH - 1))))
            col_f = (2 * o) * BRANCH_WIDTH + c * ct
            col_b = col_f + BRANCH_WIDTH
            h_f = _dot_3pass(hid, w3_ref[:, col_f:col_f + ct]) * window
            h_b = jnp.where(lag_c == 0, 0.0, _dot_3pass(hid, w3_ref[:, col_b:col_b + ct]) * window)
            for j in range(8):
                x = jnp.concatenate([h_f[j * n2h:(j + 1) * n2h], h_b[j * n2h:(j + 1) * n2h]], axis=0).astype(BF16)
                y = jnp.dot(t_ref[j], x, preferred_element_type=F32)
                for p in range(2):
                    for q, word in enumerate(_pack_halves(y[p * n2:(p + 1) * n2], ns)):
                        group = (((o * 2 + p) * nslab + c * (ns // 2) + q) * n2)
                        o2[pl.ds(group * 8 + j, n2, stride=8), :] = word


def _hy_filter_stage_a(l_total, w1p, b1, w2, b2, w3, tab, *, ct, n1, n2):
    n2h = n2 // 2
    nslab = BRANCH_WIDTH // LANE // 2
    full = lambda a: pl.BlockSpec(a.shape, lambda g: (0,) * a.ndim)
    return pl.pallas_call(
        functools.partial(_hy_filter_stage_a_kernel, l_total=l_total, n1=n1, n2=n2, ct=ct),
        name="hy_filter_stage_a",
        grid=(n1 // 8,),
        in_specs=[full(w1p), full(b1), full(w2), full(b2), full(w3),
                  pl.BlockSpec((8, 2 * n2, 2 * n2h), lambda g: (g, 0, 0))],
        out_specs=pl.BlockSpec((HY_ORDER, 2, nslab, n2, 8, LANE), lambda g: (0, 0, 0, 0, g, 0)),
        out_shape=jax.ShapeDtypeStruct((HY_ORDER, 2, nslab, n2, n1, LANE), jnp.int32),
        compiler_params=_cparams(("arbitrary",)),
    )(w1p, b1, w2, b2, w3, tab)


@functools.lru_cache(maxsize=None)
def _fft_tables(n1, n2):
    n = n1 * n2
    n2h = n2 // 2
    k2 = np.arange(n2, dtype=np.int64)[:, None]
    m = np.arange(n2h, dtype=np.int64)[None, :]
    ta = np.zeros((n1, 2 * n2, 2 * n2h))
    for a in range(n1):
        th = 2.0 * np.pi * ((k2 * (a + n1 * m)) % n) / n
        c, s = np.cos(th), np.sin(th)
        ta[a] = np.block([[c, s], [-s, c]])
    tai = np.transpose(ta, (0, 2, 1)) / n
    k1 = np.arange(n1, dtype=np.int64)
    ph = 2.0 * np.pi * ((k1[:, None] * k1[None, :]) % n1) / n1
    c, s = np.cos(ph), np.sin(ph)
    fb = np.block([[c, s], [-s, c]])
    fs = fb[:n1] + fb[n1:]
    fsr = np.stack([fs[(-k1 - e) % n1] for e in (0, 1)])
    return tuple(np.asarray(t, np.float32) for t in (ta, tai, fb, fb.T, fs, fsr))


def _fft_split(l_total):
    n = 2 * l_total
    n1 = min(128, l_total // 8)
    return n1, n // n1


def _rows2d(ref):
    rows = 1
    for s in ref.shape[:-1]:
        rows *= s
    return ref.reshape(rows, ref.shape[-1])


def _lane_slabs(shape, imap, ns):
    def slab(s):
        def index_map(*a):
            idx = imap(*a)
            return idx[:-1] + (idx[-1] * ns + s,)
        return pl.BlockSpec(shape + (LANE,), index_map)
    return [slab(s) for s in range(ns)]


def _strided_rows(ref2, group, n, j):
    return ref2[pl.ds(group * 8 + j, n, stride=8), :]


def _hy_stage_a_kernel(gr_ref, cr_ref, gi_ref, ci_ref, *refs, ns, n2, n2h, slab_major_in):
    del gr_ref, cr_ref, gi_ref, ci_ref
    if slab_major_in:
        xs = [[(_rows2d(r), s * n2h) for s in range(ns)] for r in refs[:2]]
        refs = refs[2:]
    else:
        xs = [[(_rows2d(r), 0) for r in refs[q * ns:(q + 1) * ns]] for q in range(2)]
        refs = refs[2 * ns:]
    t_ref, o_ref = refs
    o2 = _rows2d(o_ref)
    for j in range(8):
        x = jnp.concatenate([jnp.concatenate([_strided_rows(r2, grp, n2h, j) for r2, grp in part], axis=1)
                             for part in xs], axis=0).astype(BF16)
        y = jnp.dot(t_ref[j], x, preferred_element_type=F32)
        for p in range(2):
            for q, word in enumerate(_pack_halves(y[p * n2:(p + 1) * n2], ns)):
                o2[pl.ds(((p * (ns // 2) + q) * n2) * 8 + j, n2, stride=8), :] = word


def _hy_stage_a(x, pairs, tab, *, ct, n2, slab_major_in):
    n2h = n2 // 2
    n1 = x.shape[-2] if slab_major_in else x.shape[2]
    npair = len(pairs)
    nct = BRANCH_WIDTH // ct
    ns = ct // LANE
    gr = jnp.asarray([p[0][0] for p in pairs], jnp.int32)
    cr = jnp.asarray([p[0][1] for p in pairs], jnp.int32)
    gi = jnp.asarray([p[1][0] for p in pairs], jnp.int32)
    ci = jnp.asarray([p[1][1] for p in pairs], jnp.int32)
    if slab_major_in:
        in_specs = [pl.BlockSpec((None, ns, n2h, 8, LANE), lambda p, c, g, *_: (0, c, 0, g, 0)),
                    pl.BlockSpec((None, ns, n2h, 8, LANE), lambda p, c, g, *_: (1, c, 0, g, 0))]
        args = [x, x]
    else:
        in_specs = (_lane_slabs((None, n2h, 8), lambda p, c, g, gr, cr, gi, ci: (gr[p], 0, g, cr[p] * nct + c), ns)
                    + _lane_slabs((None, n2h, 8), lambda p, c, g, gr, cr, gi, ci: (gi[p], 0, g, ci[p] * nct + c), ns))
        args = [x] * (2 * ns)
    grid_spec = pltpu.PrefetchScalarGridSpec(
        num_scalar_prefetch=4,
        grid=(npair, nct, n1 // 8),
        in_specs=in_specs + [pl.BlockSpec((8, 2 * n2, 2 * n2h), lambda p, c, g, *_: (g, 0, 0))],
        out_specs=pl.BlockSpec((None, 2, ns // 2, n2, 8, LANE), lambda p, c, g, *_: (p, 0, c, 0, g, 0)),
    )
    return pl.pallas_call(
        functools.partial(_hy_stage_a_kernel, ns=ns, n2=n2, n2h=n2h, slab_major_in=slab_major_in),
        name="hy_stage_a",
        grid_spec=grid_spec,
        out_shape=jax.ShapeDtypeStruct((npair, 2, BRANCH_WIDTH // LANE // 2, n2, n1, LANE), jnp.int32),
        compiler_params=_cparams(("parallel", "parallel", "arbitrary")),
    )(gr, cr, gi, ci, *args, tab)


def _hy_spec_kernel(c_ref, cm_ref, c0_ref, fs_ref, fsr_ref, o_ref, *, kg):
    g = pl.program_id(2)
    for k in range(kg):
        s = jnp.dot(fs_ref[...], _complex_slab(c_ref, k), preferred_element_type=F32)
        if k:
            mirror, fsr = _complex_slab(cm_ref, kg - k), fsr_ref[1]
        else:
            mirror, fsr = _complex_slab(c0_ref, 0), jnp.where(g == 0, fsr_ref[0], fsr_ref[1])
        t = jnp.dot(fsr, mirror, preferred_element_type=F32)
        o_ref[0, k] = 0.5 * (s + t)
        o_ref[1, k] = 0.5 * (s - t)


def _complex_slab(ref, k):
    nh = ref.shape[1]
    return jnp.concatenate([_unpack_halves([ref[p, q, k] for q in range(nh)]) for p in range(2)],
                           axis=0).astype(BF16)


def _hy_spectrum(cs, fs, fsr, *, ct, kg):
    no, _, nslab, n2, n1, _ = cs.shape
    ns = ct // LANE // 2
    ng = n2 // kg
    return pl.pallas_call(
        functools.partial(_hy_spec_kernel, kg=kg),
        name="hy_spectrum",
        grid=(no, nslab // ns, ng),
        in_specs=[pl.BlockSpec((None, 2, ns, kg, n1, LANE), lambda o, j, g: (o, 0, j, g, 0, 0)),
                  pl.BlockSpec((None, 2, ns, kg, n1, LANE), lambda o, j, g: (o, 0, j, ng - 1 - g, 0, 0)),
                  pl.BlockSpec((None, 2, ns, 1, n1, LANE), lambda o, j, g: (o, 0, j, (kg * (ng - g)) % n2, 0, 0)),
                  pl.BlockSpec(fs.shape, lambda o, j, g: (0, 0)),
                  pl.BlockSpec(fsr.shape, lambda o, j, g: (0, 0, 0))],
        out_specs=pl.BlockSpec((None, 2, kg, n1, ct), lambda o, j, g: (o, 0, g, 0, j)),
        out_shape=jax.ShapeDtypeStruct((no, 2, n2, n1, 2 * nslab * LANE), F32),
        compiler_params=_cparams(("parallel", "parallel", "arbitrary")),
    )(cs, cs, cs, fs, fsr)


def _hy_stage_b_kernel(c_ref, k_ref, fb_ref, fbi_ref, o_ref, *, kg, n1):
    ns = 2 * c_ref.shape[1]
    for k in range(kg):
        z = jnp.dot(fb_ref[...], _complex_slab(c_ref, k), preferred_element_type=F32)
        zr, zi = z[:n1], z[n1:]
        kr, ki = k_ref[0, k], k_ref[1, k]
        y = jnp.concatenate([zr * kr - zi * ki, zr * ki + zi * kr], axis=0).astype(BF16)
        d = jnp.dot(fbi_ref[...], y, preferred_element_type=F32)
        for p in range(2):
            for q, word in enumerate(_pack_halves(d[p * n1:(p + 1) * n1], ns)):
                o_ref[p, q, k] = word


def _hy_stage_b(cs, spec, order, fb, fbi, *, ct, kg):
    _, _, nslab, n2, n1, _ = cs.shape
    ns = ct // LANE // 2
    c = 2 * nslab * LANE
    return pl.pallas_call(
        functools.partial(_hy_stage_b_kernel, kg=kg, n1=n1),
        name="hy_stage_b",
        grid=(c // ct, n2 // kg),
        in_specs=[pl.BlockSpec((None, 2, ns, kg, n1, LANE), lambda j, g: (0, 0, j, g, 0, 0)),
                  pl.BlockSpec((None, 2, kg, n1, ct), lambda j, g: (order, 0, g, 0, j)),
                  pl.BlockSpec(fb.shape, lambda j, g: (0, 0)),
                  pl.BlockSpec(fbi.shape, lambda j, g: (0, 0))],
        out_specs=pl.BlockSpec((2, ns, kg, n1, LANE), lambda j, g: (0, j, g, 0, 0)),
        out_shape=jax.ShapeDtypeStruct((2, nslab, n2, n1, LANE), jnp.int32),
        compiler_params=_cparams(("parallel", "arbitrary")),
    )(cs, spec, fb, fbi)


def _hy_stage_c_kernel(*refs, ns, n2, n2h, slab_major_u):
    d2 = _rows2d(refs[0])
    t_ref = refs[1]
    if slab_major_u:
        u2 = _rows2d(refs[2])
        u_parts = [[(u2, (b * ns + s) * n2h) for s in range(ns)] for b in range(2)]
        refs = refs[3:]
    else:
        u_parts = [[(_rows2d(r), b * n2h) for r in refs[2:2 + ns]] for b in range(2)]
        refs = refs[2 + ns:]
    g_parts = [[(_rows2d(r), b * n2h) for r in refs[:ns]] for b in range(2)]
    skip_ref, o_ref = refs[ns:]
    o2 = _rows2d(o_ref)
    skip = skip_ref[...]

    def rows(parts, j):
        return jnp.concatenate([jnp.concatenate([_strided_rows(r2, grp, n2h, j) for r2, grp in part], axis=1)
                                for part in parts], axis=0)

    for j in range(8):
        nh = ns // 2
        d = jnp.concatenate([_unpack_halves([_strided_rows(d2, (p * nh + q) * n2, n2, j) for q in range(nh)])
                             for p in range(2)], axis=0).astype(BF16)
        y = jnp.dot(t_ref[j], d, preferred_element_type=F32)
        out = rows(g_parts, j) * (y + skip * rows(u_parts, j))
        for b in range(2):
            for s in range(ns):
                o2[pl.ds(((b * ns + s) * n2h) * 8 + j, n2h, stride=8), :] = (
                    out[b * n2h:(b + 1) * n2h, s * LANE:(s + 1) * LANE])


def _hy_stage_c(d, tai, u, ucol, z4, gcol, skip, *, ct, slab_major_u):
    _, npacked, n2, n1, _ = d.shape
    nslab = 2 * npacked
    n2h = n2 // 2
    ns = ct // LANE
    uc, gc = ucol // ct, gcol // ct
    if slab_major_u:
        u_specs = [pl.BlockSpec((2, ns, n2h, 8, LANE), lambda j, g: (0, j, 0, g, 0))]
        u_args = [u]
    else:
        u_specs = _lane_slabs((2, n2h, 8), lambda j, g: (0, 0, g, uc + j), ns)
        u_args = [u] * ns
    return pl.pallas_call(
        functools.partial(_hy_stage_c_kernel, ns=ns, n2=n2, n2h=n2h, slab_major_u=slab_major_u),
        name="hy_stage_c",
        grid=(nslab // ns, n1 // 8),
        in_specs=([pl.BlockSpec((2, ns // 2, n2, 8, LANE), lambda j, g: (0, j, 0, g, 0)),
                   pl.BlockSpec((8, 2 * n2h, 2 * n2), lambda j, g: (g, 0, 0))]
                  + u_specs + _lane_slabs((2, n2h, 8), lambda j, g: (0, 0, g, gc + j), ns)
                  + [pl.BlockSpec((1, ct), lambda j, g: (0, j))]),
        out_specs=pl.BlockSpec((2, ns, n2h, 8, LANE), lambda j, g: (0, j, 0, g, 0)),
        out_shape=jax.ShapeDtypeStruct((2, nslab, n2h, n1, LANE), F32),
        compiler_params=_cparams(("parallel", "arbitrary")),
    )(d, tai, *u_args, *([z4] * ns), skip)


def _hyena(h, filt_params, skip):
    bsz, l, _ = h.shape
    assert bsz == 2
    n1, n2 = _fft_split(l)
    n2h = n2 // 2
    ta, tai, fb, fbi, fs, fsr = (jnp.asarray(t).astype(BF16) for t in _fft_tables(n1, n2))
    ct = 512
    kg = 8

    cs_k = _hy_filter_stage_a(l, *filt_params, ta, ct=ct, n1=n1, n2=n2)
    spec = _hy_spectrum(cs_k, fs, fsr, ct=ct, kg=kg)

    z4 = h.reshape(bsz, n2h, n1, -1)
    u, ucol = z4, COL_HY
    for o in range(HY_ORDER):
        first = o == 0
        cs = _hy_stage_a(u, [((0, ucol // BRANCH_WIDTH), (1, ucol // BRANCH_WIDTH))], ta, ct=ct, n2=n2,
                         slab_major_in=not first)
        d = _hy_stage_b(cs, spec, o, fb, fbi, ct=ct, kg=kg)
        u = _hy_stage_c(d, tai, u, ucol, z4, COL_HY + (o + 1) * BRANCH_WIDTH, skip[o:o + 1], ct=ct,
                        slab_major_u=not first)
        ucol = 0
    return u.reshape(bsz, BRANCH_WIDTH // LANE, l, LANE)


def _conv_tables(hy_conv_w, hy_conv_b, dn_conv_q, dn_conv_k, dn_conv_v):
    depth = hy_conv_w.shape[0]
    ident = lambda n: jnp.broadcast_to(jnp.asarray([0.0, 1.0, 0.0], F32)[None, :, None], (depth, 3, n))
    taps = jnp.concatenate([ident(COL_HY), hy_conv_w, ident(COL_DK - COL_FF), dn_conv_k, dn_conv_v,
                            ident(COL_DQ - COL_AB), dn_conv_q, ident(COL_PAD - COL_DQ - dn_conv_q.shape[2])], axis=2)
    bias = jnp.concatenate([jnp.zeros((depth, 1, COL_HY), F32), hy_conv_b[:, None, :],
                            jnp.zeros((depth, 1, COL_PAD - COL_FF), F32)], axis=2)
    return taps, bias


def _token_mixers(u_src, sh, sc, w_t, conv_t, layer, lp, states, *, period, with_output, tm):
    l = lp["seq_len"]
    bsz = u_src.shape[0] * u_src.shape[1] // l
    tblk = min(l, 2048)
    taps, bias = (a[layer:layer + 1] for a in conv_t)
    if with_output:
        h = _in_proj(u_src, sh, sc, w_t, layer, range(COL_PAD // IN_TN), taps, bias, tm=tm, period=period)
        base = 0
    else:
        lo, hi = STATE_BLOCKS[0] * IN_TN, (STATE_BLOCKS[-1] + 1) * IN_TN
        h = _in_proj(u_src, sh, sc, w_t, layer, STATE_BLOCKS, taps[:, :, lo:hi], bias[:, :, lo:hi],
                     tm=tm, period=period)
        base = COL_FF
    h = h.reshape(bsz, l, h.shape[-1])
    col = lambda c: c - base
    ab_t = jnp.swapaxes(h[:, :, col(COL_AB):col(COL_AB) + 4 * HEADS], 1, 2)
    hg_f, hg_b, hg_states = _hgrn2_scan(h, (col(COL_FF), col(COL_FB), col(COL_HI), col(COL_HQ)),
                                        lp["lb_f"], lp["lb_b"], states[0], tblk=tblk, with_output=with_output)
    dn_f, dn_b, dn_states = _deltanet_scan(h, ab_t, (col(COL_DK), col(COL_DV), col(COL_DQ)),
                                           lp["dn_a_log"], lp["dn_dt_bias"], states[1],
                                           tblk=min(l, 2048), with_output=with_output)
    new_states = (hg_states, dn_states)
    if not with_output:
        return None, new_states
    tl = min(l, 2048)
    hg_out = _combine(hg_f, hg_b, h, COL_HGATE, lp["hg_norm_w"], tl=tl, use_silu=False)
    dn_out = _combine(dn_f, dn_b, h, COL_DZ, lp["dn_norm_w"], tl=tl, use_silu=True)
    hy_out = _hyena(h, lp["hy_filt"], lp["hy_skip"])
    ysum = _branch_merge([hy_out, hg_out, dn_out], lp["w_branch"], layer, h, tm=min(l, 1024), tn=512)
    return ysum, new_states


def kernel(x, c, ctx, c_ctx, w_ada, b_ada, w_in, hy_conv_w, hy_conv_b, hy_filt_w1, hy_filt_b1, hy_filt_w2, hy_filt_b2, hy_filt_w3, hy_skip, hg_lb_logits, hg_norm_w, dn_conv_q, dn_conv_k, dn_conv_v, dn_a_log, dn_dt_bias, dn_norm_w, w_branch, w_out, ln1_g, ln1_b, w_ff1, w_ff2, ln2_g, ln2_b):
    depth = w_in.shape[0]
    bsz, _, d = x.shape
    alpha = (2 * depth) ** 0.25
    p = jax.nn.softmax(hg_lb_logits.astype(F32), axis=1)
    lower = jnp.cumsum(p, axis=1) - p[:, :1]
    cs = jnp.concatenate([c, c_ctx[None], jnp.zeros((8 - bsz - 1, d), F32)], axis=0)
    w_r = jnp.swapaxes(w_in, 1, 2)
    w_b, w_o, w_1, w_2 = w_branch.astype(BF16), w_out.astype(BF16), w_ff1, w_ff2.astype(BF16)
    b_ada3 = b_ada[:, None, :]
    conv_t = _conv_tables(hy_conv_w, hy_conv_b, dn_conv_q, dn_conv_k, dn_conv_v)
    lc = ctx.shape[1]
    h_ctx = ctx.reshape(1, bsz * lc, d)
    for l in range(depth):
        last = l == depth - 1
        row = lambda v: v.reshape(1, -1)
        lp = {
            "lb_f": row(lower[0, l]), "lb_b": row(lower[1, l]),
            "hg_norm_w": row(hg_norm_w[l]), "dn_norm_w": row(dn_norm_w[l]),
            "dn_a_log": dn_a_log[l], "dn_dt_bias": dn_dt_bias[l], "hy_skip": hy_skip[l],
            "hy_filt": (jnp.pad(hy_filt_w1[l], ((0, LANE - HY_EMB), (0, 0))), row(hy_filt_b1[l]),
                        hy_filt_w2[l], row(hy_filt_b2[l]), hy_filt_w3[l]),
            "w_branch": w_b,
        }
        g1n, b1n, g2n, b2n = row(ln1_g[l]), row(ln1_b[l]), row(ln2_g[l]), row(ln2_b[l])
        mod = _ada_mod(cs, w_ada, b_ada3, l)
        m_lat = [mod[:bsz, i * d:(i + 1) * d][:, None, :] for i in range(6)]
        m_ctx = [mod[bsz, i * d:(i + 1) * d][None, None, :] for i in range(6)]
        zero = jnp.zeros((bsz, HEADS, HEAD_DIM, HEAD_DIM), F32)
        init = ((zero, zero), (zero, zero))

        rows_c = bsz * lc
        y_ctx, ctx_states = _token_mixers(h_ctx, m_ctx[0], m_ctx[1], w_r, conv_t, l, dict(lp, seq_len=lc), init,
                                          period=lc, with_output=not last, tm=rows_c)
        if not last:
            y_ctx = y_ctx.reshape(1, rows_c, d)
            h_ctx = _mm_ln(y_ctx, w_o, l, h_ctx, m_ctx[2], g1n, b1n, tm=rows_c, tk=d, alpha=alpha)
            mid = _mlp_up(h_ctx, m_ctx[3], m_ctx[4], w_1, l, tm=rows_c, tn=1024)
            h_ctx = _mm_ln(mid, w_2, l, h_ctx, m_ctx[5], g2n, b2n, tm=rows_c, tk=1024, alpha=alpha)

        lx = x.shape[1]
        y, _ = _token_mixers(x, m_lat[0], m_lat[1], w_r, conv_t, l, dict(lp, seq_len=lx), ctx_states,
                             period=GRID_W, with_output=True, tm=min(lx, 1024))
        x = _mm_ln(y, w_o, l, x, m_lat[2], g1n, b1n, tm=min(lx, 512), tk=d, alpha=alpha)
        mid = _mlp_up(x, m_lat[3], m_lat[4], w_1, l, tm=min(lx, 1024), tn=1024)
        x = _mm_ln(mid, w_2, l, x, m_lat[5], g2n, b2n, tm=min(lx, 512), tk=2048, alpha=alpha)
    return x
```

```python
import functools
import math

import numpy as np
import jax
import jax.numpy as jnp
from jax import lax
from jax.experimental import pallas as pl
from jax.experimental.pallas import tpu as pltpu

F32 = jnp.float32
BF16 = jnp.bfloat16

D_MODEL = 2048
GRID_W = 64
BRANCH_WIDTH = 1024
N_BRANCH = 3
HY_ORDER = 2
HY_EMB = 33
HY_BANDS = (HY_EMB - 1) // 2
HY_HIDDEN = 64
HY_DECAY_TARGET = 1e-2
HY_FAST_DECAY = 0.3
HY_SLOW_DECAY = 1.5
HEADS = 8
HEAD_DIM = 128
DN_QK_HEADS = 4
HG_F_MIN = 1e-30
CHUNK = 64
LN_EPS = 1e-5
RMS_EPS = 1e-6
LANE = 128

IN_TN = 1024
COL_GATE = 0
COL_HY = 6144
COL_FF = 9216
COL_FB = 10240
COL_HI = 11264
COL_DK = 12288
COL_DV = 12800
COL_AB = 13824
COL_HQ = 14336
COL_HGATE = 15360
COL_DQ = 16384
COL_DZ = 17408
COL_PAD = 18432
SOURCE_ROWS = np.asarray([11296 + IN_TN * k for k in range(6)] + [8224 + IN_TN * k for k in range(3)]
                         + [0, 1024, 2048, 3072, 4096, 4640, 5664, 6688, 7200], np.int32)
CONV_WINDOWS = np.asarray([COL_HY <= IN_TN * j < COL_FF or COL_DK <= IN_TN * j < COL_HQ or IN_TN * j == COL_DQ
                           for j in range(COL_PAD // IN_TN)], np.int32)
STATE_BLOCKS = tuple(range(COL_FF // IN_TN, COL_HQ // IN_TN))

VMEM_LIMIT = 56 * 1024 * 1024


def _cparams(sem):
    return pltpu.CompilerParams(dimension_semantics=sem, vmem_limit_bytes=VMEM_LIMIT)


def _dot(a, b):
    return jnp.dot(a.astype(BF16), b.astype(BF16), preferred_element_type=F32)


def _dot_nt(a, b):
    return lax.dot_general(a.astype(BF16), b.astype(BF16), (((1,), (1,)), ((), ())),
                           preferred_element_type=F32)


def _dot_tn(a, b):
    return lax.dot_general(a.astype(BF16), b.astype(BF16), (((0,), (0,)), ((), ())),
                           preferred_element_type=F32)


def _sigmoid(x):
    return 1.0 / (1.0 + jnp.exp(-x))


def _silu(x):
    return x * _sigmoid(x)


def _ada_kernel(c_ref, w_ref, b_ref, o_ref):
    a = _silu(c_ref[...])
    o_ref[...] = _dot(a, w_ref[...]) + b_ref[...]


def _ada_mod(cs, w, b, layer):
    _, d, n = w.shape
    tn = 1024
    return pl.pallas_call(
        _ada_kernel,
        grid=(n // tn,),
        in_specs=[pl.BlockSpec((8, d), lambda j: (0, 0)),
                  pl.BlockSpec((None, d, tn), lambda j: (layer, 0, j)),
                  pl.BlockSpec((None, 1, tn), lambda j: (layer, 0, j))],
        out_specs=pl.BlockSpec((8, tn), lambda j: (0, j)),
        out_shape=jax.ShapeDtypeStruct((8, n), F32),
        compiler_params=_cparams(("arbitrary",)),
        name="ada_mod",
    )(cs, w, b)


def _conv3(x, w, period):
    t = x.shape[0]
    pos = jnp.bitwise_and(_iota2((t, 1), 0), period - 1)
    has_l, has_r = pos != 0, pos != period - 1
    xl = jnp.where(has_l, pltpu.roll(x, 1, 0), 0.0)
    xr = jnp.where(has_r, pltpu.roll(x, t - 1, 0), 0.0)
    return xl * w[0:1] + x * w[1:2] + xr * w[2:3]


def _mlp_up_kernel(x_ref, sh_ref, sc_ref, w_ref, o_ref, a_scr):
    @pl.when(pl.program_id(2) == 0)
    def _():
        a_scr[...] = (x_ref[...] * (1.0 + sc_ref[...]) + sh_ref[...]).astype(BF16)

    acc = jnp.dot(a_scr[...], w_ref[...].astype(BF16), preferred_element_type=F32)
    o_ref[...] = jnp.square(jnp.maximum(acc, 0.0)).astype(o_ref.dtype)


def _mlp_up(x, sh, sc, w, layer, *, tm, tn):
    bsz, l, d = x.shape
    n = w.shape[2]
    return pl.pallas_call(
        _mlp_up_kernel,
        grid=(bsz, l // tm, n // tn),
        in_specs=[pl.BlockSpec((None, tm, d), lambda b, i, j: (b, i, 0)),
                  pl.BlockSpec((None, 1, d), lambda b, i, j: (b, 0, 0)),
                  pl.BlockSpec((None, 1, d), lambda b, i, j: (b, 0, 0)),
                  pl.BlockSpec((None, d, tn), lambda b, i, j: (layer, 0, j))],
        out_specs=pl.BlockSpec((None, tm, tn), lambda b, i, j: (b, i, j)),
        out_shape=jax.ShapeDtypeStruct((bsz, l, n), BF16),
        scratch_shapes=[pltpu.VMEM((tm, d), BF16)],
        compiler_params=_cparams(("parallel", "parallel", "arbitrary")),
        name="mm_relu2",
    )(x, sh, sc, w)


def _in_proj_kernel(off_ref, conv_ref, x_ref, sh_ref, sc_ref, w_ref, cw_ref, cb_ref, o_ref, a_scr, *, period):
    del off_ref
    has_conv = conv_ref[pl.program_id(2)] != 0

    @pl.when(pl.program_id(2) == 0)
    def _():
        a_scr[...] = (x_ref[...] * (1.0 + sc_ref[...]) + sh_ref[...]).astype(BF16)

    o_ref[...] = lax.dot_general(a_scr[...], w_ref[0].astype(BF16), (((1,), (1,)), ((), ())),
                                 preferred_element_type=F32)

    @pl.when(has_conv)
    def _():
        o_ref[...] = _conv3(o_ref[...], cw_ref[...], period) + cb_ref[...]


def _in_proj(x, sh, sc, w_t, layer, windows, taps, bias, *, tm, period):
    bsz, l, d = x.shape
    nblk = len(windows)
    assert tm % period == 0 and period & (period - 1) == 0
    row_off = jnp.asarray(SOURCE_ROWS[list(windows)])
    has_conv = jnp.asarray(CONV_WINDOWS[list(windows)])
    grid_spec = pltpu.PrefetchScalarGridSpec(
        num_scalar_prefetch=2,
        grid=(bsz, l // tm, nblk),
        in_specs=[pl.BlockSpec((None, tm, d), lambda b, i, j, off, cv: (b, i, 0)),
                  pl.BlockSpec((None, 1, d), lambda b, i, j, off, cv: (b, 0, 0)),
                  pl.BlockSpec((None, 1, d), lambda b, i, j, off, cv: (b, 0, 0)),
                  pl.BlockSpec((pl.Element(1), pl.Element(IN_TN), pl.Element(d)),
                               lambda b, i, j, off, cv: (layer, pl.multiple_of(off[j], 4 * HEADS), 0)),
                  pl.BlockSpec((None, 3, IN_TN), lambda b, i, j, off, cv: (0, 0, j)),
                  pl.BlockSpec((None, 1, IN_TN), lambda b, i, j, off, cv: (0, 0, j))],
        out_specs=pl.BlockSpec((None, tm, IN_TN), lambda b, i, j, off, cv: (b, i, j)),
        scratch_shapes=[pltpu.VMEM((tm, d), BF16)],
    )
    return pl.pallas_call(
        functools.partial(_in_proj_kernel, period=period),
        name="mm_in_proj",
        grid_spec=grid_spec,
        out_shape=jax.ShapeDtypeStruct((bsz, l, nblk * IN_TN), F32),
        compiler_params=_cparams(("parallel", "parallel", "arbitrary")),
    )(row_off, has_conv, x, sh, sc, w_t, taps, bias)


def _mm_ln_kernel(a_ref, w_ref, x_ref, gate_ref, g_ref, b_ref, o_ref, acc_ref, *, nk, alpha):
    k = pl.program_id(2)

    @pl.when(k == 0)
    def _():
        acc_ref[...] = jnp.dot(a_ref[...], w_ref[...].astype(BF16), preferred_element_type=F32)

    @pl.when(k > 0)
    def _():
        acc_ref[...] += jnp.dot(a_ref[...], w_ref[...].astype(BF16), preferred_element_type=F32)

    @pl.when(k == nk - 1)
    def _():
        y = alpha * x_ref[...] + gate_ref[...] * acc_ref[...]
        mu = jnp.mean(y, axis=-1, keepdims=True)
        yc = y - mu
        var = jnp.mean(yc * yc, axis=-1, keepdims=True)
        o_ref[...] = yc * lax.rsqrt(var + LN_EPS) * g_ref[...] + b_ref[...]


def _mm_ln(a, w, layer, x, gate, ln_g, ln_b, *, tm, tk, alpha):
    bsz, l, kdim = a.shape
    d = w.shape[2]
    nk = kdim // tk
    return pl.pallas_call(
        functools.partial(_mm_ln_kernel, nk=nk, alpha=alpha),
        grid=(bsz, l // tm, nk),
        in_specs=[pl.BlockSpec((None, tm, tk), lambda b, i, k: (b, i, k)),
                  pl.BlockSpec((None, tk, d), lambda b, i, k: (layer, k, 0)),
                  pl.BlockSpec((None, tm, d), lambda b, i, k: (b, i, 0)),
                  pl.BlockSpec((None, 1, d), lambda b, i, k: (b, 0, 0)),
                  pl.BlockSpec((1, d), lambda b, i, k: (0, 0)),
                  pl.BlockSpec((1, d), lambda b, i, k: (0, 0))],
        out_specs=pl.BlockSpec((None, tm, d), lambda b, i, k: (b, i, 0)),
        out_shape=jax.ShapeDtypeStruct((bsz, l, d), F32),
        scratch_shapes=[pltpu.VMEM((tm, d), F32)],
        compiler_params=_cparams(("parallel", "parallel", "arbitrary")),
        name="mm_resid_ln",
    )(a, w, x, gate, ln_g, ln_b)


def _branch_kernel(b0_ref, b1_ref, b2_ref, w_ref, g0_ref, g1_ref, g2_ref, o_ref):
    b0 = jnp.concatenate([b0_ref[s] for s in range(b0_ref.shape[0])], axis=1).astype(BF16)
    acc = _sigmoid(g0_ref[...]) * jnp.dot(b0, w_ref[0].astype(BF16), preferred_element_type=F32)
    acc += _sigmoid(g1_ref[...]) * jnp.dot(b1_ref[...], w_ref[1].astype(BF16), preferred_element_type=F32)
    acc += _sigmoid(g2_ref[...]) * jnp.dot(b2_ref[...], w_ref[2].astype(BF16), preferred_element_type=F32)
    o_ref[...] = acc.astype(o_ref.dtype)


def _branch_merge(branches, w, layer, h, *, tm, tn):
    bsz, l, c = branches[1].shape
    d = w.shape[3]
    nj = d // tn
    br_spec = pl.BlockSpec((None, tm, c), lambda b, i, j: (b, i, 0))
    hy_spec = pl.BlockSpec((None, c // LANE, tm, LANE), lambda b, i, j: (b, 0, i, 0))

    def gate_spec(g):
        return pl.BlockSpec((None, tm, tn), lambda b, i, j: (b, i, COL_GATE // tn + g * nj + j))

    return pl.pallas_call(
        _branch_kernel,
        grid=(bsz, l // tm, nj),
        in_specs=[hy_spec, br_spec, br_spec,
                  pl.BlockSpec((None, N_BRANCH, c, tn), lambda b, i, j: (layer, 0, 0, j)),
                  gate_spec(0), gate_spec(1), gate_spec(2)],
        out_specs=pl.BlockSpec((None, tm, tn), lambda b, i, j: (b, i, j)),
        out_shape=jax.ShapeDtypeStruct((bsz, l, d), BF16),
        compiler_params=_cparams(("parallel", "parallel", "arbitrary")),
        name="branch_merge",
    )(*branches, w, h, h, h)


def _iota2(shape, axis):
    return lax.broadcasted_iota(jnp.int32, shape, axis)


def _split3(g):
    g1 = g.astype(BF16)
    r = g - g1.astype(F32)
    g2 = r.astype(BF16)
    g3 = (r - g2.astype(F32)).astype(BF16)
    return g1, g2, g3


def _waves(items, nwaves):
    n = min(nwaves, max(1, len(items) // 8))
    per = len(items) // n
    return [items[w * per:(w + 1) * per] for w in range(n)]


def _cumsum_rows(g, rev):
    row = _iota2((CHUNK, 3 * CHUNK), 0)
    col = jnp.bitwise_and(_iota2((CHUNK, 3 * CHUNK), 1), CHUNK - 1)
    tri = jnp.where((col >= row) if rev else (col <= row), 1.0, 0.0).astype(BF16)
    g3 = jnp.concatenate(_split3(g), axis=0)
    return jnp.dot(tri, g3, preferred_element_type=F32)


def _hg_prepare(items):
    for it in items:
        lb = it["lb"]
        sg = _sigmoid(it["fz"])
        it["g"] = jnp.log(jnp.maximum(lb + (1.0 - lb) * sg, HG_F_MIN))
        it["kk"] = (1.0 - lb) * (1.0 - sg)
    for it in items:
        it["bc"] = _cumsum_rows(it["g"], it["rev"])
    half = CHUNK // 2
    for it in items:
        bc = it["bc"]
        if it["rev"]:
            mid, last = bc[half:half + 1], bc[0:1]
        else:
            mid, last = bc[half - 1:half], bc[CHUNK - 1:CHUNK]
        it["e_last"] = jnp.exp(last)
        kd = it["kk"] * jnp.exp(mid - bc)
        it["ks"] = kd * jnp.exp(last - mid)
        if it["q"] is not None:
            qd = _silu(it["q"]) * jnp.exp(bc - mid)
            it["qd"], it["kd"] = qd, kd
            it["qs"] = qd * jnp.exp(mid)
    r = _iota2((CHUNK, CHUNK), 0)
    c = _iota2((CHUNK, CHUNK), 1)
    for it in items:
        it["kv"] = _dot_tn(it["v"], it["ks"])
        if it["q"] is not None:
            att = _dot_nt(it["qd"], it["kd"])
            it["att"] = jnp.where((c >= r) if it["rev"] else (c <= r), att, 0.0)
    for it in items:
        if it["q"] is not None:
            it["o_c"] = _dot(it["att"], it["v"])


def _hg_kernel(*refs, nchunk, with_output):
    if with_output:
        (ff_ref, vf_ref, qf_ref, fb_ref, vb_ref, qb_ref, lbf_ref, lbb_ref, s0f_ref, s0b_ref,
         of_ref, ob_ref, sf_ref, sb_ref, stf, stb) = refs
    else:
        (ff_ref, vf_ref, fb_ref, vb_ref, lbf_ref, lbb_ref, s0f_ref, s0b_ref,
         sf_ref, sb_ref, stf, stb) = refs
        qf_ref = qb_ref = of_ref = ob_ref = None

    @pl.when(pl.program_id(2) == 0)
    def _():
        stf[...] = s0f_ref[...]
        stb[...] = s0b_ref[...]

    def chunk_item(f_ref, v_ref, q_ref, lb_ref, ci, rev):
        sl = pl.ds(ci * CHUNK, CHUNK)
        return {"fz": f_ref[sl, :], "v": v_ref[sl, :], "q": q_ref[sl, :] if with_output else None,
                "lb": lb_ref[...], "rev": rev, "sl": sl}

    items = []
    for ci in range(nchunk):
        items.append(chunk_item(ff_ref, vf_ref, qf_ref, lbf_ref, ci, False))
        items.append(chunk_item(fb_ref, vb_ref, qb_ref, lbb_ref, nchunk - 1 - ci, True))
    state = {False: stf[...], True: stb[...]}
    for wave in _waves(items, 8):
        _hg_prepare(wave)
        for it in wave:
            st = state[it["rev"]]
            if with_output:
                o_ref = ob_ref if it["rev"] else of_ref
                o_ref[it["sl"], :] = it["o_c"] + _dot_nt(it["qs"], st)
            state[it["rev"]] = st * it["e_last"] + it["kv"]
    stf[...] = state[False]
    stb[...] = state[True]
    sf_ref[...] = state[False]
    sb_ref[...] = state[True]


def _hgrn2_scan(h, cols, lb_f, lb_b, s0, *, tblk, with_output):
    bsz, l, _ = h.shape
    nt = l // tblk
    c_ff, c_fb, c_i, c_q = (c // LANE for c in cols)

    def fwd(cb):
        return pl.BlockSpec((None, tblk, LANE), lambda b, hh, t: (b, t, cb + hh))

    def bwd(cb):
        return pl.BlockSpec((None, tblk, LANE), lambda b, hh, t: (b, nt - 1 - t, cb + hh))

    lb_spec = pl.BlockSpec((1, LANE), lambda b, hh, t: (0, hh))
    st_spec = pl.BlockSpec((None, None, HEAD_DIM, HEAD_DIM), lambda b, hh, t: (b, hh, 0, 0))
    if with_output:
        in_specs = [fwd(c_ff), fwd(c_i), fwd(c_q), bwd(c_fb), bwd(c_i), bwd(c_q)]
        args = [h] * 6
    else:
        in_specs = [fwd(c_ff), fwd(c_i), bwd(c_fb), bwd(c_i)]
        args = [h] * 4
    in_specs += [lb_spec, lb_spec, st_spec, st_spec]
    args += [lb_f, lb_b, s0[0], s0[1]]
    st_shape = jax.ShapeDtypeStruct((bsz, HEADS, HEAD_DIM, HEAD_DIM), F32)
    out_specs = [st_spec, st_spec]
    out_shape = [st_shape, st_shape]
    if with_output:
        o_shape = jax.ShapeDtypeStruct((bsz, l, HEADS * HEAD_DIM), F32)
        out_specs = [pl.BlockSpec((None, tblk, LANE), lambda b, hh, t: (b, t, hh)),
                     pl.BlockSpec((None, tblk, LANE), lambda b, hh, t: (b, nt - 1 - t, hh))] + out_specs
        out_shape = [o_shape, o_shape] + out_shape
    outs = pl.pallas_call(
        functools.partial(_hg_kernel, nchunk=tblk // CHUNK, with_output=with_output),
        name="hgrn2_scan",
        grid=(bsz, HEADS, nt),
        in_specs=in_specs,
        out_specs=out_specs,
        out_shape=out_shape,
        scratch_shapes=[pltpu.VMEM((HEAD_DIM, HEAD_DIM), F32), pltpu.VMEM((HEAD_DIM, HEAD_DIM), F32)],
        compiler_params=_cparams(("parallel", "parallel", "arbitrary")),
    )(*args)
    if with_output:
        return outs[0], outs[1], (outs[2], outs[3])
    return None, None, (outs[0], outs[1])


def _l2norm(t):
    return t * lax.rsqrt(jnp.sum(t * t, axis=-1, keepdims=True) + 1e-6)


def _softplus(x):
    return jnp.maximum(x, 0.0) + jnp.log(1.0 + jnp.exp(-jnp.abs(x)))


def _dn_prepare(items):
    r = _iota2((CHUNK, CHUNK), 0)
    c = _iota2((CHUNK, CHUNK), 1)
    eye = r == c
    for it in items:
        rev = it["rev"]
        incl = (c >= r) if rev else (c <= r)
        incl_t = (r >= c) if rev else (r <= c)
        g_b = jnp.broadcast_to(it["g"], (CHUNK, CHUNK))
        gc_col = jnp.sum(jnp.where(incl, g_b, 0.0), axis=1, keepdims=True)
        g_col = jnp.sum(jnp.where(eye, g_b, 0.0), axis=1, keepdims=True)
        gc_row = jnp.sum(jnp.where(incl_t, jnp.broadcast_to(g_col, (CHUNK, CHUNK)), 0.0), axis=0, keepdims=True)
        beta_col = jnp.sum(jnp.where(eye, jnp.broadcast_to(it["beta"], (CHUNK, CHUNK)), 0.0), axis=1,
                           keepdims=True)
        it["decay"] = jnp.where(incl, jnp.exp(jnp.where(incl, gc_col - gc_row, 0.0)), 0.0)
        g_last = gc_col[0:1] if rev else gc_col[CHUNK - 1:CHUNK]
        it["e_last"] = jnp.exp(g_last)
        egc = jnp.exp(gc_col)
        kb = it["k"] * beta_col
        it["kb"] = kb
        it["rhs"] = jnp.concatenate([it["v"] * beta_col, kb * egc], axis=1)
        it["kd"] = it["k"] * jnp.exp(g_last - gc_col)
        if it["q"] is not None:
            it["qe"] = it["q"] * egc
    for it in items:
        if it["q"] is not None:
            both = _dot_nt(jnp.concatenate([it["kb"], it["q"]], axis=0), it["k"])
            it["kk"], it["qk"] = both[:CHUNK], both[CHUNK:] * it["decay"]
        else:
            it["kk"] = _dot_nt(it["kb"], it["k"])
    for it in items:
        strict = (c > r) if it["rev"] else (c < r)
        n = jnp.where(strict, it["kk"] * it["decay"], 0.0)
        it["n"] = n
        it["p"] = jnp.where(eye, 1.0, 0.0) - n
    for it in items:
        it["m"] = _dot(it["n"], it["n"])
    for i in range(5):
        for it in items:
            if i < 4:
                pm = _dot(jnp.concatenate([it["p"], it["m"]], axis=0), it["m"])
                it["p"] = it["p"] + pm[:CHUNK]
                it["m"] = pm[CHUNK:]
            else:
                it["p"] = it["p"] + _dot(it["p"], it["m"])
    for it in items:
        it["uw"] = _dot(it["p"], it["rhs"])
    for it in items:
        bm = _dot_tn(it["kd"], it["uw"])
        it["b_c"], it["m_c"] = bm[:, :HEAD_DIM], bm[:, HEAD_DIM:]
        if it["q"] is not None:
            ow = _dot(it["qk"], it["uw"])
            it["o_c"], it["q_eff"] = ow[:, :HEAD_DIM], it["qe"] - ow[:, HEAD_DIM:]


def _dn_kernel(*refs, nchunk, with_output):
    if with_output:
        (alog_ref, dtb_ref, kf_ref, vf_ref, qf_ref, abf_ref, kb_ref, vb_ref, qb_ref, abb_ref,
         s0f_ref, s0b_ref, of_ref, ob_ref, sf_ref, sb_ref, stf, stb) = refs
    else:
        (alog_ref, dtb_ref, kf_ref, vf_ref, abf_ref, kb_ref, vb_ref, abb_ref,
         s0f_ref, s0b_ref, sf_ref, sb_ref, stf, stb) = refs
        qf_ref = qb_ref = of_ref = ob_ref = None
    hh = pl.program_id(1)

    @pl.when(pl.program_id(2) == 0)
    def _():
        stf[...] = s0f_ref[...]
        stb[...] = s0b_ref[...]

    def prep(k_ref, v_ref, q_ref, ab_ref, d):
        k = _l2norm(_silu(k_ref[...]))
        v = _silu(v_ref[...])
        q = None
        if with_output:
            q = _l2norm(_silu(q_ref[...])) * (HEAD_DIM ** -0.5)
        a = ab_ref[pl.ds(d * HEADS + hh, 1), :]
        bt = ab_ref[pl.ds(2 * HEADS + d * HEADS + hh, 1), :]
        g = -jnp.exp(alog_ref[d, hh]) * _softplus(a + dtb_ref[d, hh])
        return k, v, q, g, _sigmoid(bt)

    kf, vf, qf, gf, bf = prep(kf_ref, vf_ref, qf_ref, abf_ref, 0)
    kb, vb, qb, gb, bb = prep(kb_ref, vb_ref, qb_ref, abb_ref, 1)
    def chunk_item(k, v, q, g, beta, ci, rev):
        lo = ci * CHUNK
        return {"k": k[lo:lo + CHUNK], "v": v[lo:lo + CHUNK], "q": q[lo:lo + CHUNK] if with_output else None,
                "g": g[:, lo:lo + CHUNK], "beta": beta[:, lo:lo + CHUNK], "rev": rev, "lo": lo}

    items = []
    for ci in range(nchunk):
        items.append(chunk_item(kf, vf, qf, gf, bf, ci, False))
        items.append(chunk_item(kb, vb, qb, gb, bb, nchunk - 1 - ci, True))
    state = {False: stf[...], True: stb[...]}
    for wave in _waves(items, 4):
        _dn_prepare(wave)
        for it in wave:
            s = state[it["rev"]]
            if with_output:
                o_ref = ob_ref if it["rev"] else of_ref
                o_ref[pl.ds(it["lo"], CHUNK), :] = it["o_c"] + _dot(it["q_eff"], s)
            state[it["rev"]] = s * it["e_last"] + it["b_c"] - _dot(it["m_c"], s)
    stf[...] = state[False]
    stb[...] = state[True]
    sf_ref[...] = state[False]
    sb_ref[...] = state[True]


def _deltanet_scan(h, ab_t, cols, a_log, dt_bias, s0, *, tblk, with_output):
    bsz, l, _ = h.shape
    nt = l // tblk
    c_k, c_v, c_q = (c // LANE for c in cols)
    rep = HEADS // DN_QK_HEADS

    def tok(cb, shared, rev):
        def imap(b, hh, t):
            return (b, (nt - 1 - t) if rev else t, cb + (hh // rep if shared else hh))
        return pl.BlockSpec((None, tblk, LANE), imap)

    def ab_spec(rev):
        return pl.BlockSpec((None, 4 * HEADS, tblk), lambda b, hh, t: (b, 0, (nt - 1 - t) if rev else t))

    smem = pl.BlockSpec(memory_space=pltpu.SMEM)
    st_spec = pl.BlockSpec((None, None, HEAD_DIM, HEAD_DIM), lambda b, hh, t: (b, hh, 0, 0))
    if with_output:
        in_specs = [smem, smem, tok(c_k, True, False), tok(c_v, False, False), tok(c_q, True, False), ab_spec(False),
                    tok(c_k, True, True), tok(c_v, False, True), tok(c_q, True, True), ab_spec(True),
                    st_spec, st_spec]
        args = [a_log, dt_bias, h, h, h, ab_t, h, h, h, ab_t, s0[0], s0[1]]
    else:
        in_specs = [smem, smem, tok(c_k, True, False), tok(c_v, False, False), ab_spec(False),
                    tok(c_k, True, True), tok(c_v, False, True), ab_spec(True),
                    st_spec, st_spec]
        args = [a_log, dt_bias, h, h, ab_t, h, h, ab_t, s0[0], s0[1]]
    st_shape = jax.ShapeDtypeStruct((bsz, HEADS, HEAD_DIM, HEAD_DIM), F32)
    out_specs = [st_spec, st_spec]
    out_shape = [st_shape, st_shape]
    if with_output:
        o_shape = jax.ShapeDtypeStruct((bsz, l, HEADS * HEAD_DIM), F32)
        out_specs = [pl.BlockSpec((None, tblk, LANE), lambda b, hh, t: (b, t, hh)),
                     pl.BlockSpec((None, tblk, LANE), lambda b, hh, t: (b, nt - 1 - t, hh))] + out_specs
        out_shape = [o_shape, o_shape] + out_shape
    outs = pl.pallas_call(
        functools.partial(_dn_kernel, nchunk=tblk // CHUNK, with_output=with_output),
        name="deltanet_scan",
        grid=(bsz, HEADS, nt),
        in_specs=in_specs,
        out_specs=out_specs,
        out_shape=out_shape,
        scratch_shapes=[pltpu.VMEM((HEAD_DIM, HEAD_DIM), F32), pltpu.VMEM((HEAD_DIM, HEAD_DIM), F32)],
        compiler_params=_cparams(("parallel", "parallel", "arbitrary")),
    )(*args)
    if with_output:
        return outs[0], outs[1], (outs[2], outs[3])
    return None, None, (outs[0], outs[1])


def _combine_kernel(of_ref, ob_ref, z_ref, w_ref, o_ref, *, use_silu):
    o = of_ref[...] + ob_ref[...]
    o = o * lax.rsqrt(jnp.mean(o * o, axis=-1, keepdims=True) + RMS_EPS) * w_ref[...]
    z = z_ref[...]
    o_ref[...] = (o * (_silu(z) if use_silu else _sigmoid(z))).astype(o_ref.dtype)


def _combine(o_f, o_b, h, col_z, norm_w, *, tl, use_silu):
    bsz, l, c = o_f.shape
    cz = col_z // LANE
    spec = pl.BlockSpec((None, tl, LANE), lambda b, i, hh: (b, i, hh))
    return pl.pallas_call(
        functools.partial(_combine_kernel, use_silu=use_silu),
        name="mixer_norm_gate",
        grid=(bsz, l // tl, c // LANE),
        in_specs=[spec, spec,
                  pl.BlockSpec((None, tl, LANE), lambda b, i, hh: (b, i, cz + hh)),
                  pl.BlockSpec((1, LANE), lambda b, i, hh: (0, 0))],
        out_specs=spec,
        out_shape=jax.ShapeDtypeStruct((bsz, l, c), BF16),
        compiler_params=_cparams(("parallel", "parallel", "arbitrary")),
    )(o_f, o_b, h, norm_w)


def _dot_f32(a, b):
    return jnp.dot(a, b, precision=lax.Precision.HIGHEST, preferred_element_type=F32)


def _pack_halves(y, ns):
    nh = ns // 2
    return [pltpu.pack_elementwise([y[:, q * LANE:(q + 1) * LANE], y[:, (q + nh) * LANE:(q + nh + 1) * LANE]],
                                   packed_dtype=BF16) for q in range(nh)]


def _unpack_halves(ws):
    part = lambda i: [pltpu.unpack_elementwise(w, index=i, packed_dtype=BF16, unpacked_dtype=F32) for w in ws]
    return jnp.concatenate(part(0) + part(1), axis=1)


def _dot_3pass(a, b):
    a1 = a.astype(BF16)
    a2 = (a - a1.astype(F32)).astype(BF16)
    b1 = b.astype(BF16)
    b2 = (b - b1.astype(F32)).astype(BF16)
    return jnp.dot(jnp.concatenate([a1, a1, a2], axis=1), jnp.concatenate([b1, b2, b1], axis=0),
                   preferred_element_type=F32)


def _hy_filter_stage_a_kernel(w1_ref, b1_ref, w2_ref, b2_ref, w3_ref, t_ref, o_ref, *, l_total, n1, n2, ct):
    n2h = n2 // 2
    rows = 8 * n2h
    ns = ct // LANE
    g = pl.program_id(0)

    def lag(shape):
        r = _iota2(shape, 0)
        return 8 * g + jnp.right_shift(r, n2h.bit_length() - 1) + n1 * jnp.bitwise_and(r, n2h - 1)

    pos = lag((rows, LANE)).astype(F32)
    lane = _iota2((rows, LANE), 1)
    bidx = jnp.where(lane <= HY_BANDS, lane - 1, lane - 1 - HY_BANDS).astype(F32)
    arg = (1e-4 + bidx * ((HY_BANDS - 1 - 1e-4) / (HY_BANDS - 1))) * (pos * (2.0 * math.pi / l_total))
    feats = jnp.where(lane == 0, pos * (1.0 / (l_total - 1)),
                      jnp.where(lane <= HY_BANDS, jnp.cos(arg),
                                jnp.where(lane <= 2 * HY_BANDS, -jnp.sin(arg), 0.0)))
    hid = jnp.sin(_dot_f32(feats, w1_ref[...]) + b1_ref[...])
    hid = jnp.sin(_dot_f32(hid, w2_ref[...]) + b2_ref[...])
    lag_c = lag((rows, 1))
    t_c = lag_c.astype(F32) * (1.0 / (l_total - 1))
    lo = math.log(HY_DECAY_TARGET) / HY_SLOW_DECAY
    hi = math.log(HY_DECAY_TARGET) / HY_FAST_DECAY
    nslab = BRANCH_WIDTH // LANE // 2
    o2 = _rows2d(o_ref)
    for o in range(HY_ORDER):
        for c in range(BRANCH_WIDTH // ct):
            ch = (_iota2((1, ct), 1) + c * ct).astype(F32)
            window = jnp.exp(-t_c * jnp.abs(lo + ch * ((hi - lo) / (BRANCH_WIDTH - 1))))
            col_f = (2 * o) * BRANCH_WIDTH + c * ct
            col_b = col_f + BRANCH_WIDTH
            h_f = _dot_3pass(hid, w3_ref[:, col_f:col_f + ct]) * window
            h_b = jnp.where(lag_c == 0, 0.0, _dot_3pass(hid, w3_ref[:, col_b:col_b + ct]) * window)
            for j in range(8):
                x = jnp.concatenate([h_f[j * n2h:(j + 1) * n2h], h_b[j * n2h:(j + 1) * n2h]], axis=0).astype(BF16)
                y = jnp.dot(t_ref[j], x, preferred_element_type=F32)
                for p in range(2):
                    for q, word in enumerate(_pack_halves(y[p * n2:(p + 1) * n2], ns)):
                        group = (((o * 2 + p) * nslab + c * (ns // 2) + q) * n2)
                        o2[pl.ds(group * 8 + j, n2, stride=8), :] = word


def _hy_filter_stage_a(l_total, w1p, b1, w2, b2, w3, tab, *, ct, n1, n2):
    n2h = n2 // 2
    nslab = BRANCH_WIDTH // LANE // 2
    full = lambda a: pl.BlockSpec(a.shape, lambda g: (0,) * a.ndim)
    return pl.pallas_call(
        functools.partial(_hy_filter_stage_a_kernel, l_total=l_total, n1=n1, n2=n2, ct=ct),
        name="hy_filter_stage_a",
        grid=(n1 // 8,),
        in_specs=[full(w1p), full(b1), full(w2), full(b2), full(w3),
                  pl.BlockSpec((8, 2 * n2, 2 * n2h), lambda g: (g, 0, 0))],
        out_specs=pl.BlockSpec((HY_ORDER, 2, nslab, n2, 8, LANE), lambda g: (0, 0, 0, 0, g, 0)),
        out_shape=jax.ShapeDtypeStruct((HY_ORDER, 2, nslab, n2, n1, LANE), jnp.int32),
        compiler_params=_cparams(("arbitrary",)),
    )(w1p, b1, w2, b2, w3, tab)


@functools.lru_cache(maxsize=None)
def _fft_tables(n1, n2):
    n = n1 * n2
    n2h = n2 // 2
    k2 = np.arange(n2, dtype=np.int64)[:, None]
    m = np.arange(n2h, dtype=np.int64)[None, :]
    ta = np.zeros((n1, 2 * n2, 2 * n2h))
    for a in range(n1):
        th = 2.0 * np.pi * ((k2 * (a + n1 * m)) % n) / n
        c, s = np.cos(th), np.sin(th)
        ta[a] = np.block([[c, s], [-s, c]])
    tai = np.transpose(ta, (0, 2, 1)) / n
    k1 = np.arange(n1, dtype=np.int64)
    ph = 2.0 * np.pi * ((k1[:, None] * k1[None, :]) % n1) / n1
    c, s = np.cos(ph), np.sin(ph)
    fb = np.block([[c, s], [-s, c]])
    fs = fb[:n1] + fb[n1:]
    fsr = np.stack([fs[(-k1 - e) % n1] for e in (0, 1)])
    return tuple(np.asarray(t, np.float32) for t in (ta, tai, fb, fb.T, fs, fsr))


def _fft_split(l_total):
    n = 2 * l_total
    n1 = min(128, l_total // 8)
    return n1, n // n1


def _rows2d(ref):
    rows = 1
    for s in ref.shape[:-1]:
        rows *= s
    return ref.reshape(rows, ref.shape[-1])


def _lane_slabs(shape, imap, ns):
    def slab(s):
        def index_map(*a):
            idx = imap(*a)
            return idx[:-1] + (idx[-1] * ns + s,)
        return pl.BlockSpec(shape + (LANE,), index_map)
    return [slab(s) for s in range(ns)]


def _strided_rows(ref2, group, n, j):
    return ref2[pl.ds(group * 8 + j, n, stride=8), :]


def _hy_stage_a_kernel(gr_ref, cr_ref, gi_ref, ci_ref, *refs, ns, n2, n2h, slab_major_in):
    del gr_ref, cr_ref, gi_ref, ci_ref
    if slab_major_in:
        xs = [[(_rows2d(r), s * n2h) for s in range(ns)] for r in refs[:2]]
        refs = refs[2:]
    else:
        xs = [[(_rows2d(r), 0) for r in refs[q * ns:(q + 1) * ns]] for q in range(2)]
        refs = refs[2 * ns:]
    t_ref, o_ref = refs
    o2 = _rows2d(o_ref)
    for j in range(8):
        x = jnp.concatenate([jnp.concatenate([_strided_rows(r2, grp, n2h, j) for r2, grp in part], axis=1)
                             for part in xs], axis=0).astype(BF16)
        y = jnp.dot(t_ref[j], x, preferred_element_type=F32)
        for p in range(2):
            for q, word in enumerate(_pack_halves(y[p * n2:(p + 1) * n2], ns)):
                o2[pl.ds(((p * (ns // 2) + q) * n2) * 8 + j, n2, stride=8), :] = word


def _hy_stage_a(x, pairs, tab, *, ct, n2, slab_major_in):
    n2h = n2 // 2
    n1 = x.shape[-2] if slab_major_in else x.shape[2]
    npair = len(pairs)
    nct = BRANCH_WIDTH // ct
    ns = ct // LANE
    gr = jnp.asarray([p[0][0] for p in pairs], jnp.int32)
    cr = jnp.asarray([p[0][1] for p in pairs], jnp.int32)
    gi = jnp.asarray([p[1][0] for p in pairs], jnp.int32)
    ci = jnp.asarray([p[1][1] for p in pairs], jnp.int32)
    if slab_major_in:
        in_specs = [pl.BlockSpec((None, ns, n2h, 8, LANE), lambda p, c, g, *_: (0, c, 0, g, 0)),
                    pl.BlockSpec((None, ns, n2h, 8, LANE), lambda p, c, g, *_: (1, c, 0, g, 0))]
        args = [x, x]
    else:
        in_specs = (_lane_slabs((None, n2h, 8), lambda p, c, g, gr, cr, gi, ci: (gr[p], 0, g, cr[p] * nct + c), ns)
                    + _lane_slabs((None, n2h, 8), lambda p, c, g, gr, cr, gi, ci: (gi[p], 0, g, ci[p] * nct + c), ns))
        args = [x] * (2 * ns)
    grid_spec = pltpu.PrefetchScalarGridSpec(
        num_scalar_prefetch=4,
        grid=(npair, nct, n1 // 8),
        in_specs=in_specs + [pl.BlockSpec((8, 2 * n2, 2 * n2h), lambda p, c, g, *_: (g, 0, 0))],
        out_specs=pl.BlockSpec((None, 2, ns // 2, n2, 8, LANE), lambda p, c, g, *_: (p, 0, c, 0, g, 0)),
    )
    return pl.pallas_call(
        functools.partial(_hy_stage_a_kernel, ns=ns, n2=n2, n2h=n2h, slab_major_in=slab_major_in),
        name="hy_stage_a",
        grid_spec=grid_spec,
        out_shape=jax.ShapeDtypeStruct((npair, 2, BRANCH_WIDTH // LANE // 2, n2, n1, LANE), jnp.int32),
        compiler_params=_cparams(("parallel", "parallel", "arbitrary")),
    )(gr, cr, gi, ci, *args, tab)


def _hy_spec_kernel(c_ref, cm_ref, c0_ref, fs_ref, fsr_ref, o_ref, *, kg):
    g = pl.program_id(2)
    for k in range(kg):
        s = jnp.dot(fs_ref[...], _complex_slab(c_ref, k), preferred_element_type=F32)
        if k:
            mirror, fsr = _complex_slab(cm_ref, kg - k), fsr_ref[1]
        else:
            mirror, fsr = _complex_slab(c0_ref, 0), jnp.where(g == 0, fsr_ref[0], fsr_ref[1])
        t = jnp.dot(fsr, mirror, preferred_element_type=F32)
        o_ref[k] = pltpu.pack_elementwise([0.5 * (s + t), 0.5 * (s - t)], packed_dtype=BF16)


def _complex_slab(ref, k):
    nh = ref.shape[1]
    return jnp.concatenate([_unpack_halves([ref[p, q, k] for q in range(nh)]) for p in range(2)],
                           axis=0).astype(BF16)


def _hy_spectrum(cs, fs, fsr, *, ct, kg):
    no, _, nslab, n2, n1, _ = cs.shape
    ns = ct // LANE // 2
    ng = n2 // kg
    return pl.pallas_call(
        functools.partial(_hy_spec_kernel, kg=kg),
        name="hy_spectrum",
        grid=(no, nslab // ns, ng),
        in_specs=[pl.BlockSpec((None, 2, ns, kg, n1, LANE), lambda o, j, g: (o, 0, j, g, 0, 0)),
                  pl.BlockSpec((None, 2, ns, kg, n1, LANE), lambda o, j, g: (o, 0, j, ng - 1 - g, 0, 0)),
                  pl.BlockSpec((None, 2, ns, 1, n1, LANE), lambda o, j, g: (o, 0, j, (kg * (ng - g)) % n2, 0, 0)),
                  pl.BlockSpec(fs.shape, lambda o, j, g: (0, 0)),
                  pl.BlockSpec(fsr.shape, lambda o, j, g: (0, 0, 0))],
        out_specs=pl.BlockSpec((None, kg, n1, ct), lambda o, j, g: (o, g, 0, j)),
        out_shape=jax.ShapeDtypeStruct((no, n2, n1, 2 * nslab * LANE), jnp.int32),
        compiler_params=_cparams(("parallel", "parallel", "arbitrary")),
    )(cs, cs, cs, fs, fsr)


def _hy_stage_b_kernel(c_ref, k_ref, fb_ref, fbi_ref, o_ref, *, kg, n1):
    ns = 2 * c_ref.shape[1]
    for k in range(kg):
        z = jnp.dot(fb_ref[...], _complex_slab(c_ref, k), preferred_element_type=F32)
        zr, zi = z[:n1], z[n1:]
        kr, ki = (pltpu.unpack_elementwise(k_ref[k], index=i, packed_dtype=BF16, unpacked_dtype=F32)
                  for i in range(2))
        y = jnp.concatenate([zr * kr - zi * ki, zr * ki + zi * kr], axis=0).astype(BF16)
        d = jnp.dot(fbi_ref[...], y, preferred_element_type=F32)
        for p in range(2):
            for q, word in enumerate(_pack_halves(d[p * n1:(p + 1) * n1], ns)):
                o_ref[p, q, k] = word


def _hy_stage_b(cs, spec, order, fb, fbi, *, ct, kg):
    _, _, nslab, n2, n1, _ = cs.shape
    ns = ct // LANE // 2
    c = 2 * nslab * LANE
    return pl.pallas_call(
        functools.partial(_hy_stage_b_kernel, kg=kg, n1=n1),
        name="hy_stage_b",
        grid=(c // ct, n2 // kg),
        in_specs=[pl.BlockSpec((None, 2, ns, kg, n1, LANE), lambda j, g: (0, 0, j, g, 0, 0)),
                  pl.BlockSpec((None, kg, n1, ct), lambda j, g: (order, g, 0, j)),
                  pl.BlockSpec(fb.shape, lambda j, g: (0, 0)),
                  pl.BlockSpec(fbi.shape, lambda j, g: (0, 0))],
        out_specs=pl.BlockSpec((2, ns, kg, n1, LANE), lambda j, g: (0, j, g, 0, 0)),
        out_shape=jax.ShapeDtypeStruct((2, nslab, n2, n1, LANE), jnp.int32),
        compiler_params=_cparams(("parallel", "arbitrary")),
    )(cs, spec, fb, fbi)


def _hy_stage_c_kernel(*refs, ns, n2, n2h, slab_major_u):
    d2 = _rows2d(refs[0])
    t_ref = refs[1]
    if slab_major_u:
        u2 = _rows2d(refs[2])
        u_parts = [[(u2, (b * ns + s) * n2h) for s in range(ns)] for b in range(2)]
        refs = refs[3:]
    else:
        u_parts = [[(_rows2d(r), b * n2h) for r in refs[2:2 + ns]] for b in range(2)]
        refs = refs[2 + ns:]
    g_parts = [[(_rows2d(r), b * n2h) for r in refs[:ns]] for b in range(2)]
    skip_ref, o_ref = refs[ns:]
    o2 = _rows2d(o_ref)
    skip = skip_ref[...]

    def rows(parts, j):
        return jnp.concatenate([jnp.concatenate([_strided_rows(r2, grp, n2h, j) for r2, grp in part], axis=1)
                                for part in parts], axis=0)

    for j in range(8):
        nh = ns // 2
        d = jnp.concatenate([_unpack_halves([_strided_rows(d2, (p * nh + q) * n2, n2, j) for q in range(nh)])
                             for p in range(2)], axis=0).astype(BF16)
        y = jnp.dot(t_ref[j], d, preferred_element_type=F32)
        out = rows(g_parts, j) * (y + skip * rows(u_parts, j))
        for b in range(2):
            for s in range(ns):
                o2[pl.ds(((b * ns + s) * n2h) * 8 + j, n2h, stride=8), :] = (
                    out[b * n2h:(b + 1) * n2h, s * LANE:(s + 1) * LANE])


def _hy_stage_c(d, tai, u, ucol, z4, gcol, skip, *, ct, slab_major_u):
    _, npacked, n2, n1, _ = d.shape
    nslab = 2 * npacked
    n2h = n2 // 2
    ns = ct // LANE
    uc, gc = ucol // ct, gcol // ct
    if slab_major_u:
        u_specs = [pl.BlockSpec((2, ns, n2h, 8, LANE), lambda j, g: (0, j, 0, g, 0))]
        u_args = [u]
    else:
        u_specs = _lane_slabs((2, n2h, 8), lambda j, g: (0, 0, g, uc + j), ns)
        u_args = [u] * ns
    return pl.pallas_call(
        functools.partial(_hy_stage_c_kernel, ns=ns, n2=n2, n2h=n2h, slab_major_u=slab_major_u),
        name="hy_stage_c",
        grid=(nslab // ns, n1 // 8),
        in_specs=([pl.BlockSpec((2, ns // 2, n2, 8, LANE), lambda j, g: (0, j, 0, g, 0)),
                   pl.BlockSpec((8, 2 * n2h, 2 * n2), lambda j, g: (g, 0, 0))]
                  + u_specs + _lane_slabs((2, n2h, 8), lambda j, g: (0, 0, g, gc + j), ns)
                  + [pl.BlockSpec((1, ct), lambda j, g: (0, j))]),
        out_specs=pl.BlockSpec((2, ns, n2h, 8, LANE), lambda j, g: (0, j, 0, g, 0)),
        out_shape=jax.ShapeDtypeStruct((2, nslab, n2h, n1, LANE), F32),
        compiler_params=_cparams(("parallel", "arbitrary")),
    )(d, tai, *u_args, *([z4] * ns), skip)


def _hyena(h, filt_params, skip):
    bsz, l, _ = h.shape
    assert bsz == 2
    n1, n2 = _fft_split(l)
    n2h = n2 // 2
    ta, tai, fb, fbi, fs, fsr = (jnp.asarray(t).astype(BF16) for t in _fft_tables(n1, n2))
    ct = 512
    kg = 8

    cs_k = _hy_filter_stage_a(l, *filt_params, ta, ct=ct, n1=n1, n2=n2)
    spec = _hy_spectrum(cs_k, fs, fsr, ct=ct, kg=kg)

    z4 = h.reshape(bsz, n2h, n1, -1)
    u, ucol = z4, COL_HY
    for o in range(HY_ORDER):
        first = o == 0
        cs = _hy_stage_a(u, [((0, ucol // BRANCH_WIDTH), (1, ucol // BRANCH_WIDTH))], ta, ct=ct, n2=n2,
                         slab_major_in=not first)
        d = _hy_stage_b(cs, spec, o, fb, fbi, ct=ct, kg=kg)
        u = _hy_stage_c(d, tai, u, ucol, z4, COL_HY + (o + 1) * BRANCH_WIDTH, skip[o:o + 1], ct=ct,
                        slab_major_u=not first)
        ucol = 0
    return u.reshape(bsz, BRANCH_WIDTH // LANE, l, LANE)


def _conv_tables(hy_conv_w, hy_conv_b, dn_conv_q, dn_conv_k, dn_conv_v):
    depth = hy_conv_w.shape[0]
    ident = lambda n: jnp.broadcast_to(jnp.asarray([0.0, 1.0, 0.0], F32)[None, :, None], (depth, 3, n))
    taps = jnp.concatenate([ident(COL_HY), hy_conv_w, ident(COL_DK - COL_FF), dn_conv_k, dn_conv_v,
                            ident(COL_DQ - COL_AB), dn_conv_q, ident(COL_PAD - COL_DQ - dn_conv_q.shape[2])], axis=2)
    bias = jnp.concatenate([jnp.zeros((depth, 1, COL_HY), F32), hy_conv_b[:, None, :],
                            jnp.zeros((depth, 1, COL_PAD - COL_FF), F32)], axis=2)
    return taps, bias


def _token_mixers(u_src, sh, sc, w_t, conv_t, layer, lp, states, *, period, with_output, tm):
    l = lp["seq_len"]
    bsz = u_src.shape[0] * u_src.shape[1] // l
    tblk = min(l, 2048)
    taps, bias = (a[layer:layer + 1] for a in conv_t)
    if with_output:
        h = _in_proj(u_src, sh, sc, w_t, layer, range(COL_PAD // IN_TN), taps, bias, tm=tm, period=period)
        base = 0
    else:
        lo, hi = STATE_BLOCKS[0] * IN_TN, (STATE_BLOCKS[-1] + 1) * IN_TN
        h = _in_proj(u_src, sh, sc, w_t, layer, STATE_BLOCKS, taps[:, :, lo:hi], bias[:, :, lo:hi],
                     tm=tm, period=period)
        base = COL_FF
    h = h.reshape(bsz, l, h.shape[-1])
    col = lambda c: c - base
    ab_t = jnp.swapaxes(h[:, :, col(COL_AB):col(COL_AB) + 4 * HEADS], 1, 2)
    hg_f, hg_b, hg_states = _hgrn2_scan(h, (col(COL_FF), col(COL_FB), col(COL_HI), col(COL_HQ)),
                                        lp["lb_f"], lp["lb_b"], states[0], tblk=tblk, with_output=with_output)
    dn_f, dn_b, dn_states = _deltanet_scan(h, ab_t, (col(COL_DK), col(COL_DV), col(COL_DQ)),
                                           lp["dn_a_log"], lp["dn_dt_bias"], states[1],
                                           tblk=min(l, 2048), with_output=with_output)
    new_states = (hg_states, dn_states)
    if not with_output:
        return None, new_states
    tl = min(l, 2048)
    hg_out = _combine(hg_f, hg_b, h, COL_HGATE, lp["hg_norm_w"], tl=tl, use_silu=False)
    dn_out = _combine(dn_f, dn_b, h, COL_DZ, lp["dn_norm_w"], tl=tl, use_silu=True)
    hy_out = _hyena(h, lp["hy_filt"], lp["hy_skip"])
    ysum = _branch_merge([hy_out, hg_out, dn_out], lp["w_branch"], layer, h, tm=min(l, 1024), tn=512)
    return ysum, new_states


def kernel(x, c, ctx, c_ctx, w_ada, b_ada, w_in, hy_conv_w, hy_conv_b, hy_filt_w1, hy_filt_b1, hy_filt_w2, hy_filt_b2, hy_filt_w3, hy_skip, hg_lb_logits, hg_norm_w, dn_conv_q, dn_conv_k, dn_conv_v, dn_a_log, dn_dt_bias, dn_norm_w, w_branch, w_out, ln1_g, ln1_b, w_ff1, w_ff2, ln2_g, ln2_b):
    depth = w_in.shape[0]
    bsz, _, d = x.shape
    alpha = (2 * depth) ** 0.25
    p = jax.nn.softmax(hg_lb_logits.astype(F32), axis=1)
    lower = jnp.cumsum(p, axis=1) - p[:, :1]
    cs = jnp.concatenate([c, c_ctx[None], jnp.zeros((8 - bsz - 1, d), F32)], axis=0)
    w_r = jnp.swapaxes(w_in, 1, 2)
    w_b, w_o, w_1, w_2 = w_branch.astype(BF16), w_out.astype(BF16), w_ff1, w_ff2.astype(BF16)
    b_ada3 = b_ada[:, None, :]
    conv_t = _conv_tables(hy_conv_w, hy_conv_b, dn_conv_q, dn_conv_k, dn_conv_v)
    lc = ctx.shape[1]
    h_ctx = ctx.reshape(1, bsz * lc, d)
    for l in range(depth):
        last = l == depth - 1
        row = lambda v: v.reshape(1, -1)
        lp = {
            "lb_f": row(lower[0, l]), "lb_b": row(lower[1, l]),
            "hg_norm_w": row(hg_norm_w[l]), "dn_norm_w": row(dn_norm_w[l]),
            "dn_a_log": dn_a_log[l], "dn_dt_bias": dn_dt_bias[l], "hy_skip": hy_skip[l],
            "hy_filt": (jnp.pad(hy_filt_w1[l], ((0, LANE - HY_EMB), (0, 0))), row(hy_filt_b1[l]),
                        hy_filt_w2[l], row(hy_filt_b2[l]), hy_filt_w3[l]),
            "w_branch": w_b,
        }
        g1n, b1n, g2n, b2n = row(ln1_g[l]), row(ln1_b[l]), row(ln2_g[l]), row(ln2_b[l])
        mod = _ada_mod(cs, w_ada, b_ada3, l)
        m_lat = [mod[:bsz, i * d:(i + 1) * d][:, None, :] for i in range(6)]
        m_ctx = [mod[bsz, i * d:(i + 1) * d][None, None, :] for i in range(6)]
        zero = jnp.zeros((bsz, HEADS, HEAD_DIM, HEAD_DIM), F32)
        init = ((zero, zero), (zero, zero))

        rows_c = bsz * lc
        y_ctx, ctx_states = _token_mixers(h_ctx, m_ctx[0], m_ctx[1], w_r, conv_t, l, dict(lp, seq_len=lc), init,
                                          period=lc, with_output=not last, tm=rows_c)
        if not last:
            y_ctx = y_ctx.reshape(1, rows_c, d)
            h_ctx = _mm_ln(y_ctx, w_o, l, h_ctx, m_ctx[2], g1n, b1n, tm=rows_c, tk=d, alpha=alpha)
            mid = _mlp_up(h_ctx, m_ctx[3], m_ctx[4], w_1, l, tm=rows_c, tn=1024)
            h_ctx = _mm_ln(mid, w_2, l, h_ctx, m_ctx[5], g2n, b2n, tm=rows_c, tk=1024, alpha=alpha)

        lx = x.shape[1]
        y, _ = _token_mixers(x, m_lat[0], m_lat[1], w_r, conv_t, l, dict(lp, seq_len=lx), ctx_states,
                             period=GRID_W, with_output=True, tm=min(lx, 1024))
        x = _mm_ln(y, w_o, l, x, m_lat[2], g1n, b1n, tm=min(lx, 512), tk=d, alpha=alpha)
        mid = _mlp_up(x, m_lat[3], m_lat[4], w_1, l, tm=min(lx, 1024), tn=1024)
        x = _mm_ln(mid, w_2, l, x, m_lat[5], g2n, b2n, tm=min(lx, 512), tk=2048, alpha=alpha)
    return x
```

```python
import functools
import math

import numpy as np
import jax
import jax.numpy as jnp
from jax import lax
from jax.experimental import pallas as pl
from jax.experimental.pallas import tpu as pltpu

F32 = jnp.float32
BF16 = jnp.bfloat16

D_MODEL = 2048
GRID_W = 64
BRANCH_WIDTH = 1024
N_BRANCH = 3
HY_ORDER = 2
HY_EMB = 33
HY_BANDS = (HY_EMB - 1) // 2
HY_HIDDEN = 64
HY_DECAY_TARGET = 1e-2
HY_FAST_DECAY = 0.3
HY_SLOW_DECAY = 1.5
HEADS = 8
HEAD_DIM = 128
DN_QK_HEADS = 4
HG_F_MIN = 1e-30
CHUNK = 64
LN_EPS = 1e-5
RMS_EPS = 1e-6
LANE = 128

IN_TN = 1024
COL_GATE = 0
COL_HY = 6144
COL_FF = 9216
COL_FB = 10240
COL_HI = 11264
COL_DK = 12288
COL_DV = 12800
COL_AB = 13824
COL_HQ = 14336
COL_HGATE = 15360
COL_DQ = 16384
COL_DZ = 17408
COL_PAD = 18432
SOURCE_ROWS = np.asarray([11296 + IN_TN * k for k in range(6)] + [8224 + IN_TN * k for k in range(3)]
                         + [0, 1024, 2048, 3072, 4096, 4640, 5664, 6688, 7200], np.int32)
CONV_WINDOWS = np.asarray([1 if COL_HY <= IN_TN * j < COL_FF or IN_TN * j == COL_DK
                           else 2 if IN_TN * j in (COL_DK + IN_TN, COL_DQ) else 0
                           for j in range(COL_PAD // IN_TN)], np.int32)
STATE_BLOCKS = tuple(range(COL_FF // IN_TN, COL_HQ // IN_TN))

VMEM_LIMIT = 56 * 1024 * 1024


def _cparams(sem):
    return pltpu.CompilerParams(dimension_semantics=sem, vmem_limit_bytes=VMEM_LIMIT)


def _dot(a, b):
    return jnp.dot(a.astype(BF16), b.astype(BF16), preferred_element_type=F32)


def _dot_nt(a, b):
    return lax.dot_general(a.astype(BF16), b.astype(BF16), (((1,), (1,)), ((), ())),
                           preferred_element_type=F32)


def _dot_tn(a, b):
    return lax.dot_general(a.astype(BF16), b.astype(BF16), (((0,), (0,)), ((), ())),
                           preferred_element_type=F32)


def _sigmoid(x):
    return 1.0 / (1.0 + jnp.exp(-x))


def _silu(x):
    return x * _sigmoid(x)


def _ada_kernel(c_ref, w_ref, b_ref, o_ref):
    a = _silu(c_ref[...])
    o_ref[...] = _dot(a, w_ref[...]) + b_ref[...]


def _ada_mod(cs, w, b, layer):
    _, d, n = w.shape
    tn = 1024
    return pl.pallas_call(
        _ada_kernel,
        grid=(n // tn,),
        in_specs=[pl.BlockSpec((8, d), lambda j: (0, 0)),
                  pl.BlockSpec((None, d, tn), lambda j: (layer, 0, j)),
                  pl.BlockSpec((None, 1, tn), lambda j: (layer, 0, j))],
        out_specs=pl.BlockSpec((8, tn), lambda j: (0, j)),
        out_shape=jax.ShapeDtypeStruct((8, n), F32),
        compiler_params=_cparams(("arbitrary",)),
        name="ada_mod",
    )(cs, w, b)


def _conv3(x, w, period):
    t = x.shape[0]
    pos = jnp.bitwise_and(_iota2((t, 1), 0), period - 1)
    has_l, has_r = pos != 0, pos != period - 1
    xl = jnp.where(has_l, pltpu.roll(x, 1, 0), 0.0)
    xr = jnp.where(has_r, pltpu.roll(x, t - 1, 0), 0.0)
    return xl * w[0:1] + x * w[1:2] + xr * w[2:3]


def _mlp_up_kernel(x_ref, sh_ref, sc_ref, w_ref, o_ref, a_scr):
    @pl.when(pl.program_id(2) == 0)
    def _():
        a_scr[...] = (x_ref[...] * (1.0 + sc_ref[...]) + sh_ref[...]).astype(BF16)

    acc = jnp.dot(a_scr[...], w_ref[...].astype(BF16), preferred_element_type=F32)
    o_ref[...] = jnp.square(jnp.maximum(acc, 0.0)).astype(o_ref.dtype)


def _mlp_up(x, sh, sc, w, layer, *, tm, tn):
    bsz, l, d = x.shape
    n = w.shape[2]
    return pl.pallas_call(
        _mlp_up_kernel,
        grid=(bsz, l // tm, n // tn),
        in_specs=[pl.BlockSpec((None, tm, d), lambda b, i, j: (b, i, 0)),
                  pl.BlockSpec((None, 1, d), lambda b, i, j: (b, 0, 0)),
                  pl.BlockSpec((None, 1, d), lambda b, i, j: (b, 0, 0)),
                  pl.BlockSpec((None, d, tn), lambda b, i, j: (layer, 0, j))],
        out_specs=pl.BlockSpec((None, tm, tn), lambda b, i, j: (b, i, j)),
        out_shape=jax.ShapeDtypeStruct((bsz, l, n), BF16),
        scratch_shapes=[pltpu.VMEM((tm, d), BF16)],
        compiler_params=_cparams(("parallel", "parallel", "arbitrary")),
        name="mm_relu2",
    )(x, sh, sc, w)


def _in_proj_kernel(off_ref, conv_ref, x_ref, sh_ref, sc_ref, w_ref, cw_ref, cb_ref, o_ref, a_scr, *, period):
    del off_ref
    conv_mode = conv_ref[pl.program_id(2)]
    half = IN_TN // 2

    @pl.when(pl.program_id(2) == 0)
    def _():
        a_scr[...] = (x_ref[...] * (1.0 + sc_ref[...]) + sh_ref[...]).astype(BF16)

    o_ref[...] = lax.dot_general(a_scr[...], w_ref[0].astype(BF16), (((1,), (1,)), ((), ())),
                                 preferred_element_type=F32)

    @pl.when(conv_mode == 1)
    def _():
        o_ref[...] = _conv3(o_ref[...], cw_ref[...], period) + cb_ref[...]

    @pl.when(conv_mode == 2)
    def _():
        o_ref[:, :half] = _conv3(o_ref[:, :half], cw_ref[:, :half], period) + cb_ref[:, :half]


def _in_proj(x, sh, sc, w_t, layer, windows, taps, bias, *, tm, period):
    bsz, l, d = x.shape
    nblk = len(windows)
    assert tm % period == 0 and period & (period - 1) == 0
    row_off = jnp.asarray(SOURCE_ROWS[list(windows)])
    has_conv = jnp.asarray(CONV_WINDOWS[list(windows)])
    grid_spec = pltpu.PrefetchScalarGridSpec(
        num_scalar_prefetch=2,
        grid=(bsz, l // tm, nblk),
        in_specs=[pl.BlockSpec((None, tm, d), lambda b, i, j, off, cv: (b, i, 0)),
                  pl.BlockSpec((None, 1, d), lambda b, i, j, off, cv: (b, 0, 0)),
                  pl.BlockSpec((None, 1, d), lambda b, i, j, off, cv: (b, 0, 0)),
                  pl.BlockSpec((pl.Element(1), pl.Element(IN_TN), pl.Element(d)),
                               lambda b, i, j, off, cv: (layer, pl.multiple_of(off[j], 4 * HEADS), 0)),
                  pl.BlockSpec((None, 3, IN_TN), lambda b, i, j, off, cv: (0, 0, j)),
                  pl.BlockSpec((None, 1, IN_TN), lambda b, i, j, off, cv: (0, 0, j))],
        out_specs=pl.BlockSpec((None, tm, IN_TN), lambda b, i, j, off, cv: (b, i, j)),
        scratch_shapes=[pltpu.VMEM((tm, d), BF16)],
    )
    return pl.pallas_call(
        functools.partial(_in_proj_kernel, period=period),
        name="mm_in_proj",
        grid_spec=grid_spec,
        out_shape=jax.ShapeDtypeStruct((bsz, l, nblk * IN_TN), F32),
        compiler_params=_cparams(("parallel", "parallel", "arbitrary")),
    )(row_off, has_conv, x, sh, sc, w_t, taps, bias)


def _mm_ln_kernel(a_ref, w_ref, x_ref, gate_ref, g_ref, b_ref, o_ref, acc_ref, *, nk, alpha):
    k = pl.program_id(2)

    @pl.when(k == 0)
    def _():
        acc_ref[...] = jnp.dot(a_ref[...], w_ref[...].astype(BF16), preferred_element_type=F32)

    @pl.when(k > 0)
    def _():
        acc_ref[...] += jnp.dot(a_ref[...], w_ref[...].astype(BF16), preferred_element_type=F32)

    @pl.when(k == nk - 1)
    def _():
        y = alpha * x_ref[...] + gate_ref[...] * acc_ref[...]
        mu = jnp.mean(y, axis=-1, keepdims=True)
        yc = y - mu
        var = jnp.mean(yc * yc, axis=-1, keepdims=True)
        o_ref[...] = yc * lax.rsqrt(var + LN_EPS) * g_ref[...] + b_ref[...]


def _mm_ln(a, w, layer, x, gate, ln_g, ln_b, *, tm, tk, alpha):
    bsz, l, kdim = a.shape
    d = w.shape[2]
    nk = kdim // tk
    return pl.pallas_call(
        functools.partial(_mm_ln_kernel, nk=nk, alpha=alpha),
        grid=(bsz, l // tm, nk),
        in_specs=[pl.BlockSpec((None, tm, tk), lambda b, i, k: (b, i, k)),
                  pl.BlockSpec((None, tk, d), lambda b, i, k: (layer, k, 0)),
                  pl.BlockSpec((None, tm, d), lambda b, i, k: (b, i, 0)),
                  pl.BlockSpec((None, 1, d), lambda b, i, k: (b, 0, 0)),
                  pl.BlockSpec((1, d), lambda b, i, k: (0, 0)),
                  pl.BlockSpec((1, d), lambda b, i, k: (0, 0))],
        out_specs=pl.BlockSpec((None, tm, d), lambda b, i, k: (b, i, 0)),
        out_shape=jax.ShapeDtypeStruct((bsz, l, d), F32),
        scratch_shapes=[pltpu.VMEM((tm, d), F32)],
        compiler_params=_cparams(("parallel", "parallel", "arbitrary")),
        name="mm_resid_ln",
    )(a, w, x, gate, ln_g, ln_b)


def _branch_kernel(b0_ref, b1_ref, b2_ref, w_ref, g0_ref, g1_ref, g2_ref, o_ref):
    b0 = jnp.concatenate([b0_ref[s] for s in range(b0_ref.shape[0])], axis=1).astype(BF16)
    acc = _sigmoid(g0_ref[...]) * jnp.dot(b0, w_ref[0].astype(BF16), preferred_element_type=F32)
    acc += _sigmoid(g1_ref[...]) * jnp.dot(b1_ref[...], w_ref[1].astype(BF16), preferred_element_type=F32)
    acc += _sigmoid(g2_ref[...]) * jnp.dot(b2_ref[...], w_ref[2].astype(BF16), preferred_element_type=F32)
    o_ref[...] = acc.astype(o_ref.dtype)


def _branch_merge(branches, w, layer, h, *, tm, tn):
    bsz, l, c = branches[1].shape
    d = w.shape[3]
    nj = d // tn
    br_spec = pl.BlockSpec((None, tm, c), lambda b, i, j: (b, i, 0))
    hy_spec = pl.BlockSpec((None, c // LANE, tm, LANE), lambda b, i, j: (b, 0, i, 0))

    def gate_spec(g):
        return pl.BlockSpec((None, tm, tn), lambda b, i, j: (b, i, COL_GATE // tn + g * nj + j))

    return pl.pallas_call(
        _branch_kernel,
        grid=(bsz, l // tm, nj),
        in_specs=[hy_spec, br_spec, br_spec,
                  pl.BlockSpec((None, N_BRANCH, c, tn), lambda b, i, j: (layer, 0, 0, j)),
                  gate_spec(0), gate_spec(1), gate_spec(2)],
        out_specs=pl.BlockSpec((None, tm, tn), lambda b, i, j: (b, i, j)),
        out_shape=jax.ShapeDtypeStruct((bsz, l, d), BF16),
        compiler_params=_cparams(("parallel", "parallel", "arbitrary")),
        name="branch_merge",
    )(*branches, w, h, h, h)


def _iota2(shape, axis):
    return lax.broadcasted_iota(jnp.int32, shape, axis)


def _split3(g):
    g1 = g.astype(BF16)
    r = g - g1.astype(F32)
    g2 = r.astype(BF16)
    g3 = (r - g2.astype(F32)).astype(BF16)
    return g1, g2, g3


def _waves(items, nwaves):
    n = min(nwaves, max(1, len(items) // 8))
    per = len(items) // n
    return [items[w * per:(w + 1) * per] for w in range(n)]


def _cumsum_rows(g, rev):
    row = _iota2((CHUNK, 3 * CHUNK), 0)
    col = jnp.bitwise_and(_iota2((CHUNK, 3 * CHUNK), 1), CHUNK - 1)
    tri = jnp.where((col >= row) if rev else (col <= row), 1.0, 0.0).astype(BF16)
    g3 = jnp.concatenate(_split3(g), axis=0)
    return jnp.dot(tri, g3, preferred_element_type=F32)


def _hg_prepare(items):
    for it in items:
        lb = it["lb"]
        sg = _sigmoid(it["fz"])
        it["g"] = jnp.log(jnp.maximum(lb + (1.0 - lb) * sg, HG_F_MIN))
        it["kk"] = (1.0 - lb) * (1.0 - sg)
    for it in items:
        it["bc"] = _cumsum_rows(it["g"], it["rev"])
    half = CHUNK // 2
    for it in items:
        bc = it["bc"]
        if it["rev"]:
            mid, last = bc[half:half + 1], bc[0:1]
        else:
            mid, last = bc[half - 1:half], bc[CHUNK - 1:CHUNK]
        it["e_last"] = jnp.exp(last)
        kd = it["kk"] * jnp.exp(mid - bc)
        it["ks"] = kd * jnp.exp(last - mid)
        if it["q"] is not None:
            qd = _silu(it["q"]) * jnp.exp(bc - mid)
            it["qd"], it["kd"] = qd, kd
            it["qs"] = qd * jnp.exp(mid)
    r = _iota2((CHUNK, CHUNK), 0)
    c = _iota2((CHUNK, CHUNK), 1)
    for it in items:
        it["kv"] = _dot_tn(it["v"], it["ks"])
        if it["q"] is not None:
            att = _dot_nt(it["qd"], it["kd"])
            it["att"] = jnp.where((c >= r) if it["rev"] else (c <= r), att, 0.0)
    for it in items:
        if it["q"] is not None:
            it["o_c"] = _dot(it["att"], it["v"])


def _hg_kernel(*refs, nchunk, with_output):
    if with_output:
        (ff_ref, vf_ref, qf_ref, fb_ref, vb_ref, qb_ref, lbf_ref, lbb_ref, s0f_ref, s0b_ref,
         of_ref, ob_ref, sf_ref, sb_ref, stf, stb) = refs
    else:
        (ff_ref, vf_ref, fb_ref, vb_ref, lbf_ref, lbb_ref, s0f_ref, s0b_ref,
         sf_ref, sb_ref, stf, stb) = refs
        qf_ref = qb_ref = of_ref = ob_ref = None

    @pl.when(pl.program_id(2) == 0)
    def _():
        stf[...] = s0f_ref[...]
        stb[...] = s0b_ref[...]

    def chunk_item(f_ref, v_ref, q_ref, lb_ref, ci, rev):
        sl = pl.ds(ci * CHUNK, CHUNK)
        return {"fz": f_ref[sl, :], "v": v_ref[sl, :], "q": q_ref[sl, :] if with_output else None,
                "lb": lb_ref[...], "rev": rev, "sl": sl}

    items = []
    for ci in range(nchunk):
        items.append(chunk_item(ff_ref, vf_ref, qf_ref, lbf_ref, ci, False))
        items.append(chunk_item(fb_ref, vb_ref, qb_ref, lbb_ref, nchunk - 1 - ci, True))
    state = {False: stf[...], True: stb[...]}
    for wave in _waves(items, 8):
        _hg_prepare(wave)
        for it in wave:
            st = state[it["rev"]]
            if with_output:
                o_ref = ob_ref if it["rev"] else of_ref
                o_ref[it["sl"], :] = it["o_c"] + _dot_nt(it["qs"], st)
            state[it["rev"]] = st * it["e_last"] + it["kv"]
    stf[...] = state[False]
    stb[...] = state[True]
    sf_ref[...] = state[False]
    sb_ref[...] = state[True]


def _hgrn2_scan(h, cols, lb_f, lb_b, s0, *, tblk, with_output):
    bsz, l, _ = h.shape
    nt = l // tblk
    c_ff, c_fb, c_i, c_q = (c // LANE for c in cols)

    def fwd(cb):
        return pl.BlockSpec((None, tblk, LANE), lambda b, hh, t: (b, t, cb + hh))

    def bwd(cb):
        return pl.BlockSpec((None, tblk, LANE), lambda b, hh, t: (b, nt - 1 - t, cb + hh))

    lb_spec = pl.BlockSpec((1, LANE), lambda b, hh, t: (0, hh))
    st_spec = pl.BlockSpec((None, None, HEAD_DIM, HEAD_DIM), lambda b, hh, t: (b, hh, 0, 0))
    if with_output:
        in_specs = [fwd(c_ff), fwd(c_i), fwd(c_q), bwd(c_fb), bwd(c_i), bwd(c_q)]
        args = [h] * 6
    else:
        in_specs = [fwd(c_ff), fwd(c_i), bwd(c_fb), bwd(c_i)]
        args = [h] * 4
    in_specs += [lb_spec, lb_spec, st_spec, st_spec]
    args += [lb_f, lb_b, s0[0], s0[1]]
    st_shape = jax.ShapeDtypeStruct((bsz, HEADS, HEAD_DIM, HEAD_DIM), F32)
    out_specs = [st_spec, st_spec]
    out_shape = [st_shape, st_shape]
    if with_output:
        o_shape = jax.ShapeDtypeStruct((bsz, l, HEADS * HEAD_DIM), F32)
        out_specs = [pl.BlockSpec((None, tblk, LANE), lambda b, hh, t: (b, t, hh)),
                     pl.BlockSpec((None, tblk, LANE), lambda b, hh, t: (b, nt - 1 - t, hh))] + out_specs
        out_shape = [o_shape, o_shape] + out_shape
    outs = pl.pallas_call(
        functools.partial(_hg_kernel, nchunk=tblk // CHUNK, with_output=with_output),
        name="hgrn2_scan",
        grid=(bsz, HEADS, nt),
        in_specs=in_specs,
        out_specs=out_specs,
        out_shape=out_shape,
        scratch_shapes=[pltpu.VMEM((HEAD_DIM, HEAD_DIM), F32), pltpu.VMEM((HEAD_DIM, HEAD_DIM), F32)],
        compiler_params=_cparams(("parallel", "parallel", "arbitrary")),
    )(*args)
    if with_output:
        return outs[0], outs[1], (outs[2], outs[3])
    return None, None, (outs[0], outs[1])


def _l2norm(t):
    return t * lax.rsqrt(jnp.sum(t * t, axis=-1, keepdims=True) + 1e-6)


def _softplus(x):
    return jnp.maximum(x, 0.0) + jnp.log(1.0 + jnp.exp(-jnp.abs(x)))


def _dn_prepare(items):
    r = _iota2((CHUNK, CHUNK), 0)
    c = _iota2((CHUNK, CHUNK), 1)
    eye = r == c
    for it in items:
        rev = it["rev"]
        incl = (c >= r) if rev else (c <= r)
        incl_t = (r >= c) if rev else (r <= c)
        g_b = jnp.broadcast_to(it["g"], (CHUNK, CHUNK))
        gc_col = jnp.sum(jnp.where(incl, g_b, 0.0), axis=1, keepdims=True)
        g_col = jnp.sum(jnp.where(eye, g_b, 0.0), axis=1, keepdims=True)
        gc_row = jnp.sum(jnp.where(incl_t, jnp.broadcast_to(g_col, (CHUNK, CHUNK)), 0.0), axis=0, keepdims=True)
        beta_col = jnp.sum(jnp.where(eye, jnp.broadcast_to(it["beta"], (CHUNK, CHUNK)), 0.0), axis=1,
                           keepdims=True)
        it["decay"] = jnp.where(incl, jnp.exp(jnp.where(incl, gc_col - gc_row, 0.0)), 0.0)
        g_last = gc_col[0:1] if rev else gc_col[CHUNK - 1:CHUNK]
        it["e_last"] = jnp.exp(g_last)
        egc = jnp.exp(gc_col)
        kb = it["k"] * beta_col
        it["kb"] = kb
        it["rhs"] = jnp.concatenate([it["v"] * beta_col, kb * egc], axis=1)
        it["kd"] = it["k"] * jnp.exp(g_last - gc_col)
        if it["q"] is not None:
            it["qe"] = it["q"] * egc
    for it in items:
        if it["q"] is not None:
            both = _dot_nt(jnp.concatenate([it["kb"], it["q"]], axis=0), it["k"])
            it["kk"], it["qk"] = both[:CHUNK], both[CHUNK:] * it["decay"]
        else:
            it["kk"] = _dot_nt(it["kb"], it["k"])
    for it in items:
        strict = (c > r) if it["rev"] else (c < r)
        n = jnp.where(strict, it["kk"] * it["decay"], 0.0)
        it["n"] = n
        it["p"] = jnp.where(eye, 1.0, 0.0) - n
    for it in items:
        it["m"] = _dot(it["n"], it["n"])
    for i in range(5):
        for it in items:
            if i < 4:
                pm = _dot(jnp.concatenate([it["p"], it["m"]], axis=0), it["m"])
                it["p"] = it["p"] + pm[:CHUNK]
                it["m"] = pm[CHUNK:]
            else:
                it["p"] = it["p"] + _dot(it["p"], it["m"])
    for it in items:
        it["uw"] = _dot(it["p"], it["rhs"])
    for it in items:
        bm = _dot_tn(it["kd"], it["uw"])
        it["b_c"], it["m_c"] = bm[:, :HEAD_DIM], bm[:, HEAD_DIM:]
        if it["q"] is not None:
            ow = _dot(it["qk"], it["uw"])
            it["o_c"], it["q_eff"] = ow[:, :HEAD_DIM], it["qe"] - ow[:, HEAD_DIM:]


def _dn_kernel(*refs, nchunk, with_output):
    if with_output:
        (alog_ref, dtb_ref, kf_ref, vf_ref, qf_ref, abf_ref, kb_ref, vb_ref, qb_ref, abb_ref,
         s0f_ref, s0b_ref, of_ref, ob_ref, sf_ref, sb_ref, stf, stb) = refs
    else:
        (alog_ref, dtb_ref, kf_ref, vf_ref, abf_ref, kb_ref, vb_ref, abb_ref,
         s0f_ref, s0b_ref, sf_ref, sb_ref, stf, stb) = refs
        qf_ref = qb_ref = of_ref = ob_ref = None
    hh = pl.program_id(1)

    @pl.when(pl.program_id(2) == 0)
    def _():
        stf[...] = s0f_ref[...]
        stb[...] = s0b_ref[...]

    def prep(k_ref, v_ref, q_ref, ab_ref, d):
        k = _l2norm(_silu(k_ref[...]))
        v = _silu(v_ref[...])
        q = None
        if with_output:
            q = _l2norm(_silu(q_ref[...])) * (HEAD_DIM ** -0.5)
        a = ab_ref[pl.ds(d * HEADS + hh, 1), :]
        bt = ab_ref[pl.ds(2 * HEADS + d * HEADS + hh, 1), :]
        g = -jnp.exp(alog_ref[d, hh]) * _softplus(a + dtb_ref[d, hh])
        return k, v, q, g, _sigmoid(bt)

    kf, vf, qf, gf, bf = prep(kf_ref, vf_ref, qf_ref, abf_ref, 0)
    kb, vb, qb, gb, bb = prep(kb_ref, vb_ref, qb_ref, abb_ref, 1)
    def chunk_item(k, v, q, g, beta, ci, rev):
        lo = ci * CHUNK
        return {"k": k[lo:lo + CHUNK], "v": v[lo:lo + CHUNK], "q": q[lo:lo + CHUNK] if with_output else None,
                "g": g[:, lo:lo + CHUNK], "beta": beta[:, lo:lo + CHUNK], "rev": rev, "lo": lo}

    items = []
    for ci in range(nchunk):
        items.append(chunk_item(kf, vf, qf, gf, bf, ci, False))
        items.append(chunk_item(kb, vb, qb, gb, bb, nchunk - 1 - ci, True))
    state = {False: stf[...], True: stb[...]}
    for wave in _waves(items, 4):
        _dn_prepare(wave)
        for it in wave:
            s = state[it["rev"]]
            if with_output:
                o_ref = ob_ref if it["rev"] else of_ref
                o_ref[pl.ds(it["lo"], CHUNK), :] = it["o_c"] + _dot(it["q_eff"], s)
            state[it["rev"]] = s * it["e_last"] + it["b_c"] - _dot(it["m_c"], s)
    stf[...] = state[False]
    stb[...] = state[True]
    sf_ref[...] = state[False]
    sb_ref[...] = state[True]


def _deltanet_scan(h, ab_t, cols, a_log, dt_bias, s0, *, tblk, with_output):
    bsz, l, _ = h.shape
    nt = l // tblk
    c_k, c_v, c_q = (c // LANE for c in cols)
    rep = HEADS // DN_QK_HEADS

    def tok(cb, shared, rev):
        def imap(b, hh, t):
            return (b, (nt - 1 - t) if rev else t, cb + (hh // rep if shared else hh))
        return pl.BlockSpec((None, tblk, LANE), imap)

    def ab_spec(rev):
        return pl.BlockSpec((None, 4 * HEADS, tblk), lambda b, hh, t: (b, 0, (nt - 1 - t) if rev else t))

    smem = pl.BlockSpec(memory_space=pltpu.SMEM)
    st_spec = pl.BlockSpec((None, None, HEAD_DIM, HEAD_DIM), lambda b, hh, t: (b, hh, 0, 0))
    if with_output:
        in_specs = [smem, smem, tok(c_k, True, False), tok(c_v, False, False), tok(c_q, True, False), ab_spec(False),
                    tok(c_k, True, True), tok(c_v, False, True), tok(c_q, True, True), ab_spec(True),
                    st_spec, st_spec]
        args = [a_log, dt_bias, h, h, h, ab_t, h, h, h, ab_t, s0[0], s0[1]]
    else:
        in_specs = [smem, smem, tok(c_k, True, False), tok(c_v, False, False), ab_spec(False),
                    tok(c_k, True, True), tok(c_v, False, True), ab_spec(True),
                    st_spec, st_spec]
        args = [a_log, dt_bias, h, h, ab_t, h, h, ab_t, s0[0], s0[1]]
    st_shape = jax.ShapeDtypeStruct((bsz, HEADS, HEAD_DIM, HEAD_DIM), F32)
    out_specs = [st_spec, st_spec]
    out_shape = [st_shape, st_shape]
    if with_output:
        o_shape = jax.ShapeDtypeStruct((bsz, l, HEADS * HEAD_DIM), F32)
        out_specs = [pl.BlockSpec((None, tblk, LANE), lambda b, hh, t: (b, t, hh)),
                     pl.BlockSpec((None, tblk, LANE), lambda b, hh, t: (b, nt - 1 - t, hh))] + out_specs
        out_shape = [o_shape, o_shape] + out_shape
    outs = pl.pallas_call(
        functools.partial(_dn_kernel, nchunk=tblk // CHUNK, with_output=with_output),
        name="deltanet_scan",
        grid=(bsz, HEADS, nt),
        in_specs=in_specs,
        out_specs=out_specs,
        out_shape=out_shape,
        scratch_shapes=[pltpu.VMEM((HEAD_DIM, HEAD_DIM), F32), pltpu.VMEM((HEAD_DIM, HEAD_DIM), F32)],
        compiler_params=_cparams(("parallel", "parallel", "arbitrary")),
    )(*args)
    if with_output:
        return outs[0], outs[1], (outs[2], outs[3])
    return None, None, (outs[0], outs[1])


def _combine_kernel(of_ref, ob_ref, z_ref, w_ref, o_ref, *, use_silu):
    o = of_ref[...] + ob_ref[...]
    o = o * lax.rsqrt(jnp.mean(o * o, axis=-1, keepdims=True) + RMS_EPS) * w_ref[...]
    z = z_ref[...]
    o_ref[...] = (o * (_silu(z) if use_silu else _sigmoid(z))).astype(o_ref.dtype)


def _combine(o_f, o_b, h, col_z, norm_w, *, tl, use_silu):
    bsz, l, c = o_f.shape
    cz = col_z // LANE
    spec = pl.BlockSpec((None, tl, LANE), lambda b, i, hh: (b, i, hh))
    return pl.pallas_call(
        functools.partial(_combine_kernel, use_silu=use_silu),
        name="mixer_norm_gate",
        grid=(bsz, l // tl, c // LANE),
        in_specs=[spec, spec,
                  pl.BlockSpec((None, tl, LANE), lambda b, i, hh: (b, i, cz + hh)),
                  pl.BlockSpec((1, LANE), lambda b, i, hh: (0, 0))],
        out_specs=spec,
        out_shape=jax.ShapeDtypeStruct((bsz, l, c), BF16),
        compiler_params=_cparams(("parallel", "parallel", "arbitrary")),
    )(o_f, o_b, h, norm_w)


def _dot_f32(a, b):
    return jnp.dot(a, b, precision=lax.Precision.HIGHEST, preferred_element_type=F32)


def _pack_halves(y, ns):
    nh = ns // 2
    return [pltpu.pack_elementwise([y[:, q * LANE:(q + 1) * LANE], y[:, (q + nh) * LANE:(q + nh + 1) * LANE]],
                                   packed_dtype=BF16) for q in range(nh)]


def _unpack_halves(ws):
    part = lambda i: [pltpu.unpack_elementwise(w, index=i, packed_dtype=BF16, unpacked_dtype=F32) for w in ws]
    return jnp.concatenate(part(0) + part(1), axis=1)


def _dot_3pass(a, b):
    a1 = a.astype(BF16)
    a2 = (a - a1.astype(F32)).astype(BF16)
    b1 = b.astype(BF16)
    b2 = (b - b1.astype(F32)).astype(BF16)
    return jnp.dot(jnp.concatenate([a1, a1, a2], axis=1), jnp.concatenate([b1, b2, b1], axis=0),
                   preferred_element_type=F32)


def _hy_filter_stage_a_kernel(w1_ref, b1_ref, w2_ref, b2_ref, w3_ref, t_ref, o_ref, *, l_total, n1, n2, ct):
    n2h = n2 // 2
    rows = 8 * n2h
    ns = ct // LANE
    g = pl.program_id(0)

    def lag(shape):
        r = _iota2(shape, 0)
        return 8 * g + jnp.right_shift(r, n2h.bit_length() - 1) + n1 * jnp.bitwise_and(r, n2h - 1)

    pos = lag((rows, LANE)).astype(F32)
    lane = _iota2((rows, LANE), 1)
    bidx = jnp.where(lane <= HY_BANDS, lane - 1, lane - 1 - HY_BANDS).astype(F32)
    arg = (1e-4 + bidx * ((HY_BANDS - 1 - 1e-4) / (HY_BANDS - 1))) * (pos * (2.0 * math.pi / l_total))
    feats = jnp.where(lane == 0, pos * (1.0 / (l_total - 1)),
                      jnp.where(lane <= HY_BANDS, jnp.cos(arg),
                                jnp.where(lane <= 2 * HY_BANDS, -jnp.sin(arg), 0.0)))
    hid = jnp.sin(_dot_f32(feats, w1_ref[...]) + b1_ref[...])
    hid = jnp.sin(_dot_f32(hid, w2_ref[...]) + b2_ref[...])
    lag_c = lag((rows, 1))
    t_c = lag_c.astype(F32) * (1.0 / (l_total - 1))
    lo = math.log(HY_DECAY_TARGET) / HY_SLOW_DECAY
    hi = math.log(HY_DECAY_TARGET) / HY_FAST_DECAY
    nslab = BRANCH_WIDTH // LANE // 2
    o2 = _rows2d(o_ref)
    for o in range(HY_ORDER):
        for c in range(BRANCH_WIDTH // ct):
            ch = (_iota2((1, ct), 1) + c * ct).astype(F32)
            window = jnp.exp(-t_c * jnp.abs(lo + ch * ((hi - lo) / (BRANCH_WIDTH - 1))))
            col_f = (2 * o) * BRANCH_WIDTH + c * ct
            col_b = col_f + BRANCH_WIDTH
            h_f = _dot_3pass(hid, w3_ref[:, col_f:col_f + ct]) * window
            h_b = jnp.where(lag_c == 0, 0.0, _dot_3pass(hid, w3_ref[:, col_b:col_b + ct]) * window)
            for j in range(8):
                x = jnp.concatenate([h_f[j * n2h:(j + 1) * n2h], h_b[j * n2h:(j + 1) * n2h]], axis=0).astype(BF16)
                y = jnp.dot(t_ref[j], x, preferred_element_type=F32)
                for p in range(2):
                    for q, word in enumerate(_pack_halves(y[p * n2:(p + 1) * n2], ns)):
                        group = (((o * 2 + p) * nslab + c * (ns // 2) + q) * n2)
                        o2[pl.ds(group * 8 + j, n2, stride=8), :] = word


def _hy_filter_stage_a(l_total, w1p, b1, w2, b2, w3, tab, *, ct, n1, n2):
    n2h = n2 // 2
    nslab = BRANCH_WIDTH // LANE // 2
    full = lambda a: pl.BlockSpec(a.shape, lambda g: (0,) * a.ndim)
    return pl.pallas_call(
        functools.partial(_hy_filter_stage_a_kernel, l_total=l_total, n1=n1, n2=n2, ct=ct),
        name="hy_filter_stage_a",
        grid=(n1 // 8,),
        in_specs=[full(w1p), full(b1), full(w2), full(b2), full(w3),
                  pl.BlockSpec((8, 2 * n2, 2 * n2h), lambda g: (g, 0, 0))],
        out_specs=pl.BlockSpec((HY_ORDER, 2, nslab, n2, 8, LANE), lambda g: (0, 0, 0, 0, g, 0)),
        out_shape=jax.ShapeDtypeStruct((HY_ORDER, 2, nslab, n2, n1, LANE), jnp.int32),
        compiler_params=_cparams(("arbitrary",)),
    )(w1p, b1, w2, b2, w3, tab)


@functools.lru_cache(maxsize=None)
def _fft_tables(n1, n2):
    n = n1 * n2
    n2h = n2 // 2
    k2 = np.arange(n2, dtype=np.int64)[:, None]
    m = np.arange(n2h, dtype=np.int64)[None, :]
    ta = np.zeros((n1, 2 * n2, 2 * n2h))
    for a in range(n1):
        th = 2.0 * np.pi * ((k2 * (a + n1 * m)) % n) / n
        c, s = np.cos(th), np.sin(th)
        ta[a] = np.block([[c, s], [-s, c]])
    tai = np.transpose(ta, (0, 2, 1)) / n
    k1 = np.arange(n1, dtype=np.int64)
    ph = 2.0 * np.pi * ((k1[:, None] * k1[None, :]) % n1) / n1
    c, s = np.cos(ph), np.sin(ph)
    fb = np.block([[c, s], [-s, c]])
    fs = fb[:n1] + fb[n1:]
    fsr = np.stack([fs[(-k1 - e) % n1] for e in (0, 1)])
    return tuple(np.asarray(t, np.float32) for t in (ta, tai, fb, fb.T, fs, fsr))


def _fft_split(l_total):
    n = 2 * l_total
    n1 = min(128, l_total // 8)
    return n1, n // n1


def _rows2d(ref):
    rows = 1
    for s in ref.shape[:-1]:
        rows *= s
    return ref.reshape(rows, ref.shape[-1])


def _lane_slabs(shape, imap, ns):
    def slab(s):
        def index_map(*a):
            idx = imap(*a)
            return idx[:-1] + (idx[-1] * ns + s,)
        return pl.BlockSpec(shape + (LANE,), index_map)
    return [slab(s) for s in range(ns)]


def _strided_rows(ref2, group, n, j):
    return ref2[pl.ds(group * 8 + j, n, stride=8), :]


def _hy_stage_a_kernel(gr_ref, cr_ref, gi_ref, ci_ref, *refs, ns, n2, n2h, slab_major_in):
    del gr_ref, cr_ref, gi_ref, ci_ref
    if slab_major_in:
        xs = [[(_rows2d(r), s * n2h) for s in range(ns)] for r in refs[:2]]
        refs = refs[2:]
    else:
        xs = [[(_rows2d(r), 0) for r in refs[q * ns:(q + 1) * ns]] for q in range(2)]
        refs = refs[2 * ns:]
    t_ref, o_ref = refs
    o2 = _rows2d(o_ref)
    for j in range(8):
        x = jnp.concatenate([jnp.concatenate([_strided_rows(r2, grp, n2h, j) for r2, grp in part], axis=1)
                             for part in xs], axis=0).astype(BF16)
        y = jnp.dot(t_ref[j], x, preferred_element_type=F32)
        for p in range(2):
            for q, word in enumerate(_pack_halves(y[p * n2:(p + 1) * n2], ns)):
                o2[pl.ds(((p * (ns // 2) + q) * n2) * 8 + j, n2, stride=8), :] = word


def _hy_stage_a(x, pairs, tab, *, ct, n2, slab_major_in):
    n2h = n2 // 2
    n1 = x.shape[-2] if slab_major_in else x.shape[2]
    npair = len(pairs)
    nct = BRANCH_WIDTH // ct
    ns = ct // LANE
    gr = jnp.asarray([p[0][0] for p in pairs], jnp.int32)
    cr = jnp.asarray([p[0][1] for p in pairs], jnp.int32)
    gi = jnp.asarray([p[1][0] for p in pairs], jnp.int32)
    ci = jnp.asarray([p[1][1] for p in pairs], jnp.int32)
    if slab_major_in:
        in_specs = [pl.BlockSpec((None, ns, n2h, 8, LANE), lambda p, c, g, *_: (0, c, 0, g, 0)),
                    pl.BlockSpec((None, ns, n2h, 8, LANE), lambda p, c, g, *_: (1, c, 0, g, 0))]
        args = [x, x]
    else:
        in_specs = (_lane_slabs((None, n2h, 8), lambda p, c, g, gr, cr, gi, ci: (gr[p], 0, g, cr[p] * nct + c), ns)
                    + _lane_slabs((None, n2h, 8), lambda p, c, g, gr, cr, gi, ci: (gi[p], 0, g, ci[p] * nct + c), ns))
        args = [x] * (2 * ns)
    grid_spec = pltpu.PrefetchScalarGridSpec(
        num_scalar_prefetch=4,
        grid=(npair, nct, n1 // 8),
        in_specs=in_specs + [pl.BlockSpec((8, 2 * n2, 2 * n2h), lambda p, c, g, *_: (g, 0, 0))],
        out_specs=pl.BlockSpec((None, 2, ns // 2, n2, 8, LANE), lambda p, c, g, *_: (p, 0, c, 0, g, 0)),
    )
    return pl.pallas_call(
        functools.partial(_hy_stage_a_kernel, ns=ns, n2=n2, n2h=n2h, slab_major_in=slab_major_in),
        name="hy_stage_a",
        grid_spec=grid_spec,
        out_shape=jax.ShapeDtypeStruct((npair, 2, BRANCH_WIDTH // LANE // 2, n2, n1, LANE), jnp.int32),
        compiler_params=_cparams(("parallel", "parallel", "arbitrary")),
    )(gr, cr, gi, ci, *args, tab)


def _hy_spec_kernel(c_ref, cm_ref, c0_ref, fs_ref, fsr_ref, o_ref, *, kg):
    g = pl.program_id(2)
    for k in range(kg):
        s = jnp.dot(fs_ref[...], _complex_slab(c_ref, k), preferred_element_type=F32)
        if k:
            mirror, fsr = _complex_slab(cm_ref, kg - k), fsr_ref[1]
        else:
            mirror, fsr = _complex_slab(c0_ref, 0), jnp.where(g == 0, fsr_ref[0], fsr_ref[1])
        t = jnp.dot(fsr, mirror, preferred_element_type=F32)
        o_ref[k] = pltpu.pack_elementwise([0.5 * (s + t), 0.5 * (s - t)], packed_dtype=BF16)


def _complex_slab(ref, k):
    nh = ref.shape[1]
    return jnp.concatenate([_unpack_halves([ref[p, q, k] for q in range(nh)]) for p in range(2)],
                           axis=0).astype(BF16)


def _hy_spectrum(cs, fs, fsr, *, ct, kg):
    no, _, nslab, n2, n1, _ = cs.shape
    ns = ct // LANE // 2
    ng = n2 // kg
    return pl.pallas_call(
        functools.partial(_hy_spec_kernel, kg=kg),
        name="hy_spectrum",
        grid=(no, nslab // ns, ng),
        in_specs=[pl.BlockSpec((None, 2, ns, kg, n1, LANE), lambda o, j, g: (o, 0, j, g, 0, 0)),
                  pl.BlockSpec((None, 2, ns, kg, n1, LANE), lambda o, j, g: (o, 0, j, ng - 1 - g, 0, 0)),
                  pl.BlockSpec((None, 2, ns, 1, n1, LANE), lambda o, j, g: (o, 0, j, (kg * (ng - g)) % n2, 0, 0)),
                  pl.BlockSpec(fs.shape, lambda o, j, g: (0, 0)),
                  pl.BlockSpec(fsr.shape, lambda o, j, g: (0, 0, 0))],
        out_specs=pl.BlockSpec((None, kg, n1, ct), lambda o, j, g: (o, g, 0, j)),
        out_shape=jax.ShapeDtypeStruct((no, n2, n1, 2 * nslab * LANE), jnp.int32),
        compiler_params=_cparams(("parallel", "parallel", "arbitrary")),
    )(cs, cs, cs, fs, fsr)


def _hy_stage_b_kernel(c_ref, k_ref, fb_ref, fbi_ref, o_ref, *, kg, n1):
    ns = 2 * c_ref.shape[1]
    for k in range(kg):
        z = jnp.dot(fb_ref[...], _complex_slab(c_ref, k), preferred_element_type=F32)
        zr, zi = z[:n1], z[n1:]
        kr, ki = (pltpu.unpack_elementwise(k_ref[k], index=i, packed_dtype=BF16, unpacked_dtype=F32)
                  for i in range(2))
        y = jnp.concatenate([zr * kr - zi * ki, zr * ki + zi * kr], axis=0).astype(BF16)
        d = jnp.dot(fbi_ref[...], y, preferred_element_type=F32)
        for p in range(2):
            for q, word in enumerate(_pack_halves(d[p * n1:(p + 1) * n1], ns)):
                o_ref[p, q, k] = word


def _hy_stage_b(cs, spec, order, fb, fbi, *, ct, kg):
    _, _, nslab, n2, n1, _ = cs.shape
    ns = ct // LANE // 2
    c = 2 * nslab * LANE
    return pl.pallas_call(
        functools.partial(_hy_stage_b_kernel, kg=kg, n1=n1),
        name="hy_stage_b",
        grid=(c // ct, n2 // kg),
        in_specs=[pl.BlockSpec((None, 2, ns, kg, n1, LANE), lambda j, g: (0, 0, j, g, 0, 0)),
                  pl.BlockSpec((None, kg, n1, ct), lambda j, g: (order, g, 0, j)),
                  pl.BlockSpec(fb.shape, lambda j, g: (0, 0)),
                  pl.BlockSpec(fbi.shape, lambda j, g: (0, 0))],
        out_specs=pl.BlockSpec((2, ns, kg, n1, LANE), lambda j, g: (0, j, g, 0, 0)),
        out_shape=jax.ShapeDtypeStruct((2, nslab, n2, n1, LANE), jnp.int32),
        compiler_params=_cparams(("parallel", "arbitrary")),
    )(cs, spec, fb, fbi)


def _hy_stage_c_kernel(*refs, ns, n2, n2h, slab_major_u):
    d2 = _rows2d(refs[0])
    t_ref = refs[1]
    if slab_major_u:
        u2 = _rows2d(refs[2])
        u_parts = [[(u2, (b * ns + s) * n2h) for s in range(ns)] for b in range(2)]
        refs = refs[3:]
    else:
        u_parts = [[(_rows2d(r), b * n2h) for r in refs[2:2 + ns]] for b in range(2)]
        refs = refs[2 + ns:]
    g_parts = [[(_rows2d(r), b * n2h) for r in refs[:ns]] for b in range(2)]
    skip_ref, o_ref = refs[ns:]
    o2 = _rows2d(o_ref)
    skip = skip_ref[...]

    def rows(parts, j):
        return jnp.concatenate([jnp.concatenate([_strided_rows(r2, grp, n2h, j) for r2, grp in part], axis=1)
                                for part in parts], axis=0)

    for j in range(8):
        nh = ns // 2
        d = jnp.concatenate([_unpack_halves([_strided_rows(d2, (p * nh + q) * n2, n2, j) for q in range(nh)])
                             for p in range(2)], axis=0).astype(BF16)
        y = jnp.dot(t_ref[j], d, preferred_element_type=F32)
        out = rows(g_parts, j) * (y + skip * rows(u_parts, j))
        for b in range(2):
            for s in range(ns):
                o2[pl.ds(((b * ns + s) * n2h) * 8 + j, n2h, stride=8), :] = (
                    out[b * n2h:(b + 1) * n2h, s * LANE:(s + 1) * LANE])


def _hy_stage_c(d, tai, u, ucol, z4, gcol, skip, *, ct, slab_major_u):
    _, npacked, n2, n1, _ = d.shape
    nslab = 2 * npacked
    n2h = n2 // 2
    ns = ct // LANE
    uc, gc = ucol // ct, gcol // ct
    if slab_major_u:
        u_specs = [pl.BlockSpec((2, ns, n2h, 8, LANE), lambda j, g: (0, j, 0, g, 0))]
        u_args = [u]
    else:
        u_specs = _lane_slabs((2, n2h, 8), lambda j, g: (0, 0, g, uc + j), ns)
        u_args = [u] * ns
    return pl.pallas_call(
        functools.partial(_hy_stage_c_kernel, ns=ns, n2=n2, n2h=n2h, slab_major_u=slab_major_u),
        name="hy_stage_c",
        grid=(nslab // ns, n1 // 8),
        in_specs=([pl.BlockSpec((2, ns // 2, n2, 8, LANE), lambda j, g: (0, j, 0, g, 0)),
                   pl.BlockSpec((8, 2 * n2h, 2 * n2), lambda j, g: (g, 0, 0))]
                  + u_specs + _lane_slabs((2, n2h, 8), lambda j, g: (0, 0, g, gc + j), ns)
                  + [pl.BlockSpec((1, ct), lambda j, g: (0, j))]),
        out_specs=pl.BlockSpec((2, ns, n2h, 8, LANE), lambda j, g: (0, j, 0, g, 0)),
        out_shape=jax.ShapeDtypeStruct((2, nslab, n2h, n1, LANE), F32),
        compiler_params=_cparams(("parallel", "arbitrary")),
    )(d, tai, *u_args, *([z4] * ns), skip)


def _hyena(h, filt_params, skip):
    bsz, l, _ = h.shape
    assert bsz == 2
    n1, n2 = _fft_split(l)
    n2h = n2 // 2
    ta, tai, fb, fbi, fs, fsr = (jnp.asarray(t).astype(BF16) for t in _fft_tables(n1, n2))
    ct = 512
    kg = 8

    cs_k = _hy_filter_stage_a(l, *filt_params, ta, ct=ct, n1=n1, n2=n2)
    spec = _hy_spectrum(cs_k, fs, fsr, ct=ct, kg=kg)

    z4 = h.reshape(bsz, n2h, n1, -1)
    u, ucol = z4, COL_HY
    for o in range(HY_ORDER):
        first = o == 0
        cs = _hy_stage_a(u, [((0, ucol // BRANCH_WIDTH), (1, ucol // BRANCH_WIDTH))], ta, ct=ct, n2=n2,
                         slab_major_in=not first)
        d = _hy_stage_b(cs, spec, o, fb, fbi, ct=ct, kg=kg)
        u = _hy_stage_c(d, tai, u, ucol, z4, COL_HY + (o + 1) * BRANCH_WIDTH, skip[o:o + 1], ct=ct,
                        slab_major_u=not first)
        ucol = 0
    return u.reshape(bsz, BRANCH_WIDTH // LANE, l, LANE)


def _conv_tables(hy_conv_w, hy_conv_b, dn_conv_q, dn_conv_k, dn_conv_v):
    depth = hy_conv_w.shape[0]
    ident = lambda n: jnp.broadcast_to(jnp.asarray([0.0, 1.0, 0.0], F32)[None, :, None], (depth, 3, n))
    taps = jnp.concatenate([ident(COL_HY), hy_conv_w, ident(COL_DK - COL_FF), dn_conv_k, dn_conv_v,
                            ident(COL_DQ - COL_AB), dn_conv_q, ident(COL_PAD - COL_DQ - dn_conv_q.shape[2])], axis=2)
    bias = jnp.concatenate([jnp.zeros((depth, 1, COL_HY), F32), hy_conv_b[:, None, :],
                            jnp.zeros((depth, 1, COL_PAD - COL_FF), F32)], axis=2)
    return taps, bias


def _token_mixers(u_src, sh, sc, w_t, conv_t, layer, lp, states, *, period, with_output, tm):
    l = lp["seq_len"]
    bsz = u_src.shape[0] * u_src.shape[1] // l
    tblk = min(l, 2048)
    taps, bias = (a[layer:layer + 1] for a in conv_t)
    if with_output:
        h = _in_proj(u_src, sh, sc, w_t, layer, range(COL_PAD // IN_TN), taps, bias, tm=tm, period=period)
        base = 0
    else:
        lo, hi = STATE_BLOCKS[0] * IN_TN, (STATE_BLOCKS[-1] + 1) * IN_TN
        h = _in_proj(u_src, sh, sc, w_t, layer, STATE_BLOCKS, taps[:, :, lo:hi], bias[:, :, lo:hi],
                     tm=tm, period=period)
        base = COL_FF
    h = h.reshape(bsz, l, h.shape[-1])
    col = lambda c: c - base
    ab_t = jnp.swapaxes(h[:, :, col(COL_AB):col(COL_AB) + 4 * HEADS], 1, 2)
    hg_f, hg_b, hg_states = _hgrn2_scan(h, (col(COL_FF), col(COL_FB), col(COL_HI), col(COL_HQ)),
                                        lp["lb_f"], lp["lb_b"], states[0], tblk=tblk, with_output=with_output)
    dn_f, dn_b, dn_states = _deltanet_scan(h, ab_t, (col(COL_DK), col(COL_DV), col(COL_DQ)),
                                           lp["dn_a_log"], lp["dn_dt_bias"], states[1],
                                           tblk=min(l, 2048), with_output=with_output)
    new_states = (hg_states, dn_states)
    if not with_output:
        return None, new_states
    tl = min(l, 2048)
    hg_out = _combine(hg_f, hg_b, h, COL_HGATE, lp["hg_norm_w"], tl=tl, use_silu=False)
    dn_out = _combine(dn_f, dn_b, h, COL_DZ, lp["dn_norm_w"], tl=tl, use_silu=True)
    hy_out = _hyena(h, lp["hy_filt"], lp["hy_skip"])
    ysum = _branch_merge([hy_out, hg_out, dn_out], lp["w_branch"], layer, h, tm=min(l, 1024), tn=512)
    return ysum, new_states


def kernel(x, c, ctx, c_ctx, w_ada, b_ada, w_in, hy_conv_w, hy_conv_b, hy_filt_w1, hy_filt_b1, hy_filt_w2, hy_filt_b2, hy_filt_w3, hy_skip, hg_lb_logits, hg_norm_w, dn_conv_q, dn_conv_k, dn_conv_v, dn_a_log, dn_dt_bias, dn_norm_w, w_branch, w_out, ln1_g, ln1_b, w_ff1, w_ff2, ln2_g, ln2_b):
    depth = w_in.shape[0]
    bsz, _, d = x.shape
    alpha = (2 * depth) ** 0.25
    p = jax.nn.softmax(hg_lb_logits.astype(F32), axis=1)
    lower = jnp.cumsum(p, axis=1) - p[:, :1]
    cs = jnp.concatenate([c, c_ctx[None], jnp.zeros((8 - bsz - 1, d), F32)], axis=0)
    w_r = jnp.swapaxes(w_in, 1, 2)
    w_b, w_o, w_1, w_2 = w_branch.astype(BF16), w_out.astype(BF16), w_ff1, w_ff2.astype(BF16)
    b_ada3 = b_ada[:, None, :]
    conv_t = _conv_tables(hy_conv_w, hy_conv_b, dn_conv_q, dn_conv_k, dn_conv_v)
    lc = ctx.shape[1]
    h_ctx = ctx.reshape(1, bsz * lc, d)
    for l in range(depth):
        last = l == depth - 1
        row = lambda v: v.reshape(1, -1)
        lp = {
            "lb_f": row(lower[0, l]), "lb_b": row(lower[1, l]),
            "hg_norm_w": row(hg_norm_w[l]), "dn_norm_w": row(dn_norm_w[l]),
            "dn_a_log": dn_a_log[l], "dn_dt_bias": dn_dt_bias[l], "hy_skip": hy_skip[l],
            "hy_filt": (jnp.pad(hy_filt_w1[l], ((0, LANE - HY_EMB), (0, 0))), row(hy_filt_b1[l]),
                        hy_filt_w2[l], row(hy_filt_b2[l]), hy_filt_w3[l]),
            "w_branch": w_b,
        }
        g1n, b1n, g2n, b2n = row(ln1_g[l]), row(ln1_b[l]), row(ln2_g[l]), row(ln2_b[l])
        mod = _ada_mod(cs, w_ada, b_ada3, l)
        m_lat = [mod[:bsz, i * d:(i + 1) * d][:, None, :] for i in range(6)]
        m_ctx = [mod[bsz, i * d:(i + 1) * d][None, None, :] for i in range(6)]
        zero = jnp.zeros((bsz, HEADS, HEAD_DIM, HEAD_DIM), F32)
        init = ((zero, zero), (zero, zero))

        rows_c = bsz * lc
        y_ctx, ctx_states = _token_mixers(h_ctx, m_ctx[0], m_ctx[1], w_r, conv_t, l, dict(lp, seq_len=lc), init,
                                          period=lc, with_output=not last, tm=rows_c)
        if not last:
            y_ctx = y_ctx.reshape(1, rows_c, d)
            h_ctx = _mm_ln(y_ctx, w_o, l, h_ctx, m_ctx[2], g1n, b1n, tm=rows_c, tk=d, alpha=alpha)
            mid = _mlp_up(h_ctx, m_ctx[3], m_ctx[4], w_1, l, tm=rows_c, tn=1024)
            h_ctx = _mm_ln(mid, w_2, l, h_ctx, m_ctx[5], g2n, b2n, tm=rows_c, tk=1024, alpha=alpha)

        lx = x.shape[1]
        y, _ = _token_mixers(x, m_lat[0], m_lat[1], w_r, conv_t, l, dict(lp, seq_len=lx), ctx_states,
                             period=GRID_W, with_output=True, tm=min(lx, 1024))
        x = _mm_ln(y, w_o, l, x, m_lat[2], g1n, b1n, tm=min(lx, 512), tk=d, alpha=alpha)
        mid = _mlp_up(x, m_lat[3], m_lat[4], w_1, l, tm=min(lx, 1024), tn=1024)
        x = _mm_ln(mid, w_2, l, x, m_lat[5], g2n, b2n, tm=min(lx, 512), tk=2048, alpha=alpha)
    return x
```

```python
import functools
import math

import numpy as np
import jax
import jax.numpy as jnp
from jax import lax
from jax.experimental import pallas as pl
from jax.experimental.pallas import tpu as pltpu

F32 = jnp.float32
BF16 = jnp.bfloat16

D_MODEL = 2048
GRID_W = 64
BRANCH_WIDTH = 1024
N_BRANCH = 3
HY_ORDER = 2
HY_EMB = 33
HY_BANDS = (HY_EMB - 1) // 2
HY_HIDDEN = 64
HY_DECAY_TARGET = 1e-2
HY_FAST_DECAY = 0.3
HY_SLOW_DECAY = 1.5
HEADS = 8
HEAD_DIM = 128
DN_QK_HEADS = 4
HG_F_MIN = 1e-30
CHUNK = 64
LN_EPS = 1e-5
RMS_EPS = 1e-6
LANE = 128

IN_TN = 1024
COL_GATE = 0
COL_HY = 6144
COL_FF = 9216
COL_FB = 10240
COL_HI = 11264
COL_DK = 12288
COL_DV = 12800
COL_AB = 13824
COL_HQ = 14336
COL_HGATE = 15360
COL_DQ = 16384
COL_DZ = 17408
COL_PAD = 18432
SOURCE_ROWS = np.asarray([11296 + IN_TN * k for k in range(6)] + [8224 + IN_TN * k for k in range(3)]
                         + [0, 1024, 2048, 3072, 4096, 4640, 5664, 6688, 7200], np.int32)
CONV_WINDOWS = np.asarray([1 if COL_HY <= IN_TN * j < COL_FF or IN_TN * j == COL_DK
                           else 2 if IN_TN * j in (COL_DK + IN_TN, COL_DQ) else 0
                           for j in range(COL_PAD // IN_TN)], np.int32)
STATE_BLOCKS = tuple(range(COL_FF // IN_TN, COL_HQ // IN_TN))

VMEM_LIMIT = 56 * 1024 * 1024


def _cparams(sem):
    return pltpu.CompilerParams(dimension_semantics=sem, vmem_limit_bytes=VMEM_LIMIT)


def _dot(a, b):
    return jnp.dot(a.astype(BF16), b.astype(BF16), preferred_element_type=F32)


def _dot_nt(a, b):
    return lax.dot_general(a.astype(BF16), b.astype(BF16), (((1,), (1,)), ((), ())),
                           preferred_element_type=F32)


def _dot_tn(a, b):
    return lax.dot_general(a.astype(BF16), b.astype(BF16), (((0,), (0,)), ((), ())),
                           preferred_element_type=F32)


def _sigmoid(x):
    return 1.0 / (1.0 + jnp.exp(-x))


def _silu(x):
    return x * _sigmoid(x)


def _ada_kernel(c_ref, w_ref, b_ref, o_ref):
    a = _silu(c_ref[...])
    o_ref[...] = _dot(a, w_ref[...]) + b_ref[...]


def _ada_mod(cs, w, b, layer):
    _, d, n = w.shape
    tn = 1024
    return pl.pallas_call(
        _ada_kernel,
        grid=(n // tn,),
        in_specs=[pl.BlockSpec((8, d), lambda j: (0, 0)),
                  pl.BlockSpec((None, d, tn), lambda j: (layer, 0, j)),
                  pl.BlockSpec((None, 1, tn), lambda j: (layer, 0, j))],
        out_specs=pl.BlockSpec((8, tn), lambda j: (0, j)),
        out_shape=jax.ShapeDtypeStruct((8, n), F32),
        compiler_params=_cparams(("arbitrary",)),
        name="ada_mod",
    )(cs, w, b)


def _conv3(x, w, period):
    t = x.shape[0]
    pos = jnp.bitwise_and(_iota2((t, 1), 0), period - 1)
    has_l, has_r = pos != 0, pos != period - 1
    xl = jnp.where(has_l, pltpu.roll(x, 1, 0), 0.0)
    xr = jnp.where(has_r, pltpu.roll(x, t - 1, 0), 0.0)
    return xl * w[0:1] + x * w[1:2] + xr * w[2:3]


def _mlp_up_kernel(x_ref, sh_ref, sc_ref, w_ref, o_ref, a_scr):
    @pl.when(pl.program_id(2) == 0)
    def _():
        a_scr[...] = (x_ref[...] * (1.0 + sc_ref[...]) + sh_ref[...]).astype(BF16)

    acc = jnp.dot(a_scr[...], w_ref[...].astype(BF16), preferred_element_type=F32)
    o_ref[...] = jnp.square(jnp.maximum(acc, 0.0)).astype(o_ref.dtype)


def _mlp_up(x, sh, sc, w, layer, *, tm, tn):
    bsz, l, d = x.shape
    n = w.shape[2]
    return pl.pallas_call(
        _mlp_up_kernel,
        grid=(bsz, l // tm, n // tn),
        in_specs=[pl.BlockSpec((None, tm, d), lambda b, i, j: (b, i, 0)),
                  pl.BlockSpec((None, 1, d), lambda b, i, j: (b, 0, 0)),
                  pl.BlockSpec((None, 1, d), lambda b, i, j: (b, 0, 0)),
                  pl.BlockSpec((None, d, tn), lambda b, i, j: (layer, 0, j))],
        out_specs=pl.BlockSpec((None, tm, tn), lambda b, i, j: (b, i, j)),
        out_shape=jax.ShapeDtypeStruct((bsz, l, n), BF16),
        scratch_shapes=[pltpu.VMEM((tm, d), BF16)],
        compiler_params=_cparams(("parallel", "parallel", "arbitrary")),
        name="mm_relu2",
    )(x, sh, sc, w)


def _in_proj_kernel(off_ref, conv_ref, x_ref, sh_ref, sc_ref, w_ref, cw_ref, cb_ref, o_ref, a_scr, *, period):
    del off_ref
    conv_mode = conv_ref[pl.program_id(2)]
    half = IN_TN // 2

    @pl.when(pl.program_id(2) == 0)
    def _():
        a_scr[...] = (x_ref[...] * (1.0 + sc_ref[...]) + sh_ref[...]).astype(BF16)

    o_ref[...] = lax.dot_general(a_scr[...], w_ref[0].astype(BF16), (((1,), (1,)), ((), ())),
                                 preferred_element_type=F32)

    @pl.when(conv_mode == 1)
    def _():
        o_ref[...] = _conv3(o_ref[...], cw_ref[...], period) + cb_ref[...]

    @pl.when(conv_mode == 2)
    def _():
        o_ref[:, :half] = _conv3(o_ref[:, :half], cw_ref[:, :half], period) + cb_ref[:, :half]


def _in_proj(x, sh, sc, w_t, layer, windows, taps, bias, *, tm, period):
    bsz, l, d = x.shape
    nblk = len(windows)
    assert tm % period == 0 and period & (period - 1) == 0
    row_off = jnp.asarray(SOURCE_ROWS[list(windows)])
    has_conv = jnp.asarray(CONV_WINDOWS[list(windows)])
    grid_spec = pltpu.PrefetchScalarGridSpec(
        num_scalar_prefetch=2,
        grid=(bsz, l // tm, nblk),
        in_specs=[pl.BlockSpec((None, tm, d), lambda b, i, j, off, cv: (b, i, 0)),
                  pl.BlockSpec((None, 1, d), lambda b, i, j, off, cv: (b, 0, 0)),
                  pl.BlockSpec((None, 1, d), lambda b, i, j, off, cv: (b, 0, 0)),
                  pl.BlockSpec((pl.Element(1), pl.Element(IN_TN), pl.Element(d)),
                               lambda b, i, j, off, cv: (layer, pl.multiple_of(off[j], 4 * HEADS), 0)),
                  pl.BlockSpec((None, 3, IN_TN), lambda b, i, j, off, cv: (0, 0, j)),
                  pl.BlockSpec((None, 1, IN_TN), lambda b, i, j, off, cv: (0, 0, j))],
        out_specs=pl.BlockSpec((None, tm, IN_TN), lambda b, i, j, off, cv: (b, i, j)),
        scratch_shapes=[pltpu.VMEM((tm, d), BF16)],
    )
    return pl.pallas_call(
        functools.partial(_in_proj_kernel, period=period),
        name="mm_in_proj",
        grid_spec=grid_spec,
        out_shape=jax.ShapeDtypeStruct((bsz, l, nblk * IN_TN), F32),
        compiler_params=_cparams(("parallel", "parallel", "arbitrary")),
    )(row_off, has_conv, x, sh, sc, w_t, taps, bias)


def _mm_ln_kernel(a_ref, w_ref, x_ref, gate_ref, g_ref, b_ref, o_ref, acc_ref, *, nk, alpha):
    k = pl.program_id(2)

    @pl.when(k == 0)
    def _():
        acc_ref[...] = jnp.dot(a_ref[...], w_ref[...].astype(BF16), preferred_element_type=F32)

    @pl.when(k > 0)
    def _():
        acc_ref[...] += jnp.dot(a_ref[...], w_ref[...].astype(BF16), preferred_element_type=F32)

    @pl.when(k == nk - 1)
    def _():
        y = alpha * x_ref[...] + gate_ref[...] * acc_ref[...]
        mu = jnp.mean(y, axis=-1, keepdims=True)
        yc = y - mu
        var = jnp.mean(yc * yc, axis=-1, keepdims=True)
        o_ref[...] = yc * lax.rsqrt(var + LN_EPS) * g_ref[...] + b_ref[...]


def _mm_ln(a, w, layer, x, gate, ln_g, ln_b, *, tm, tk, alpha):
    bsz, l, kdim = a.shape
    d = w.shape[2]
    nk = kdim // tk
    return pl.pallas_call(
        functools.partial(_mm_ln_kernel, nk=nk, alpha=alpha),
        grid=(bsz, l // tm, nk),
        in_specs=[pl.BlockSpec((None, tm, tk), lambda b, i, k: (b, i, k)),
                  pl.BlockSpec((None, tk, d), lambda b, i, k: (layer, k, 0)),
                  pl.BlockSpec((None, tm, d), lambda b, i, k: (b, i, 0)),
                  pl.BlockSpec((None, 1, d), lambda b, i, k: (b, 0, 0)),
                  pl.BlockSpec((1, d), lambda b, i, k: (0, 0)),
                  pl.BlockSpec((1, d), lambda b, i, k: (0, 0))],
        out_specs=pl.BlockSpec((None, tm, d), lambda b, i, k: (b, i, 0)),
        out_shape=jax.ShapeDtypeStruct((bsz, l, d), F32),
        scratch_shapes=[pltpu.VMEM((tm, d), F32)],
        compiler_params=_cparams(("parallel", "parallel", "arbitrary")),
        name="mm_resid_ln",
    )(a, w, x, gate, ln_g, ln_b)


def _branch_kernel(b0_ref, b1_ref, b2_ref, w_ref, g0_ref, g1_ref, g2_ref, o_ref):
    b0 = jnp.concatenate([b0_ref[s] for s in range(b0_ref.shape[0])], axis=1).astype(BF16)
    acc = _sigmoid(g0_ref[...]) * jnp.dot(b0, w_ref[0].astype(BF16), preferred_element_type=F32)
    acc += _sigmoid(g1_ref[...]) * jnp.dot(b1_ref[...], w_ref[1].astype(BF16), preferred_element_type=F32)
    acc += _sigmoid(g2_ref[...]) * jnp.dot(b2_ref[...], w_ref[2].astype(BF16), preferred_element_type=F32)
    o_ref[...] = acc.astype(o_ref.dtype)


def _branch_merge(branches, w, layer, h, *, tm, tn):
    bsz, l, c = branches[1].shape
    d = w.shape[3]
    nj = d // tn
    br_spec = pl.BlockSpec((None, tm, c), lambda b, i, j: (b, i, 0))
    hy_spec = pl.BlockSpec((None, c // LANE, tm, LANE), lambda b, i, j: (b, 0, i, 0))

    def gate_spec(g):
        return pl.BlockSpec((None, tm, tn), lambda b, i, j: (b, i, COL_GATE // tn + g * nj + j))

    return pl.pallas_call(
        _branch_kernel,
        grid=(bsz, l // tm, nj),
        in_specs=[hy_spec, br_spec, br_spec,
                  pl.BlockSpec((None, N_BRANCH, c, tn), lambda b, i, j: (layer, 0, 0, j)),
                  gate_spec(0), gate_spec(1), gate_spec(2)],
        out_specs=pl.BlockSpec((None, tm, tn), lambda b, i, j: (b, i, j)),
        out_shape=jax.ShapeDtypeStruct((bsz, l, d), BF16),
        compiler_params=_cparams(("parallel", "parallel", "arbitrary")),
        name="branch_merge",
    )(*branches, w, h, h, h)


def _iota2(shape, axis):
    return lax.broadcasted_iota(jnp.int32, shape, axis)


def _split3(g):
    g1 = g.astype(BF16)
    r = g - g1.astype(F32)
    g2 = r.astype(BF16)
    g3 = (r - g2.astype(F32)).astype(BF16)
    return g1, g2, g3


def _waves(items, nwaves):
    n = min(nwaves, max(1, len(items) // 8))
    per = len(items) // n
    return [items[w * per:(w + 1) * per] for w in range(n)]


def _cumsum_rows(g, rev):
    row = _iota2((CHUNK, 3 * CHUNK), 0)
    col = jnp.bitwise_and(_iota2((CHUNK, 3 * CHUNK), 1), CHUNK - 1)
    tri = jnp.where((col >= row) if rev else (col <= row), 1.0, 0.0).astype(BF16)
    g3 = jnp.concatenate(_split3(g), axis=0)
    return jnp.dot(tri, g3, preferred_element_type=F32)


def _hg_prepare(items):
    for it in items:
        lb = it["lb"]
        sg = _sigmoid(it["fz"])
        it["g"] = jnp.log(jnp.maximum(lb + (1.0 - lb) * sg, HG_F_MIN))
        it["kk"] = (1.0 - lb) * (1.0 - sg)
    for it in items:
        it["bc"] = _cumsum_rows(it["g"], it["rev"])
    half = CHUNK // 2
    for it in items:
        bc = it["bc"]
        if it["rev"]:
            mid, last = bc[half:half + 1], bc[0:1]
        else:
            mid, last = bc[half - 1:half], bc[CHUNK - 1:CHUNK]
        it["e_last"] = jnp.exp(last)
        kd = it["kk"] * jnp.exp(mid - bc)
        it["ks"] = kd * jnp.exp(last - mid)
        if it["q"] is not None:
            qd = _silu(it["q"]) * jnp.exp(bc - mid)
            it["qd"], it["kd"] = qd, kd
            it["qs"] = qd * jnp.exp(mid)
    r = _iota2((CHUNK, CHUNK), 0)
    c = _iota2((CHUNK, CHUNK), 1)
    for it in items:
        it["kv"] = _dot_tn(it["v"], it["ks"])
        if it["q"] is not None:
            att = _dot_nt(it["qd"], it["kd"])
            it["att"] = jnp.where((c >= r) if it["rev"] else (c <= r), att, 0.0)
    for it in items:
        if it["q"] is not None:
            it["o_c"] = _dot(it["att"], it["v"])


def _hg_kernel(*refs, nchunk, with_output):
    if with_output:
        (ff_ref, vf_ref, qf_ref, fb_ref, vb_ref, qb_ref, lbf_ref, lbb_ref, s0f_ref, s0b_ref,
         of_ref, ob_ref, sf_ref, sb_ref, stf, stb) = refs
    else:
        (ff_ref, vf_ref, fb_ref, vb_ref, lbf_ref, lbb_ref, s0f_ref, s0b_ref,
         sf_ref, sb_ref, stf, stb) = refs
        qf_ref = qb_ref = of_ref = ob_ref = None

    @pl.when(pl.program_id(2) == 0)
    def _():
        stf[...] = s0f_ref[...]
        stb[...] = s0b_ref[...]

    def chunk_item(f_ref, v_ref, q_ref, lb_ref, ci, rev):
        sl = pl.ds(ci * CHUNK, CHUNK)
        return {"fz": f_ref[sl, :], "v": v_ref[sl, :], "q": q_ref[sl, :] if with_output else None,
                "lb": lb_ref[...], "rev": rev, "sl": sl}

    items = []
    for ci in range(nchunk):
        items.append(chunk_item(ff_ref, vf_ref, qf_ref, lbf_ref, ci, False))
        items.append(chunk_item(fb_ref, vb_ref, qb_ref, lbb_ref, nchunk - 1 - ci, True))
    state = {False: stf[...], True: stb[...]}
    for wave in _waves(items, 8):
        _hg_prepare(wave)
        for it in wave:
            st = state[it["rev"]]
            if with_output:
                o_ref = ob_ref if it["rev"] else of_ref
                o_ref[it["sl"], :] = it["o_c"] + _dot_nt(it["qs"], st)
            state[it["rev"]] = st * it["e_last"] + it["kv"]
    stf[...] = state[False]
    stb[...] = state[True]
    sf_ref[...] = state[False]
    sb_ref[...] = state[True]


def _hgrn2_scan(h, cols, lb_f, lb_b, s0, *, tblk, with_output):
    bsz, l, _ = h.shape
    nt = l // tblk
    c_ff, c_fb, c_i, c_q = (c // LANE for c in cols)

    def fwd(cb):
        return pl.BlockSpec((None, tblk, LANE), lambda b, hh, t: (b, t, cb + hh))

    def bwd(cb):
        return pl.BlockSpec((None, tblk, LANE), lambda b, hh, t: (b, nt - 1 - t, cb + hh))

    lb_spec = pl.BlockSpec((1, LANE), lambda b, hh, t: (0, hh))
    st_spec = pl.BlockSpec((None, None, HEAD_DIM, HEAD_DIM), lambda b, hh, t: (b, hh, 0, 0))
    if with_output:
        in_specs = [fwd(c_ff), fwd(c_i), fwd(c_q), bwd(c_fb), bwd(c_i), bwd(c_q)]
        args = [h] * 6
    else:
        in_specs = [fwd(c_ff), fwd(c_i), bwd(c_fb), bwd(c_i)]
        args = [h] * 4
    in_specs += [lb_spec, lb_spec, st_spec, st_spec]
    args += [lb_f, lb_b, s0[0], s0[1]]
    st_shape = jax.ShapeDtypeStruct((bsz, HEADS, HEAD_DIM, HEAD_DIM), F32)
    out_specs = [st_spec, st_spec]
    out_shape = [st_shape, st_shape]
    if with_output:
        o_shape = jax.ShapeDtypeStruct((bsz, l, HEADS * HEAD_DIM), F32)
        out_specs = [pl.BlockSpec((None, tblk, LANE), lambda b, hh, t: (b, t, hh)),
                     pl.BlockSpec((None, tblk, LANE), lambda b, hh, t: (b, nt - 1 - t, hh))] + out_specs
        out_shape = [o_shape, o_shape] + out_shape
    outs = pl.pallas_call(
        functools.partial(_hg_kernel, nchunk=tblk // CHUNK, with_output=with_output),
        name="hgrn2_scan",
        grid=(bsz, HEADS, nt),
        in_specs=in_specs,
        out_specs=out_specs,
        out_shape=out_shape,
        scratch_shapes=[pltpu.VMEM((HEAD_DIM, HEAD_DIM), F32), pltpu.VMEM((HEAD_DIM, HEAD_DIM), F32)],
        compiler_params=_cparams(("parallel", "parallel", "arbitrary")),
    )(*args)
    if with_output:
        return outs[0], outs[1], (outs[2], outs[3])
    return None, None, (outs[0], outs[1])


def _l2norm(t):
    return t * lax.rsqrt(jnp.sum(t * t, axis=-1, keepdims=True) + 1e-6)


def _softplus(x):
    return jnp.maximum(x, 0.0) + jnp.log(1.0 + jnp.exp(-jnp.abs(x)))


def _dn_prepare(items):
    r = _iota2((CHUNK, CHUNK), 0)
    c = _iota2((CHUNK, CHUNK), 1)
    eye = r == c
    for it in items:
        rev = it["rev"]
        incl = (c >= r) if rev else (c <= r)
        incl_t = (r >= c) if rev else (r <= c)
        g_b = jnp.broadcast_to(it["g"], (CHUNK, CHUNK))
        gc_col = jnp.sum(jnp.where(incl, g_b, 0.0), axis=1, keepdims=True)
        g_col = jnp.sum(jnp.where(eye, g_b, 0.0), axis=1, keepdims=True)
        gc_row = jnp.sum(jnp.where(incl_t, jnp.broadcast_to(g_col, (CHUNK, CHUNK)), 0.0), axis=0, keepdims=True)
        beta_col = jnp.sum(jnp.where(eye, jnp.broadcast_to(it["beta"], (CHUNK, CHUNK)), 0.0), axis=1,
                           keepdims=True)
        it["decay"] = jnp.where(incl, jnp.exp(jnp.where(incl, gc_col - gc_row, 0.0)), 0.0)
        g_last = gc_col[0:1] if rev else gc_col[CHUNK - 1:CHUNK]
        it["e_last"] = jnp.exp(g_last)
        egc = jnp.exp(gc_col)
        kb = it["k"] * beta_col
        it["kb"] = kb
        it["rhs"] = jnp.concatenate([it["v"] * beta_col, kb * egc], axis=1)
        it["kd"] = it["k"] * jnp.exp(g_last - gc_col)
        if it["q"] is not None:
            it["qe"] = it["q"] * egc
    for it in items:
        if it["q"] is not None:
            both = _dot_nt(jnp.concatenate([it["kb"], it["q"]], axis=0), it["k"])
            it["kk"], it["qk"] = both[:CHUNK], both[CHUNK:] * it["decay"]
        else:
            it["kk"] = _dot_nt(it["kb"], it["k"])
    for it in items:
        strict = (c > r) if it["rev"] else (c < r)
        n = jnp.where(strict, it["kk"] * it["decay"], 0.0)
        it["n"] = n
        it["p"] = jnp.where(eye, 1.0, 0.0) - n
    for it in items:
        it["m"] = _dot(it["n"], it["n"])
    for i in range(5):
        for it in items:
            if i < 4:
                pm = _dot(jnp.concatenate([it["p"], it["m"]], axis=0), it["m"])
                it["p"] = it["p"] + pm[:CHUNK]
                it["m"] = pm[CHUNK:]
            else:
                it["p"] = it["p"] + _dot(it["p"], it["m"])
    for it in items:
        it["uw"] = _dot(it["p"], it["rhs"])
    for it in items:
        bm = _dot_tn(it["kd"], it["uw"])
        it["b_c"], it["m_c"] = bm[:, :HEAD_DIM], bm[:, HEAD_DIM:]
        if it["q"] is not None:
            ow = _dot(it["qk"], it["uw"])
            it["o_c"], it["q_eff"] = ow[:, :HEAD_DIM], it["qe"] - ow[:, HEAD_DIM:]


def _dn_kernel(*refs, nchunk, with_output):
    if with_output:
        (alog_ref, dtb_ref, kf_ref, vf_ref, qf_ref, abf_ref, kb_ref, vb_ref, qb_ref, abb_ref,
         s0f_ref, s0b_ref, of_ref, ob_ref, sf_ref, sb_ref, stf, stb) = refs
    else:
        (alog_ref, dtb_ref, kf_ref, vf_ref, abf_ref, kb_ref, vb_ref, abb_ref,
         s0f_ref, s0b_ref, sf_ref, sb_ref, stf, stb) = refs
        qf_ref = qb_ref = of_ref = ob_ref = None
    hh = pl.program_id(1)

    @pl.when(pl.program_id(2) == 0)
    def _():
        stf[...] = s0f_ref[...]
        stb[...] = s0b_ref[...]

    def prep(k_ref, v_ref, q_ref, ab_ref, d):
        k = _l2norm(_silu(k_ref[...]))
        v = _silu(v_ref[...])
        q = None
        if with_output:
            q = _l2norm(_silu(q_ref[...])) * (HEAD_DIM ** -0.5)
        a = ab_ref[pl.ds(d * HEADS + hh, 1), :]
        bt = ab_ref[pl.ds(2 * HEADS + d * HEADS + hh, 1), :]
        g = -jnp.exp(alog_ref[d, hh]) * _softplus(a + dtb_ref[d, hh])
        return k, v, q, g, _sigmoid(bt)

    kf, vf, qf, gf, bf = prep(kf_ref, vf_ref, qf_ref, abf_ref, 0)
    kb, vb, qb, gb, bb = prep(kb_ref, vb_ref, qb_ref, abb_ref, 1)
    def chunk_item(k, v, q, g, beta, ci, rev):
        lo = ci * CHUNK
        return {"k": k[lo:lo + CHUNK], "v": v[lo:lo + CHUNK], "q": q[lo:lo + CHUNK] if with_output else None,
                "g": g[:, lo:lo + CHUNK], "beta": beta[:, lo:lo + CHUNK], "rev": rev, "lo": lo}

    items = []
    for ci in range(nchunk):
        items.append(chunk_item(kf, vf, qf, gf, bf, ci, False))
        items.append(chunk_item(kb, vb, qb, gb, bb, nchunk - 1 - ci, True))
    state = {False: stf[...], True: stb[...]}
    for wave in _waves(items, 4):
        _dn_prepare(wave)
        for it in wave:
            s = state[it["rev"]]
            if with_output:
                o_ref = ob_ref if it["rev"] else of_ref
                o_ref[pl.ds(it["lo"], CHUNK), :] = it["o_c"] + _dot(it["q_eff"], s)
            state[it["rev"]] = s * it["e_last"] + it["b_c"] - _dot(it["m_c"], s)
    stf[...] = state[False]
    stb[...] = state[True]
    sf_ref[...] = state[False]
    sb_ref[...] = state[True]


def _deltanet_scan(h, ab_t, cols, a_log, dt_bias, s0, *, tblk, with_output):
    bsz, l, _ = h.shape
    nt = l // tblk
    c_k, c_v, c_q = (c // LANE for c in cols)
    rep = HEADS // DN_QK_HEADS

    def tok(cb, shared, rev):
        def imap(b, hh, t):
            return (b, (nt - 1 - t) if rev else t, cb + (hh // rep if shared else hh))
        return pl.BlockSpec((None, tblk, LANE), imap)

    def ab_spec(rev):
        return pl.BlockSpec((None, 4 * HEADS, tblk), lambda b, hh, t: (b, 0, (nt - 1 - t) if rev else t))

    smem = pl.BlockSpec(memory_space=pltpu.SMEM)
    st_spec = pl.BlockSpec((None, None, HEAD_DIM, HEAD_DIM), lambda b, hh, t: (b, hh, 0, 0))
    if with_output:
        in_specs = [smem, smem, tok(c_k, True, False), tok(c_v, False, False), tok(c_q, True, False), ab_spec(False),
                    tok(c_k, True, True), tok(c_v, False, True), tok(c_q, True, True), ab_spec(True),
                    st_spec, st_spec]
        args = [a_log, dt_bias, h, h, h, ab_t, h, h, h, ab_t, s0[0], s0[1]]
    else:
        in_specs = [smem, smem, tok(c_k, True, False), tok(c_v, False, False), ab_spec(False),
                    tok(c_k, True, True), tok(c_v, False, True), ab_spec(True),
                    st_spec, st_spec]
        args = [a_log, dt_bias, h, h, ab_t, h, h, ab_t, s0[0], s0[1]]
    st_shape = jax.ShapeDtypeStruct((bsz, HEADS, HEAD_DIM, HEAD_DIM), F32)
    out_specs = [st_spec, st_spec]
    out_shape = [st_shape, st_shape]
    if with_output:
        o_shape = jax.ShapeDtypeStruct((bsz, l, HEADS * HEAD_DIM), F32)
        out_specs = [pl.BlockSpec((None, tblk, LANE), lambda b, hh, t: (b, t, hh)),
                     pl.BlockSpec((None, tblk, LANE), lambda b, hh, t: (b, nt - 1 - t, hh))] + out_specs
        out_shape = [o_shape, o_shape] + out_shape
    outs = pl.pallas_call(
        functools.partial(_dn_kernel, nchunk=tblk // CHUNK, with_output=with_output),
        name="deltanet_scan",
        grid=(bsz, HEADS, nt),
        in_specs=in_specs,
        out_specs=out_specs,
        out_shape=out_shape,
        scratch_shapes=[pltpu.VMEM((HEAD_DIM, HEAD_DIM), F32), pltpu.VMEM((HEAD_DIM, HEAD_DIM), F32)],
        compiler_params=_cparams(("parallel", "parallel", "arbitrary")),
    )(*args)
    if with_output:
        return outs[0], outs[1], (outs[2], outs[3])
    return None, None, (outs[0], outs[1])


def _combine_kernel(of_ref, ob_ref, z_ref, w_ref, o_ref, *, use_silu):
    for hh in range(of_ref.shape[1] // HEAD_DIM):
        sl = slice(hh * HEAD_DIM, (hh + 1) * HEAD_DIM)
        o = of_ref[:, sl] + ob_ref[:, sl]
        o = o * lax.rsqrt(jnp.mean(o * o, axis=-1, keepdims=True) + RMS_EPS) * w_ref[...]
        z = z_ref[:, sl]
        o_ref[:, sl] = (o * (_silu(z) if use_silu else _sigmoid(z))).astype(o_ref.dtype)


def _combine(o_f, o_b, h, col_z, norm_w, *, tl, use_silu):
    bsz, l, c = o_f.shape
    assert col_z % c == 0
    spec = pl.BlockSpec((None, tl, c), lambda b, i: (b, i, 0))
    return pl.pallas_call(
        functools.partial(_combine_kernel, use_silu=use_silu),
        name="mixer_norm_gate",
        grid=(bsz, l // tl),
        in_specs=[spec, spec,
                  pl.BlockSpec((None, tl, c), lambda b, i: (b, i, col_z // c)),
                  pl.BlockSpec((1, HEAD_DIM), lambda b, i: (0, 0))],
        out_specs=spec,
        out_shape=jax.ShapeDtypeStruct((bsz, l, c), BF16),
        compiler_params=_cparams(("parallel", "parallel")),
    )(o_f, o_b, h, norm_w)


def _dot_f32(a, b):
    return jnp.dot(a, b, precision=lax.Precision.HIGHEST, preferred_element_type=F32)


def _pack_halves(y, ns):
    nh = ns // 2
    return [pltpu.pack_elementwise([y[:, q * LANE:(q + 1) * LANE], y[:, (q + nh) * LANE:(q + nh + 1) * LANE]],
                                   packed_dtype=BF16) for q in range(nh)]


def _unpack_halves(ws):
    part = lambda i: [pltpu.unpack_elementwise(w, index=i, packed_dtype=BF16, unpacked_dtype=F32) for w in ws]
    return jnp.concatenate(part(0) + part(1), axis=1)


def _dot_3pass(a, b):
    a1 = a.astype(BF16)
    a2 = (a - a1.astype(F32)).astype(BF16)
    b1 = b.astype(BF16)
    b2 = (b - b1.astype(F32)).astype(BF16)
    return jnp.dot(jnp.concatenate([a1, a1, a2], axis=1), jnp.concatenate([b1, b2, b1], axis=0),
                   preferred_element_type=F32)


def _hy_filter_stage_a_kernel(w1_ref, b1_ref, w2_ref, b2_ref, w3_ref, t_ref, o_ref, *, l_total, n1, n2, ct):
    n2h = n2 // 2
    rows = 8 * n2h
    ns = ct // LANE
    g = pl.program_id(0)

    def lag(shape):
        r = _iota2(shape, 0)
        return 8 * g + jnp.right_shift(r, n2h.bit_length() - 1) + n1 * jnp.bitwise_and(r, n2h - 1)

    pos = lag((rows, LANE)).astype(F32)
    lane = _iota2((rows, LANE), 1)
    bidx = jnp.where(lane <= HY_BANDS, lane - 1, lane - 1 - HY_BANDS).astype(F32)
    arg = (1e-4 + bidx * ((HY_BANDS - 1 - 1e-4) / (HY_BANDS - 1))) * (pos * (2.0 * math.pi / l_total))
    feats = jnp.where(lane == 0, pos * (1.0 / (l_total - 1)),
                      jnp.where(lane <= HY_BANDS, jnp.cos(arg),
                                jnp.where(lane <= 2 * HY_BANDS, -jnp.sin(arg), 0.0)))
    hid = jnp.sin(_dot_f32(feats, w1_ref[...]) + b1_ref[...])
    hid = jnp.sin(_dot_f32(hid, w2_ref[...]) + b2_ref[...])
    lag_c = lag((rows, 1))
    t_c = lag_c.astype(F32) * (1.0 / (l_total - 1))
    lo = math.log(HY_DECAY_TARGET) / HY_SLOW_DECAY
    hi = math.log(HY_DECAY_TARGET) / HY_FAST_DECAY
    nslab = BRANCH_WIDTH // LANE // 2
    o2 = _rows2d(o_ref)
    for o in range(HY_ORDER):
        for c in range(BRANCH_WIDTH // ct):
            ch = (_iota2((1, ct), 1) + c * ct).astype(F32)
            window = jnp.exp(-t_c * jnp.abs(lo + ch * ((hi - lo) / (BRANCH_WIDTH - 1))))
            col_f = (2 * o) * BRANCH_WIDTH + c * ct
            col_b = col_f + BRANCH_WIDTH
            h_f = _dot_3pass(hid, w3_ref[:, col_f:col_f + ct]) * window
            h_b = jnp.where(lag_c == 0, 0.0, _dot_3pass(hid, w3_ref[:, col_b:col_b + ct]) * window)
            for j in range(8):
                x = jnp.concatenate([h_f[j * n2h:(j + 1) * n2h], h_b[j * n2h:(j + 1) * n2h]], axis=0).astype(BF16)
                y = jnp.dot(t_ref[j], x, preferred_element_type=F32)
                for p in range(2):
                    for q, word in enumerate(_pack_halves(y[p * n2:(p + 1) * n2], ns)):
                        group = (((o * 2 + p) * nslab + c * (ns // 2) + q) * n2)
                        o2[pl.ds(group * 8 + j, n2, stride=8), :] = word


def _hy_filter_stage_a(l_total, w1p, b1, w2, b2, w3, tab, *, ct, n1, n2):
    n2h = n2 // 2
    nslab = BRANCH_WIDTH // LANE // 2
    full = lambda a: pl.BlockSpec(a.shape, lambda g: (0,) * a.ndim)
    return pl.pallas_call(
        functools.partial(_hy_filter_stage_a_kernel, l_total=l_total, n1=n1, n2=n2, ct=ct),
        name="hy_filter_stage_a",
        grid=(n1 // 8,),
        in_specs=[full(w1p), full(b1), full(w2), full(b2), full(w3),
                  pl.BlockSpec((8, 2 * n2, 2 * n2h), lambda g: (g, 0, 0))],
        out_specs=pl.BlockSpec((HY_ORDER, 2, nslab, n2, 8, LANE), lambda g: (0, 0, 0, 0, g, 0)),
        out_shape=jax.ShapeDtypeStruct((HY_ORDER, 2, nslab, n2, n1, LANE), jnp.int32),
        compiler_params=_cparams(("arbitrary",)),
    )(w1p, b1, w2, b2, w3, tab)


@functools.lru_cache(maxsize=None)
def _fft_tables(n1, n2):
    n = n1 * n2
    n2h = n2 // 2
    k2 = np.arange(n2, dtype=np.int64)[:, None]
    m = np.arange(n2h, dtype=np.int64)[None, :]
    ta = np.zeros((n1, 2 * n2, 2 * n2h))
    for a in range(n1):
        th = 2.0 * np.pi * ((k2 * (a + n1 * m)) % n) / n
        c, s = np.cos(th), np.sin(th)
        ta[a] = np.block([[c, s], [-s, c]])
    tai = np.transpose(ta, (0, 2, 1)) / n
    k1 = np.arange(n1, dtype=np.int64)
    ph = 2.0 * np.pi * ((k1[:, None] * k1[None, :]) % n1) / n1
    c, s = np.cos(ph), np.sin(ph)
    fb = np.block([[c, s], [-s, c]])
    fs = fb[:n1] + fb[n1:]
    fsr = np.stack([fs[(-k1 - e) % n1] for e in (0, 1)])
    return tuple(np.asarray(t, np.float32) for t in (ta, tai, fb, fb.T, fs, fsr))


def _fft_split(l_total):
    n = 2 * l_total
    n1 = min(128, l_total // 8)
    return n1, n // n1


def _rows2d(ref):
    rows = 1
    for s in ref.shape[:-1]:
        rows *= s
    return ref.reshape(rows, ref.shape[-1])


def _lane_slabs(shape, imap, ns):
    def slab(s):
        def index_map(*a):
            idx = imap(*a)
            return idx[:-1] + (idx[-1] * ns + s,)
        return pl.BlockSpec(shape + (LANE,), index_map)
    return [slab(s) for s in range(ns)]


def _strided_rows(ref2, group, n, j):
    return ref2[pl.ds(group * 8 + j, n, stride=8), :]


def _hy_stage_a_kernel(gr_ref, cr_ref, gi_ref, ci_ref, *refs, ns, n2, n2h, slab_major_in):
    del gr_ref, cr_ref, gi_ref, ci_ref
    if slab_major_in:
        xs = [[(_rows2d(r), s * n2h) for s in range(ns)] for r in refs[:2]]
        refs = refs[2:]
    else:
        xs = [[(_rows2d(r), 0) for r in refs[q * ns:(q + 1) * ns]] for q in range(2)]
        refs = refs[2 * ns:]
    t_ref, o_ref = refs
    o2 = _rows2d(o_ref)
    for j in range(8):
        x = jnp.concatenate([jnp.concatenate([_strided_rows(r2, grp, n2h, j) for r2, grp in part], axis=1)
                             for part in xs], axis=0).astype(BF16)
        y = jnp.dot(t_ref[j], x, preferred_element_type=F32)
        for p in range(2):
            for q, word in enumerate(_pack_halves(y[p * n2:(p + 1) * n2], ns)):
                o2[pl.ds(((p * (ns // 2) + q) * n2) * 8 + j, n2, stride=8), :] = word


def _hy_stage_a(x, pairs, tab, *, ct, n2, slab_major_in):
    n2h = n2 // 2
    n1 = x.shape[-2] if slab_major_in else x.shape[2]
    npair = len(pairs)
    nct = BRANCH_WIDTH // ct
    ns = ct // LANE
    gr = jnp.asarray([p[0][0] for p in pairs], jnp.int32)
    cr = jnp.asarray([p[0][1] for p in pairs], jnp.int32)
    gi = jnp.asarray([p[1][0] for p in pairs], jnp.int32)
    ci = jnp.asarray([p[1][1] for p in pairs], jnp.int32)
    if slab_major_in:
        in_specs = [pl.BlockSpec((None, ns, n2h, 8, LANE), lambda p, c, g, *_: (0, c, 0, g, 0)),
                    pl.BlockSpec((None, ns, n2h, 8, LANE), lambda p, c, g, *_: (1, c, 0, g, 0))]
        args = [x, x]
    else:
        in_specs = (_lane_slabs((None, n2h, 8), lambda p, c, g, gr, cr, gi, ci: (gr[p], 0, g, cr[p] * nct + c), ns)
                    + _lane_slabs((None, n2h, 8), lambda p, c, g, gr, cr, gi, ci: (gi[p], 0, g, ci[p] * nct + c), ns))
        args = [x] * (2 * ns)
    grid_spec = pltpu.PrefetchScalarGridSpec(
        num_scalar_prefetch=4,
        grid=(npair, nct, n1 // 8),
        in_specs=in_specs + [pl.BlockSpec((8, 2 * n2, 2 * n2h), lambda p, c, g, *_: (g, 0, 0))],
        out_specs=pl.BlockSpec((None, 2, ns // 2, n2, 8, LANE), lambda p, c, g, *_: (p, 0, c, 0, g, 0)),
    )
    return pl.pallas_call(
        functools.partial(_hy_stage_a_kernel, ns=ns, n2=n2, n2h=n2h, slab_major_in=slab_major_in),
        name="hy_stage_a",
        grid_spec=grid_spec,
        out_shape=jax.ShapeDtypeStruct((npair, 2, BRANCH_WIDTH // LANE // 2, n2, n1, LANE), jnp.int32),
        compiler_params=_cparams(("parallel", "parallel", "arbitrary")),
    )(gr, cr, gi, ci, *args, tab)


def _hy_spec_kernel(c_ref, cm_ref, c0_ref, fs_ref, fsr_ref, o_ref, *, kg):
    g = pl.program_id(2)
    for k in range(kg):
        s = jnp.dot(fs_ref[...], _complex_slab(c_ref, k), preferred_element_type=F32)
        if k:
            mirror, fsr = _complex_slab(cm_ref, kg - k), fsr_ref[1]
        else:
            mirror, fsr = _complex_slab(c0_ref, 0), jnp.where(g == 0, fsr_ref[0], fsr_ref[1])
        t = jnp.dot(fsr, mirror, preferred_element_type=F32)
        o_ref[k] = pltpu.pack_elementwise([0.5 * (s + t), 0.5 * (s - t)], packed_dtype=BF16)


def _complex_slab(ref, k):
    nh = ref.shape[1]
    return jnp.concatenate([_unpack_halves([ref[p, q, k] for q in range(nh)]) for p in range(2)],
                           axis=0).astype(BF16)


def _hy_spectrum(cs, fs, fsr, *, ct, kg):
    no, _, nslab, n2, n1, _ = cs.shape
    ns = ct // LANE // 2
    ng = n2 // kg
    return pl.pallas_call(
        functools.partial(_hy_spec_kernel, kg=kg),
        name="hy_spectrum",
        grid=(no, nslab // ns, ng),
        in_specs=[pl.BlockSpec((None, 2, ns, kg, n1, LANE), lambda o, j, g: (o, 0, j, g, 0, 0)),
                  pl.BlockSpec((None, 2, ns, kg, n1, LANE), lambda o, j, g: (o, 0, j, ng - 1 - g, 0, 0)),
                  pl.BlockSpec((None, 2, ns, 1, n1, LANE), lambda o, j, g: (o, 0, j, (kg * (ng - g)) % n2, 0, 0)),
                  pl.BlockSpec(fs.shape, lambda o, j, g: (0, 0)),
                  pl.BlockSpec(fsr.shape, lambda o, j, g: (0, 0, 0))],
        out_specs=pl.BlockSpec((None, kg, n1, ct), lambda o, j, g: (o, g, 0, j)),
        out_shape=jax.ShapeDtypeStruct((no, n2, n1, 2 * nslab * LANE), jnp.int32),
        compiler_params=_cparams(("parallel", "parallel", "arbitrary")),
    )(cs, cs, cs, fs, fsr)


def _hy_stage_b_kernel(c_ref, k_ref, fb_ref, fbi_ref, o_ref, *, kg, n1):
    ns = 2 * c_ref.shape[1]
    for k in range(kg):
        z = jnp.dot(fb_ref[...], _complex_slab(c_ref, k), preferred_element_type=F32)
        zr, zi = z[:n1], z[n1:]
        kr, ki = (pltpu.unpack_elementwise(k_ref[k], index=i, packed_dtype=BF16, unpacked_dtype=F32)
                  for i in range(2))
        y = jnp.concatenate([zr * kr - zi * ki, zr * ki + zi * kr], axis=0).astype(BF16)
        d = jnp.dot(fbi_ref[...], y, preferred_element_type=F32)
        for p in range(2):
            for q, word in enumerate(_pack_halves(d[p * n1:(p + 1) * n1], ns)):
                o_ref[p, q, k] = word


def _hy_stage_b(cs, spec, order, fb, fbi, *, ct, kg):
    _, _, nslab, n2, n1, _ = cs.shape
    ns = ct // LANE // 2
    c = 2 * nslab * LANE
    return pl.pallas_call(
        functools.partial(_hy_stage_b_kernel, kg=kg, n1=n1),
        name="hy_stage_b",
        grid=(c // ct, n2 // kg),
        in_specs=[pl.BlockSpec((None, 2, ns, kg, n1, LANE), lambda j, g: (0, 0, j, g, 0, 0)),
                  pl.BlockSpec((None, kg, n1, ct), lambda j, g: (order, g, 0, j)),
                  pl.BlockSpec(fb.shape, lambda j, g: (0, 0)),
                  pl.BlockSpec(fbi.shape, lambda j, g: (0, 0))],
        out_specs=pl.BlockSpec((2, ns, kg, n1, LANE), lambda j, g: (0, j, g, 0, 0)),
        out_shape=jax.ShapeDtypeStruct((2, nslab, n2, n1, LANE), jnp.int32),
        compiler_params=_cparams(("parallel", "arbitrary")),
    )(cs, spec, fb, fbi)


def _hy_stage_c_kernel(*refs, ns, n2, n2h, slab_major_u):
    d2 = _rows2d(refs[0])
    t_ref = refs[1]
    if slab_major_u:
        u2 = _rows2d(refs[2])
        u_parts = [[(u2, (b * ns + s) * n2h) for s in range(ns)] for b in range(2)]
        refs = refs[3:]
    else:
        u_parts = [[(_rows2d(r), b * n2h) for r in refs[2:2 + ns]] for b in range(2)]
        refs = refs[2 + ns:]
    g_parts = [[(_rows2d(r), b * n2h) for r in refs[:ns]] for b in range(2)]
    skip_ref, o_ref = refs[ns:]
    o2 = _rows2d(o_ref)
    skip = skip_ref[...]

    def rows(parts, j):
        return jnp.concatenate([jnp.concatenate([_strided_rows(r2, grp, n2h, j) for r2, grp in part], axis=1)
                                for part in parts], axis=0)

    for j in range(8):
        nh = ns // 2
        d = jnp.concatenate([_unpack_halves([_strided_rows(d2, (p * nh + q) * n2, n2, j) for q in range(nh)])
                             for p in range(2)], axis=0).astype(BF16)
        y = jnp.dot(t_ref[j], d, preferred_element_type=F32)
        out = rows(g_parts, j) * (y + skip * rows(u_parts, j))
        for b in range(2):
            for s in range(ns):
                o2[pl.ds(((b * ns + s) * n2h) * 8 + j, n2h, stride=8), :] = (
                    out[b * n2h:(b + 1) * n2h, s * LANE:(s + 1) * LANE])


def _hy_stage_c(d, tai, u, ucol, z4, gcol, skip, *, ct, slab_major_u):
    _, npacked, n2, n1, _ = d.shape
    nslab = 2 * npacked
    n2h = n2 // 2
    ns = ct // LANE
    uc, gc = ucol // ct, gcol // ct
    if slab_major_u:
        u_specs = [pl.BlockSpec((2, ns, n2h, 8, LANE), lambda j, g: (0, j, 0, g, 0))]
        u_args = [u]
    else:
        u_specs = _lane_slabs((2, n2h, 8), lambda j, g: (0, 0, g, uc + j), ns)
        u_args = [u] * ns
    return pl.pallas_call(
        functools.partial(_hy_stage_c_kernel, ns=ns, n2=n2, n2h=n2h, slab_major_u=slab_major_u),
        name="hy_stage_c",
        grid=(nslab // ns, n1 // 8),
        in_specs=([pl.BlockSpec((2, ns // 2, n2, 8, LANE), lambda j, g: (0, j, 0, g, 0)),
                   pl.BlockSpec((8, 2 * n2h, 2 * n2), lambda j, g: (g, 0, 0))]
                  + u_specs + _lane_slabs((2, n2h, 8), lambda j, g: (0, 0, g, gc + j), ns)
                  + [pl.BlockSpec((1, ct), lambda j, g: (0, j))]),
        out_specs=pl.BlockSpec((2, ns, n2h, 8, LANE), lambda j, g: (0, j, 0, g, 0)),
        out_shape=jax.ShapeDtypeStruct((2, nslab, n2h, n1, LANE), F32),
        compiler_params=_cparams(("parallel", "arbitrary")),
    )(d, tai, *u_args, *([z4] * ns), skip)


def _hyena(h, filt_params, skip):
    bsz, l, _ = h.shape
    assert bsz == 2
    n1, n2 = _fft_split(l)
    n2h = n2 // 2
    ta, tai, fb, fbi, fs, fsr = (jnp.asarray(t).astype(BF16) for t in _fft_tables(n1, n2))
    ct = 512
    kg = 8

    cs_k = _hy_filter_stage_a(l, *filt_params, ta, ct=ct, n1=n1, n2=n2)
    spec = _hy_spectrum(cs_k, fs, fsr, ct=ct, kg=kg)

    z4 = h.reshape(bsz, n2h, n1, -1)
    u, ucol = z4, COL_HY
    for o in range(HY_ORDER):
        first = o == 0
        cs = _hy_stage_a(u, [((0, ucol // BRANCH_WIDTH), (1, ucol // BRANCH_WIDTH))], ta, ct=ct, n2=n2,
                         slab_major_in=not first)
        d = _hy_stage_b(cs, spec, o, fb, fbi, ct=ct, kg=kg)
        u = _hy_stage_c(d, tai, u, ucol, z4, COL_HY + (o + 1) * BRANCH_WIDTH, skip[o:o + 1], ct=ct,
                        slab_major_u=not first)
        ucol = 0
    return u.reshape(bsz, BRANCH_WIDTH // LANE, l, LANE)


def _conv_tables(hy_conv_w, hy_conv_b, dn_conv_q, dn_conv_k, dn_conv_v):
    depth = hy_conv_w.shape[0]
    ident = lambda n: jnp.broadcast_to(jnp.asarray([0.0, 1.0, 0.0], F32)[None, :, None], (depth, 3, n))
    taps = jnp.concatenate([ident(COL_HY), hy_conv_w, ident(COL_DK - COL_FF), dn_conv_k, dn_conv_v,
                            ident(COL_DQ - COL_AB), dn_conv_q, ident(COL_PAD - COL_DQ - dn_conv_q.shape[2])], axis=2)
    bias = jnp.concatenate([jnp.zeros((depth, 1, COL_HY), F32), hy_conv_b[:, None, :],
                            jnp.zeros((depth, 1, COL_PAD - COL_FF), F32)], axis=2)
    return taps, bias


def _token_mixers(u_src, sh, sc, w_t, conv_t, layer, lp, states, *, period, with_output, tm):
    l = lp["seq_len"]
    bsz = u_src.shape[0] * u_src.shape[1] // l
    tblk = min(l, 2048)
    taps, bias = (a[layer:layer + 1] for a in conv_t)
    if with_output:
        h = _in_proj(u_src, sh, sc, w_t, layer, range(COL_PAD // IN_TN), taps, bias, tm=tm, period=period)
        base = 0
    else:
        lo, hi = STATE_BLOCKS[0] * IN_TN, (STATE_BLOCKS[-1] + 1) * IN_TN
        h = _in_proj(u_src, sh, sc, w_t, layer, STATE_BLOCKS, taps[:, :, lo:hi], bias[:, :, lo:hi],
                     tm=tm, period=period)
        base = COL_FF
    h = h.reshape(bsz, l, h.shape[-1])
    col = lambda c: c - base
    ab_t = jnp.swapaxes(h[:, :, col(COL_AB):col(COL_AB) + 4 * HEADS], 1, 2)
    hg_f, hg_b, hg_states = _hgrn2_scan(h, (col(COL_FF), col(COL_FB), col(COL_HI), col(COL_HQ)),
                                        lp["lb_f"], lp["lb_b"], states[0], tblk=tblk, with_output=with_output)
    dn_f, dn_b, dn_states = _deltanet_scan(h, ab_t, (col(COL_DK), col(COL_DV), col(COL_DQ)),
                                           lp["dn_a_log"], lp["dn_dt_bias"], states[1],
                                           tblk=min(l, 2048), with_output=with_output)
    new_states = (hg_states, dn_states)
    if not with_output:
        return None, new_states
    tl = min(l, 512)
    hg_out = _combine(hg_f, hg_b, h, COL_HGATE, lp["hg_norm_w"], tl=tl, use_silu=False)
    dn_out = _combine(dn_f, dn_b, h, COL_DZ, lp["dn_norm_w"], tl=tl, use_silu=True)
    hy_out = _hyena(h, lp["hy_filt"], lp["hy_skip"])
    ysum = _branch_merge([hy_out, hg_out, dn_out], lp["w_branch"], layer, h, tm=min(l, 1024), tn=512)
    return ysum, new_states


def kernel(x, c, ctx, c_ctx, w_ada, b_ada, w_in, hy_conv_w, hy_conv_b, hy_filt_w1, hy_filt_b1, hy_filt_w2, hy_filt_b2, hy_filt_w3, hy_skip, hg_lb_logits, hg_norm_w, dn_conv_q, dn_conv_k, dn_conv_v, dn_a_log, dn_dt_bias, dn_norm_w, w_branch, w_out, ln1_g, ln1_b, w_ff1, w_ff2, ln2_g, ln2_b):
    depth = w_in.shape[0]
    bsz, _, d = x.shape
    alpha = (2 * depth) ** 0.25
    p = jax.nn.softmax(hg_lb_logits.astype(F32), axis=1)
    lower = jnp.cumsum(p, axis=1) - p[:, :1]
    cs = jnp.concatenate([c, c_ctx[None], jnp.zeros((8 - bsz - 1, d), F32)], axis=0)
    w_r = jnp.swapaxes(w_in, 1, 2)
    w_b, w_o, w_1, w_2 = w_branch.astype(BF16), w_out.astype(BF16), w_ff1, w_ff2.astype(BF16)
    b_ada3 = b_ada[:, None, :]
    conv_t = _conv_tables(hy_conv_w, hy_conv_b, dn_conv_q, dn_conv_k, dn_conv_v)
    lc = ctx.shape[1]
    h_ctx = ctx.reshape(1, bsz * lc, d)
    for l in range(depth):
        last = l == depth - 1
        row = lambda v: v.reshape(1, -1)
        lp = {
            "lb_f": row(lower[0, l]), "lb_b": row(lower[1, l]),
            "hg_norm_w": row(hg_norm_w[l]), "dn_norm_w": row(dn_norm_w[l]),
            "dn_a_log": dn_a_log[l], "dn_dt_bias": dn_dt_bias[l], "hy_skip": hy_skip[l],
            "hy_filt": (jnp.pad(hy_filt_w1[l], ((0, LANE - HY_EMB), (0, 0))), row(hy_filt_b1[l]),
                        hy_filt_w2[l], row(hy_filt_b2[l]), hy_filt_w3[l]),
            "w_branch": w_b,
        }
        g1n, b1n, g2n, b2n = row(ln1_g[l]), row(ln1_b[l]), row(ln2_g[l]), row(ln2_b[l])
        mod = _ada_mod(cs, w_ada, b_ada3, l)
        m_lat = [mod[:bsz, i * d:(i + 1) * d][:, None, :] for i in range(6)]
        m_ctx = [mod[bsz, i * d:(i + 1) * d][None, None, :] for i in range(6)]
        zero = jnp.zeros((bsz, HEADS, HEAD_DIM, HEAD_DIM), F32)
        init = ((zero, zero), (zero, zero))

        rows_c = bsz * lc
        y_ctx, ctx_states = _token_mixers(h_ctx, m_ctx[0], m_ctx[1], w_r, conv_t, l, dict(lp, seq_len=lc), init,
                                          period=lc, with_output=not last, tm=rows_c)
        if not last:
            y_ctx = y_ctx.reshape(1, rows_c, d)
            h_ctx = _mm_ln(y_ctx, w_o, l, h_ctx, m_ctx[2], g1n, b1n, tm=rows_c, tk=d, alpha=alpha)
            mid = _mlp_up(h_ctx, m_ctx[3], m_ctx[4], w_1, l, tm=rows_c, tn=1024)
            h_ctx = _mm_ln(mid, w_2, l, h_ctx, m_ctx[5], g2n, b2n, tm=rows_c, tk=1024, alpha=alpha)

        lx = x.shape[1]
        y, _ = _token_mixers(x, m_lat[0], m_lat[1], w_r, conv_t, l, dict(lp, seq_len=lx), ctx_states,
                             period=GRID_W, with_output=True, tm=min(lx, 1024))
        x = _mm_ln(y, w_o, l, x, m_lat[2], g1n, b1n, tm=min(lx, 512), tk=d, alpha=alpha)
        mid = _mlp_up(x, m_lat[3], m_lat[4], w_1, l, tm=min(lx, 1024), tn=1024)
        x = _mm_ln(mid, w_2, l, x, m_lat[5], g2n, b2n, tm=min(lx, 512), tk=2048, alpha=alpha)
    return x
```
